```python
import math
import jax
import jax.numpy as jnp
from jax import lax
import numpy as np

D_MODEL = 1024
BATCH = 4
SEQ = 4096
DEPTH = 2
DEC_BATCH = 32
DEC_SEQ = 4
PAST_LEN = 16384
PAGE_SIZE = 128

HEAD_DIM = 128
N_DIL_HEADS = 4
DIL_GROUPS = ((128, 1), (512, 4), (2048, 16))
N_GROUPS = len(DIL_GROUPS)
DIL_SPAN = 128
ATT_WIDTH = N_DIL_HEADS * HEAD_DIM
N_CROSS_HEADS = 4
CROSS_WIDTH = N_CROSS_HEADS * HEAD_DIM
N_MEM = 256
ROT_DIM = HEAD_DIM // 4
ROPE_THETA = 500000.0
S5_WIDTH = ATT_WIDTH
S5_GROUP = 16
S5_GROUPS = S5_WIDTH // S5_GROUP
S5_STATE = 64
D_FF = 2816
CONV_W = 3
BLOCK = 128
EPS = 1e-6
NEG = -1e30
N_A_LAYERS = (DEPTH + 1) // 2
N_B_LAYERS = DEPTH // 2
QKV_WIDTH = 3 * N_GROUPS * ATT_WIDTH
IN_A = QKV_WIDTH + CROSS_WIDTH
IN_B = S5_WIDTH + CROSS_WIDTH
MIX_OUT = ATT_WIDTH + CROSS_WIDTH
SCALE = HEAD_DIM ** -0.5

kernel_name = 'dilated_s5_hybrid_decode_step'


def rmsnorm(x, g):
    xf = x.astype(jnp.float32)
    y = xf * lax.rsqrt(jnp.mean(xf * xf, axis=-1, keepdims=True) + EPS)
    return (y * g.astype(jnp.float32)).astype(x.dtype)


def rope_partial(x, pos):
    half = ROT_DIM // 2
    inv = jnp.exp(-math.log(ROPE_THETA) * jnp.arange(half, dtype=jnp.float32) / half)
    ang = pos.astype(jnp.float32)[:, None] * inv[None, :]
    cos = jnp.cos(ang)[:, None, :]
    sin = jnp.sin(ang)[:, None, :]
    xf = x.astype(jnp.float32)
    x1, x2, rest = xf[..., :half], xf[..., half:ROT_DIM], xf[..., ROT_DIM:]
    return jnp.concatenate([x1 * cos - x2 * sin, x2 * cos + x1 * sin, rest], axis=-1).astype(x.dtype)


def banded_attention(q, k, v):
    n, l, h, d = q.shape
    nb = -(-l // BLOCK)
    pad = nb * BLOCK - l
    qb = jnp.pad(q.astype(jnp.float32), ((0, 0), (0, pad), (0, 0), (0, 0))).reshape(n, nb, BLOCK, h, d)

    def band(t):
        tp = jnp.pad(t.astype(jnp.float32), ((0, 0), (BLOCK, pad), (0, 0), (0, 0))).reshape(n, nb + 1, BLOCK, h, d)
        return jnp.concatenate([tp[:, :-1], tp[:, 1:]], axis=2)

    kb, vb = band(k), band(v)
    s = jnp.einsum('nbqhd,nbkhd->nbhqk', qb, kb) * SCALE
    qi = jnp.arange(BLOCK)[:, None]
    kj = jnp.arange(2 * BLOCK)[None, :]
    dist = BLOCK + qi - kj
    in_band = (dist >= 0) & (dist <= DIL_SPAN)
    past_ok = (jnp.arange(nb) > 0)[:, None, None] | (kj >= BLOCK)[None]
    mask = in_band[None] & past_ok
    s = jnp.where(mask[None, :, None], s, NEG)
    lse = jax.nn.logsumexp(s, axis=-1)
    p = jnp.exp(s - lse[..., None])
    o = jnp.einsum('nbhqk,nbkhd->nbqhd', p, vb).reshape(n, nb * BLOCK, h, d)[:, :l]
    lse = lse.transpose(0, 1, 3, 2).reshape(n, nb * BLOCK, h)[:, :l]
    return o, lse


def dilated_prompt(q, k, v, r):
    b, s, h, d = q.shape
    l = s // r

    def to_res(t):
        return t.reshape(b, l, r, h, d).transpose(0, 2, 1, 3, 4).reshape(b * r, l, h, d)

    o, lse = banded_attention(to_res(q), to_res(k), to_res(v))
    o = o.reshape(b, r, l, h, d).transpose(0, 2, 1, 3, 4).reshape(b, s, h, d)
    lse = lse.reshape(b, r, l, h).transpose(0, 2, 1, 3).reshape(b, s, h)
    return o, lse


def dilated_sample(q, k, v, kv_buf, r):
    t = q.shape[1]
    lb = kv_buf.shape[1]
    k_all = jnp.concatenate([kv_buf[:, :, 0].astype(jnp.float32), k.astype(jnp.float32)], axis=1)
    v_all = jnp.concatenate([kv_buf[:, :, 1].astype(jnp.float32), v.astype(jnp.float32)], axis=1)
    idx = lb + jnp.arange(t)[:, None] - r * jnp.arange(DIL_SPAN + 1)[None, :]
    valid = idx >= 0
    idx = jnp.maximum(idx, 0)
    kg = k_all[:, idx]
    vg = v_all[:, idx]
    s = jnp.einsum('bthd,btjhd->bthj', q.astype(jnp.float32), kg) * SCALE
    s = jnp.where(valid[None, :, None, :], s, NEG)
    lse = jax.nn.logsumexp(s, axis=-1)
    p = jnp.exp(s - lse[..., None])
    o = jnp.einsum('bthj,btjhd->bthd', p, vg)
    new_buf = jnp.concatenate([kv_buf, jnp.stack([k, v], axis=2).astype(kv_buf.dtype)], axis=1)[:, t:]
    return o, lse, new_buf


def combine_groups(outs, lses):
    wts = jax.nn.softmax(jnp.stack(lses, axis=0), axis=0)
    return jnp.einsum('gnsh,gnshd->nshd', wts, jnp.stack(outs, axis=0))


def s5_mixer(u, s0, lam_re, lam_im, log_dt, b_re, b_im, c_re, c_im, d_skip, w_glu, b_glu):
    n, s, _ = u.shape
    ug = u.astype(jnp.float32).reshape(n, s, S5_GROUPS, S5_GROUP)
    a_re = lam_re.astype(jnp.float32)
    a_im = lam_im.astype(jnp.float32)
    dt = jnp.exp(log_dt.astype(jnp.float32))[:, None]
    mag = jnp.exp(a_re * dt)
    lb_re = mag * jnp.cos(a_im * dt)
    lb_im = mag * jnp.sin(a_im * dt)
    den = a_re * a_re + a_im * a_im
    xr, yi = lb_re - 1.0, lb_im
    f_re = (xr * a_re + yi * a_im) / den
    f_im = (yi * a_re - xr * a_im) / den
    br = b_re.astype(jnp.float32)
    bi = b_im.astype(jnp.float32)
    bb_re = f_re[..., None] * br - f_im[..., None] * bi
    bb_im = f_re[..., None] * bi + f_im[..., None] * br
    bu_re = jnp.einsum('nsgc,gpc->nsgp', ug, bb_re)
    bu_im = jnp.einsum('nsgc,gpc->nsgp', ug, bb_im)
    ar0 = jnp.broadcast_to(lb_re, bu_re.shape)
    ai0 = jnp.broadcast_to(lb_im, bu_re.shape)

    def combine(e1, e2):
        a1r, a1i, b1r, b1i = e1
        a2r, a2i, b2r, b2i = e2
        return (a2r * a1r - a2i * a1i, a2r * a1i + a2i * a1r,
                a2r * b1r - a2i * b1i + b2r, a2r * b1i + a2i * b1r + b2i)

    cr, ci, sr, si = lax.associative_scan(combine, (ar0, ai0, bu_re, bu_im), axis=1)
    if s0 is not None:
        s0r = s0[:, 0].astype(jnp.float32)[:, None]
        s0i = s0[:, 1].astype(jnp.float32)[:, None]
        sr, si = sr + cr * s0r - ci * s0i, si + cr * s0i + ci * s0r
    y = (jnp.einsum('nsgp,gcp->nsgc', sr, c_re.astype(jnp.float32))
         - jnp.einsum('nsgp,gcp->nsgc', si, c_im.astype(jnp.float32))
         + d_skip.astype(jnp.float32) * ug).reshape(n, s, S5_WIDTH)
    y = jax.nn.gelu(y)
    out = y * jax.nn.sigmoid(y @ w_glu.astype(jnp.float32) + b_glu.astype(jnp.float32))
    state = jnp.stack([sr[:, -1], si[:, -1]], axis=1)
    return out.astype(u.dtype), state.astype(u.dtype)


def memory_kv(mem, g_mem, w_kv, g_k):
    n, m, _ = mem.shape
    kv = (rmsnorm(mem, g_mem) @ w_kv).reshape(n, m, 2, N_CROSS_HEADS, HEAD_DIM)
    return jnp.stack([rmsnorm(kv[:, :, 0], g_k), kv[:, :, 1]], axis=2)


def cross_attn(qc, kv, g_q):
    n, s, _ = qc.shape
    q = rmsnorm(qc.reshape(n, s, N_CROSS_HEADS, HEAD_DIM), g_q).astype(jnp.float32)
    sc = jnp.einsum('nshd,nmhd->nhsm', q, kv[:, :, 0].astype(jnp.float32)) * SCALE
    p = jax.nn.softmax(sc, axis=-1)
    o = jnp.einsum('nhsm,nmhd->nshd', p, kv[:, :, 1].astype(jnp.float32))
    return o.reshape(n, s, CROSS_WIDTH)


def conv_ffn(h, buf, w_up, conv_w, conv_b, w_down):
    s = h.shape[1]
    up = h @ w_up
    ext = jnp.concatenate([buf.astype(up.dtype), up], axis=1)
    c = conv_b
    for j in range(CONV_W):
        c = c + conv_w[j] * ext[:, j:j + s]
    a, b = jnp.split(c, 2, axis=-1)
    return (jax.nn.silu(a) * b) @ w_down, ext[:, s:]


def trunk(x, pos, p, mem=None, win_in=None, mem_kv_in=None, s5_in=None, conv_in=None):
    is_prompt = win_in is None
    n, s, _ = x.shape
    new_win = [[] for _ in range(N_GROUPS)]
    new_mem, new_s5, new_conv = [], [], []
    for i in range(DEPTH):
        h = rmsnorm(x, p['g_mix'][i])
        if is_prompt:
            kv_m = memory_kv(mem, p['g_mem'][i], p['w_mem_kv'][i], p['g_k_cross'][i])
            new_mem.append(kv_m)
        else:
            kv_m = mem_kv_in[i]
        if i % 2 == 0:
            ia = i // 2
            proj = h @ p['w_in_a'][ia]
            qkv = proj[..., :QKV_WIDTH].reshape(n, s, 3, N_GROUPS, N_DIL_HEADS, HEAD_DIM)
            qc = proj[..., QKV_WIDTH:]
            outs, lses = [], []
            for g, (w, r) in enumerate(DIL_GROUPS):
                q = rope_partial(rmsnorm(qkv[:, :, 0, g], p['g_q_dil'][ia, g]), pos)
                k = rope_partial(rmsnorm(qkv[:, :, 1, g], p['g_k_dil'][ia, g]), pos)
                v = qkv[:, :, 2, g]
                if is_prompt:
                    o, l = dilated_prompt(q, k, v, r)
                    keep = min(w, s)
                    new_win[g].append(jnp.stack([k, v], axis=2)[:, s - keep:])
                else:
                    o, l, nbuf = dilated_sample(q, k, v, win_in[g][ia], r)
                    new_win[g].append(nbuf)
                outs.append(o)
                lses.append(l)
            mix = combine_groups(outs, lses).reshape(n, s, ATT_WIDTH)
        else:
            ib = i // 2
            proj = h @ p['w_in_b'][ib]
            u = proj[..., :S5_WIDTH]
            qc = proj[..., S5_WIDTH:]
            s0 = None if is_prompt else s5_in[ib]
            mix, s_last = s5_mixer(u, s0, p['s5_lam_re'][ib], p['s5_lam_im'][ib], p['s5_log_dt'][ib],
                                   p['s5_b_re'][ib], p['s5_b_im'][ib], p['s5_c_re'][ib], p['s5_c_im'][ib],
                                   p['s5_d'][ib], p['w_glu'][ib], p['b_glu'][ib])
            new_s5.append(s_last)
        cross = cross_attn(qc, kv_m, p['g_q_cross'][i])
        merged = jnp.concatenate([mix.astype(x.dtype), cross.astype(x.dtype)], axis=-1)
        x = x + merged @ p['w_out'][i]
        h = rmsnorm(x, p['g_ffn'][i])
        buf = jnp.zeros((n, CONV_W - 1, 2 * D_FF), x.dtype) if is_prompt else conv_in[i]
        f, nconv = conv_ffn(h, buf, p['w_up'][i], p['conv_w'][i], p['conv_b'][i], p['w_down'][i])
        new_conv.append(nconv)
        x = x + f
    wins = [jnp.stack(nw, axis=0) for nw in new_win]
    mem_out = jnp.stack(new_mem, axis=0) if is_prompt else None
    return x, wins, mem_out, jnp.stack(new_s5, axis=0), jnp.stack(new_conv, axis=0)


def setup_inputs(seed: int = 0) -> dict:
    key = jax.random.key(seed)
    ks = iter(jax.random.split(key, 48))

    def nrm(shape, scale):
        return scale * jax.random.normal(next(ks), shape, jnp.float32)

    def gain(shape):
        return 1.0 + 0.05 * jax.random.normal(next(ks), shape, jnp.float32)

    lw = [min(w, PAST_LEN) for w, _ in DIL_GROUPS]
    kv_tail = (2, N_DIL_HEADS, HEAD_DIM)
    n_idx = jnp.arange(S5_STATE, dtype=jnp.float32)
    return {
        'x_prompt': nrm((BATCH, SEQ, D_MODEL), 1.0),
        'x_sample': nrm((DEC_BATCH, DEC_SEQ, D_MODEL), 1.0),
        'cache_win0_kv': nrm((N_A_LAYERS, DEC_BATCH, lw[0]) + kv_tail, 1.0),
        'cache_win1_kv': nrm((N_A_LAYERS, DEC_BATCH, lw[1]) + kv_tail, 1.0),
        'cache_win2_kv': nrm((N_A_LAYERS, DEC_BATCH, lw[2]) + kv_tail, 1.0),
        'cache_mem_kv': nrm((DEPTH, DEC_BATCH, N_MEM, 2, N_CROSS_HEADS, HEAD_DIM), 1.0),
        'state_s5': nrm((N_B_LAYERS, DEC_BATCH, 2, S5_GROUPS, S5_STATE), 0.1),
        'state_ffn_conv': nrm((DEPTH, DEC_BATCH, CONV_W - 1, 2 * D_FF), 1.0),
        'mem_prompt': nrm((BATCH, N_MEM, D_MODEL), 1.0),
        'g_mix': gain((DEPTH, D_MODEL)),
        'g_ffn': gain((DEPTH, D_MODEL)),
        'w_in_a': nrm((N_A_LAYERS, D_MODEL, IN_A), D_MODEL ** -0.5),
        'g_q_dil': gain((N_A_LAYERS, N_GROUPS, HEAD_DIM)),
        'g_k_dil': gain((N_A_LAYERS, N_GROUPS, HEAD_DIM)),
        'w_in_b': nrm((N_B_LAYERS, D_MODEL, IN_B), D_MODEL ** -0.5),
        's5_lam_re': -0.5 * jnp.exp(0.05 * jax.random.normal(next(ks), (N_B_LAYERS, S5_GROUPS, S5_STATE), jnp.float32)),
        's5_lam_im': math.pi * n_idx + 0.01 * jax.random.normal(next(ks), (N_B_LAYERS, S5_GROUPS, S5_STATE), jnp.float32),
        's5_log_dt': jax.random.uniform(next(ks), (N_B_LAYERS, S5_GROUPS), jnp.float32, math.log(0.001), math.log(0.1)),
        's5_b_re': nrm((N_B_LAYERS, S5_GROUPS, S5_STATE, S5_GROUP), (2 * S5_GROUP) ** -0.5),
        's5_b_im': nrm((N_B_LAYERS, S5_GROUPS, S5_STATE, S5_GROUP), (2 * S5_GROUP) ** -0.5),
        's5_c_re': nrm((N_B_LAYERS, S5_GROUPS, S5_GROUP, S5_STATE), (2 * S5_STATE) ** -0.5),
        's5_c_im': nrm((N_B_LAYERS, S5_GROUPS, S5_GROUP, S5_STATE), (2 * S5_STATE) ** -0.5),
        's5_d': nrm((N_B_LAYERS, S5_GROUPS, S5_GROUP), 1.0),
        'w_glu': nrm((N_B_LAYERS, S5_WIDTH, S5_WIDTH), S5_WIDTH ** -0.5),
        'b_glu': nrm((N_B_LAYERS, S5_WIDTH), 0.02),
        'g_mem': gain((DEPTH, D_MODEL)),
        'w_mem_kv': nrm((DEPTH, D_MODEL, 2 * CROSS_WIDTH), D_MODEL ** -0.5),
        'g_q_cross': gain((DEPTH, HEAD_DIM)),
        'g_k_cross': gain((DEPTH, HEAD_DIM)),
        'w_out': nrm((DEPTH, MIX_OUT, D_MODEL), MIX_OUT ** -0.5),
        'w_up': nrm((DEPTH, D_MODEL, 2 * D_FF), D_MODEL ** -0.5),
        'conv_w': nrm((DEPTH, CONV_W, 2 * D_FF), CONV_W ** -0.5),
        'conv_b': nrm((DEPTH, 2 * D_FF), 0.02),
        'w_down': nrm((DEPTH, D_FF, D_MODEL), D_FF ** -0.5),
    }


def reference(x_prompt, x_sample, cache_win0_kv, cache_win1_kv, cache_win2_kv, cache_mem_kv,
              state_s5, state_ffn_conv, mem_prompt, g_mix, g_ffn, w_in_a, g_q_dil, g_k_dil, w_in_b,
              s5_lam_re, s5_lam_im, s5_log_dt, s5_b_re, s5_b_im, s5_c_re, s5_c_im, s5_d, w_glu, b_glu,
              g_mem, w_mem_kv, g_q_cross, g_k_cross, w_out, w_up, conv_w, conv_b, w_down):
    params = dict(g_mix=g_mix, g_ffn=g_ffn, w_in_a=w_in_a, g_q_dil=g_q_dil, g_k_dil=g_k_dil,
                  w_in_b=w_in_b, s5_lam_re=s5_lam_re, s5_lam_im=s5_lam_im, s5_log_dt=s5_log_dt,
                  s5_b_re=s5_b_re, s5_b_im=s5_b_im, s5_c_re=s5_c_re, s5_c_im=s5_c_im, s5_d=s5_d,
                  w_glu=w_glu, b_glu=b_glu, g_mem=g_mem, w_mem_kv=w_mem_kv, g_q_cross=g_q_cross,
                  g_k_cross=g_k_cross, w_out=w_out, w_up=w_up, conv_w=conv_w, conv_b=conv_b,
                  w_down=w_down)
    pos_p = jnp.arange(x_prompt.shape[1], dtype=jnp.int32)
    pos_s = PAST_LEN + jnp.arange(x_sample.shape[1], dtype=jnp.int32)
    y_prompt, win_p, mem_p, s5_p, conv_p = trunk(x_prompt, pos_p, params, mem=mem_prompt)
    y_sample, win_s, _, s5_s, conv_s = trunk(
        x_sample, pos_s, params,
        win_in=(cache_win0_kv, cache_win1_kv, cache_win2_kv),
        mem_kv_in=cache_mem_kv, s5_in=state_s5, conv_in=state_ffn_conv)
    return (y_prompt, y_sample, win_p[0], win_p[1], win_p[2], mem_p, s5_p, conv_p,
            win_s[0], win_s[1], win_s[2], s5_s, conv_s)
```

```python
import functools
import math

import jax
import jax.numpy as jnp
from jax import lax
from jax.experimental import pallas as pl
from jax.experimental.pallas import tpu as pltpu

HEAD_DIM = 128
N_HEADS = 4
DIL_GROUPS = ((128, 1), (512, 4), (2048, 16))
N_GROUPS = len(DIL_GROUPS)
DIL_SPAN = 128
BLOCK = 128
ATT_WIDTH = N_HEADS * HEAD_DIM
ROT_DIM = HEAD_DIM // 4
ROT_HALF = ROT_DIM // 2
ROPE_THETA = 500000.0
S5_GROUP = 16
S5_STATE = 64
S5_HALF = 256
CONV_W = 3
EPS = 1e-6
NEG = -1e30
SCALE = HEAD_DIM ** -0.5
PAST_LEN = 16384

VMEM_LIMIT_V7X = 56 * 1024 * 1024
BF16 = jnp.bfloat16
F32 = jnp.float32
HIGHEST = lax.Precision.HIGHEST


def _params(*sem):
    return pltpu.CompilerParams(dimension_semantics=sem, vmem_limit_bytes=VMEM_LIMIT_V7X)


def _rms(x, g):
    return x * lax.rsqrt(jnp.mean(x * x, axis=-1, keepdims=True) + EPS) * g


def _dot(a, b):
    return jnp.dot(a, b, preferred_element_type=F32)


def _dot_nt(a, b):
    return lax.dot_general(a, b, (((1,), (1,)), ((), ())), preferred_element_type=F32)


def _full(shape):
    nd = len(shape)
    return pl.BlockSpec(shape, lambda *_: (0,) * nd)


def _mem_kv_kernel(mem_ref, g_ref, w_ref, gk_ref, out_ref):
    h = _rms(mem_ref[...], g_ref[0]).astype(BF16)
    kv = _dot(h, w_ref[0])
    for hd in range(N_HEADS):
        sl = slice(hd * HEAD_DIM, (hd + 1) * HEAD_DIM)
        out_ref[0, :, sl] = _rms(kv[:, sl], gk_ref[0])
    out_ref[0, :, ATT_WIDTH:] = kv[:, ATT_WIDTH:]


def _mem_kv(mem2d, g_mem, w_kv_bf, g_k):
    depth, d, _ = w_kv_bf.shape
    rows = mem2d.shape[0]
    tm = min(rows, 512)
    return pl.pallas_call(
        _mem_kv_kernel,
        grid=(depth, rows // tm),
        in_specs=[
            pl.BlockSpec((tm, d), lambda l, i: (i, 0)),
            pl.BlockSpec((1, 1, d), lambda l, i: (l, 0, 0)),
            pl.BlockSpec((1, d, 2 * ATT_WIDTH), lambda l, i: (l, 0, 0)),
            pl.BlockSpec((1, 1, HEAD_DIM), lambda l, i: (l, 0, 0)),
        ],
        out_specs=pl.BlockSpec((1, tm, 2 * ATT_WIDTH), lambda l, i: (l, i, 0)),
        out_shape=jax.ShapeDtypeStruct((depth, rows, 2 * ATT_WIDTH), F32),
        compiler_params=_params("arbitrary", "arbitrary"),
        name="mem_kv",
    )(mem2d, g_mem.reshape(depth, 1, d), w_kv_bf, g_k.reshape(depth, 1, HEAD_DIM))


def _cross_heads(qc, kv, gq):
    outs = []
    for hd in range(N_HEADS):
        sl = slice(hd * HEAD_DIM, (hd + 1) * HEAD_DIM)
        q = _rms(qc[:, sl], gq).astype(BF16)
        k = kv[:, sl].astype(BF16)
        v = kv[:, ATT_WIDTH + hd * HEAD_DIM:ATT_WIDTH + (hd + 1) * HEAD_DIM].astype(BF16)
        s = _dot_nt(q, k) * SCALE
        m = jnp.max(s, axis=-1, keepdims=True)
        p = jnp.exp(s - m)
        l = jnp.sum(p, axis=-1, keepdims=True)
        outs.append(_dot(p.astype(BF16), v) / l)
    return outs


def _rope(x, cos_t, sin_lo, sin_hi):
    return (x * cos_t + pltpu.roll(x, HEAD_DIM - ROT_HALF, 1) * sin_lo
            + pltpu.roll(x, ROT_HALF, 1) * sin_hi)


def _in_proj_a_kernel(*refs, fuse_cross, win_rows):
    if fuse_cross:
        (x_ref, g_ref, w_ref, cos_ref, slo_ref, shi_ref, gq_ref, gk_ref, gqc_ref, mem_ref,
         qkv_ref, w0_ref, w1_ref, w2_ref, cr_ref) = refs
    else:
        (x_ref, g_ref, w_ref, cos_ref, slo_ref, shi_ref, gq_ref, gk_ref,
         qkv_ref, w0_ref, w1_ref, w2_ref, cr_ref) = refs
    win_refs = (w0_ref, w1_ref, w2_ref)
    tm = x_ref.shape[0]
    h = _rms(x_ref[...], g_ref[...]).astype(BF16)
    cos_t, sin_lo, sin_hi = cos_ref[...], slo_ref[...], shi_ref[...]
    for role in range(3):
        for g in range(N_GROUPS):
            c = role * N_GROUPS + g
            y = _dot(h, w_ref[:, c * ATT_WIDTH:(c + 1) * ATT_WIDTH])
            wr = win_rows[g]
            for hd in range(N_HEADS):
                yh = y[:, hd * HEAD_DIM:(hd + 1) * HEAD_DIM]
                if role == 0:
                    yh = _rope(_rms(yh, gq_ref[g]), cos_t, sin_lo, sin_hi)
                elif role == 1:
                    yh = _rope(_rms(yh, gk_ref[g]), cos_t, sin_lo, sin_hi)
                qkv_ref[role * N_GROUPS * N_HEADS + g * N_HEADS + hd] = yh.astype(BF16)
                if role > 0:
                    col = (role - 1) * ATT_WIDTH + hd * HEAD_DIM
                    win_refs[g][:, col:col + HEAD_DIM] = yh[tm - wr:, :]
    qc = _dot(h, w_ref[:, 3 * N_GROUPS * ATT_WIDTH:])
    if fuse_cross:
        outs = _cross_heads(qc, mem_ref[0], gqc_ref[...])
        for hd in range(N_HEADS):
            cr_ref[:, hd * HEAD_DIM:(hd + 1) * HEAD_DIM] = outs[hd].astype(cr_ref.dtype)
    else:
        cr_ref[...] = qc


def _in_proj_b_kernel(*refs, fuse_cross):
    if fuse_cross:
        x_ref, g_ref, w_ref, gqc_ref, mem_ref, u_ref, cr_ref = refs
    else:
        x_ref, g_ref, w_ref, u_ref, cr_ref = refs
    h = _rms(x_ref[...], g_ref[...]).astype(BF16)
    u = _dot(h, w_ref[:, :ATT_WIDTH])
    u_ref[0] = u[:, :S5_HALF].astype(BF16)
    u_ref[1] = u[:, S5_HALF:].astype(BF16)
    qc = _dot(h, w_ref[:, ATT_WIDTH:])
    if fuse_cross:
        outs = _cross_heads(qc, mem_ref[0], gqc_ref[...])
        for hd in range(N_HEADS):
            cr_ref[:, hd * HEAD_DIM:(hd + 1) * HEAD_DIM] = outs[hd].astype(cr_ref.dtype)
    else:
        cr_ref[...] = qc


def _rope_tables(pos):
    inv = jnp.exp(-math.log(ROPE_THETA) * jnp.arange(ROT_HALF, dtype=F32) / ROT_HALF)
    ang = pos.astype(F32)[:, None] * inv[None, :]
    cos, sin = jnp.cos(ang), jnp.sin(ang)
    rows = pos.shape[0]
    ones = jnp.ones((rows, HEAD_DIM - ROT_DIM), F32)
    zeros = jnp.zeros((rows, HEAD_DIM - ROT_DIM), F32)
    z16 = jnp.zeros((rows, ROT_HALF), F32)
    cos_t = jnp.concatenate([cos, cos, ones], axis=1)
    sin_lo = jnp.concatenate([-sin, z16, zeros], axis=1)
    sin_hi = jnp.concatenate([z16, sin, zeros], axis=1)
    return cos_t, sin_lo, sin_hi


def _in_proj_a(x2d, seq, tm, g_mix, w_bf, tables, g_q, g_k, g_qc, mem_kv, win_keep):
    rows, d = x2d.shape
    n_in = w_bf.shape[1]
    tiles_per_seq = seq // tm if seq >= tm else 1
    n_seq = rows // seq
    fuse_cross = mem_kv is not None
    tab_tiles = tables[0].shape[0] // tm
    win_rows = tuple(min(k, tm) for k in win_keep)

    def win_spec(keep):
        wr = min(keep, tm)
        nblk = keep // wr
        first = tiles_per_seq - nblk

        def imap(i):
            b = i // tiles_per_seq
            t = i % tiles_per_seq
            return (b * nblk + jnp.maximum(t - first, 0), 0)
        return pl.BlockSpec((wr, 2 * ATT_WIDTH), imap)

    in_specs = [
        pl.BlockSpec((tm, d), lambda i: (i, 0)),
        _full((1, d)),
        _full((d, n_in)),
        pl.BlockSpec((tm, HEAD_DIM), lambda i: (i % tab_tiles, 0)),
        pl.BlockSpec((tm, HEAD_DIM), lambda i: (i % tab_tiles, 0)),
        pl.BlockSpec((tm, HEAD_DIM), lambda i: (i % tab_tiles, 0)),
        _full((N_GROUPS, HEAD_DIM)),
        _full((N_GROUPS, HEAD_DIM)),
    ]
    args = [x2d, g_mix.reshape(1, d), w_bf, *tables, g_q, g_k]
    if fuse_cross:
        in_specs += [_full((1, HEAD_DIM)),
                     pl.BlockSpec((1,) + mem_kv.shape[1:], lambda i: (i // tiles_per_seq, 0, 0))]
        args += [g_qc.reshape(1, HEAD_DIM), mem_kv]
    out_specs = [
        pl.BlockSpec((3 * N_GROUPS * N_HEADS, tm, HEAD_DIM), lambda i: (0, i, 0)),
        *[win_spec(k) for k in win_keep],
        pl.BlockSpec((tm, ATT_WIDTH), lambda i: (i, 0)),
    ]
    out_shape = [
        jax.ShapeDtypeStruct((3 * N_GROUPS * N_HEADS, rows, HEAD_DIM), BF16),
        *[jax.ShapeDtypeStruct((n_seq * k, 2 * ATT_WIDTH), F32) for k in win_keep],
        jax.ShapeDtypeStruct((rows, ATT_WIDTH), BF16 if fuse_cross else F32),
    ]
    return pl.pallas_call(
        functools.partial(_in_proj_a_kernel, fuse_cross=fuse_cross, win_rows=win_rows),
        grid=(rows // tm,),
        in_specs=in_specs, out_specs=out_specs, out_shape=out_shape,
        compiler_params=_params("arbitrary"),
        name="in_proj_a",
    )(*args)


def _in_proj_b(x2d, seq, tm, g_mix, w_bf, g_qc, mem_kv):
    rows, d = x2d.shape
    n_in = w_bf.shape[1]
    tiles_per_seq = seq // tm if seq >= tm else 1
    fuse_cross = mem_kv is not None
    in_specs = [pl.BlockSpec((tm, d), lambda i: (i, 0)), _full((1, d)), _full((d, n_in))]
    args = [x2d, g_mix.reshape(1, d), w_bf]
    if fuse_cross:
        in_specs += [_full((1, HEAD_DIM)),
                     pl.BlockSpec((1,) + mem_kv.shape[1:], lambda i: (i // tiles_per_seq, 0, 0))]
        args += [g_qc.reshape(1, HEAD_DIM), mem_kv]
    return pl.pallas_call(
        functools.partial(_in_proj_b_kernel, fuse_cross=fuse_cross),
        grid=(rows // tm,),
        in_specs=in_specs,
        out_specs=[pl.BlockSpec((2, tm, S5_HALF), lambda i: (0, i, 0)),
                   pl.BlockSpec((tm, ATT_WIDTH), lambda i: (i, 0))],
        out_shape=[jax.ShapeDtypeStruct((2, rows, S5_HALF), BF16),
                   jax.ShapeDtypeStruct((rows, ATT_WIDTH), BF16 if fuse_cross else F32)],
        compiler_params=_params("arbitrary"),
        name="in_proj_b",
    )(*args)


def _cross_sample_kernel(qc_ref, gq_ref, mem_ref, out_ref):
    outs = _cross_heads(qc_ref[0], mem_ref[0], gq_ref[...])
    for hd in range(N_HEADS):
        out_ref[0, :, hd * HEAD_DIM:(hd + 1) * HEAD_DIM] = outs[hd].astype(out_ref.dtype)


def _cross_sample(qc, g_qc, mem_kv):
    nb, tq, _ = qc.shape
    return pl.pallas_call(
        _cross_sample_kernel,
        grid=(nb,),
        in_specs=[pl.BlockSpec((1, tq, ATT_WIDTH), lambda b: (b, 0, 0)),
                  _full((1, HEAD_DIM)),
                  pl.BlockSpec((1,) + mem_kv.shape[1:], lambda b: (b, 0, 0))],
        out_specs=pl.BlockSpec((1, tq, ATT_WIDTH), lambda b: (b, 0, 0)),
        out_shape=jax.ShapeDtypeStruct((nb, tq, ATT_WIDTH), BF16),
        compiler_params=_params("arbitrary"),
        name="cross_sample",
    )(qc, g_qc.reshape(1, HEAD_DIM), mem_kv)


def _band_block(q, kc, vc, kp, vp):
    qi = lax.broadcasted_iota(jnp.int32, (BLOCK, BLOCK), 0)
    kj = lax.broadcasted_iota(jnp.int32, (BLOCK, BLOCK), 1)
    sc = jnp.where(kj <= qi, _dot_nt(q, kc) * SCALE, NEG)
    m = jnp.max(sc, axis=-1, keepdims=True)
    if kp is not None:
        sp = jnp.where(kj >= qi, _dot_nt(q, kp) * SCALE, NEG)
        m = jnp.maximum(m, jnp.max(sp, axis=-1, keepdims=True))
    pc = jnp.exp(sc - m)
    l = jnp.sum(pc, axis=-1, keepdims=True)
    o = _dot(pc.astype(BF16), vc)
    if kp is not None:
        pp = jnp.exp(sp - m)
        l = l + jnp.sum(pp, axis=-1, keepdims=True)
        o = o + _dot(pp.astype(BF16), vp)
    return o / l, m + jnp.log(l)


def _attn_prompt_kernel(*refs, seq):
    qkv_refs = refs[:3 * N_GROUPS]
    out_ref = refs[3 * N_GROUPS]
    o_scr, l_scr = refs[3 * N_GROUPS + 1:]
    for g, (_, r) in enumerate(DIL_GROUPS):
        q_ref, k_ref, v_ref = qkv_refs[3 * g:3 * g + 3]
        nblk = seq // r // BLOCK

        def store(rho, blk, o, lse, g=g, r=r):
            start = blk * (BLOCK * r) + rho
            if r == 1:
                idx = pl.ds(pl.multiple_of(start, BLOCK), BLOCK)
            else:
                idx = pl.ds(start, BLOCK, stride=r)
            o_scr[g, idx, :] = o
            l_scr[g, idx, :] = jnp.broadcast_to(lse, (BLOCK, HEAD_DIM))

        for rho in range(r):
            lanes = slice(rho * HEAD_DIM, (rho + 1) * HEAD_DIM)
            o, lse = _band_block(q_ref[0, 0, :BLOCK, lanes], k_ref[0, 0, :BLOCK, lanes],
                                 v_ref[0, 0, :BLOCK, lanes], None, None)
            store(rho, 0, o, lse)

            def body(blk, carry, lanes=lanes, rho=rho, q_ref=q_ref, k_ref=k_ref, v_ref=v_ref,
                     store=store):
                cur = pl.ds(pl.multiple_of(blk * BLOCK, BLOCK), BLOCK)
                prev = pl.ds(pl.multiple_of((blk - 1) * BLOCK, BLOCK), BLOCK)
                o, lse = _band_block(q_ref[0, 0, cur, lanes], k_ref[0, 0, cur, lanes],
                                     v_ref[0, 0, cur, lanes], k_ref[0, 0, prev, lanes],
                                     v_ref[0, 0, prev, lanes])
                store(rho, blk, o, lse)
                return carry
            if nblk > 1:
                lax.fori_loop(1, nblk, body, 0)

    def combine(c, carry):
        rows = pl.ds(pl.multiple_of(c * BLOCK, BLOCK), BLOCK)
        ls = [l_scr[g, rows, :] for g in range(N_GROUPS)]
        m = jnp.maximum(jnp.maximum(ls[0], ls[1]), ls[2])
        es = [jnp.exp(l - m) for l in ls]
        num = es[0] * o_scr[0, rows, :] + es[1] * o_scr[1, rows, :] + es[2] * o_scr[2, rows, :]
        out_ref[0, rows, :] = (num / (es[0] + es[1] + es[2])).astype(out_ref.dtype)
        return carry
    lax.fori_loop(0, seq // BLOCK, combine, 0)


def _attn_prompt(qkv, n_seq, seq):
    in_specs, args = [], []
    for g, (_, r) in enumerate(DIL_GROUPS):
        view = qkv.reshape(qkv.shape[0], n_seq, seq // r, r * HEAD_DIM)
        for role in range(3):
            base = role * N_GROUPS * N_HEADS + g * N_HEADS
            in_specs.append(pl.BlockSpec((1, 1, seq // r, r * HEAD_DIM),
                                         lambda b, h, base=base: (base + h, b, 0, 0)))
            args.append(view)
    return pl.pallas_call(
        functools.partial(_attn_prompt_kernel, seq=seq),
        grid=(n_seq, N_HEADS),
        in_specs=in_specs,
        out_specs=pl.BlockSpec((1, seq, HEAD_DIM), lambda b, h: (b, 0, h)),
        out_shape=jax.ShapeDtypeStruct((n_seq, seq, ATT_WIDTH), BF16),
        scratch_shapes=[pltpu.VMEM((N_GROUPS, seq, HEAD_DIM), F32),
                        pltpu.VMEM((N_GROUPS, seq, HEAD_DIM), F32)],
        compiler_params=_params("arbitrary", "arbitrary"),
        name="attn_prompt",
    )(*args)


def _attn_sample_kernel(q_ref, kvn_ref, win_ref, o_ref, lse_ref, new_ref, *, r, t_new):
    lb = win_ref.shape[1]
    tq = q_ref.shape[1]
    for c in range(2 * ATT_WIDTH // HEAD_DIM):
        sl = slice(c * HEAD_DIM, (c + 1) * HEAD_DIM)
        new_ref[0, :lb - t_new, sl] = win_ref[0, t_new:, sl]
    new_ref[0, lb - t_new:, :] = kvn_ref[0]
    t_i = lax.broadcasted_iota(jnp.int32, (tq, lb), 0)
    k_i = lax.broadcasted_iota(jnp.int32, (tq, lb), 1)
    dist = lb + t_i - k_i
    ok_buf = (dist % r == 0) & (dist <= r * DIL_SPAN)
    t_n = lax.broadcasted_iota(jnp.int32, (tq, BLOCK), 0)
    k_n = lax.broadcasted_iota(jnp.int32, (tq, BLOCK), 1) - (BLOCK - t_new)
    dn = t_n - k_n
    ok_new = (k_n >= 0) & (dn >= 0) & (dn % r == 0) & (dn <= r * DIL_SPAN)
    tail = pl.ds(lb - BLOCK, BLOCK)
    for hd in range(N_HEADS):
        sl = slice(hd * HEAD_DIM, (hd + 1) * HEAD_DIM)
        slv = slice(ATT_WIDTH + hd * HEAD_DIM, ATT_WIDTH + (hd + 1) * HEAD_DIM)
        q = q_ref[0, :, sl]
        sb = jnp.where(ok_buf, _dot_nt(q, win_ref[0, :, sl].astype(BF16)) * SCALE, NEG)
        sn = jnp.where(ok_new, _dot_nt(q, new_ref[0, tail, sl].astype(BF16)) * SCALE, NEG)
        m = jnp.maximum(jnp.max(sb, axis=-1, keepdims=True), jnp.max(sn, axis=-1, keepdims=True))
        pb = jnp.exp(sb - m)
        pn = jnp.exp(sn - m)
        l = jnp.sum(pb, axis=-1, keepdims=True) + jnp.sum(pn, axis=-1, keepdims=True)
        o = _dot(pb.astype(BF16), win_ref[0, :, slv].astype(BF16))
        o = o + _dot(pn.astype(BF16), new_ref[0, tail, slv].astype(BF16))
        o_ref[0, :, sl] = o / l
        lse_ref[0, :, sl] = jnp.broadcast_to(m + jnp.log(l), (tq, HEAD_DIM))


def _attn_sample(q, kv_new, win, r):
    nb, tq, _ = q.shape
    t_new = kv_new.shape[1]
    lb = win.shape[1]
    return pl.pallas_call(
        functools.partial(_attn_sample_kernel, r=r, t_new=t_new),
        grid=(nb,),
        in_specs=[pl.BlockSpec((1, tq, ATT_WIDTH), lambda b: (b, 0, 0)),
                  pl.BlockSpec((1, t_new, 2 * ATT_WIDTH), lambda b: (b, 0, 0)),
                  pl.BlockSpec((1, lb, 2 * ATT_WIDTH), lambda b: (b, 0, 0))],
        out_specs=[pl.BlockSpec((1, tq, ATT_WIDTH), lambda b: (b, 0, 0)),
                   pl.BlockSpec((1, tq, ATT_WIDTH), lambda b: (b, 0, 0)),
                   pl.BlockSpec((1, lb, 2 * ATT_WIDTH), lambda b: (b, 0, 0))],
        out_shape=[jax.ShapeDtypeStruct((nb, tq, ATT_WIDTH), F32),
                   jax.ShapeDtypeStruct((nb, tq, ATT_WIDTH), F32),
                   jax.ShapeDtypeStruct((nb, lb, 2 * ATT_WIDTH), F32)],
        compiler_params=_params("arbitrary"),
        name=f"attn_sample_r{r}",
    )(q, kv_new, win)


def _combine_kernel(o0, o1, o2, l0, l1, l2, out_ref):
    ls = [l0[...], l1[...], l2[...]]
    m = jnp.maximum(jnp.maximum(ls[0], ls[1]), ls[2])
    es = [jnp.exp(l - m) for l in ls]
    num = es[0] * o0[...] + es[1] * o1[...] + es[2] * o2[...]
    out_ref[...] = (num / (es[0] + es[1] + es[2])).astype(out_ref.dtype)


def _combine_groups(outs, lses):
    shape = outs[0].shape
    return pl.pallas_call(
        _combine_kernel,
        in_specs=[_full(shape)] * 6,
        out_specs=_full(shape),
        out_shape=jax.ShapeDtypeStruct(shape, BF16),
        grid=(1,),
        compiler_params=_params("arbitrary"),
        name="combine_groups",
    )(*outs, *lses)


FF_CHUNK = 256
CARRY_ROWS = 8


def _out_ffn_kernel(*refs, seq_len, tiles_per_seq, tail_rows):
    if seq_len is None:
        (x_ref, mix_ref, cr_ref, wo_ref, g_ref, wup_ref, cw_ref, cb_ref, wdn_ref,
         out_ref, tail_ref, carry) = refs
    else:
        (x_ref, mix_ref, cr_ref, wo_ref, g_ref, wup_ref, cw_ref, cb_ref, wdn_ref, e1_ref, e2_ref,
         out_ref, tail_ref) = refs
    tm = x_ref.shape[0]
    d_ff = wdn_ref.shape[0]
    x1 = (x_ref[...] + _dot(mix_ref[...], wo_ref[:ATT_WIDTH, :])
          + _dot(cr_ref[...], wo_ref[ATT_WIDTH:, :]))
    h = _rms(x1, g_ref[...]).astype(BF16)
    row = lax.broadcasted_iota(jnp.int32, (tm, 1), 0)
    if seq_len is None:
        @pl.when(pl.program_id(0) % tiles_per_seq == 0)
        def _():
            carry[...] = jnp.zeros_like(carry)
        t = row
    else:
        t = row % seq_len
    has1 = t >= 1
    has2 = t >= 2

    def conv(cols):
        up = _dot(h, wup_ref[:, cols])
        if seq_len is None:
            last = carry[CARRY_ROWS - 1:CARRY_ROWS, cols]
            e1 = last
            e2 = jnp.where(row == 0, carry[CARRY_ROWS - 2:CARRY_ROWS - 1, cols], last)
            carry[:, cols] = up[tm - CARRY_ROWS:, :]
        else:
            e1 = e1_ref[:, cols]
            e2 = e2_ref[:, cols]
        tail_ref[0, :, cols] = up[tm - tail_rows:, :]
        prev1 = jnp.where(has1, pltpu.roll(up, 1, 0), e1)
        prev2 = jnp.where(has2, pltpu.roll(up, 2, 0), e2)
        return (cb_ref[:, cols] + cw_ref[0:1, cols] * prev2 + cw_ref[1:2, cols] * prev1
                + cw_ref[2:3, cols] * up)

    acc = jnp.zeros_like(x1)
    for j in range(d_ff // FF_CHUNK):
        a = conv(slice(j * FF_CHUNK, (j + 1) * FF_CHUNK))
        b = conv(slice(d_ff + j * FF_CHUNK, d_ff + (j + 1) * FF_CHUNK))
        act = (a * jax.nn.sigmoid(a) * b).astype(BF16)
        acc = acc + _dot(act, wdn_ref[j * FF_CHUNK:(j + 1) * FF_CHUNK, :])
    out_ref[...] = x1 + acc


def _out_ffn(x2d, mix, cross, w_out_bf, g_ffn, w_up_bf, conv_w, conv_b, w_down_bf, tm,
             tiles_per_seq=None, seq_len=None, e1=None, e2=None, tail_rows=8):
    rows, d = x2d.shape
    d_ff = w_down_bf.shape[0]
    n_tiles = rows // tm
    in_specs = [
        pl.BlockSpec((tm, d), lambda i: (i, 0)),
        pl.BlockSpec((tm, ATT_WIDTH), lambda i: (i, 0)),
        pl.BlockSpec((tm, ATT_WIDTH), lambda i: (i, 0)),
        _full(w_out_bf.shape), _full((1, d)), _full(w_up_bf.shape),
        _full((CONV_W, 2 * d_ff)), _full((1, 2 * d_ff)), _full(w_down_bf.shape),
    ]
    args = [x2d, mix, cross, w_out_bf, g_ffn.reshape(1, d), w_up_bf, conv_w, conv_b.reshape(1, 2 * d_ff),
            w_down_bf]
    scratch = []
    if seq_len is None:
        scratch = [pltpu.VMEM((CARRY_ROWS, 2 * d_ff), F32)]
    else:
        in_specs += [pl.BlockSpec((tm, 2 * d_ff), lambda i: (i, 0))] * 2
        args += [e1, e2]
    return pl.pallas_call(
        functools.partial(_out_ffn_kernel, seq_len=seq_len, tiles_per_seq=tiles_per_seq,
                          tail_rows=tail_rows),
        grid=(n_tiles,),
        in_specs=in_specs,
        out_specs=[pl.BlockSpec((tm, d), lambda i: (i, 0)),
                   pl.BlockSpec((1, tail_rows, 2 * d_ff), lambda i: (i, 0, 0))],
        out_shape=[jax.ShapeDtypeStruct((rows, d), F32),
                   jax.ShapeDtypeStruct((n_tiles, tail_rows, 2 * d_ff), F32)],
        scratch_shapes=scratch,
        compiler_params=_params("arbitrary"),
        name="out_ffn",
    )(*args)


def _swap_halves(x):
    return pltpu.roll(x, S5_STATE, 1)


def _s5_prep_kernel(lam_ref, logdt_ref, bt_ref, c_ref, dmat_ref, e_ref, cl_ref, k_ref, lam_out_ref, *, sub):
    n_groups = e_ref.shape[0]
    lane = lax.broadcasted_iota(jnp.int32, lam_ref.shape, 1)
    first = lane < S5_STATE
    sign = jnp.where(first, -1.0, 1.0)
    a = lam_ref[...]
    a_sw = _swap_halves(a)
    are = jnp.where(first, a, a_sw)
    aim = jnp.where(first, a_sw, a)
    dt = jnp.exp(logdt_ref[...])
    mag = jnp.exp(are * dt)
    lr = mag * jnp.cos(aim * dt)
    li = mag * jnp.sin(aim * dt)
    den = are * are + aim * aim
    xr = lr - 1.0
    f_re = (xr * are + li * aim) / den
    f_im = (li * are - xr * aim) / den
    lb = sign * li

    def cmul(x, m_re, m_sw):
        return x * m_re + _swap_halves(x) * m_sw

    c = c_ref[...]
    e = cmul(bt_ref[...], f_re, sign * f_im)
    cl = c
    pw = jnp.where(first, 1.0, 0.0)
    for k in range(sub):
        kk = sub - 1 - k
        e_ref[:, kk * S5_GROUP:(kk + 1) * S5_GROUP, :] = e.reshape(n_groups, S5_GROUP, 2 * S5_STATE)
        cl = cmul(cl, lr, lb)
        cl_ref[:, k * S5_GROUP:(k + 1) * S5_GROUP, :] = (cl * -sign).reshape(n_groups, S5_GROUP, 2 * S5_STATE)
        e = cmul(e, lr, lb)
        pw = cmul(pw, lr, lb)
    lam_out_ref[...] = pw
    c_neg = (c * -sign).reshape(n_groups, S5_GROUP, 2 * S5_STATE)
    k_ref[...] = lax.dot_general(c_neg, e_ref[...], (((2,), (2,)), ((0,), (0,))),
                                 precision=HIGHEST, preferred_element_type=F32) + dmat_ref[...]


def _s5_prep(lam_re, lam_im, log_dt, b_re, b_im, c_re, c_im, d_skip, sub):
    n_groups = lam_re.shape[0]
    gc = n_groups * S5_GROUP
    rep = lambda t: jnp.repeat(t, S5_GROUP, axis=0)
    lam_p = rep(jnp.concatenate([lam_re, lam_im], axis=-1))
    logdt = rep(log_dt.reshape(n_groups, 1))
    bt_p = jnp.concatenate([b_re.transpose(0, 2, 1), b_im.transpose(0, 2, 1)], axis=-1).reshape(gc, 2 * S5_STATE)
    c_p = jnp.concatenate([c_re, c_im], axis=-1).reshape(gc, 2 * S5_STATE)
    eye_c = jnp.eye(S5_GROUP, dtype=F32)
    dmat = jnp.pad(d_skip[:, :, None] * eye_c[None], ((0, 0), (0, 0), ((sub - 1) * S5_GROUP, 0)))
    e_all, cl_all, k_all, lam_sub = pl.pallas_call(
        functools.partial(_s5_prep_kernel, sub=sub),
        grid=(1,),
        in_specs=[_full(lam_p.shape), _full(logdt.shape), _full(bt_p.shape), _full(c_p.shape),
                  _full(dmat.shape)],
        out_specs=[_full((n_groups, sub * S5_GROUP, 2 * S5_STATE)),
                   _full((n_groups, sub * S5_GROUP, 2 * S5_STATE)),
                   _full((n_groups, S5_GROUP, sub * S5_GROUP)),
                   _full((gc, 2 * S5_STATE))],
        out_shape=[jax.ShapeDtypeStruct((n_groups, sub * S5_GROUP, 2 * S5_STATE), F32),
                   jax.ShapeDtypeStruct((n_groups, sub * S5_GROUP, 2 * S5_STATE), F32),
                   jax.ShapeDtypeStruct((n_groups, S5_GROUP, sub * S5_GROUP), F32),
                   jax.ShapeDtypeStruct((gc, 2 * S5_STATE), F32)],
        compiler_params=_params("arbitrary"),
        name="s5_prep",
    )(lam_p, logdt, bt_p, c_p, dmat)
    gh = n_groups // 2
    eye_g = jnp.eye(gh, dtype=F32)
    kx = k_all.reshape(2, gh, S5_GROUP, sub, S5_GROUP).transpose(0, 3, 1, 4, 2)
    toep = (kx[:, :, :, :, None, :] * eye_g[None, None, :, None, :, None]).reshape(2, sub * S5_HALF, S5_HALF)
    ex = e_all.reshape(2, gh, sub, S5_GROUP, 2 * S5_STATE).transpose(0, 2, 1, 3, 4)
    b_all = (ex[:, :, :, :, None, :] * eye_g[None, None, :, None, :, None]).reshape(
        2, sub * S5_HALF, gh * 2 * S5_STATE)
    cx = cl_all.reshape(2, gh, sub, S5_GROUP, 2 * S5_STATE).transpose(0, 4, 2, 1, 3)
    c_all = (eye_g[None, :, None, None, :, None] * cx[:, None]).reshape(2, gh * 2 * S5_STATE, sub * S5_HALF)
    lam_sub = lam_sub.reshape(2, gh, S5_GROUP, 2 * S5_STATE)[:, :, 0]
    return toep.astype(BF16), b_all.astype(BF16), c_all.astype(BF16), lam_sub


def _s5_toep_kernel(v_ref, w_ref, y_ref, *, sub):
    for tau in range(sub):
        y_ref[0, :, tau * S5_HALF:(tau + 1) * S5_HALF] = _dot(
            v_ref[0, :, :(tau + 1) * S5_HALF], w_ref[0, (sub - 1 - tau) * S5_HALF:, :])


def _matmul_kernel(a_ref, b_ref, o_ref):
    o_ref[0] = _dot(a_ref[0].astype(BF16), b_ref[0])


def _matmul_add_kernel(a_ref, b_ref, y_ref, o_ref):
    o_ref[0] = y_ref[0] + _dot(a_ref[0].astype(BF16), b_ref[0])


def _s5_scan_kernel(s_ref, lam_ref, x0_ref, xprev_ref, xfin_ref, *, n_seq, n_sub):
    lane = lax.broadcasted_iota(jnp.int32, lam_ref.shape[1:], 1)
    first = lane < S5_STATE
    lam = lam_ref[0]
    lam_sw = pltpu.roll(lam, S5_STATE, 1)
    m_re = jnp.where(first, lam, lam_sw)
    m_sw = jnp.where(first, -lam_sw, lam)
    for b in range(n_seq):
        def body(n, x, b=b):
            row = b * n_sub + n
            xprev_ref[0, row] = x
            return x * m_re + pltpu.roll(x, S5_STATE, 1) * m_sw + s_ref[0, row]
        xfin_ref[0, b] = lax.fori_loop(0, n_sub, body, x0_ref[0, b])


def _glu_kernel(y_ref, w_ref, b_ref, o_ref):
    y = jax.nn.gelu(jnp.concatenate([y_ref[0], y_ref[1]], axis=-1))
    z = _dot(y.astype(BF16), w_ref[...]) + b_ref[...]
    o_ref[...] = (y * jax.nn.sigmoid(z)).astype(o_ref.dtype)


def _s5_mixer(u2, n_seq, seq, sub, x0, toep, b_all, c_all, lam_sub, w_glu_bf, b_glu):
    rows = u2.shape[1]
    n_sub = seq // sub
    m = rows // sub
    width = sub * S5_HALF
    n_state = b_all.shape[2]
    gh = n_state // (2 * S5_STATE)
    v = u2.reshape(2, m, width)
    tms = min(m, 256)
    y_intra = pl.pallas_call(
        functools.partial(_s5_toep_kernel, sub=sub),
        grid=(2, m // tms),
        in_specs=[pl.BlockSpec((1, tms, width), lambda h, i: (h, i, 0)),
                  pl.BlockSpec((1, width, S5_HALF), lambda h, i: (h, 0, 0))],
        out_specs=pl.BlockSpec((1, tms, width), lambda h, i: (h, i, 0)),
        out_shape=jax.ShapeDtypeStruct((2, m, width), F32),
        compiler_params=_params("arbitrary", "arbitrary"),
        name="s5_toeplitz",
    )(v, toep)
    nc = 512
    s_loc = pl.pallas_call(
        _matmul_kernel,
        grid=(2, n_state // nc, m // tms),
        in_specs=[pl.BlockSpec((1, tms, width), lambda h, j, i: (h, i, 0)),
                  pl.BlockSpec((1, width, nc), lambda h, j, i: (h, 0, j))],
        out_specs=pl.BlockSpec((1, tms, nc), lambda h, j, i: (h, i, j)),
        out_shape=jax.ShapeDtypeStruct((2, m, n_state), F32),
        compiler_params=_params("arbitrary", "arbitrary", "arbitrary"),
        name="s5_state_in",
    )(v, b_all)
    st_shape = (2, m, gh, 2 * S5_STATE)
    x_prev, x_fin = pl.pallas_call(
        functools.partial(_s5_scan_kernel, n_seq=n_seq, n_sub=n_sub),
        grid=(2,),
        in_specs=[pl.BlockSpec((1,) + st_shape[1:], lambda h: (h, 0, 0, 0)),
                  pl.BlockSpec((1, gh, 2 * S5_STATE), lambda h: (h, 0, 0)),
                  pl.BlockSpec((1, n_seq, gh, 2 * S5_STATE), lambda h: (h, 0, 0, 0))],
        out_specs=[pl.BlockSpec((1,) + st_shape[1:], lambda h: (h, 0, 0, 0)),
                   pl.BlockSpec((1, n_seq, gh, 2 * S5_STATE), lambda h: (h, 0, 0, 0))],
        out_shape=[jax.ShapeDtypeStruct(st_shape, F32),
                   jax.ShapeDtypeStruct((2, n_seq, gh, 2 * S5_STATE), F32)],
        compiler_params=_params("arbitrary"),
        name="s5_scan",
    )(s_loc.reshape(st_shape), lam_sub, x0)
    wc = min(width, 512)
    y = pl.pallas_call(
        _matmul_add_kernel,
        grid=(2, width // wc, m // tms),
        in_specs=[pl.BlockSpec((1, tms, n_state), lambda h, j, i: (h, i, 0)),
                  pl.BlockSpec((1, n_state, wc), lambda h, j, i: (h, 0, j)),
                  pl.BlockSpec((1, tms, wc), lambda h, j, i: (h, i, j))],
        out_specs=pl.BlockSpec((1, tms, wc), lambda h, j, i: (h, i, j)),
        out_shape=jax.ShapeDtypeStruct((2, m, width), F32),
        compiler_params=_params("arbitrary", "arbitrary", "arbitrary"),
        name="s5_state_out",
    )(x_prev.reshape(2, m, n_state), c_all, y_intra)
    tm = min(rows, 512)
    mix = pl.pallas_call(
        _glu_kernel,
        grid=(rows // tm,),
        in_specs=[pl.BlockSpec((2, tm, S5_HALF), lambda i: (0, i, 0)),
                  _full(w_glu_bf.shape), _full((1, ATT_WIDTH))],
        out_specs=pl.BlockSpec((tm, ATT_WIDTH), lambda i: (i, 0)),
        out_shape=jax.ShapeDtypeStruct((rows, ATT_WIDTH), BF16),
        compiler_params=_params("arbitrary"),
        name="s5_glu",
    )(y.reshape(2, rows, S5_HALF), w_glu_bf, b_glu.reshape(1, ATT_WIDTH))
    return mix, x_fin


def _pack_state(s):
    n = s.shape[0]
    gh = s.shape[2] // 2
    return s.reshape(n, 2, 2, gh, S5_STATE).transpose(2, 0, 3, 1, 4).reshape(2, n, gh, 2 * S5_STATE)


def _unpack_state(x):
    _, n, gh, _ = x.shape
    return x.reshape(2, n, gh, 2, S5_STATE).transpose(1, 3, 0, 2, 4).reshape(n, 2, 2 * gh, S5_STATE)


PROMPT_TILE = 512
PROMPT_SUB = 16
Q_PAD = 16


def _pad_rows(t, n):
    return jnp.pad(t, ((0, 0), (0, n - t.shape[1]), (0, 0)))


def kernel(x_prompt, x_sample, cache_win0_kv, cache_win1_kv, cache_win2_kv, cache_mem_kv, state_s5,
           state_ffn_conv, mem_prompt, g_mix, g_ffn, w_in_a, g_q_dil, g_k_dil, w_in_b, s5_lam_re,
           s5_lam_im, s5_log_dt, s5_b_re, s5_b_im, s5_c_re, s5_c_im, s5_d, w_glu, b_glu, g_mem,
           w_mem_kv, g_q_cross, g_k_cross, w_out, w_up, conv_w, conv_b, w_down):
    nb, seq, d = x_prompt.shape
    db, ts, _ = x_sample.shape
    depth = g_mix.shape[0]
    n_mem = mem_prompt.shape[1]
    d_ff2 = w_up.shape[2]
    assert ts >= CONV_W - 1 and seq % PROMPT_TILE == 0 and ts <= Q_PAD
    caches = (cache_win0_kv, cache_win1_kv, cache_win2_kv)
    w_in_a_bf, w_in_b_bf, w_out_bf = w_in_a.astype(BF16), w_in_b.astype(BF16), w_out.astype(BF16)
    w_up_bf, w_down_bf, w_glu_bf = w_up.astype(BF16), w_down.astype(BF16), w_glu.astype(BF16)
    w_mem_bf = w_mem_kv.astype(BF16)

    tab_p = _rope_tables(jnp.arange(seq, dtype=jnp.int32))
    tab_s = tuple(jnp.tile(t, (db, 1)) for t in _rope_tables(PAST_LEN + jnp.arange(ts, dtype=jnp.int32)))
    win_keep = tuple(min(w, seq) for w, _ in DIL_GROUPS)
    rows_s = db * ts

    mem_p = _mem_kv(mem_prompt.reshape(nb * n_mem, d), g_mem, w_mem_bf, g_k_cross)

    xp = x_prompt.reshape(nb * seq, d)
    xs = x_sample.reshape(rows_s, d)
    p_win, s_win = [[] for _ in DIL_GROUPS], [[] for _ in DIL_GROUPS]
    p_s5, s_s5, p_conv, s_conv = [], [], [], []
    tiles_per_seq = seq // PROMPT_TILE
    for i in range(depth):
        mem_i = mem_p[i].reshape(nb, n_mem, 2 * ATT_WIDTH)
        mem_s = cache_mem_kv[i].reshape(db, n_mem, 2 * ATT_WIDTH)
        if i % 2 == 0:
            ia = i // 2
            qkv, w0, w1, w2, cross_p = _in_proj_a(xp, seq, PROMPT_TILE, g_mix[i], w_in_a_bf[ia], tab_p,
                                                  g_q_dil[ia], g_k_dil[ia], g_q_cross[i], mem_i, win_keep)
            for g, w in enumerate((w0, w1, w2)):
                p_win[g].append(w.reshape(nb, win_keep[g], 2, N_HEADS, HEAD_DIM))
            mix_p = _attn_prompt(qkv, nb, seq).reshape(nb * seq, ATT_WIDTH)
            qkv_s, k0, k1, k2, qc_s = _in_proj_a(xs, rows_s, rows_s, g_mix[i], w_in_a_bf[ia], tab_s,
                                                 g_q_dil[ia], g_k_dil[ia], None, None, (rows_s,) * N_GROUPS)
            outs, lses = [], []
            for g, ((_, r), kv_new) in enumerate(zip(DIL_GROUPS, (k0, k1, k2))):
                q_g = qkv_s[g * N_HEADS:(g + 1) * N_HEADS]
                q_g = _pad_rows(q_g.transpose(1, 0, 2).reshape(db, ts, ATT_WIDTH), Q_PAD)
                win = caches[g][ia]
                lb = win.shape[1]
                o_g, lse_g, new_g = _attn_sample(q_g, kv_new.reshape(db, ts, 2 * ATT_WIDTH),
                                                 win.reshape(db, lb, 2 * ATT_WIDTH), r)
                outs.append(o_g[:, :ts].reshape(rows_s, ATT_WIDTH))
                lses.append(lse_g[:, :ts].reshape(rows_s, ATT_WIDTH))
                s_win[g].append(new_g.reshape(win.shape))
            mix_s = _combine_groups(outs, lses)
        else:
            ib = i // 2
            prm = (s5_lam_re[ib], s5_lam_im[ib], s5_log_dt[ib], s5_b_re[ib], s5_b_im[ib], s5_c_re[ib],
                   s5_c_im[ib], s5_d[ib])
            gh = s5_lam_re.shape[1] // 2
            u_p, cross_p = _in_proj_b(xp, seq, PROMPT_TILE, g_mix[i], w_in_b_bf[ib], g_q_cross[i], mem_i)
            x0 = jnp.zeros((2, nb, gh, 2 * S5_STATE), F32)
            mix_p, fin_p = _s5_mixer(u_p, nb, seq, PROMPT_SUB, x0, *_s5_prep(*prm, PROMPT_SUB),
                                     w_glu_bf[ib], b_glu[ib])
            p_s5.append(_unpack_state(fin_p))
            u_s, qc_s = _in_proj_b(xs, rows_s, rows_s, g_mix[i], w_in_b_bf[ib], None, None)
            mix_s, fin_s = _s5_mixer(u_s, db, ts, ts, _pack_state(state_s5[ib]), *_s5_prep(*prm, ts),
                                     w_glu_bf[ib], b_glu[ib])
            s_s5.append(_unpack_state(fin_s))
        cross_s = _cross_sample(_pad_rows(qc_s.reshape(db, ts, ATT_WIDTH), Q_PAD), g_q_cross[i], mem_s)
        cross_s = cross_s[:, :ts].reshape(rows_s, ATT_WIDTH)

        ffn_w = (w_out_bf[i], g_ffn[i], w_up_bf[i], conv_w[i], conv_b[i], w_down_bf[i])
        xp, tails = _out_ffn(xp, mix_p, cross_p, *ffn_w, PROMPT_TILE, tiles_per_seq=tiles_per_seq)
        p_conv.append(tails.reshape(nb, tiles_per_seq, CARRY_ROWS, d_ff2)[:, -1, CARRY_ROWS - (CONV_W - 1):])
        buf = state_ffn_conv[i]
        zero = jnp.zeros((db, ts - 2, d_ff2), F32)
        e1 = jnp.concatenate([buf[:, 1:2], zero, zero[:, :1]], axis=1).reshape(rows_s, d_ff2)
        e2 = jnp.concatenate([buf, zero], axis=1).reshape(rows_s, d_ff2)
        xs, tails = _out_ffn(xs, mix_s, cross_s, *ffn_w, rows_s, seq_len=ts, e1=e1, e2=e2, tail_rows=rows_s)
        s_conv.append(tails.reshape(db, ts, d_ff2)[:, ts - (CONV_W - 1):])

    kv_tail = (2, N_HEADS, HEAD_DIM)
    return (xp.reshape(nb, seq, d), xs.reshape(db, ts, d),
            jnp.stack(p_win[0]), jnp.stack(p_win[1]), jnp.stack(p_win[2]),
            mem_p.reshape((depth, nb, n_mem) + kv_tail),
            jnp.stack(p_s5), jnp.stack(p_conv),
            jnp.stack(s_win[0]), jnp.stack(s_win[1]), jnp.stack(s_win[2]),
            jnp.stack(s_s5), jnp.stack(s_conv))
```

```python
import functools
import math

import jax
import jax.numpy as jnp
from jax import lax
from jax.experimental import pallas as pl
from jax.experimental.pallas import tpu as pltpu

HEAD_DIM = 128
N_HEADS = 4
DIL_GROUPS = ((128, 1), (512, 4), (2048, 16))
N_GROUPS = len(DIL_GROUPS)
DIL_SPAN = 128
BLOCK = 128
ATT_WIDTH = N_HEADS * HEAD_DIM
KV_ROWS = 2 * N_HEADS
ROT_DIM = HEAD_DIM // 4
ROT_HALF = ROT_DIM // 2
ROPE_THETA = 500000.0
S5_GROUP = 16
S5_STATE = 64
S5_HALF = 256
CONV_W = 3
EPS = 1e-6
NEG = -1e30
SCALE = HEAD_DIM ** -0.5
PAST_LEN = 16384

VMEM_LIMIT_V7X = 56 * 1024 * 1024
BF16 = jnp.bfloat16
F32 = jnp.float32
HIGHEST = lax.Precision.HIGHEST


def _params(*sem):
    return pltpu.CompilerParams(dimension_semantics=sem, vmem_limit_bytes=VMEM_LIMIT_V7X)


def _rms(x, g):
    return x * lax.rsqrt(jnp.mean(x * x, axis=-1, keepdims=True) + EPS) * g


def _dot(a, b):
    return jnp.dot(a, b, preferred_element_type=F32)


def _dot_nt(a, b, precision=None):
    return lax.dot_general(a, b, (((1,), (1,)), ((), ())), precision=precision,
                           preferred_element_type=F32)


def _full(shape):
    nd = len(shape)
    return pl.BlockSpec(shape, lambda *_: (0,) * nd)


def _kv_rows(kv, head, n):
    return pl.ds(kv * N_HEADS + head, n, stride=KV_ROWS)


def _mem_kv_kernel(mem_ref, g_ref, w_ref, gk_ref, out_ref):
    tm = mem_ref.shape[0]
    h = _rms(mem_ref[...], g_ref[0]).astype(BF16)
    kv = _dot(h, w_ref[0])
    for hd in range(N_HEADS):
        sl = slice(hd * HEAD_DIM, (hd + 1) * HEAD_DIM)
        out_ref[0, _kv_rows(0, hd, tm), :] = _rms(kv[:, sl], gk_ref[0])
        out_ref[0, _kv_rows(1, hd, tm), :] = kv[:, ATT_WIDTH + hd * HEAD_DIM:ATT_WIDTH + (hd + 1) * HEAD_DIM]


def _mem_kv(mem2d, g_mem, w_kv_bf, g_k):
    depth, d, _ = w_kv_bf.shape
    rows = mem2d.shape[0]
    tm = min(rows, 512)
    return pl.pallas_call(
        _mem_kv_kernel,
        grid=(depth, rows // tm),
        in_specs=[
            pl.BlockSpec((tm, d), lambda l, i: (i, 0)),
            pl.BlockSpec((1, 1, d), lambda l, i: (l, 0, 0)),
            pl.BlockSpec((1, d, 2 * ATT_WIDTH), lambda l, i: (l, 0, 0)),
            pl.BlockSpec((1, 1, HEAD_DIM), lambda l, i: (l, 0, 0)),
        ],
        out_specs=pl.BlockSpec((1, tm * KV_ROWS, HEAD_DIM), lambda l, i: (l, i, 0)),
        out_shape=jax.ShapeDtypeStruct((depth, rows * KV_ROWS, HEAD_DIM), F32),
        compiler_params=_params("arbitrary", "arbitrary"),
        name="mem_kv",
    )(mem2d, g_mem.reshape(depth, 1, d), w_kv_bf, g_k.reshape(depth, 1, HEAD_DIM))


def _cross_heads(qc, mem_ref, gq):
    n_mem = mem_ref.shape[1] // KV_ROWS
    outs = []
    for hd in range(N_HEADS):
        q = _rms(qc[:, hd * HEAD_DIM:(hd + 1) * HEAD_DIM], gq).astype(BF16)
        k = mem_ref[0, _kv_rows(0, hd, n_mem), :].astype(BF16)
        v = mem_ref[0, _kv_rows(1, hd, n_mem), :].astype(BF16)
        s = _dot_nt(q, k) * SCALE
        m = jnp.max(s, axis=-1, keepdims=True)
        p = jnp.exp(s - m)
        l = jnp.sum(p, axis=-1, keepdims=True)
        outs.append(_dot(p.astype(BF16), v) / l)
    return outs


def _rope(x, cos_t, sin_lo, sin_hi):
    return (x * cos_t + pltpu.roll(x, HEAD_DIM - ROT_HALF, 1) * sin_lo
            + pltpu.roll(x, ROT_HALF, 1) * sin_hi)


def _in_proj_a_kernel(*refs, fuse_cross, dilated, win_rows):
    n_in = 10 if fuse_cross else 8
    x_ref, g_ref, w_ref, cos_ref, slo_ref, shi_ref, gq_ref, gk_ref = refs[:8]
    n_qkv = N_GROUPS if dilated else 1
    qkv_refs = refs[n_in:n_in + n_qkv]
    win_refs = refs[n_in + n_qkv:n_in + n_qkv + N_GROUPS]
    cr_ref = refs[n_in + n_qkv + N_GROUPS]
    stage = refs[n_in + n_qkv + N_GROUPS + 1] if dilated else None
    tm = x_ref.shape[0]
    h = _rms(x_ref[...], g_ref[...]).astype(BF16)
    cos_t, sin_lo, sin_hi = cos_ref[...], slo_ref[...], shi_ref[...]
    for role in range(3):
        for g, (_, r) in enumerate(DIL_GROUPS):
            c = role * N_GROUPS + g
            y = _dot(h, w_ref[:, c * ATT_WIDTH:(c + 1) * ATT_WIDTH])
            wr = win_rows[g]
            for hd in range(N_HEADS):
                yh = y[:, hd * HEAD_DIM:(hd + 1) * HEAD_DIM]
                if role == 0:
                    yh = _rope(_rms(yh, gq_ref[g]), cos_t, sin_lo, sin_hi)
                elif role == 1:
                    yh = _rope(_rms(yh, gk_ref[g]), cos_t, sin_lo, sin_hi)
                if role > 0:
                    win_refs[g][_kv_rows(role - 1, hd, wr), :] = yh[tm - wr:, :]
                if not dilated:
                    qkv_refs[0][role * N_GROUPS * N_HEADS + g * N_HEADS + hd] = yh.astype(BF16)
                elif r == 1:
                    qkv_refs[g][role * N_HEADS + hd, 0] = yh.astype(BF16)
                else:
                    stage[...] = yh
                    for rho in range(r):
                        qkv_refs[g][role * N_HEADS + hd, 0, :, rho * HEAD_DIM:(rho + 1) * HEAD_DIM] = (
                            stage[pl.ds(rho, tm // r, stride=r), :].astype(BF16))
    qc = _dot(h, w_ref[:, 3 * N_GROUPS * ATT_WIDTH:])
    if fuse_cross:
        outs = _cross_heads(qc, refs[9], refs[8][...])
        for hd in range(N_HEADS):
            cr_ref[:, hd * HEAD_DIM:(hd + 1) * HEAD_DIM] = outs[hd].astype(cr_ref.dtype)
    else:
        cr_ref[...] = qc


def _in_proj_b_kernel(*refs, fuse_cross, sub):
    if fuse_cross:
        x_ref, g_ref, w_ref, gqc_ref, mem_ref, u_ref, cr_ref, stage = refs
    else:
        x_ref, g_ref, w_ref, u_ref, cr_ref, stage = refs
    tm = x_ref.shape[0]
    h = _rms(x_ref[...], g_ref[...]).astype(BF16)
    u = _dot(h, w_ref[:, :ATT_WIDTH])
    for cb in range(ATT_WIDTH // HEAD_DIM):
        stage[cb] = u[:, cb * HEAD_DIM:(cb + 1) * HEAD_DIM]
        half, off = divmod(cb * HEAD_DIM, S5_HALF)
        for tau in range(sub):
            u_ref[half, :, tau * S5_HALF + off:tau * S5_HALF + off + HEAD_DIM] = stage[
                cb, pl.ds(tau, tm // sub, stride=sub), :].astype(BF16)
    qc = _dot(h, w_ref[:, ATT_WIDTH:])
    if fuse_cross:
        outs = _cross_heads(qc, mem_ref, gqc_ref[...])
        for hd in range(N_HEADS):
            cr_ref[:, hd * HEAD_DIM:(hd + 1) * HEAD_DIM] = outs[hd].astype(cr_ref.dtype)
    else:
        cr_ref[...] = qc


def _rope_tables(pos):
    inv = jnp.exp(-math.log(ROPE_THETA) * jnp.arange(ROT_HALF, dtype=F32) / ROT_HALF)
    ang = pos.astype(F32)[:, None] * inv[None, :]
    cos, sin = jnp.cos(ang), jnp.sin(ang)
    rows = pos.shape[0]
    ones = jnp.ones((rows, HEAD_DIM - ROT_DIM), F32)
    zeros = jnp.zeros((rows, HEAD_DIM - ROT_DIM), F32)
    z16 = jnp.zeros((rows, ROT_HALF), F32)
    cos_t = jnp.concatenate([cos, cos, ones], axis=1)
    sin_lo = jnp.concatenate([-sin, z16, zeros], axis=1)
    sin_hi = jnp.concatenate([z16, sin, zeros], axis=1)
    return cos_t, sin_lo, sin_hi


def _in_proj_a(x2d, seq, tm, g_mix, w_bf, tables, g_q, g_k, g_qc, mem_kv, win_keep, dilated):
    rows, d = x2d.shape
    n_in = w_bf.shape[1]
    tiles_per_seq = seq // tm
    n_seq = rows // seq
    fuse_cross = mem_kv is not None
    tab_tiles = tables[0].shape[0] // tm
    win_rows = tuple(min(k, tm) for k in win_keep)

    def win_spec(keep):
        wr = min(keep, tm)
        nblk = keep // wr
        first = tiles_per_seq - nblk

        def imap(i):
            b = i // tiles_per_seq
            t = i % tiles_per_seq
            return (b * nblk + jnp.maximum(t - first, 0), 0)
        return pl.BlockSpec((wr * KV_ROWS, HEAD_DIM), imap)

    in_specs = [
        pl.BlockSpec((tm, d), lambda i: (i, 0)),
        _full((1, d)),
        _full((d, n_in)),
        pl.BlockSpec((tm, HEAD_DIM), lambda i: (i % tab_tiles, 0)),
        pl.BlockSpec((tm, HEAD_DIM), lambda i: (i % tab_tiles, 0)),
        pl.BlockSpec((tm, HEAD_DIM), lambda i: (i % tab_tiles, 0)),
        _full((N_GROUPS, HEAD_DIM)),
        _full((N_GROUPS, HEAD_DIM)),
    ]
    args = [x2d, g_mix.reshape(1, d), w_bf, *tables, g_q, g_k]
    if fuse_cross:
        in_specs += [_full((1, HEAD_DIM)),
                     pl.BlockSpec((1,) + mem_kv.shape[1:], lambda i: (i // tiles_per_seq, 0, 0))]
        args += [g_qc.reshape(1, HEAD_DIM), mem_kv]
    if dilated:
        qkv_specs = [pl.BlockSpec((3 * N_HEADS, 1, tm // r, r * HEAD_DIM),
                                  lambda i: (0, i // tiles_per_seq, i % tiles_per_seq, 0))
                     for _, r in DIL_GROUPS]
        qkv_shapes = [jax.ShapeDtypeStruct((3 * N_HEADS, n_seq, seq // r, r * HEAD_DIM), BF16)
                      for _, r in DIL_GROUPS]
        scratch = [pltpu.VMEM((tm, HEAD_DIM), F32)]
    else:
        qkv_specs = [pl.BlockSpec((3 * N_GROUPS * N_HEADS, tm, HEAD_DIM), lambda i: (0, i, 0))]
        qkv_shapes = [jax.ShapeDtypeStruct((3 * N_GROUPS * N_HEADS, rows, HEAD_DIM), BF16)]
        scratch = []
    out_specs = [*qkv_specs, *[win_spec(k) for k in win_keep],
                 pl.BlockSpec((tm, ATT_WIDTH), lambda i: (i, 0))]
    out_shape = [*qkv_shapes,
                 *[jax.ShapeDtypeStruct((n_seq * k * KV_ROWS, HEAD_DIM), F32) for k in win_keep],
                 jax.ShapeDtypeStruct((rows, ATT_WIDTH), BF16 if fuse_cross else F32)]
    return pl.pallas_call(
        functools.partial(_in_proj_a_kernel, fuse_cross=fuse_cross, dilated=dilated, win_rows=win_rows),
        grid=(rows // tm,),
        in_specs=in_specs, out_specs=out_specs, out_shape=out_shape, scratch_shapes=scratch,
        compiler_params=_params("arbitrary"),
        name="in_proj_a",
    )(*args)


def _in_proj_b(x2d, seq, tm, sub, g_mix, w_bf, g_qc, mem_kv):
    rows, d = x2d.shape
    n_in = w_bf.shape[1]
    tiles_per_seq = seq // tm
    fuse_cross = mem_kv is not None
    in_specs = [pl.BlockSpec((tm, d), lambda i: (i, 0)), _full((1, d)), _full((d, n_in))]
    args = [x2d, g_mix.reshape(1, d), w_bf]
    if fuse_cross:
        in_specs += [_full((1, HEAD_DIM)),
                     pl.BlockSpec((1,) + mem_kv.shape[1:], lambda i: (i // tiles_per_seq, 0, 0))]
        args += [g_qc.reshape(1, HEAD_DIM), mem_kv]
    return pl.pallas_call(
        functools.partial(_in_proj_b_kernel, fuse_cross=fuse_cross, sub=sub),
        grid=(rows // tm,),
        in_specs=in_specs,
        out_specs=[pl.BlockSpec((2, tm // sub, sub * S5_HALF), lambda i: (0, i, 0)),
                   pl.BlockSpec((tm, ATT_WIDTH), lambda i: (i, 0))],
        out_shape=[jax.ShapeDtypeStruct((2, rows // sub, sub * S5_HALF), BF16),
                   jax.ShapeDtypeStruct((rows, ATT_WIDTH), BF16 if fuse_cross else F32)],
        scratch_shapes=[pltpu.VMEM((ATT_WIDTH // HEAD_DIM, tm, HEAD_DIM), F32)],
        compiler_params=_params("arbitrary"),
        name="in_proj_b",
    )(*args)


def _cross_sample_kernel(qc_ref, gq_ref, mem_ref, out_ref):
    outs = _cross_heads(qc_ref[0], mem_ref, gq_ref[...])
    for hd in range(N_HEADS):
        out_ref[0, :, hd * HEAD_DIM:(hd + 1) * HEAD_DIM] = outs[hd].astype(out_ref.dtype)


def _cross_sample(qc, g_qc, mem_kv):
    nb, tq, _ = qc.shape
    return pl.pallas_call(
        _cross_sample_kernel,
        grid=(nb,),
        in_specs=[pl.BlockSpec((1, tq, ATT_WIDTH), lambda b: (b, 0, 0)),
                  _full((1, HEAD_DIM)),
                  pl.BlockSpec((1,) + mem_kv.shape[1:], lambda b: (b, 0, 0))],
        out_specs=pl.BlockSpec((1, tq, ATT_WIDTH), lambda b: (b, 0, 0)),
        out_shape=jax.ShapeDtypeStruct((nb, tq, ATT_WIDTH), BF16),
        compiler_params=_params("arbitrary"),
        name="cross_sample",
    )(qc, g_qc.reshape(1, HEAD_DIM), mem_kv)


def _band_block(q, kc, vc, kp, vp):
    qi = lax.broadcasted_iota(jnp.int32, (BLOCK, BLOCK), 0)
    kj = lax.broadcasted_iota(jnp.int32, (BLOCK, BLOCK), 1)
    sc = jnp.where(kj <= qi, _dot_nt(q, kc) * SCALE, NEG)
    m = jnp.max(sc, axis=-1, keepdims=True)
    if kp is not None:
        sp = jnp.where(kj >= qi, _dot_nt(q, kp) * SCALE, NEG)
        m = jnp.maximum(m, jnp.max(sp, axis=-1, keepdims=True))
    pc = jnp.exp(sc - m)
    l = jnp.sum(pc, axis=-1, keepdims=True)
    o = _dot(pc.astype(BF16), vc)
    if kp is not None:
        pp = jnp.exp(sp - m)
        l = l + jnp.sum(pp, axis=-1, keepdims=True)
        o = o + _dot(pp.astype(BF16), vp)
    return o / l, m + jnp.log(l)


def _attn_prompt_kernel(q0, q1, q2, k0, k1, k2, v0, v1, v2, out_ref, o_scr, l_scr, *, seq):
    qkv_refs = ((q0, k0, v0), (q1, k1, v1), (q2, k2, v2))
    for g, (_, r) in enumerate(DIL_GROUPS):
        q_ref, k_ref, v_ref = qkv_refs[g]
        nblk = seq // r // BLOCK

        def store(rho, blk, o, lse, g=g, r=r):
            start = blk * (BLOCK * r) + rho
            if r == 1:
                idx = pl.ds(pl.multiple_of(start, BLOCK), BLOCK)
            else:
                idx = pl.ds(start, BLOCK, stride=r)
            o_scr[g, idx, :] = o
            l_scr[g, idx, :] = jnp.broadcast_to(lse, (BLOCK, HEAD_DIM))

        for rho in range(r):
            lanes = slice(rho * HEAD_DIM, (rho + 1) * HEAD_DIM)
            o, lse = _band_block(q_ref[0, 0, :BLOCK, lanes], k_ref[0, 0, :BLOCK, lanes],
                                 v_ref[0, 0, :BLOCK, lanes], None, None)
            store(rho, 0, o, lse)

            def body(blk, carry, lanes=lanes, rho=rho, q_ref=q_ref, k_ref=k_ref, v_ref=v_ref,
                     store=store):
                cur = pl.ds(pl.multiple_of(blk * BLOCK, BLOCK), BLOCK)
                prev = pl.ds(pl.multiple_of((blk - 1) * BLOCK, BLOCK), BLOCK)
                o, lse = _band_block(q_ref[0, 0, cur, lanes], k_ref[0, 0, cur, lanes],
                                     v_ref[0, 0, cur, lanes], k_ref[0, 0, prev, lanes],
                                     v_ref[0, 0, prev, lanes])
                store(rho, blk, o, lse)
                return carry
            if nblk > 1:
                lax.fori_loop(1, nblk, body, 0)

    def combine(c, carry):
        rows = pl.ds(pl.multiple_of(c * BLOCK, BLOCK), BLOCK)
        ls = [l_scr[g, rows, :] for g in range(N_GROUPS)]
        m = jnp.maximum(jnp.maximum(ls[0], ls[1]), ls[2])
        es = [jnp.exp(l - m) for l in ls]
        num = es[0] * o_scr[0, rows, :] + es[1] * o_scr[1, rows, :] + es[2] * o_scr[2, rows, :]
        out_ref[0, rows, :] = (num / (es[0] + es[1] + es[2])).astype(out_ref.dtype)
        return carry
    lax.fori_loop(0, seq // BLOCK, combine, 0)


def _attn_prompt(qkv_groups, n_seq, seq):
    in_specs, args = [], []
    for role in range(3):
        for g, (_, r) in enumerate(DIL_GROUPS):
            in_specs.append(pl.BlockSpec((1, 1, seq // r, r * HEAD_DIM),
                                         lambda b, h, role=role: (role * N_HEADS + h, b, 0, 0)))
            args.append(qkv_groups[g])
    return pl.pallas_call(
        functools.partial(_attn_prompt_kernel, seq=seq),
        grid=(n_seq, N_HEADS),
        in_specs=in_specs,
        out_specs=pl.BlockSpec((1, seq, HEAD_DIM), lambda b, h: (b, 0, h)),
        out_shape=jax.ShapeDtypeStruct((n_seq, seq, ATT_WIDTH), BF16),
        scratch_shapes=[pltpu.VMEM((N_GROUPS, seq, HEAD_DIM), F32),
                        pltpu.VMEM((N_GROUPS, seq, HEAD_DIM), F32)],
        compiler_params=_params("arbitrary", "arbitrary"),
        name="attn_prompt",
    )(*args)


COPY_ROWS = 1024


def _attn_sample_kernel(q_ref, kvn_ref, win_ref, o_ref, lse_ref, new_ref, *, r, t_new):
    lb = win_ref.shape[0] // KV_ROWS
    tq = q_ref.shape[1]
    shift = t_new * KV_ROWS
    n_keep = (lb - t_new) * KV_ROWS
    n_full = n_keep // COPY_ROWS

    def copy(c, carry):
        dst = pl.multiple_of(c * COPY_ROWS, COPY_ROWS)
        src = pl.multiple_of(c * COPY_ROWS + shift, KV_ROWS)
        new_ref[pl.ds(dst, COPY_ROWS), :] = win_ref[pl.ds(src, COPY_ROWS), :]
        return carry
    lax.fori_loop(0, n_full, copy, 0)
    rest = n_keep - n_full * COPY_ROWS
    if rest:
        new_ref[n_full * COPY_ROWS:n_keep, :] = win_ref[n_full * COPY_ROWS + shift:, :]
    new_ref[n_keep:, :] = kvn_ref[...]
    t_i = lax.broadcasted_iota(jnp.int32, (tq, lb), 0)
    k_i = lax.broadcasted_iota(jnp.int32, (tq, lb), 1)
    dist = lb + t_i - k_i
    ok_buf = (dist % r == 0) & (dist <= r * DIL_SPAN)
    t_n = lax.broadcasted_iota(jnp.int32, (tq, BLOCK), 0)
    k_n = lax.broadcasted_iota(jnp.int32, (tq, BLOCK), 1) - (BLOCK - t_new)
    dn = t_n - k_n
    ok_new = (k_n >= 0) & (dn >= 0) & (dn % r == 0) & (dn <= r * DIL_SPAN)
    tail0 = (lb - BLOCK) * KV_ROWS
    for hd in range(N_HEADS):
        sl = slice(hd * HEAD_DIM, (hd + 1) * HEAD_DIM)
        q = q_ref[0, :, sl]
        kb = win_ref[_kv_rows(0, hd, lb), :].astype(BF16)
        vb = win_ref[_kv_rows(1, hd, lb), :].astype(BF16)
        kn = new_ref[pl.ds(tail0 + hd, BLOCK, stride=KV_ROWS), :].astype(BF16)
        vn = new_ref[pl.ds(tail0 + N_HEADS + hd, BLOCK, stride=KV_ROWS), :].astype(BF16)
        sb = jnp.where(ok_buf, _dot_nt(q, kb) * SCALE, NEG)
        sn = jnp.where(ok_new, _dot_nt(q, kn) * SCALE, NEG)
        m = jnp.maximum(jnp.max(sb, axis=-1, keepdims=True), jnp.max(sn, axis=-1, keepdims=True))
        pb = jnp.exp(sb - m)
        pn = jnp.exp(sn - m)
        l = jnp.sum(pb, axis=-1, keepdims=True) + jnp.sum(pn, axis=-1, keepdims=True)
        o = _dot(pb.astype(BF16), vb) + _dot(pn.astype(BF16), vn)
        o_ref[0, :, sl] = o / l
        lse_ref[0, :, sl] = jnp.broadcast_to(m + jnp.log(l), (tq, HEAD_DIM))


def _attn_sample(q, kv_new, win, nb, r):
    tq = q.shape[1]
    rows_new = kv_new.shape[0] // nb
    rows_win = win.shape[0] // nb
    return pl.pallas_call(
        functools.partial(_attn_sample_kernel, r=r, t_new=rows_new // KV_ROWS),
        grid=(nb,),
        in_specs=[pl.BlockSpec((1, tq, ATT_WIDTH), lambda b: (b, 0, 0)),
                  pl.BlockSpec((rows_new, HEAD_DIM), lambda b: (b, 0)),
                  pl.BlockSpec((rows_win, HEAD_DIM), lambda b: (b, 0))],
        out_specs=[pl.BlockSpec((1, tq, ATT_WIDTH), lambda b: (b, 0, 0)),
                   pl.BlockSpec((1, tq, ATT_WIDTH), lambda b: (b, 0, 0)),
                   pl.BlockSpec((rows_win, HEAD_DIM), lambda b: (b, 0))],
        out_shape=[jax.ShapeDtypeStruct((nb, tq, ATT_WIDTH), F32),
                   jax.ShapeDtypeStruct((nb, tq, ATT_WIDTH), F32),
                   jax.ShapeDtypeStruct(win.shape, F32)],
        compiler_params=_params("arbitrary"),
        name=f"attn_sample_r{r}",
    )(q, kv_new, win)


def _combine_kernel(o0, o1, o2, l0, l1, l2, out_ref):
    ls = [l0[...], l1[...], l2[...]]
    m = jnp.maximum(jnp.maximum(ls[0], ls[1]), ls[2])
    es = [jnp.exp(l - m) for l in ls]
    num = es[0] * o0[...] + es[1] * o1[...] + es[2] * o2[...]
    out_ref[...] = (num / (es[0] + es[1] + es[2])).astype(out_ref.dtype)


def _combine_groups(outs, lses):
    shape = outs[0].shape
    return pl.pallas_call(
        _combine_kernel,
        in_specs=[_full(shape)] * 6,
        out_specs=_full(shape),
        out_shape=jax.ShapeDtypeStruct(shape, BF16),
        grid=(1,),
        compiler_params=_params("arbitrary"),
        name="combine_groups",
    )(*outs, *lses)


FF_CHUNK = 256
CARRY_ROWS = 8


def _out_ffn_kernel(*refs, seq_len, tiles_per_seq, tail_rows):
    if seq_len is None:
        (x_ref, mix_ref, cr_ref, wo_ref, g_ref, wup_ref, cw_ref, cb_ref, wdn_ref,
         out_ref, tail_ref, carry) = refs
    else:
        (x_ref, mix_ref, cr_ref, wo_ref, g_ref, wup_ref, cw_ref, cb_ref, wdn_ref, e1_ref, e2_ref,
         out_ref, tail_ref) = refs
    tm = x_ref.shape[0]
    d_ff = wdn_ref.shape[0]
    x1 = (x_ref[...] + _dot(mix_ref[...], wo_ref[:ATT_WIDTH, :])
          + _dot(cr_ref[...], wo_ref[ATT_WIDTH:, :]))
    h = _rms(x1, g_ref[...]).astype(BF16)
    row = lax.broadcasted_iota(jnp.int32, (tm, 1), 0)
    if seq_len is None:
        @pl.when(pl.program_id(0) % tiles_per_seq == 0)
        def _():
            carry[...] = jnp.zeros_like(carry)
        t = row
    else:
        t = row % seq_len
    has1 = t >= 1
    has2 = t >= 2

    def conv(cols):
        up = _dot(h, wup_ref[:, cols])
        if seq_len is None:
            last = carry[CARRY_ROWS - 1:CARRY_ROWS, cols]
            e1 = last
            e2 = jnp.where(row == 0, carry[CARRY_ROWS - 2:CARRY_ROWS - 1, cols], last)
            carry[:, cols] = up[tm - CARRY_ROWS:, :]
        else:
            e1 = e1_ref[:, cols]
            e2 = e2_ref[:, cols]
        tail_ref[0, :, cols] = up[tm - tail_rows:, :]
        prev1 = jnp.where(has1, pltpu.roll(up, 1, 0), e1)
        prev2 = jnp.where(has2, pltpu.roll(up, 2, 0), e2)
        return (cb_ref[:, cols] + cw_ref[0:1, cols] * prev2 + cw_ref[1:2, cols] * prev1
                + cw_ref[2:3, cols] * up)

    acc = jnp.zeros_like(x1)
    for j in range(d_ff // FF_CHUNK):
        a = conv(slice(j * FF_CHUNK, (j + 1) * FF_CHUNK))
        b = conv(slice(d_ff + j * FF_CHUNK, d_ff + (j + 1) * FF_CHUNK))
        act = (a * jax.nn.sigmoid(a) * b).astype(BF16)
        acc = acc + _dot(act, wdn_ref[j * FF_CHUNK:(j + 1) * FF_CHUNK, :])
    out_ref[...] = x1 + acc


def _out_ffn(x2d, mix, cross, w_out_bf, g_ffn, w_up_bf, conv_w, conv_b, w_down_bf, tm,
             tiles_per_seq=None, seq_len=None, e1=None, e2=None, tail_rows=CARRY_ROWS):
    rows, d = x2d.shape
    d_ff = w_down_bf.shape[0]
    n_tiles = rows // tm
    in_specs = [
        pl.BlockSpec((tm, d), lambda i: (i, 0)),
        pl.BlockSpec((tm, ATT_WIDTH), lambda i: (i, 0)),
        pl.BlockSpec((tm, ATT_WIDTH), lambda i: (i, 0)),
        _full(w_out_bf.shape), _full((1, d)), _full(w_up_bf.shape),
        _full((CONV_W, 2 * d_ff)), _full((1, 2 * d_ff)), _full(w_down_bf.shape),
    ]
    args = [x2d, mix, cross, w_out_bf, g_ffn.reshape(1, d), w_up_bf, conv_w, conv_b.reshape(1, 2 * d_ff),
            w_down_bf]
    scratch = []
    if seq_len is None:
        scratch = [pltpu.VMEM((CARRY_ROWS, 2 * d_ff), F32)]
    else:
        in_specs += [pl.BlockSpec((tm, 2 * d_ff), lambda i: (i, 0))] * 2
        args += [e1, e2]
    return pl.pallas_call(
        functools.partial(_out_ffn_kernel, seq_len=seq_len, tiles_per_seq=tiles_per_seq,
                          tail_rows=tail_rows),
        grid=(n_tiles,),
        in_specs=in_specs,
        out_specs=[pl.BlockSpec((tm, d), lambda i: (i, 0)),
                   pl.BlockSpec((1, tail_rows, 2 * d_ff), lambda i: (i, 0, 0))],
        out_shape=[jax.ShapeDtypeStruct((rows, d), F32),
                   jax.ShapeDtypeStruct((n_tiles, tail_rows, 2 * d_ff), F32)],
        scratch_shapes=scratch,
        compiler_params=_params("arbitrary"),
        name="out_ffn",
    )(*args)


def _swap_halves(x):
    return pltpu.roll(x, S5_STATE, 1)


def _s5_prep_kernel(lam_ref, logdt_ref, bt_ref, c_ref, d_ref, e_ref, cl_ref, toep_ref, lam_out_ref, *, sub):
    gc = lam_ref.shape[0]
    lane = lax.broadcasted_iota(jnp.int32, lam_ref.shape, 1)
    first = lane < S5_STATE
    sign = jnp.where(first, -1.0, 1.0)
    a = lam_ref[...]
    a_sw = _swap_halves(a)
    are = jnp.where(first, a, a_sw)
    aim = jnp.where(first, a_sw, a)
    dt = jnp.exp(logdt_ref[...])
    mag = jnp.exp(are * dt)
    lr = mag * jnp.cos(aim * dt)
    li = mag * jnp.sin(aim * dt)
    den = are * are + aim * aim
    xr = lr - 1.0
    f_re = (xr * are + li * aim) / den
    f_im = (li * are - xr * aim) / den
    lb = sign * li

    def cmul(x, m_re, m_sw):
        return x * m_re + _swap_halves(x) * m_sw

    c = c_ref[...]
    c_neg = c * -sign
    ri = lax.broadcasted_iota(jnp.int32, (gc, gc), 0)
    ci = lax.broadcasted_iota(jnp.int32, (gc, gc), 1)
    same_group = (ri // S5_GROUP) == (ci // S5_GROUP)
    e = cmul(bt_ref[...], f_re, sign * f_im)
    cl = c
    pw = jnp.where(first, 1.0, 0.0)
    for k in range(sub):
        kk = sub - 1 - k
        e_ref[kk] = e
        kmat = jnp.where(same_group, _dot_nt(e, c_neg, precision=HIGHEST), 0.0)
        if k == 0:
            kmat = kmat + jnp.where(ri == ci, d_ref[...], 0.0)
        for half in range(2):
            sl = slice(half * S5_HALF, (half + 1) * S5_HALF)
            toep_ref[half, kk * S5_HALF:(kk + 1) * S5_HALF, :] = kmat[sl, sl].astype(toep_ref.dtype)
        cl = cmul(cl, lr, lb)
        cl_ref[k] = cl * -sign
        e = cmul(e, lr, lb)
        pw = cmul(pw, lr, lb)
    lam_out_ref[...] = pw


def _expand_kernel(tab_ref, out_ref):
    out_ref[...] = jnp.zeros_like(out_ref)
    for g in range(S5_HALF // S5_GROUP):
        rows = slice(g * S5_GROUP, (g + 1) * S5_GROUP)
        out_ref[0, rows, g * 2 * S5_STATE:(g + 1) * 2 * S5_STATE] = tab_ref[0, rows, :].astype(out_ref.dtype)


def _expand(tab, sub):
    n_state = (S5_HALF // S5_GROUP) * 2 * S5_STATE
    return pl.pallas_call(
        _expand_kernel,
        grid=(2, sub),
        in_specs=[pl.BlockSpec((1, S5_HALF, 2 * S5_STATE), lambda h, t: (t, h, 0))],
        out_specs=pl.BlockSpec((1, S5_HALF, n_state), lambda h, t: (h, t, 0)),
        out_shape=jax.ShapeDtypeStruct((2, sub * S5_HALF, n_state), BF16),
        compiler_params=_params("arbitrary", "arbitrary"),
        name="s5_expand",
    )(tab)


def _s5_prep(lam_re, lam_im, log_dt, b_re, b_im, c_re, c_im, d_skip, sub):
    n_groups = lam_re.shape[0]
    gc = n_groups * S5_GROUP
    rep = lambda t: jnp.repeat(t, S5_GROUP, axis=0)
    lam_p = rep(jnp.concatenate([lam_re, lam_im], axis=-1))
    logdt = rep(log_dt.reshape(n_groups, 1))
    bt_p = jnp.concatenate([b_re.transpose(0, 2, 1), b_im.transpose(0, 2, 1)], axis=-1).reshape(gc, 2 * S5_STATE)
    c_p = jnp.concatenate([c_re, c_im], axis=-1).reshape(gc, 2 * S5_STATE)
    tab_shape = (sub, gc, 2 * S5_STATE)
    e_tab, cl_tab, toep, lam_sub = pl.pallas_call(
        functools.partial(_s5_prep_kernel, sub=sub),
        grid=(1,),
        in_specs=[_full(lam_p.shape), _full(logdt.shape), _full(bt_p.shape), _full(c_p.shape), _full((gc, 1))],
        out_specs=[_full(tab_shape), _full(tab_shape), _full((2, sub * S5_HALF, S5_HALF)),
                   _full((gc, 2 * S5_STATE))],
        out_shape=[jax.ShapeDtypeStruct(tab_shape, F32), jax.ShapeDtypeStruct(tab_shape, F32),
                   jax.ShapeDtypeStruct((2, sub * S5_HALF, S5_HALF), BF16),
                   jax.ShapeDtypeStruct((gc, 2 * S5_STATE), F32)],
        compiler_params=_params("arbitrary"),
        name="s5_prep",
    )(lam_p, logdt, bt_p, c_p, d_skip.reshape(gc, 1))
    lam_sub = lam_sub[::S5_GROUP].reshape(2, n_groups // 2, 2 * S5_STATE)
    return toep, _expand(e_tab, sub), _expand(cl_tab, sub), lam_sub


def _s5_toep_kernel(v_ref, w_ref, y_ref, *, sub):
    for tau in range(sub):
        y_ref[0, :, tau * S5_HALF:(tau + 1) * S5_HALF] = _dot(
            v_ref[0, :, :(tau + 1) * S5_HALF], w_ref[0, (sub - 1 - tau) * S5_HALF:, :])


def _matmul_kernel(a_ref, b_ref, o_ref):
    o_ref[0] = _dot(a_ref[0], b_ref[0])


def _matmul_nt_add_kernel(a_ref, b_ref, y_ref, o_ref):
    o_ref[0] = y_ref[0] + _dot_nt(a_ref[0].astype(BF16), b_ref[0])


def _s5_scan_kernel(s_ref, lam_ref, x0_ref, xprev_ref, xfin_ref, *, n_seq, n_sub):
    lane = lax.broadcasted_iota(jnp.int32, lam_ref.shape[1:], 1)
    first = lane < S5_STATE
    lam = lam_ref[0]
    lam_sw = _swap_halves(lam)
    m_re = jnp.where(first, lam, lam_sw)
    m_sw = jnp.where(first, -lam_sw, lam)
    x_init = tuple(x0_ref[0, b] for b in range(n_seq))
    xs_init = tuple(_swap_halves(x) for x in x_init)

    def body(n, carry):
        xs, xss = carry
        new_x, new_xs = [], []
        for b in range(n_seq):
            row = b * n_sub + n
            xprev_ref[0, row] = xs[b]
            s = s_ref[0, row]
            new_x.append(xs[b] * m_re + xss[b] * m_sw + s)
            new_xs.append(xss[b] * m_re - xs[b] * m_sw + _swap_halves(s))
        return tuple(new_x), tuple(new_xs)
    fin, _ = lax.fori_loop(0, n_sub, body, (x_init, xs_init))
    for b in range(n_seq):
        xfin_ref[0, b] = fin[b]


def _glu_kernel(y_ref, w_ref, b_ref, o_ref, stage, *, sub):
    tmv = y_ref.shape[1]
    n_cb = ATT_WIDTH // HEAD_DIM
    for cb in range(n_cb):
        half, off = divmod(cb * HEAD_DIM, S5_HALF)
        for tau in range(sub):
            stage[cb, pl.ds(tau, tmv, stride=sub), :] = (
                y_ref[half, :, tau * S5_HALF + off:tau * S5_HALF + off + HEAD_DIM])
    y = jax.nn.gelu(jnp.concatenate([stage[cb] for cb in range(n_cb)], axis=-1))
    z = _dot(y.astype(BF16), w_ref[...]) + b_ref[...]
    o_ref[...] = (y * jax.nn.sigmoid(z)).astype(o_ref.dtype)


def _s5_mixer(v, n_seq, seq, sub, x0, toep, b_all, c_all_t, lam_sub, w_glu_bf, b_glu):
    _, m, width = v.shape
    rows = m * sub
    n_sub = seq // sub
    n_state = b_all.shape[2]
    gh = n_state // (2 * S5_STATE)
    tms = min(m, 256)
    y_intra = pl.pallas_call(
        functools.partial(_s5_toep_kernel, sub=sub),
        grid=(2, m // tms),
        in_specs=[pl.BlockSpec((1, tms, width), lambda h, i: (h, i, 0)),
                  pl.BlockSpec((1, width, S5_HALF), lambda h, i: (h, 0, 0))],
        out_specs=pl.BlockSpec((1, tms, width), lambda h, i: (h, i, 0)),
        out_shape=jax.ShapeDtypeStruct((2, m, width), F32),
        compiler_params=_params("arbitrary", "arbitrary"),
        name="s5_toeplitz",
    )(v, toep)
    nc = 512
    s_loc = pl.pallas_call(
        _matmul_kernel,
        grid=(2, n_state // nc, m // tms),
        in_specs=[pl.BlockSpec((1, tms, width), lambda h, j, i: (h, i, 0)),
                  pl.BlockSpec((1, width, nc), lambda h, j, i: (h, 0, j))],
        out_specs=pl.BlockSpec((1, tms, nc), lambda h, j, i: (h, i, j)),
        out_shape=jax.ShapeDtypeStruct((2, m, n_state), F32),
        compiler_params=_params("arbitrary", "arbitrary", "arbitrary"),
        name="s5_state_in",
    )(v, b_all)
    st_shape = (2, m, gh, 2 * S5_STATE)
    x_prev, x_fin = pl.pallas_call(
        functools.partial(_s5_scan_kernel, n_seq=n_seq, n_sub=n_sub),
        grid=(2,),
        in_specs=[pl.BlockSpec((1,) + st_shape[1:], lambda h: (h, 0, 0, 0)),
                  pl.BlockSpec((1, gh, 2 * S5_STATE), lambda h: (h, 0, 0)),
                  pl.BlockSpec((1, n_seq, gh, 2 * S5_STATE), lambda h: (h, 0, 0, 0))],
        out_specs=[pl.BlockSpec((1,) + st_shape[1:], lambda h: (h, 0, 0, 0)),
                   pl.BlockSpec((1, n_seq, gh, 2 * S5_STATE), lambda h: (h, 0, 0, 0))],
        out_shape=[jax.ShapeDtypeStruct(st_shape, F32),
                   jax.ShapeDtypeStruct((2, n_seq, gh, 2 * S5_STATE), F32)],
        compiler_params=_params("arbitrary"),
        name="s5_scan",
    )(s_loc.reshape(st_shape), lam_sub, x0)
    wc = min(width, 512)
    y = pl.pallas_call(
        _matmul_nt_add_kernel,
        grid=(2, width // wc, m // tms),
        in_specs=[pl.BlockSpec((1, tms, n_state), lambda h, j, i: (h, i, 0)),
                  pl.BlockSpec((1, wc, n_state), lambda h, j, i: (h, j, 0)),
                  pl.BlockSpec((1, tms, wc), lambda h, j, i: (h, i, j))],
        out_specs=pl.BlockSpec((1, tms, wc), lambda h, j, i: (h, i, j)),
        out_shape=jax.ShapeDtypeStruct((2, m, width), F32),
        compiler_params=_params("arbitrary", "arbitrary", "arbitrary"),
        name="s5_state_out",
    )(x_prev.reshape(2, m, n_state), c_all_t, y_intra)
    tm = min(rows, 512)
    mix = pl.pallas_call(
        functools.partial(_glu_kernel, sub=sub),
        grid=(rows // tm,),
        in_specs=[pl.BlockSpec((2, tm // sub, width), lambda i: (0, i, 0)),
                  _full(w_glu_bf.shape), _full((1, ATT_WIDTH))],
        out_specs=pl.BlockSpec((tm, ATT_WIDTH), lambda i: (i, 0)),
        out_shape=jax.ShapeDtypeStruct((rows, ATT_WIDTH), BF16),
        scratch_shapes=[pltpu.VMEM((ATT_WIDTH // HEAD_DIM, tm, HEAD_DIM), F32)],
        compiler_params=_params("arbitrary"),
        name="s5_glu",
    )(y, w_glu_bf, b_glu.reshape(1, ATT_WIDTH))
    return mix, x_fin


def _pack_state(s):
    n = s.shape[0]
    gh = s.shape[2] // 2
    return s.reshape(n, 2, 2, gh, S5_STATE).transpose(2, 0, 3, 1, 4).reshape(2, n, gh, 2 * S5_STATE)


def _unpack_state(x):
    _, n, gh, _ = x.shape
    return x.reshape(2, n, gh, 2, S5_STATE).transpose(1, 3, 0, 2, 4).reshape(n, 2, 2 * gh, S5_STATE)


PROMPT_TILE = 512
PROMPT_SUB = 16
Q_PAD = 16


def _pad_rows(t, n):
    return jnp.pad(t, ((0, 0), (0, n - t.shape[1]), (0, 0)))


def kernel(x_prompt, x_sample, cache_win0_kv, cache_win1_kv, cache_win2_kv, cache_mem_kv, state_s5,
           state_ffn_conv, mem_prompt, g_mix, g_ffn, w_in_a, g_q_dil, g_k_dil, w_in_b, s5_lam_re,
           s5_lam_im, s5_log_dt, s5_b_re, s5_b_im, s5_c_re, s5_c_im, s5_d, w_glu, b_glu, g_mem,
           w_mem_kv, g_q_cross, g_k_cross, w_out, w_up, conv_w, conv_b, w_down):
    nb, seq, d = x_prompt.shape
    db, ts, _ = x_sample.shape
    depth = g_mix.shape[0]
    n_mem = mem_prompt.shape[1]
    d_ff2 = w_up.shape[2]
    assert ts >= CONV_W - 1 and seq % PROMPT_TILE == 0 and ts <= Q_PAD
    caches = (cache_win0_kv, cache_win1_kv, cache_win2_kv)
    w_in_a_bf, w_in_b_bf, w_out_bf = w_in_a.astype(BF16), w_in_b.astype(BF16), w_out.astype(BF16)
    w_up_bf, w_down_bf, w_glu_bf = w_up.astype(BF16), w_down.astype(BF16), w_glu.astype(BF16)
    w_mem_bf = w_mem_kv.astype(BF16)

    tab_p = _rope_tables(jnp.arange(seq, dtype=jnp.int32))
    tab_s = tuple(jnp.tile(t, (db, 1)) for t in _rope_tables(PAST_LEN + jnp.arange(ts, dtype=jnp.int32)))
    win_keep = tuple(min(w, seq) for w, _ in DIL_GROUPS)
    rows_s = db * ts
    kv_tail = (2, N_HEADS, HEAD_DIM)

    mem_p = _mem_kv(mem_prompt.reshape(nb * n_mem, d), g_mem, w_mem_bf, g_k_cross)

    xp = x_prompt.reshape(nb * seq, d)
    xs = x_sample.reshape(rows_s, d)
    p_win, s_win = [[] for _ in DIL_GROUPS], [[] for _ in DIL_GROUPS]
    p_s5, s_s5, p_conv, s_conv = [], [], [], []
    tiles_per_seq = seq // PROMPT_TILE
    for i in range(depth):
        mem_i = mem_p[i].reshape(nb, n_mem * KV_ROWS, HEAD_DIM)
        mem_s = cache_mem_kv[i].reshape(db, n_mem * KV_ROWS, HEAD_DIM)
        if i % 2 == 0:
            ia = i // 2
            *qkv_groups, w0, w1, w2, cross_p = _in_proj_a(
                xp, seq, PROMPT_TILE, g_mix[i], w_in_a_bf[ia], tab_p, g_q_dil[ia], g_k_dil[ia],
                g_q_cross[i], mem_i, win_keep, dilated=True)
            for g, w in enumerate((w0, w1, w2)):
                p_win[g].append(w.reshape((nb, win_keep[g]) + kv_tail))
            mix_p = _attn_prompt(qkv_groups, nb, seq).reshape(nb * seq, ATT_WIDTH)
            qkv_s, k0, k1, k2, qc_s = _in_proj_a(
                xs, rows_s, rows_s, g_mix[i], w_in_a_bf[ia], tab_s, g_q_dil[ia], g_k_dil[ia],
                None, None, (rows_s,) * N_GROUPS, dilated=False)
            outs, lses = [], []
            for g, ((_, r), kv_new) in enumerate(zip(DIL_GROUPS, (k0, k1, k2))):
                q_g = qkv_s[g * N_HEADS:(g + 1) * N_HEADS]
                q_g = _pad_rows(q_g.transpose(1, 0, 2).reshape(db, ts, ATT_WIDTH), Q_PAD)
                win = caches[g][ia]
                o_g, lse_g, new_g = _attn_sample(q_g, kv_new, win.reshape(-1, HEAD_DIM), db, r)
                outs.append(o_g[:, :ts].reshape(rows_s, ATT_WIDTH))
                lses.append(lse_g[:, :ts].reshape(rows_s, ATT_WIDTH))
                s_win[g].append(new_g.reshape(win.shape))
            mix_s = _combine_groups(outs, lses)
        else:
            ib = i // 2
            prm = (s5_lam_re[ib], s5_lam_im[ib], s5_log_dt[ib], s5_b_re[ib], s5_b_im[ib], s5_c_re[ib],
                   s5_c_im[ib], s5_d[ib])
            gh = s5_lam_re.shape[1] // 2
            u_p, cross_p = _in_proj_b(xp, seq, PROMPT_TILE, PROMPT_SUB, g_mix[i], w_in_b_bf[ib],
                                      g_q_cross[i], mem_i)
            x0 = jnp.zeros((2, nb, gh, 2 * S5_STATE), F32)
            mix_p, fin_p = _s5_mixer(u_p, nb, seq, PROMPT_SUB, x0, *_s5_prep(*prm, PROMPT_SUB),
                                     w_glu_bf[ib], b_glu[ib])
            p_s5.append(_unpack_state(fin_p))
            u_s, qc_s = _in_proj_b(xs, rows_s, rows_s, ts, g_mix[i], w_in_b_bf[ib], None, None)
            mix_s, fin_s = _s5_mixer(u_s, db, ts, ts, _pack_state(state_s5[ib]), *_s5_prep(*prm, ts),
                                     w_glu_bf[ib], b_glu[ib])
            s_s5.append(_unpack_state(fin_s))
        cross_s = _cross_sample(_pad_rows(qc_s.reshape(db, ts, ATT_WIDTH), Q_PAD), g_q_cross[i], mem_s)
        cross_s = cross_s[:, :ts].reshape(rows_s, ATT_WIDTH)

        ffn_w = (w_out_bf[i], g_ffn[i], w_up_bf[i], conv_w[i], conv_b[i], w_down_bf[i])
        xp, tails = _out_ffn(xp, mix_p, cross_p, *ffn_w, PROMPT_TILE, tiles_per_seq=tiles_per_seq)
        p_conv.append(tails.reshape(nb, tiles_per_seq, CARRY_ROWS, d_ff2)[:, -1, CARRY_ROWS - (CONV_W - 1):])
        buf = state_ffn_conv[i]
        zero = jnp.zeros((db, ts - 2, d_ff2), F32)
        e1 = jnp.concatenate([buf[:, 1:2], zero, zero[:, :1]], axis=1).reshape(rows_s, d_ff2)
        e2 = jnp.concatenate([buf, zero], axis=1).reshape(rows_s, d_ff2)
        xs, tails = _out_ffn(xs, mix_s, cross_s, *ffn_w, rows_s, seq_len=ts, e1=e1, e2=e2, tail_rows=rows_s)
        s_conv.append(tails.reshape(db, ts, d_ff2)[:, ts - (CONV_W - 1):])

    return (xp.reshape(nb, seq, d), xs.reshape(db, ts, d),
            jnp.stack(p_win[0]), jnp.stack(p_win[1]), jnp.stack(p_win[2]),
            mem_p.reshape((depth, nb, n_mem) + kv_tail),
            jnp.stack(p_s5), jnp.stack(p_conv),
            jnp.stack(s_win[0]), jnp.stack(s_win[1]), jnp.stack(s_win[2]),
            jnp.stack(s_s5), jnp.stack(s_conv))
```

```python
import functools
import math

import jax
import jax.numpy as jnp
from jax import lax
from jax.experimental import pallas as pl
from jax.experimental.pallas import tpu as pltpu

HEAD_DIM = 128
N_HEADS = 4
DIL_GROUPS = ((128, 1), (512, 4), (2048, 16))
N_GROUPS = len(DIL_GROUPS)
DIL_SPAN = 128
BLOCK = 128
ATT_WIDTH = N_HEADS * HEAD_DIM
KV_ROWS = 2 * N_HEADS
ROT_DIM = HEAD_DIM // 4
ROT_HALF = ROT_DIM // 2
ROPE_THETA = 500000.0
S5_GROUP = 16
S5_STATE = 64
S5_HALF = 256
CONV_W = 3
EPS = 1e-6
NEG = -1e30
SCALE = HEAD_DIM ** -0.5
PAST_LEN = 16384

VMEM_LIMIT_V7X = 56 * 1024 * 1024
BF16 = jnp.bfloat16
F32 = jnp.float32
HIGHEST = lax.Precision.HIGHEST


def _params(*sem):
    return pltpu.CompilerParams(dimension_semantics=sem, vmem_limit_bytes=VMEM_LIMIT_V7X)


def _rms(x, g):
    return x * lax.rsqrt(jnp.mean(x * x, axis=-1, keepdims=True) + EPS) * g


def _dot(a, b):
    return jnp.dot(a, b, preferred_element_type=F32)


def _dot_nt(a, b, precision=None):
    return lax.dot_general(a, b, (((1,), (1,)), ((), ())), precision=precision,
                           preferred_element_type=F32)


def _full(shape):
    nd = len(shape)
    return pl.BlockSpec(shape, lambda *_: (0,) * nd)


def _kv_rows(kv, head, n):
    return pl.ds(kv * N_HEADS + head, n, stride=KV_ROWS)


def _mem_kv_kernel(mem_ref, g_ref, w_ref, gk_ref, out_ref):
    tm = mem_ref.shape[0]
    h = _rms(mem_ref[...], g_ref[0]).astype(BF16)
    kv = _dot(h, w_ref[0])
    for hd in range(N_HEADS):
        sl = slice(hd * HEAD_DIM, (hd + 1) * HEAD_DIM)
        out_ref[0, _kv_rows(0, hd, tm), :] = _rms(kv[:, sl], gk_ref[0])
        out_ref[0, _kv_rows(1, hd, tm), :] = kv[:, ATT_WIDTH + hd * HEAD_DIM:ATT_WIDTH + (hd + 1) * HEAD_DIM]


def _mem_kv(mem2d, g_mem, w_kv_bf, g_k):
    depth, d, _ = w_kv_bf.shape
    rows = mem2d.shape[0]
    tm = min(rows, 512)
    return pl.pallas_call(
        _mem_kv_kernel,
        grid=(depth, rows // tm),
        in_specs=[
            pl.BlockSpec((tm, d), lambda l, i: (i, 0)),
            pl.BlockSpec((1, 1, d), lambda l, i: (l, 0, 0)),
            pl.BlockSpec((1, d, 2 * ATT_WIDTH), lambda l, i: (l, 0, 0)),
            pl.BlockSpec((1, 1, HEAD_DIM), lambda l, i: (l, 0, 0)),
        ],
        out_specs=pl.BlockSpec((1, tm * KV_ROWS, HEAD_DIM), lambda l, i: (l, i, 0)),
        out_shape=jax.ShapeDtypeStruct((depth, rows * KV_ROWS, HEAD_DIM), F32),
        compiler_params=_params("arbitrary", "arbitrary"),
        name="mem_kv",
    )(mem2d, g_mem.reshape(depth, 1, d), w_kv_bf, g_k.reshape(depth, 1, HEAD_DIM))


def _cross_heads(qc, mem_ref, gq):
    n_mem = mem_ref.shape[1] // KV_ROWS
    outs = []
    for hd in range(N_HEADS):
        q = _rms(qc[:, hd * HEAD_DIM:(hd + 1) * HEAD_DIM], gq).astype(BF16)
        k = mem_ref[0, _kv_rows(0, hd, n_mem), :].astype(BF16)
        v = mem_ref[0, _kv_rows(1, hd, n_mem), :].astype(BF16)
        s = _dot_nt(q, k) * SCALE
        m = jnp.max(s, axis=-1, keepdims=True)
        p = jnp.exp(s - m)
        l = jnp.sum(p, axis=-1, keepdims=True)
        outs.append(_dot(p.astype(BF16), v) / l)
    return outs


def _rope(x, cos_t, sin_lo, sin_hi):
    return (x * cos_t + pltpu.roll(x, HEAD_DIM - ROT_HALF, 1) * sin_lo
            + pltpu.roll(x, ROT_HALF, 1) * sin_hi)


def _cache_copies(old_refs, kvn_refs, new_refs, sem, b):
    copies = []
    for g in range(N_GROUPS):
        n = old_refs[g].shape[1]
        n_new = kvn_refs[g].shape[1]
        copies.append(pltpu.make_async_copy(old_refs[g].at[b, pl.ds(n_new, n - n_new)],
                                            new_refs[g].at[b, pl.ds(0, n - n_new)], sem.at[g]))
        copies.append(pltpu.make_async_copy(kvn_refs[g].at[b], new_refs[g].at[b, pl.ds(n - n_new, n_new)],
                                            sem.at[N_GROUPS + g]))
    return copies


def _in_proj_a_kernel(*refs, fuse_cross, dilated, win_rows, move_caches):
    n_in = (10 if fuse_cross else 8) + (2 * N_GROUPS if move_caches else 0)
    x_ref, g_ref, w_ref, cos_ref, slo_ref, shi_ref, gq_ref, gk_ref = refs[:8]
    n_qkv = N_GROUPS if dilated else 1
    qkv_refs = refs[n_in:n_in + n_qkv]
    win_refs = refs[n_in + n_qkv:n_in + n_qkv + N_GROUPS]
    cr_ref = refs[n_in + n_qkv + N_GROUPS]
    n_out = n_qkv + N_GROUPS + 1 + (N_GROUPS if move_caches else 0)
    scratch = list(refs[n_in + n_out:])
    stage = scratch.pop(0) if dilated else None
    if move_caches:
        old_refs = refs[n_in - 2 * N_GROUPS:n_in - N_GROUPS]
        kvn_refs = refs[n_in - N_GROUPS:n_in]
        new_refs = refs[n_in + n_out - N_GROUPS:n_in + n_out]
        n_moves = old_refs[0].shape[0]
        copies = _cache_copies(old_refs, kvn_refs, new_refs, scratch.pop(0),
                               jnp.minimum(pl.program_id(0), n_moves - 1))
        moving = pl.program_id(0) < n_moves

        @pl.when(moving)
        def _():
            for cp in copies:
                cp.start()
    tm = x_ref.shape[0]
    h = _rms(x_ref[...], g_ref[...]).astype(BF16)
    cos_t, sin_lo, sin_hi = cos_ref[...], slo_ref[...], shi_ref[...]
    for role in range(3):
        for g, (_, r) in enumerate(DIL_GROUPS):
            c = role * N_GROUPS + g
            y = _dot(h, w_ref[:, c * ATT_WIDTH:(c + 1) * ATT_WIDTH])
            wr = win_rows[g]
            for hd in range(N_HEADS):
                yh = y[:, hd * HEAD_DIM:(hd + 1) * HEAD_DIM]
                if role == 0:
                    yh = _rope(_rms(yh, gq_ref[g]), cos_t, sin_lo, sin_hi)
                elif role == 1:
                    yh = _rope(_rms(yh, gk_ref[g]), cos_t, sin_lo, sin_hi)
                if role > 0:
                    win_refs[g][_kv_rows(role - 1, hd, wr), :] = yh[tm - wr:, :]
                if not dilated:
                    qkv_refs[0][role * N_GROUPS * N_HEADS + g * N_HEADS + hd] = yh.astype(BF16)
                elif r == 1:
                    qkv_refs[g][role * N_HEADS + hd, 0] = yh.astype(BF16)
                else:
                    stage[...] = yh
                    for rho in range(r):
                        qkv_refs[g][role * N_HEADS + hd, 0, :, rho * HEAD_DIM:(rho + 1) * HEAD_DIM] = (
                            stage[pl.ds(rho, tm // r, stride=r), :].astype(BF16))
    qc = _dot(h, w_ref[:, 3 * N_GROUPS * ATT_WIDTH:])
    if fuse_cross:
        outs = _cross_heads(qc, refs[9], refs[8][...])
        for hd in range(N_HEADS):
            cr_ref[:, hd * HEAD_DIM:(hd + 1) * HEAD_DIM] = outs[hd].astype(cr_ref.dtype)
    else:
        cr_ref[...] = qc
    if move_caches:
        @pl.when(moving)
        def _():
            for cp in copies:
                cp.wait()


def _in_proj_b_kernel(*refs, fuse_cross, sub):
    if fuse_cross:
        x_ref, g_ref, w_ref, gqc_ref, mem_ref, u_ref, cr_ref, stage = refs
    else:
        x_ref, g_ref, w_ref, u_ref, cr_ref, stage = refs
    tm = x_ref.shape[0]
    h = _rms(x_ref[...], g_ref[...]).astype(BF16)
    u = _dot(h, w_ref[:, :ATT_WIDTH])
    for cb in range(ATT_WIDTH // HEAD_DIM):
        stage[cb] = u[:, cb * HEAD_DIM:(cb + 1) * HEAD_DIM]
        half, off = divmod(cb * HEAD_DIM, S5_HALF)
        for tau in range(sub):
            u_ref[half, :, tau * S5_HALF + off:tau * S5_HALF + off + HEAD_DIM] = stage[
                cb, pl.ds(tau, tm // sub, stride=sub), :].astype(BF16)
    qc = _dot(h, w_ref[:, ATT_WIDTH:])
    if fuse_cross:
        outs = _cross_heads(qc, mem_ref, gqc_ref[...])
        for hd in range(N_HEADS):
            cr_ref[:, hd * HEAD_DIM:(hd + 1) * HEAD_DIM] = outs[hd].astype(cr_ref.dtype)
    else:
        cr_ref[...] = qc


def _rope_tables(pos):
    inv = jnp.exp(-math.log(ROPE_THETA) * jnp.arange(ROT_HALF, dtype=F32) / ROT_HALF)
    ang = pos.astype(F32)[:, None] * inv[None, :]
    cos, sin = jnp.cos(ang), jnp.sin(ang)
    rows = pos.shape[0]
    ones = jnp.ones((rows, HEAD_DIM - ROT_DIM), F32)
    zeros = jnp.zeros((rows, HEAD_DIM - ROT_DIM), F32)
    z16 = jnp.zeros((rows, ROT_HALF), F32)
    cos_t = jnp.concatenate([cos, cos, ones], axis=1)
    sin_lo = jnp.concatenate([-sin, z16, zeros], axis=1)
    sin_hi = jnp.concatenate([z16, sin, zeros], axis=1)
    return cos_t, sin_lo, sin_hi


def _in_proj_a(x2d, seq, tm, g_mix, w_bf, tables, g_q, g_k, g_qc, mem_kv, win_keep, dilated, moves=None):
    rows, d = x2d.shape
    n_in = w_bf.shape[1]
    tiles_per_seq = seq // tm
    n_seq = rows // seq
    fuse_cross = mem_kv is not None
    tab_tiles = tables[0].shape[0] // tm
    win_rows = tuple(min(k, tm) for k in win_keep)

    def win_spec(keep):
        wr = min(keep, tm)
        nblk = keep // wr
        first = tiles_per_seq - nblk

        def imap(i):
            b = i // tiles_per_seq
            t = i % tiles_per_seq
            return (b * nblk + jnp.maximum(t - first, 0), 0)
        return pl.BlockSpec((wr * KV_ROWS, HEAD_DIM), imap)

    in_specs = [
        pl.BlockSpec((tm, d), lambda i: (i, 0)),
        _full((1, d)),
        _full((d, n_in)),
        pl.BlockSpec((tm, HEAD_DIM), lambda i: (i % tab_tiles, 0)),
        pl.BlockSpec((tm, HEAD_DIM), lambda i: (i % tab_tiles, 0)),
        pl.BlockSpec((tm, HEAD_DIM), lambda i: (i % tab_tiles, 0)),
        _full((N_GROUPS, HEAD_DIM)),
        _full((N_GROUPS, HEAD_DIM)),
    ]
    args = [x2d, g_mix.reshape(1, d), w_bf, *tables, g_q, g_k]
    if fuse_cross:
        in_specs += [_full((1, HEAD_DIM)),
                     pl.BlockSpec((1,) + mem_kv.shape[1:], lambda i: (i // tiles_per_seq, 0, 0))]
        args += [g_qc.reshape(1, HEAD_DIM), mem_kv]
    if dilated:
        qkv_specs = [pl.BlockSpec((3 * N_HEADS, 1, tm // r, r * HEAD_DIM),
                                  lambda i: (0, i // tiles_per_seq, i % tiles_per_seq, 0))
                     for _, r in DIL_GROUPS]
        qkv_shapes = [jax.ShapeDtypeStruct((3 * N_HEADS, n_seq, seq // r, r * HEAD_DIM), BF16)
                      for _, r in DIL_GROUPS]
        scratch = [pltpu.VMEM((tm, HEAD_DIM), F32)]
    else:
        qkv_specs = [pl.BlockSpec((3 * N_GROUPS * N_HEADS, tm, HEAD_DIM), lambda i: (0, i, 0))]
        qkv_shapes = [jax.ShapeDtypeStruct((3 * N_GROUPS * N_HEADS, rows, HEAD_DIM), BF16)]
        scratch = []
    out_specs = [*qkv_specs, *[win_spec(k) for k in win_keep],
                 pl.BlockSpec((tm, ATT_WIDTH), lambda i: (i, 0))]
    out_shape = [*qkv_shapes,
                 *[jax.ShapeDtypeStruct((n_seq * k * KV_ROWS, HEAD_DIM), F32) for k in win_keep],
                 jax.ShapeDtypeStruct((rows, ATT_WIDTH), BF16 if fuse_cross else F32)]
    if moves is not None:
        old, new_rows = moves
        assert old[0].shape[0] <= rows // tm
        any_spec = pl.BlockSpec(memory_space=pl.ANY)
        in_specs += [any_spec] * (2 * N_GROUPS)
        args += [*old, *new_rows]
        out_specs += [any_spec] * N_GROUPS
        out_shape += [jax.ShapeDtypeStruct(o.shape, o.dtype) for o in old]
        scratch = scratch + [pltpu.SemaphoreType.DMA((2 * N_GROUPS,))]
    return pl.pallas_call(
        functools.partial(_in_proj_a_kernel, fuse_cross=fuse_cross, dilated=dilated, win_rows=win_rows,
                          move_caches=moves is not None),
        grid=(rows // tm,),
        in_specs=in_specs, out_specs=out_specs, out_shape=out_shape, scratch_shapes=scratch,
        compiler_params=_params("arbitrary"),
        name="in_proj_a",
    )(*args)


def _in_proj_b(x2d, seq, tm, sub, g_mix, w_bf, g_qc, mem_kv):
    rows, d = x2d.shape
    n_in = w_bf.shape[1]
    tiles_per_seq = seq // tm
    fuse_cross = mem_kv is not None
    in_specs = [pl.BlockSpec((tm, d), lambda i: (i, 0)), _full((1, d)), _full((d, n_in))]
    args = [x2d, g_mix.reshape(1, d), w_bf]
    if fuse_cross:
        in_specs += [_full((1, HEAD_DIM)),
                     pl.BlockSpec((1,) + mem_kv.shape[1:], lambda i: (i // tiles_per_seq, 0, 0))]
        args += [g_qc.reshape(1, HEAD_DIM), mem_kv]
    return pl.pallas_call(
        functools.partial(_in_proj_b_kernel, fuse_cross=fuse_cross, sub=sub),
        grid=(rows // tm,),
        in_specs=in_specs,
        out_specs=[pl.BlockSpec((2, tm // sub, sub * S5_HALF), lambda i: (0, i, 0)),
                   pl.BlockSpec((tm, ATT_WIDTH), lambda i: (i, 0))],
        out_shape=[jax.ShapeDtypeStruct((2, rows // sub, sub * S5_HALF), BF16),
                   jax.ShapeDtypeStruct((rows, ATT_WIDTH), BF16 if fuse_cross else F32)],
        scratch_shapes=[pltpu.VMEM((ATT_WIDTH // HEAD_DIM, tm, HEAD_DIM), F32)],
        compiler_params=_params("arbitrary"),
        name="in_proj_b",
    )(*args)


def _cross_sample_kernel(qc_ref, gq_ref, mem_ref, out_ref):
    outs = _cross_heads(qc_ref[0], mem_ref, gq_ref[...])
    for hd in range(N_HEADS):
        out_ref[0, :, hd * HEAD_DIM:(hd + 1) * HEAD_DIM] = outs[hd].astype(out_ref.dtype)


def _cross_sample(qc, g_qc, mem_kv):
    nb, tq, _ = qc.shape
    return pl.pallas_call(
        _cross_sample_kernel,
        grid=(nb,),
        in_specs=[pl.BlockSpec((1, tq, ATT_WIDTH), lambda b: (b, 0, 0)),
                  _full((1, HEAD_DIM)),
                  pl.BlockSpec((1,) + mem_kv.shape[1:], lambda b: (b, 0, 0))],
        out_specs=pl.BlockSpec((1, tq, ATT_WIDTH), lambda b: (b, 0, 0)),
        out_shape=jax.ShapeDtypeStruct((nb, tq, ATT_WIDTH), BF16),
        compiler_params=_params("arbitrary"),
        name="cross_sample",
    )(qc, g_qc.reshape(1, HEAD_DIM), mem_kv)


INFLIGHT = 4


def _band_block(q, kc, vc, kp, vp):
    qi = lax.broadcasted_iota(jnp.int32, (BLOCK, BLOCK), 0)
    kj = lax.broadcasted_iota(jnp.int32, (BLOCK, BLOCK), 1)
    sc = jnp.where(kj <= qi, _dot_nt(q, kc) * SCALE, NEG)
    m = jnp.max(sc, axis=-1, keepdims=True)
    if kp is not None:
        sp = jnp.where(kj >= qi, _dot_nt(q, kp) * SCALE, NEG)
        m = jnp.maximum(m, jnp.max(sp, axis=-1, keepdims=True))
    pc = jnp.exp(sc - m)
    l = jnp.sum(pc, axis=-1, keepdims=True)
    o = _dot(pc.astype(BF16), vc)
    if kp is not None:
        pp = jnp.exp(sp - m)
        l = l + jnp.sum(pp, axis=-1, keepdims=True)
        o = o + _dot(pp.astype(BF16), vp)
    return o / l, m + jnp.log(l)


def _attn_prompt_kernel(q0, q1, q2, k0, k1, k2, v0, v1, v2, out_ref, o_scr, l_scr, *, seq):
    qkv_refs = ((q0, k0, v0), (q1, k1, v1), (q2, k2, v2))
    for g, (_, r) in enumerate(DIL_GROUPS):
        q_ref, k_ref, v_ref = qkv_refs[g]
        nblk = seq // r // BLOCK

        def store(rho, blk, o, lse, g=g, r=r):
            start = blk * (BLOCK * r) + rho
            if r == 1:
                idx = pl.ds(pl.multiple_of(start, BLOCK), BLOCK)
            else:
                idx = pl.ds(start, BLOCK, stride=r)
            o_scr[g, idx, :] = o
            l_scr[g, idx, :] = jnp.broadcast_to(lse, (BLOCK, HEAD_DIM))

        def first(rho, q_ref=q_ref, k_ref=k_ref, v_ref=v_ref, store=store):
            lanes = slice(rho * HEAD_DIM, (rho + 1) * HEAD_DIM)
            o, lse = _band_block(q_ref[0, 0, :BLOCK, lanes], k_ref[0, 0, :BLOCK, lanes],
                                 v_ref[0, 0, :BLOCK, lanes], None, None)
            store(rho, 0, o, lse)

        def later(rho, blk, q_ref=q_ref, k_ref=k_ref, v_ref=v_ref, store=store):
            lanes = slice(rho * HEAD_DIM, (rho + 1) * HEAD_DIM)
            cur = pl.ds(pl.multiple_of(blk * BLOCK, BLOCK), BLOCK)
            prev = pl.ds(pl.multiple_of((blk - 1) * BLOCK, BLOCK), BLOCK)
            o, lse = _band_block(q_ref[0, 0, cur, lanes], k_ref[0, 0, cur, lanes],
                                 v_ref[0, 0, cur, lanes], k_ref[0, 0, prev, lanes],
                                 v_ref[0, 0, prev, lanes])
            store(rho, blk, o, lse)

        for rho0 in range(0, r, INFLIGHT):
            rhos = range(rho0, min(r, rho0 + INFLIGHT))
            for rho in rhos:
                first(rho)

            def body(blk, carry, rhos=rhos, later=later):
                for rho in rhos:
                    later(rho, blk)
                return carry
            if nblk > 1:
                lax.fori_loop(1, nblk, body, 0, unroll=max(1, INFLIGHT // len(rhos)))

    def combine(c, carry):
        rows = pl.ds(pl.multiple_of(c * BLOCK, BLOCK), BLOCK)
        ls = [l_scr[g, rows, :] for g in range(N_GROUPS)]
        m = jnp.maximum(jnp.maximum(ls[0], ls[1]), ls[2])
        es = [jnp.exp(l - m) for l in ls]
        num = es[0] * o_scr[0, rows, :] + es[1] * o_scr[1, rows, :] + es[2] * o_scr[2, rows, :]
        out_ref[0, rows, :] = (num / (es[0] + es[1] + es[2])).astype(out_ref.dtype)
        return carry
    lax.fori_loop(0, seq // BLOCK, combine, 0)


def _attn_prompt(qkv_groups, n_seq, seq):
    in_specs, args = [], []
    for role in range(3):
        for g, (_, r) in enumerate(DIL_GROUPS):
            in_specs.append(pl.BlockSpec((1, 1, seq // r, r * HEAD_DIM),
                                         lambda b, h, role=role: (role * N_HEADS + h, b, 0, 0)))
            args.append(qkv_groups[g])
    return pl.pallas_call(
        functools.partial(_attn_prompt_kernel, seq=seq),
        grid=(n_seq, N_HEADS),
        in_specs=in_specs,
        out_specs=pl.BlockSpec((1, seq, HEAD_DIM), lambda b, h: (b, 0, h)),
        out_shape=jax.ShapeDtypeStruct((n_seq, seq, ATT_WIDTH), BF16),
        scratch_shapes=[pltpu.VMEM((N_GROUPS, seq, HEAD_DIM), F32),
                        pltpu.VMEM((N_GROUPS, seq, HEAD_DIM), F32)],
        compiler_params=_params("arbitrary", "arbitrary"),
        name="attn_prompt",
    )(*args)


def _attn_sample_kernel(q_ref, new_ref, old_ref, o_ref, lse_ref, *, r, t_new, lb, compact):
    tq = q_ref.shape[1]
    if compact:
        n_tok = lb // r
        pieces = [new_ref[:, u * KV_ROWS:(u + 1) * KV_ROWS, :].reshape(n_tok * KV_ROWS, HEAD_DIM).astype(BF16)
                  for u in range(t_new)]
        first_idx = [r - t_new + u for u in range(t_new)]
        step = r
    else:
        n_tok = lb
        pieces = [new_ref[...].astype(BF16)]
        first_idx = [0]
        step = 1
    col = lax.broadcasted_iota(jnp.int32, (tq, n_tok * KV_ROWS), 1)
    t_q = lax.broadcasted_iota(jnp.int32, (tq, n_tok * KV_ROWS), 0)
    col_row = col % KV_ROWS
    dists = [lb - t_new + t_q - (f + (col // KV_ROWS) * step) for f in first_idx]
    in_reach = [(d >= 0) & (d % r == 0) & (d <= r * DIL_SPAN) for d in dists]
    t_col = lax.broadcasted_iota(jnp.int32, (tq, 1), 0)
    for hd in range(N_HEADS):
        sl = slice(hd * HEAD_DIM, (hd + 1) * HEAD_DIM)
        q = q_ref[0, :, sl]
        qf = q.astype(F32)
        scores = [jnp.where(ok & (col_row == hd), _dot_nt(q, x) * SCALE, NEG) for ok, x in zip(in_reach, pieces)]
        old_s, old_v = [], []
        for j in range(t_new):
            dist = lb + t_col - j
            ok = (dist % r == 0) & (dist <= r * DIL_SPAN)
            k_j = old_ref[j * KV_ROWS + hd:j * KV_ROWS + hd + 1, :]
            old_s.append(jnp.where(ok, jnp.sum(qf * k_j, axis=-1, keepdims=True) * SCALE, NEG))
            old_v.append(old_ref[j * KV_ROWS + N_HEADS + hd:j * KV_ROWS + N_HEADS + hd + 1, :])
        m = functools.reduce(jnp.maximum, [jnp.max(s, axis=-1, keepdims=True) for s in scores] + old_s)
        ps = [jnp.exp(s - m) for s in scores]
        old_p = [jnp.exp(s - m) for s in old_s]
        l = sum(jnp.sum(p, axis=-1, keepdims=True) for p in ps) + sum(old_p)
        o = (sum(_dot(pltpu.roll(p, N_HEADS, 1).astype(BF16), x) for p, x in zip(ps, pieces))
             + sum(p * v for p, v in zip(old_p, old_v)))
        o_ref[0, :, sl] = o / l
        lse_ref[0, :, sl] = jnp.broadcast_to(m + jnp.log(l), (tq, HEAD_DIM))


def _attn_sample(q, new_buf, old_buf, r, t_new):
    nb, tq, _ = q.shape
    lb = new_buf.shape[1] // KV_ROWS
    compact = r % t_new == 0
    if compact:
        new_view = new_buf.reshape(nb, lb // r, r // t_new, t_new * KV_ROWS, HEAD_DIM)
        new_spec = pl.BlockSpec((None, lb // r, None, t_new * KV_ROWS, HEAD_DIM),
                                lambda b: (b, 0, r // t_new - 1, 0, 0))
    else:
        new_view = new_buf
        new_spec = pl.BlockSpec((None, lb * KV_ROWS, HEAD_DIM), lambda b: (b, 0, 0))
    return pl.pallas_call(
        functools.partial(_attn_sample_kernel, r=r, t_new=t_new, lb=lb, compact=compact),
        grid=(nb,),
        in_specs=[pl.BlockSpec((1, tq, ATT_WIDTH), lambda b: (b, 0, 0)),
                  new_spec,
                  pl.BlockSpec((None, t_new * KV_ROWS, HEAD_DIM), lambda b: (b, 0, 0))],
        out_specs=[pl.BlockSpec((1, tq, ATT_WIDTH), lambda b: (b, 0, 0)),
                   pl.BlockSpec((1, tq, ATT_WIDTH), lambda b: (b, 0, 0))],
        out_shape=[jax.ShapeDtypeStruct((nb, tq, ATT_WIDTH), F32),
                   jax.ShapeDtypeStruct((nb, tq, ATT_WIDTH), F32)],
        compiler_params=_params("arbitrary"),
        name=f"attn_sample_r{r}",
    )(q, new_view, old_buf)


def _combine_kernel(o0, o1, o2, l0, l1, l2, out_ref):
    ls = [l0[...], l1[...], l2[...]]
    m = jnp.maximum(jnp.maximum(ls[0], ls[1]), ls[2])
    es = [jnp.exp(l - m) for l in ls]
    num = es[0] * o0[...] + es[1] * o1[...] + es[2] * o2[...]
    out_ref[...] = (num / (es[0] + es[1] + es[2])).astype(out_ref.dtype)


def _combine_groups(outs, lses):
    shape = outs[0].shape
    return pl.pallas_call(
        _combine_kernel,
        in_specs=[_full(shape)] * 6,
        out_specs=_full(shape),
        out_shape=jax.ShapeDtypeStruct(shape, BF16),
        grid=(1,),
        compiler_params=_params("arbitrary"),
        name="combine_groups",
    )(*outs, *lses)


FF_CHUNK = 256
CARRY_ROWS = 8
SHIFT_SLOTS = 4


def _out_ffn_kernel(*refs, seq_len, tiles_per_seq, tail_rows):
    if seq_len is None:
        (x_ref, mix_ref, cr_ref, wo_ref, g_ref, wup_ref, cw_ref, cb_ref, wdn_ref,
         out_ref, tail_ref, shift, carry) = refs
    else:
        (x_ref, mix_ref, cr_ref, wo_ref, g_ref, wup_ref, cw_ref, cb_ref, wdn_ref, e1_ref, e2_ref,
         out_ref, tail_ref, shift) = refs
    tm = x_ref.shape[0]
    d_ff = wdn_ref.shape[0]
    x1 = (x_ref[...] + _dot(mix_ref[...], wo_ref[:ATT_WIDTH, :])
          + _dot(cr_ref[...], wo_ref[ATT_WIDTH:, :]))
    h = _rms(x1, g_ref[...]).astype(BF16)
    if seq_len is None:
        @pl.when(pl.program_id(0) % tiles_per_seq == 0)
        def _():
            carry[...] = jnp.zeros_like(carry)
    else:
        t = lax.broadcasted_iota(jnp.int32, (tm, 1), 0) % seq_len
        has1 = t >= 1
        has2 = t >= 2

    def chunk_cols(j):
        return (slice(j * FF_CHUNK, (j + 1) * FF_CHUNK),
                slice(d_ff + j * FF_CHUNK, d_ff + (j + 1) * FF_CHUNK))

    def up_proj(j):
        return tuple(_dot(h, wup_ref[:, cols]) for cols in chunk_cols(j))

    def conv(up, cols, slot):
        buf = shift.at[slot]
        if seq_len is None:
            buf[:CARRY_ROWS, :] = carry[:, cols]
            carry[:, cols] = up[tm - CARRY_ROWS:, :]
        else:
            buf[:CARRY_ROWS, :] = jnp.zeros((CARRY_ROWS, FF_CHUNK), F32)
        buf[CARRY_ROWS:, :] = up
        tail_ref[0, :, cols] = up[tm - tail_rows:, :]
        prev1 = buf[CARRY_ROWS - 1:CARRY_ROWS - 1 + tm, :]
        prev2 = buf[CARRY_ROWS - 2:CARRY_ROWS - 2 + tm, :]
        if seq_len is not None:
            prev1 = jnp.where(has1, prev1, e1_ref[:, cols])
            prev2 = jnp.where(has2, prev2, e2_ref[:, cols])
        return (cb_ref[:, cols] + cw_ref[0:1, cols] * prev2 + cw_ref[1:2, cols] * prev1
                + cw_ref[2:3, cols] * up)

    acc = jnp.zeros_like(x1)
    n_chunks = d_ff // FF_CHUNK
    ups = up_proj(0)
    act = None
    for j in range(n_chunks + 1):
        nxt = up_proj(j + 1) if j + 1 < n_chunks else None
        if act is not None:
            acc = acc + _dot(act, wdn_ref[(j - 1) * FF_CHUNK:j * FF_CHUNK, :])
        if j < n_chunks:
            a, b = (conv(up, cols, 2 * (j % 2) + s) for s, (up, cols) in enumerate(zip(ups, chunk_cols(j))))
            act = (a * jax.nn.sigmoid(a) * b).astype(BF16)
        ups = nxt
    out_ref[...] = x1 + acc


def _out_ffn(x2d, mix, cross, w_out_bf, g_ffn, w_up_bf, conv_w, conv_b, w_down_bf, tm,
             tiles_per_seq=None, seq_len=None, e1=None, e2=None, tail_rows=CARRY_ROWS):
    rows, d = x2d.shape
    d_ff = w_down_bf.shape[0]
    n_tiles = rows // tm
    in_specs = [
        pl.BlockSpec((tm, d), lambda i: (i, 0)),
        pl.BlockSpec((tm, ATT_WIDTH), lambda i: (i, 0)),
        pl.BlockSpec((tm, ATT_WIDTH), lambda i: (i, 0)),
        _full(w_out_bf.shape), _full((1, d)), _full(w_up_bf.shape),
        _full((CONV_W, 2 * d_ff)), _full((1, 2 * d_ff)), _full(w_down_bf.shape),
    ]
    args = [x2d, mix, cross, w_out_bf, g_ffn.reshape(1, d), w_up_bf, conv_w, conv_b.reshape(1, 2 * d_ff),
            w_down_bf]
    scratch = [pltpu.VMEM((SHIFT_SLOTS, CARRY_ROWS + tm, FF_CHUNK), F32)]
    if seq_len is None:
        scratch += [pltpu.VMEM((CARRY_ROWS, 2 * d_ff), F32)]
    else:
        in_specs += [pl.BlockSpec((tm, 2 * d_ff), lambda i: (i, 0))] * 2
        args += [e1, e2]
    return pl.pallas_call(
        functools.partial(_out_ffn_kernel, seq_len=seq_len, tiles_per_seq=tiles_per_seq,
                          tail_rows=tail_rows),
        grid=(n_tiles,),
        in_specs=in_specs,
        out_specs=[pl.BlockSpec((tm, d), lambda i: (i, 0)),
                   pl.BlockSpec((1, tail_rows, 2 * d_ff), lambda i: (i, 0, 0))],
        out_shape=[jax.ShapeDtypeStruct((rows, d), F32),
                   jax.ShapeDtypeStruct((n_tiles, tail_rows, 2 * d_ff), F32)],
        scratch_shapes=scratch,
        compiler_params=_params("arbitrary"),
        name="out_ffn",
    )(*args)


def _swap_halves(x):
    return pltpu.roll(x, S5_STATE, 1)


def _s5_prep_kernel(lam_ref, logdt_ref, bt_ref, c_ref, d_ref, e_ref, cl_ref, toep_ref, lam_out_ref, *, sub):
    gc = lam_ref.shape[0]
    lane = lax.broadcasted_iota(jnp.int32, lam_ref.shape, 1)
    first = lane < S5_STATE
    sign = jnp.where(first, -1.0, 1.0)
    a = lam_ref[...]
    a_sw = _swap_halves(a)
    are = jnp.where(first, a, a_sw)
    aim = jnp.where(first, a_sw, a)
    dt = jnp.exp(logdt_ref[...])
    mag = jnp.exp(are * dt)
    lr = mag * jnp.cos(aim * dt)
    li = mag * jnp.sin(aim * dt)
    den = are * are + aim * aim
    xr = lr - 1.0
    f_re = (xr * are + li * aim) / den
    f_im = (li * are - xr * aim) / den
    lb = sign * li

    def cmul(x, m_re, m_sw):
        return x * m_re + _swap_halves(x) * m_sw

    c = c_ref[...]
    c_neg = c * -sign
    ri = lax.broadcasted_iota(jnp.int32, (gc, gc), 0)
    ci = lax.broadcasted_iota(jnp.int32, (gc, gc), 1)
    same_group = (ri // S5_GROUP) == (ci // S5_GROUP)
    e = cmul(bt_ref[...], f_re, sign * f_im)
    cl = c
    pw = jnp.where(first, 1.0, 0.0)
    for k in range(sub):
        kk = sub - 1 - k
        e_ref[kk] = e
        kmat = jnp.where(same_group, _dot_nt(e, c_neg, precision=HIGHEST), 0.0)
        if k == 0:
            kmat = kmat + jnp.where(ri == ci, d_ref[...], 0.0)
        for half in range(2):
            sl = slice(half * S5_HALF, (half + 1) * S5_HALF)
            toep_ref[half, kk * S5_HALF:(kk + 1) * S5_HALF, :] = kmat[sl, sl].astype(toep_ref.dtype)
        cl = cmul(cl, lr, lb)
        cl_ref[k] = cl * -sign
        e = cmul(e, lr, lb)
        pw = cmul(pw, lr, lb)
    lam_out_ref[...] = pw


def _expand_kernel(tab_ref, out_ref):
    out_ref[...] = jnp.zeros_like(out_ref)
    for g in range(S5_HALF // S5_GROUP):
        rows = slice(g * S5_GROUP, (g + 1) * S5_GROUP)
        out_ref[0, rows, g * 2 * S5_STATE:(g + 1) * 2 * S5_STATE] = tab_ref[0, rows, :].astype(out_ref.dtype)


def _expand(tab, sub):
    n_state = (S5_HALF // S5_GROUP) * 2 * S5_STATE
    return pl.pallas_call(
        _expand_kernel,
        grid=(2, sub),
        in_specs=[pl.BlockSpec((1, S5_HALF, 2 * S5_STATE), lambda h, t: (t, h, 0))],
        out_specs=pl.BlockSpec((1, S5_HALF, n_state), lambda h, t: (h, t, 0)),
        out_shape=jax.ShapeDtypeStruct((2, sub * S5_HALF, n_state), BF16),
        compiler_params=_params("arbitrary", "arbitrary"),
        name="s5_expand",
    )(tab)


def _s5_prep(lam_re, lam_im, log_dt, b_re, b_im, c_re, c_im, d_skip, sub):
    n_groups = lam_re.shape[0]
    gc = n_groups * S5_GROUP
    rep = lambda t: jnp.repeat(t, S5_GROUP, axis=0)
    lam_p = rep(jnp.concatenate([lam_re, lam_im], axis=-1))
    logdt = rep(log_dt.reshape(n_groups, 1))
    bt_p = jnp.concatenate([b_re.transpose(0, 2, 1), b_im.transpose(0, 2, 1)], axis=-1).reshape(gc, 2 * S5_STATE)
    c_p = jnp.concatenate([c_re, c_im], axis=-1).reshape(gc, 2 * S5_STATE)
    tab_shape = (sub, gc, 2 * S5_STATE)
    e_tab, cl_tab, toep, lam_sub = pl.pallas_call(
        functools.partial(_s5_prep_kernel, sub=sub),
        grid=(1,),
        in_specs=[_full(lam_p.shape), _full(logdt.shape), _full(bt_p.shape), _full(c_p.shape), _full((gc, 1))],
        out_specs=[_full(tab_shape), _full(tab_shape), _full((2, sub * S5_HALF, S5_HALF)),
                   _full((gc, 2 * S5_STATE))],
        out_shape=[jax.ShapeDtypeStruct(tab_shape, F32), jax.ShapeDtypeStruct(tab_shape, F32),
                   jax.ShapeDtypeStruct((2, sub * S5_HALF, S5_HALF), BF16),
                   jax.ShapeDtypeStruct((gc, 2 * S5_STATE), F32)],
        compiler_params=_params("arbitrary"),
        name="s5_prep",
    )(lam_p, logdt, bt_p, c_p, d_skip.reshape(gc, 1))
    lam_sub = lam_sub[::S5_GROUP].reshape(2, n_groups // 2, 2 * S5_STATE)
    return toep, _expand(e_tab, sub), _expand(cl_tab, sub), lam_sub


def _s5_toep_kernel(v_ref, w_ref, y_ref, *, sub):
    for tau in range(sub):
        y_ref[0, :, tau * S5_HALF:(tau + 1) * S5_HALF] = _dot(
            v_ref[0, :, :(tau + 1) * S5_HALF], w_ref[0, (sub - 1 - tau) * S5_HALF:, :])


def _matmul_kernel(a_ref, b_ref, o_ref):
    o_ref[0] = _dot(a_ref[0], b_ref[0])


def _matmul_nt_add_kernel(a_ref, b_ref, y_ref, o_ref):
    o_ref[0] = y_ref[0] + _dot_nt(a_ref[0].astype(BF16), b_ref[0])


def _s5_scan_kernel(s_ref, lam_ref, x0_ref, xprev_ref, xfin_ref, *, n_seq, n_sub):
    lane = lax.broadcasted_iota(jnp.int32, lam_ref.shape[1:], 1)
    first = lane < S5_STATE
    lam = lam_ref[0]
    lam_sw = _swap_halves(lam)
    m_re = jnp.where(first, lam, lam_sw)
    m_sw = jnp.where(first, -lam_sw, lam)
    x_init = tuple(x0_ref[0, b] for b in range(n_seq))
    xs_init = tuple(_swap_halves(x) for x in x_init)

    def body(n, carry):
        xs, xss = carry
        new_x, new_xs = [], []
        for b in range(n_seq):
            row = b * n_sub + n
            xprev_ref[0, row] = xs[b]
            s = s_ref[0, row]
            new_x.append(xs[b] * m_re + xss[b] * m_sw + s)
            new_xs.append(xss[b] * m_re - xs[b] * m_sw + _swap_halves(s))
        return tuple(new_x), tuple(new_xs)
    fin, _ = lax.fori_loop(0, n_sub, body, (x_init, xs_init))
    for b in range(n_seq):
        xfin_ref[0, b] = fin[b]


def _glu_kernel(y_ref, w_ref, b_ref, o_ref, stage, *, sub):
    tmv = y_ref.shape[1]
    n_cb = ATT_WIDTH // HEAD_DIM
    for cb in range(n_cb):
        half, off = divmod(cb * HEAD_DIM, S5_HALF)
        for tau in range(sub):
            stage[cb, pl.ds(tau, tmv, stride=sub), :] = (
                y_ref[half, :, tau * S5_HALF + off:tau * S5_HALF + off + HEAD_DIM])
    y = jax.nn.gelu(jnp.concatenate([stage[cb] for cb in range(n_cb)], axis=-1))
    z = _dot(y.astype(BF16), w_ref[...]) + b_ref[...]
    o_ref[...] = (y * jax.nn.sigmoid(z)).astype(o_ref.dtype)


def _s5_mixer(v, n_seq, seq, sub, x0, toep, b_all, c_all_t, lam_sub, w_glu_bf, b_glu):
    _, m, width = v.shape
    rows = m * sub
    n_sub = seq // sub
    n_state = b_all.shape[2]
    gh = n_state // (2 * S5_STATE)
    tms = min(m, 256)
    y_intra = pl.pallas_call(
        functools.partial(_s5_toep_kernel, sub=sub),
        grid=(2, m // tms),
        in_specs=[pl.BlockSpec((1, tms, width), lambda h, i: (h, i, 0)),
                  pl.BlockSpec((1, width, S5_HALF), lambda h, i: (h, 0, 0))],
        out_specs=pl.BlockSpec((1, tms, width), lambda h, i: (h, i, 0)),
        out_shape=jax.ShapeDtypeStruct((2, m, width), F32),
        compiler_params=_params("arbitrary", "arbitrary"),
        name="s5_toeplitz",
    )(v, toep)
    nc = 512
    s_loc = pl.pallas_call(
        _matmul_kernel,
        grid=(2, n_state // nc, m // tms),
        in_specs=[pl.BlockSpec((1, tms, width), lambda h, j, i: (h, i, 0)),
                  pl.BlockSpec((1, width, nc), lambda h, j, i: (h, 0, j))],
        out_specs=pl.BlockSpec((1, tms, nc), lambda h, j, i: (h, i, j)),
        out_shape=jax.ShapeDtypeStruct((2, m, n_state), F32),
        compiler_params=_params("arbitrary", "arbitrary", "arbitrary"),
        name="s5_state_in",
    )(v, b_all)
    st_shape = (2, m, gh, 2 * S5_STATE)
    x_prev, x_fin = pl.pallas_call(
        functools.partial(_s5_scan_kernel, n_seq=n_seq, n_sub=n_sub),
        grid=(2,),
        in_specs=[pl.BlockSpec((1,) + st_shape[1:], lambda h: (h, 0, 0, 0)),
                  pl.BlockSpec((1, gh, 2 * S5_STATE), lambda h: (h, 0, 0)),
                  pl.BlockSpec((1, n_seq, gh, 2 * S5_STATE), lambda h: (h, 0, 0, 0))],
        out_specs=[pl.BlockSpec((1,) + st_shape[1:], lambda h: (h, 0, 0, 0)),
                   pl.BlockSpec((1, n_seq, gh, 2 * S5_STATE), lambda h: (h, 0, 0, 0))],
        out_shape=[jax.ShapeDtypeStruct(st_shape, F32),
                   jax.ShapeDtypeStruct((2, n_seq, gh, 2 * S5_STATE), F32)],
        compiler_params=_params("arbitrary"),
        name="s5_scan",
    )(s_loc.reshape(st_shape), lam_sub, x0)
    wc = min(width, 512)
    y = pl.pallas_call(
        _matmul_nt_add_kernel,
        grid=(2, width // wc, m // tms),
        in_specs=[pl.BlockSpec((1, tms, n_state), lambda h, j, i: (h, i, 0)),
                  pl.BlockSpec((1, wc, n_state), lambda h, j, i: (h, j, 0)),
                  pl.BlockSpec((1, tms, wc), lambda h, j, i: (h, i, j))],
        out_specs=pl.BlockSpec((1, tms, wc), lambda h, j, i: (h, i, j)),
        out_shape=jax.ShapeDtypeStruct((2, m, width), F32),
        compiler_params=_params("arbitrary", "arbitrary", "arbitrary"),
        name="s5_state_out",
    )(x_prev.reshape(2, m, n_state), c_all_t, y_intra)
    tm = min(rows, 512)
    mix = pl.pallas_call(
        functools.partial(_glu_kernel, sub=sub),
        grid=(rows // tm,),
        in_specs=[pl.BlockSpec((2, tm // sub, width), lambda i: (0, i, 0)),
                  _full(w_glu_bf.shape), _full((1, ATT_WIDTH))],
        out_specs=pl.BlockSpec((tm, ATT_WIDTH), lambda i: (i, 0)),
        out_shape=jax.ShapeDtypeStruct((rows, ATT_WIDTH), BF16),
        scratch_shapes=[pltpu.VMEM((ATT_WIDTH // HEAD_DIM, tm, HEAD_DIM), F32)],
        compiler_params=_params("arbitrary"),
        name="s5_glu",
    )(y, w_glu_bf, b_glu.reshape(1, ATT_WIDTH))
    return mix, x_fin


def _pack_state(s):
    n = s.shape[0]
    gh = s.shape[2] // 2
    return s.reshape(n, 2, 2, gh, S5_STATE).transpose(2, 0, 3, 1, 4).reshape(2, n, gh, 2 * S5_STATE)


def _unpack_state(x):
    _, n, gh, _ = x.shape
    return x.reshape(2, n, gh, 2, S5_STATE).transpose(1, 3, 0, 2, 4).reshape(n, 2, 2 * gh, S5_STATE)


PROMPT_TILE = 512
PROMPT_SUB = 16
Q_PAD = 16


def _pad_rows(t, n):
    return jnp.pad(t, ((0, 0), (0, n - t.shape[1]), (0, 0)))


def kernel(x_prompt, x_sample, cache_win0_kv, cache_win1_kv, cache_win2_kv, cache_mem_kv, state_s5,
           state_ffn_conv, mem_prompt, g_mix, g_ffn, w_in_a, g_q_dil, g_k_dil, w_in_b, s5_lam_re,
           s5_lam_im, s5_log_dt, s5_b_re, s5_b_im, s5_c_re, s5_c_im, s5_d, w_glu, b_glu, g_mem,
           w_mem_kv, g_q_cross, g_k_cross, w_out, w_up, conv_w, conv_b, w_down):
    nb, seq, d = x_prompt.shape
    db, ts, _ = x_sample.shape
    depth = g_mix.shape[0]
    n_mem = mem_prompt.shape[1]
    d_ff2 = w_up.shape[2]
    assert ts >= CONV_W - 1 and seq % PROMPT_TILE == 0 and ts <= Q_PAD
    caches = (cache_win0_kv, cache_win1_kv, cache_win2_kv)
    w_in_a_bf, w_in_b_bf, w_out_bf = w_in_a.astype(BF16), w_in_b.astype(BF16), w_out.astype(BF16)
    w_up_bf, w_down_bf, w_glu_bf = w_up.astype(BF16), w_down.astype(BF16), w_glu.astype(BF16)
    w_mem_bf = w_mem_kv.astype(BF16)

    tab_p = _rope_tables(jnp.arange(seq, dtype=jnp.int32))
    tab_s = tuple(jnp.tile(t, (db, 1)) for t in _rope_tables(PAST_LEN + jnp.arange(ts, dtype=jnp.int32)))
    win_keep = tuple(min(w, seq) for w, _ in DIL_GROUPS)
    rows_s = db * ts
    kv_tail = (2, N_HEADS, HEAD_DIM)

    mem_p = _mem_kv(mem_prompt.reshape(nb * n_mem, d), g_mem, w_mem_bf, g_k_cross)

    xp = x_prompt.reshape(nb * seq, d)
    xs = x_sample.reshape(rows_s, d)
    p_win, s_win = [[] for _ in DIL_GROUPS], [[] for _ in DIL_GROUPS]
    p_s5, s_s5, p_conv, s_conv = [], [], [], []
    tiles_per_seq = seq // PROMPT_TILE
    for i in range(depth):
        mem_i = mem_p[i].reshape(nb, n_mem * KV_ROWS, HEAD_DIM)
        mem_s = cache_mem_kv[i].reshape(db, n_mem * KV_ROWS, HEAD_DIM)
        if i % 2 == 0:
            ia = i // 2
            qkv_s, *kv_new, qc_s = _in_proj_a(
                xs, rows_s, rows_s, g_mix[i], w_in_a_bf[ia], tab_s, g_q_dil[ia], g_k_dil[ia],
                None, None, (rows_s,) * N_GROUPS, dilated=False)
            old_bufs = [c[ia].reshape(db, -1, HEAD_DIM) for c in caches]
            kv_new = [k.reshape(db, ts * KV_ROWS, HEAD_DIM) for k in kv_new]
            *qkv_groups, w0, w1, w2, cross_p, nb0, nb1, nb2 = _in_proj_a(
                xp, seq, PROMPT_TILE, g_mix[i], w_in_a_bf[ia], tab_p, g_q_dil[ia], g_k_dil[ia],
                g_q_cross[i], mem_i, win_keep, dilated=True, moves=(old_bufs, kv_new))
            for g, w in enumerate((w0, w1, w2)):
                p_win[g].append(w.reshape((nb, win_keep[g]) + kv_tail))
            mix_p = _attn_prompt(qkv_groups, nb, seq).reshape(nb * seq, ATT_WIDTH)
            outs, lses = [], []
            for g, ((_, r), new_g) in enumerate(zip(DIL_GROUPS, (nb0, nb1, nb2))):
                q_g = qkv_s[g * N_HEADS:(g + 1) * N_HEADS]
                q_g = _pad_rows(q_g.transpose(1, 0, 2).reshape(db, ts, ATT_WIDTH), Q_PAD)
                o_g, lse_g = _attn_sample(q_g, new_g, old_bufs[g], r, ts)
                outs.append(o_g[:, :ts].reshape(rows_s, ATT_WIDTH))
                lses.append(lse_g[:, :ts].reshape(rows_s, ATT_WIDTH))
                s_win[g].append(new_g.reshape(caches[g][ia].shape))
            mix_s = _combine_groups(outs, lses)
        else:
            ib = i // 2
            prm = (s5_lam_re[ib], s5_lam_im[ib], s5_log_dt[ib], s5_b_re[ib], s5_b_im[ib], s5_c_re[ib],
                   s5_c_im[ib], s5_d[ib])
            gh = s5_lam_re.shape[1] // 2
            u_p, cross_p = _in_proj_b(xp, seq, PROMPT_TILE, PROMPT_SUB, g_mix[i], w_in_b_bf[ib],
                                      g_q_cross[i], mem_i)
            x0 = jnp.zeros((2, nb, gh, 2 * S5_STATE), F32)
            mix_p, fin_p = _s5_mixer(u_p, nb, seq, PROMPT_SUB, x0, *_s5_prep(*prm, PROMPT_SUB),
                                     w_glu_bf[ib], b_glu[ib])
            p_s5.append(_unpack_state(fin_p))
            u_s, qc_s = _in_proj_b(xs, rows_s, rows_s, ts, g_mix[i], w_in_b_bf[ib], None, None)
            mix_s, fin_s = _s5_mixer(u_s, db, ts, ts, _pack_state(state_s5[ib]), *_s5_prep(*prm, ts),
                                     w_glu_bf[ib], b_glu[ib])
            s_s5.append(_unpack_state(fin_s))
        cross_s = _cross_sample(_pad_rows(qc_s.reshape(db, ts, ATT_WIDTH), Q_PAD), g_q_cross[i], mem_s)
        cross_s = cross_s[:, :ts].reshape(rows_s, ATT_WIDTH)

        ffn_w = (w_out_bf[i], g_ffn[i], w_up_bf[i], conv_w[i], conv_b[i], w_down_bf[i])
        xp, tails = _out_ffn(xp, mix_p, cross_p, *ffn_w, PROMPT_TILE, tiles_per_seq=tiles_per_seq)
        p_conv.append(tails.reshape(nb, tiles_per_seq, CARRY_ROWS, d_ff2)[:, -1, CARRY_ROWS - (CONV_W - 1):])
        buf = state_ffn_conv[i]
        zero = jnp.zeros((db, ts - 2, d_ff2), F32)
        e1 = jnp.concatenate([buf[:, 1:2], zero, zero[:, :1]], axis=1).reshape(rows_s, d_ff2)
        e2 = jnp.concatenate([buf, zero], axis=1).reshape(rows_s, d_ff2)
        xs, tails = _out_ffn(xs, mix_s, cross_s, *ffn_w, rows_s, seq_len=ts, e1=e1, e2=e2, tail_rows=rows_s)
        s_conv.append(tails.reshape(db, ts, d_ff2)[:, ts - (CONV_W - 1):])

    return (xp.reshape(nb, seq, d), xs.reshape(db, ts, d),
            jnp.stack(p_win[0]), jnp.stack(p_win[1]), jnp.stack(p_win[2]),
            mem_p.reshape((depth, nb, n_mem) + kv_tail),
            jnp.stack(p_s5), jnp.stack(p_conv),
            jnp.stack(s_win[0]), jnp.stack(s_win[1]), jnp.stack(s_win[2]),
            jnp.stack(s_s5), jnp.stack(s_conv))
```

```python
import functools
import math

import jax
import jax.numpy as jnp
from jax import lax
from jax.experimental import pallas as pl
from jax.experimental.pallas import tpu as pltpu

HEAD_DIM = 128
N_HEADS = 4
DIL_GROUPS = ((128, 1), (512, 4), (2048, 16))
N_GROUPS = len(DIL_GROUPS)
DIL_SPAN = 128
BLOCK = 128
ATT_WIDTH = N_HEADS * HEAD_DIM
KV_ROWS = 2 * N_HEADS
ROT_DIM = HEAD_DIM // 4
ROT_HALF = ROT_DIM // 2
ROPE_THETA = 500000.0
S5_GROUP = 16
S5_STATE = 64
S5_HALF = 256
CONV_W = 3
EPS = 1e-6
NEG = -1e30
SCALE = HEAD_DIM ** -0.5
PAST_LEN = 16384

VMEM_LIMIT_V7X = 56 * 1024 * 1024
BF16 = jnp.bfloat16
F32 = jnp.float32
HIGHEST = lax.Precision.HIGHEST


def _params(*sem):
    return pltpu.CompilerParams(dimension_semantics=sem, vmem_limit_bytes=VMEM_LIMIT_V7X)


def _rms(x, g):
    return x * lax.rsqrt(jnp.mean(x * x, axis=-1, keepdims=True) + EPS) * g


def _dot(a, b):
    return jnp.dot(a, b, preferred_element_type=F32)


def _dot_nt(a, b, precision=None):
    return lax.dot_general(a, b, (((1,), (1,)), ((), ())), precision=precision,
                           preferred_element_type=F32)


def _full(shape):
    nd = len(shape)
    return pl.BlockSpec(shape, lambda *_: (0,) * nd)


def _kv_rows(kv, head, n):
    return pl.ds(kv * N_HEADS + head, n, stride=KV_ROWS)


def _mem_kv_kernel(mem_ref, g_ref, w_ref, gk_ref, out_ref):
    tm = mem_ref.shape[0]
    h = _rms(mem_ref[...], g_ref[0]).astype(BF16)
    kv = _dot(h, w_ref[0])
    for hd in range(N_HEADS):
        sl = slice(hd * HEAD_DIM, (hd + 1) * HEAD_DIM)
        out_ref[0, _kv_rows(0, hd, tm), :] = _rms(kv[:, sl], gk_ref[0])
        out_ref[0, _kv_rows(1, hd, tm), :] = kv[:, ATT_WIDTH + hd * HEAD_DIM:ATT_WIDTH + (hd + 1) * HEAD_DIM]


def _mem_kv(mem2d, g_mem, w_kv_bf, g_k):
    depth, d, _ = w_kv_bf.shape
    rows = mem2d.shape[0]
    tm = min(rows, 512)
    return pl.pallas_call(
        _mem_kv_kernel,
        grid=(depth, rows // tm),
        in_specs=[
            pl.BlockSpec((tm, d), lambda l, i: (i, 0)),
            pl.BlockSpec((1, 1, d), lambda l, i: (l, 0, 0)),
            pl.BlockSpec((1, d, 2 * ATT_WIDTH), lambda l, i: (l, 0, 0)),
            pl.BlockSpec((1, 1, HEAD_DIM), lambda l, i: (l, 0, 0)),
        ],
        out_specs=pl.BlockSpec((1, tm * KV_ROWS, HEAD_DIM), lambda l, i: (l, i, 0)),
        out_shape=jax.ShapeDtypeStruct((depth, rows * KV_ROWS, HEAD_DIM), F32),
        compiler_params=_params("arbitrary", "arbitrary"),
        name="mem_kv",
    )(mem2d, g_mem.reshape(depth, 1, d), w_kv_bf, g_k.reshape(depth, 1, HEAD_DIM))


def _cross_heads(qc, mem_ref, gq):
    n_mem = mem_ref.shape[1] // KV_ROWS
    outs = []
    for hd in range(N_HEADS):
        q = _rms(qc[:, hd * HEAD_DIM:(hd + 1) * HEAD_DIM], gq).astype(BF16)
        k = mem_ref[0, _kv_rows(0, hd, n_mem), :].astype(BF16)
        v = mem_ref[0, _kv_rows(1, hd, n_mem), :].astype(BF16)
        s = _dot_nt(q, k) * SCALE
        m = jnp.max(s, axis=-1, keepdims=True)
        p = jnp.exp(s - m)
        l = jnp.sum(p, axis=-1, keepdims=True)
        outs.append(_dot(p.astype(BF16), v) / l)
    return outs


def _rope(x, cos_t, sin_lo, sin_hi):
    return (x * cos_t + pltpu.roll(x, HEAD_DIM - ROT_HALF, 1) * sin_lo
            + pltpu.roll(x, ROT_HALF, 1) * sin_hi)


def _in_proj_a_kernel(*refs, fuse_cross, dilated, win_rows, win_first, tiles_per_seq):
    n_in = 10 if fuse_cross else 8
    x_ref, g_ref, w_ref, cos_ref, slo_ref, shi_ref, gq_ref, gk_ref = refs[:8]
    n_qkv = N_GROUPS if dilated else 1
    qkv_refs = refs[n_in:n_in + n_qkv]
    win_refs = refs[n_in + n_qkv:n_in + n_qkv + N_GROUPS]
    cr_ref = refs[n_in + n_qkv + N_GROUPS]
    kv_stash = refs[n_in + n_qkv + N_GROUPS + 1]
    stage = refs[n_in + n_qkv + N_GROUPS + 2] if dilated else None
    tm = x_ref.shape[0]
    h = _rms(x_ref[...], g_ref[...]).astype(BF16)
    cos_t, sin_lo, sin_hi = cos_ref[...], slo_ref[...], shi_ref[...]
    def proj(c):
        return _dot(h, w_ref[:, c * ATT_WIDTH:(c + 1) * ATT_WIDTH])

    tile = pl.program_id(0) % tiles_per_seq
    in_window = [tile >= first for first in win_first]
    for g in range(N_GROUPS):
        @pl.when(jnp.logical_not(in_window[g]))
        def _(g=g):
            win_refs[g][...] = jnp.zeros_like(win_refs[g])

    y = proj(0)
    for c in range(3 * N_GROUPS):
        y_next = proj(c + 1)
        role, g = divmod(c, N_GROUPS)
        r = DIL_GROUPS[g][1]
        wr = win_rows[g]
        for hd in range(N_HEADS):
            yh = y[:, hd * HEAD_DIM:(hd + 1) * HEAD_DIM]
            if role == 0:
                yh = _rope(_rms(yh, gq_ref[g]), cos_t, sin_lo, sin_hi)
            elif role == 1:
                yh = _rope(_rms(yh, gk_ref[g]), cos_t, sin_lo, sin_hi)
            if role > 0:
                kv_stash[((role - 1) * N_GROUPS + g) * N_HEADS + hd] = yh
            if not dilated:
                qkv_refs[0][role * N_GROUPS * N_HEADS + g * N_HEADS + hd] = yh.astype(BF16)
            elif r == 1:
                qkv_refs[g][role * N_HEADS + hd, 0] = yh.astype(BF16)
            else:
                stage[hd] = yh
                for rho in range(r):
                    qkv_refs[g][role * N_HEADS + hd, 0, :, rho * HEAD_DIM:(rho + 1) * HEAD_DIM] = (
                        stage[hd, pl.ds(rho, tm // r, stride=r), :].astype(BF16))
        y = y_next
    qc = y
    if fuse_cross:
        outs = _cross_heads(qc, refs[9], refs[8][...])
        for hd in range(N_HEADS):
            cr_ref[:, hd * HEAD_DIM:(hd + 1) * HEAD_DIM] = outs[hd].astype(cr_ref.dtype)
    else:
        cr_ref[...] = qc
    for g in range(N_GROUPS):
        @pl.when(in_window[g])
        def _(g=g):
            wr = win_rows[g]
            for kv in range(2):
                for hd in range(N_HEADS):
                    win_refs[g][_kv_rows(kv, hd, wr), :] = kv_stash[(kv * N_GROUPS + g) * N_HEADS + hd,
                                                                    tm - wr:, :]


def _in_proj_b_kernel(*refs, fuse_cross, sub):
    if fuse_cross:
        x_ref, g_ref, w_ref, gqc_ref, mem_ref, u_ref, cr_ref, stage = refs
    else:
        x_ref, g_ref, w_ref, u_ref, cr_ref, stage = refs
    tm = x_ref.shape[0]
    h = _rms(x_ref[...], g_ref[...]).astype(BF16)
    u = _dot(h, w_ref[:, :ATT_WIDTH])
    for cb in range(ATT_WIDTH // HEAD_DIM):
        stage[cb] = u[:, cb * HEAD_DIM:(cb + 1) * HEAD_DIM]
        half, off = divmod(cb * HEAD_DIM, S5_HALF)
        for tau in range(sub):
            u_ref[half, :, tau * S5_HALF + off:tau * S5_HALF + off + HEAD_DIM] = stage[
                cb, pl.ds(tau, tm // sub, stride=sub), :].astype(BF16)
    qc = _dot(h, w_ref[:, ATT_WIDTH:])
    if fuse_cross:
        outs = _cross_heads(qc, mem_ref, gqc_ref[...])
        for hd in range(N_HEADS):
            cr_ref[:, hd * HEAD_DIM:(hd + 1) * HEAD_DIM] = outs[hd].astype(cr_ref.dtype)
    else:
        cr_ref[...] = qc


def _rope_tables(pos):
    inv = jnp.exp(-math.log(ROPE_THETA) * jnp.arange(ROT_HALF, dtype=F32) / ROT_HALF)
    ang = pos.astype(F32)[:, None] * inv[None, :]
    cos, sin = jnp.cos(ang), jnp.sin(ang)
    rows = pos.shape[0]
    ones = jnp.ones((rows, HEAD_DIM - ROT_DIM), F32)
    zeros = jnp.zeros((rows, HEAD_DIM - ROT_DIM), F32)
    z16 = jnp.zeros((rows, ROT_HALF), F32)
    cos_t = jnp.concatenate([cos, cos, ones], axis=1)
    sin_lo = jnp.concatenate([-sin, z16, zeros], axis=1)
    sin_hi = jnp.concatenate([z16, sin, zeros], axis=1)
    return cos_t, sin_lo, sin_hi


def _in_proj_a(x2d, seq, tm, g_mix, w_bf, tables, g_q, g_k, g_qc, mem_kv, win_keep, dilated):
    rows, d = x2d.shape
    n_in = w_bf.shape[1]
    tiles_per_seq = seq // tm
    n_seq = rows // seq
    fuse_cross = mem_kv is not None
    tab_tiles = tables[0].shape[0] // tm
    win_rows = tuple(min(k, tm) for k in win_keep)
    win_first = tuple(tiles_per_seq - k // wr for k, wr in zip(win_keep, win_rows))

    def win_spec(keep):
        wr = min(keep, tm)
        nblk = keep // wr
        first = tiles_per_seq - nblk

        def imap(i):
            b = i // tiles_per_seq
            t = i % tiles_per_seq
            return (b * nblk + jnp.maximum(t - first, 0), 0)
        return pl.BlockSpec((wr * KV_ROWS, HEAD_DIM), imap)

    in_specs = [
        pl.BlockSpec((tm, d), lambda i: (i, 0)),
        _full((1, d)),
        _full((d, n_in)),
        pl.BlockSpec((tm, HEAD_DIM), lambda i: (i % tab_tiles, 0)),
        pl.BlockSpec((tm, HEAD_DIM), lambda i: (i % tab_tiles, 0)),
        pl.BlockSpec((tm, HEAD_DIM), lambda i: (i % tab_tiles, 0)),
        _full((N_GROUPS, HEAD_DIM)),
        _full((N_GROUPS, HEAD_DIM)),
    ]
    args = [x2d, g_mix.reshape(1, d), w_bf, *tables, g_q, g_k]
    if fuse_cross:
        in_specs += [_full((1, HEAD_DIM)),
                     pl.BlockSpec((1,) + mem_kv.shape[1:], lambda i: (i // tiles_per_seq, 0, 0))]
        args += [g_qc.reshape(1, HEAD_DIM), mem_kv]
    if dilated:
        qkv_specs = [pl.BlockSpec((3 * N_HEADS, 1, tm // r, r * HEAD_DIM),
                                  lambda i: (0, i // tiles_per_seq, i % tiles_per_seq, 0))
                     for _, r in DIL_GROUPS]
        qkv_shapes = [jax.ShapeDtypeStruct((3 * N_HEADS, n_seq, seq // r, r * HEAD_DIM), BF16)
                      for _, r in DIL_GROUPS]
        scratch = [pltpu.VMEM((N_HEADS, tm, HEAD_DIM), F32)]
    else:
        qkv_specs = [pl.BlockSpec((3 * N_GROUPS * N_HEADS, tm, HEAD_DIM), lambda i: (0, i, 0))]
        qkv_shapes = [jax.ShapeDtypeStruct((3 * N_GROUPS * N_HEADS, rows, HEAD_DIM), BF16)]
        scratch = []
    scratch = [pltpu.VMEM((2 * N_GROUPS * N_HEADS, tm, HEAD_DIM), F32)] + scratch
    out_specs = [*qkv_specs, *[win_spec(k) for k in win_keep],
                 pl.BlockSpec((tm, ATT_WIDTH), lambda i: (i, 0))]
    out_shape = [*qkv_shapes,
                 *[jax.ShapeDtypeStruct((n_seq * k * KV_ROWS, HEAD_DIM), F32) for k in win_keep],
                 jax.ShapeDtypeStruct((rows, ATT_WIDTH), BF16 if fuse_cross else F32)]
    return pl.pallas_call(
        functools.partial(_in_proj_a_kernel, fuse_cross=fuse_cross, dilated=dilated, win_rows=win_rows,
                          win_first=win_first, tiles_per_seq=tiles_per_seq),
        grid=(rows // tm,),
        in_specs=in_specs, out_specs=out_specs, out_shape=out_shape, scratch_shapes=scratch,
        compiler_params=_params("arbitrary"),
        name="in_proj_a",
    )(*args)


def _in_proj_b(x2d, seq, tm, sub, g_mix, w_bf, g_qc, mem_kv):
    rows, d = x2d.shape
    n_in = w_bf.shape[1]
    tiles_per_seq = seq // tm
    fuse_cross = mem_kv is not None
    in_specs = [pl.BlockSpec((tm, d), lambda i: (i, 0)), _full((1, d)), _full((d, n_in))]
    args = [x2d, g_mix.reshape(1, d), w_bf]
    if fuse_cross:
        in_specs += [_full((1, HEAD_DIM)),
                     pl.BlockSpec((1,) + mem_kv.shape[1:], lambda i: (i // tiles_per_seq, 0, 0))]
        args += [g_qc.reshape(1, HEAD_DIM), mem_kv]
    return pl.pallas_call(
        functools.partial(_in_proj_b_kernel, fuse_cross=fuse_cross, sub=sub),
        grid=(rows // tm,),
        in_specs=in_specs,
        out_specs=[pl.BlockSpec((2, tm // sub, sub * S5_HALF), lambda i: (0, i, 0)),
                   pl.BlockSpec((tm, ATT_WIDTH), lambda i: (i, 0))],
        out_shape=[jax.ShapeDtypeStruct((2, rows // sub, sub * S5_HALF), BF16),
                   jax.ShapeDtypeStruct((rows, ATT_WIDTH), BF16 if fuse_cross else F32)],
        scratch_shapes=[pltpu.VMEM((ATT_WIDTH // HEAD_DIM, tm, HEAD_DIM), F32)],
        compiler_params=_params("arbitrary"),
        name="in_proj_b",
    )(*args)


CROSS_BATCH = 4


def _cross_sample_kernel(qc_ref, gq_ref, mem_ref, out_ref):
    for b in range(qc_ref.shape[0]):
        outs = _cross_heads(qc_ref[b], mem_ref.at[pl.ds(b, 1)], gq_ref[...])
        for hd in range(N_HEADS):
            out_ref[b, :, hd * HEAD_DIM:(hd + 1) * HEAD_DIM] = outs[hd].astype(out_ref.dtype)


def _cross_sample(qc, g_qc, mem_kv, first):
    nb, tq, _ = qc.shape
    cb = math.gcd(nb, CROSS_BATCH)
    assert first % cb == 0
    return pl.pallas_call(
        _cross_sample_kernel,
        grid=(nb // cb,),
        in_specs=[pl.BlockSpec((cb, tq, ATT_WIDTH), lambda b: (b, 0, 0)),
                  _full((1, HEAD_DIM)),
                  pl.BlockSpec((cb,) + mem_kv.shape[1:], lambda b: (first // cb + b, 0, 0))],
        out_specs=pl.BlockSpec((cb, tq, ATT_WIDTH), lambda b: (b, 0, 0)),
        out_shape=jax.ShapeDtypeStruct((nb, tq, ATT_WIDTH), BF16),
        compiler_params=_params("arbitrary"),
        name="cross_sample",
    )(qc, g_qc.reshape(1, HEAD_DIM), mem_kv)


INFLIGHT = 4


def _band_block(q, kc, vc, kp, vp):
    qi = lax.broadcasted_iota(jnp.int32, (BLOCK, BLOCK), 0)
    kj = lax.broadcasted_iota(jnp.int32, (BLOCK, BLOCK), 1)
    sc = jnp.where(kj <= qi, _dot_nt(q, kc) * SCALE, NEG)
    m = jnp.max(sc, axis=-1, keepdims=True)
    if kp is not None:
        sp = jnp.where(kj >= qi, _dot_nt(q, kp) * SCALE, NEG)
        m = jnp.maximum(m, jnp.max(sp, axis=-1, keepdims=True))
    pc = jnp.exp(sc - m)
    l = jnp.sum(pc, axis=-1, keepdims=True)
    o = _dot(pc.astype(BF16), vc)
    if kp is not None:
        pp = jnp.exp(sp - m)
        l = l + jnp.sum(pp, axis=-1, keepdims=True)
        o = o + _dot(pp.astype(BF16), vp)
    return o / l, m + jnp.log(l)


def _attn_prompt_kernel(q0, q1, q2, k0, k1, k2, v0, v1, v2, out_ref, o_scr, l_scr, *, seq):
    qkv_refs = ((q0, k0, v0), (q1, k1, v1), (q2, k2, v2))
    for g, (_, r) in enumerate(DIL_GROUPS):
        q_ref, k_ref, v_ref = qkv_refs[g]
        nblk = seq // r // BLOCK

        def store(rho, blk, o, lse, g=g, r=r):
            start = blk * (BLOCK * r) + rho
            if r == 1:
                idx = pl.ds(pl.multiple_of(start, BLOCK), BLOCK)
            else:
                idx = pl.ds(start, BLOCK, stride=r)
            o_scr[g, idx, :] = o
            l_scr[g, idx, :] = jnp.broadcast_to(lse, (BLOCK, HEAD_DIM))

        def first(rho, q_ref=q_ref, k_ref=k_ref, v_ref=v_ref, store=store):
            lanes = slice(rho * HEAD_DIM, (rho + 1) * HEAD_DIM)
            o, lse = _band_block(q_ref[0, 0, :BLOCK, lanes], k_ref[0, 0, :BLOCK, lanes],
                                 v_ref[0, 0, :BLOCK, lanes], None, None)
            store(rho, 0, o, lse)

        def later(rho, blk, q_ref=q_ref, k_ref=k_ref, v_ref=v_ref, store=store):
            lanes = slice(rho * HEAD_DIM, (rho + 1) * HEAD_DIM)
            cur = pl.ds(pl.multiple_of(blk * BLOCK, BLOCK), BLOCK)
            prev = pl.ds(pl.multiple_of((blk - 1) * BLOCK, BLOCK), BLOCK)
            o, lse = _band_block(q_ref[0, 0, cur, lanes], k_ref[0, 0, cur, lanes],
                                 v_ref[0, 0, cur, lanes], k_ref[0, 0, prev, lanes],
                                 v_ref[0, 0, prev, lanes])
            store(rho, blk, o, lse)

        for rho0 in range(0, r, INFLIGHT):
            rhos = range(rho0, min(r, rho0 + INFLIGHT))
            for rho in rhos:
                first(rho)

            def body(blk, carry, rhos=rhos, later=later):
                for rho in rhos:
                    later(rho, blk)
                return carry
            if nblk > 1:
                lax.fori_loop(1, nblk, body, 0, unroll=max(1, INFLIGHT // len(rhos)))

    def combine(c, carry):
        rows = pl.ds(pl.multiple_of(c * BLOCK, BLOCK), BLOCK)
        ls = [l_scr[g, rows, :] for g in range(N_GROUPS)]
        m = jnp.maximum(jnp.maximum(ls[0], ls[1]), ls[2])
        es = [jnp.exp(l - m) for l in ls]
        num = es[0] * o_scr[0, rows, :] + es[1] * o_scr[1, rows, :] + es[2] * o_scr[2, rows, :]
        out_ref[0, rows, :] = (num / (es[0] + es[1] + es[2])).astype(out_ref.dtype)
        return carry
    lax.fori_loop(0, seq // BLOCK, combine, 0)


def _attn_prompt(qkv_groups, n_seq, seq):
    in_specs, args = [], []
    for role in range(3):
        for g, (_, r) in enumerate(DIL_GROUPS):
            in_specs.append(pl.BlockSpec((1, 1, seq // r, r * HEAD_DIM),
                                         lambda b, h, role=role: (role * N_HEADS + h, b, 0, 0)))
            args.append(qkv_groups[g])
    return pl.pallas_call(
        functools.partial(_attn_prompt_kernel, seq=seq),
        grid=(n_seq, N_HEADS),
        in_specs=in_specs,
        out_specs=pl.BlockSpec((1, seq, HEAD_DIM), lambda b, h: (b, 0, h)),
        out_shape=jax.ShapeDtypeStruct((n_seq, seq, ATT_WIDTH), BF16),
        scratch_shapes=[pltpu.VMEM((N_GROUPS, seq, HEAD_DIM), F32),
                        pltpu.VMEM((N_GROUPS, seq, HEAD_DIM), F32)],
        compiler_params=_params("arbitrary", "arbitrary"),
        name="attn_prompt",
    )(*args)


COPY_ROWS = 1024


def _attn_sample_kernel(q_ref, kvn_ref, win_ref, o_ref, lse_ref, new_ref, *, r, t_new):
    lb = win_ref.shape[0] // KV_ROWS
    tq = q_ref.shape[1]
    shift = t_new * KV_ROWS
    n_keep = (lb - t_new) * KV_ROWS
    n_full = n_keep // COPY_ROWS

    def copy(c, carry):
        dst = pl.multiple_of(c * COPY_ROWS, COPY_ROWS)
        src = pl.multiple_of(c * COPY_ROWS + shift, KV_ROWS)
        new_ref[pl.ds(dst, COPY_ROWS), :] = win_ref[pl.ds(src, COPY_ROWS), :]
        return carry
    lax.fori_loop(0, n_full, copy, 0)
    rest = n_keep - n_full * COPY_ROWS
    if rest:
        new_ref[n_full * COPY_ROWS:n_keep, :] = win_ref[n_full * COPY_ROWS + shift:, :]
    new_ref[n_keep:, :] = kvn_ref[...]
    t_i = lax.broadcasted_iota(jnp.int32, (tq, lb), 0)
    k_i = lax.broadcasted_iota(jnp.int32, (tq, lb), 1)
    dist = lb + t_i - k_i
    ok_buf = (dist % r == 0) & (dist <= r * DIL_SPAN)
    t_n = lax.broadcasted_iota(jnp.int32, (tq, BLOCK), 0)
    k_n = lax.broadcasted_iota(jnp.int32, (tq, BLOCK), 1) - (BLOCK - t_new)
    dn = t_n - k_n
    ok_new = (k_n >= 0) & (dn >= 0) & (dn % r == 0) & (dn <= r * DIL_SPAN)
    tail0 = (lb - BLOCK) * KV_ROWS
    for hd in range(N_HEADS):
        sl = slice(hd * HEAD_DIM, (hd + 1) * HEAD_DIM)
        q = q_ref[0, :, sl]
        kb = win_ref[_kv_rows(0, hd, lb), :].astype(BF16)
        vb = win_ref[_kv_rows(1, hd, lb), :].astype(BF16)
        kn = new_ref[pl.ds(tail0 + hd, BLOCK, stride=KV_ROWS), :].astype(BF16)
        vn = new_ref[pl.ds(tail0 + N_HEADS + hd, BLOCK, stride=KV_ROWS), :].astype(BF16)
        sb = jnp.where(ok_buf, _dot_nt(q, kb) * SCALE, NEG)
        sn = jnp.where(ok_new, _dot_nt(q, kn) * SCALE, NEG)
        m = jnp.maximum(jnp.max(sb, axis=-1, keepdims=True), jnp.max(sn, axis=-1, keepdims=True))
        pb = jnp.exp(sb - m)
        pn = jnp.exp(sn - m)
        l = jnp.sum(pb, axis=-1, keepdims=True) + jnp.sum(pn, axis=-1, keepdims=True)
        o = _dot(pb.astype(BF16), vb) + _dot(pn.astype(BF16), vn)
        o_ref[0, :, sl] = o / l
        lse_ref[0, :, sl] = jnp.broadcast_to(m + jnp.log(l), (tq, HEAD_DIM))


def _attn_sample(q, kv_new, win, nb, r):
    tq = q.shape[1]
    rows_new = kv_new.shape[0] // nb
    rows_win = win.shape[0] // nb
    return pl.pallas_call(
        functools.partial(_attn_sample_kernel, r=r, t_new=rows_new // KV_ROWS),
        grid=(nb,),
        in_specs=[pl.BlockSpec((1, tq, ATT_WIDTH), lambda b: (b, 0, 0)),
                  pl.BlockSpec((rows_new, HEAD_DIM), lambda b: (b, 0)),
                  pl.BlockSpec((rows_win, HEAD_DIM), lambda b: (b, 0))],
        out_specs=[pl.BlockSpec((1, tq, ATT_WIDTH), lambda b: (b, 0, 0)),
                   pl.BlockSpec((1, tq, ATT_WIDTH), lambda b: (b, 0, 0)),
                   pl.BlockSpec((rows_win, HEAD_DIM), lambda b: (b, 0))],
        out_shape=[jax.ShapeDtypeStruct((nb, tq, ATT_WIDTH), F32),
                   jax.ShapeDtypeStruct((nb, tq, ATT_WIDTH), F32),
                   jax.ShapeDtypeStruct(win.shape, F32)],
        compiler_params=_params("arbitrary"),
        name=f"attn_sample_r{r}",
    )(q, kv_new, win)


def _combine_kernel(o0, o1, o2, l0, l1, l2, out_ref):
    ls = [l0[...], l1[...], l2[...]]
    m = jnp.maximum(jnp.maximum(ls[0], ls[1]), ls[2])
    es = [jnp.exp(l - m) for l in ls]
    num = es[0] * o0[...] + es[1] * o1[...] + es[2] * o2[...]
    out_ref[...] = (num / (es[0] + es[1] + es[2])).astype(out_ref.dtype)


def _combine_groups(outs, lses):
    shape = outs[0].shape
    return pl.pallas_call(
        _combine_kernel,
        in_specs=[_full(shape)] * 6,
        out_specs=_full(shape),
        out_shape=jax.ShapeDtypeStruct(shape, BF16),
        grid=(1,),
        compiler_params=_params("arbitrary"),
        name="combine_groups",
    )(*outs, *lses)


FF_CHUNK = 256
CARRY_ROWS = 8
SHIFT_SLOTS = 4


def _out_ffn_kernel(*refs, seq_len, tiles_per_seq, tail_rows):
    if seq_len is None:
        (x_ref, mix_ref, cr_ref, wo_ref, g_ref, wup_ref, cw_ref, cb_ref, wdn_ref,
         out_ref, tail_ref, shift, carry) = refs
    else:
        (x_ref, mix_ref, cr_ref, wo_ref, g_ref, wup_ref, cw_ref, cb_ref, wdn_ref, e1_ref, e2_ref,
         out_ref, tail_ref, shift) = refs
    tm = x_ref.shape[0]
    d_ff = wdn_ref.shape[0]
    x1 = (x_ref[...] + _dot(mix_ref[...], wo_ref[:ATT_WIDTH, :])
          + _dot(cr_ref[...], wo_ref[ATT_WIDTH:, :]))
    h = _rms(x1, g_ref[...]).astype(BF16)
    if seq_len is None:
        @pl.when(pl.program_id(0) % tiles_per_seq == 0)
        def _():
            carry[...] = jnp.zeros_like(carry)
    else:
        t = lax.broadcasted_iota(jnp.int32, (tm, 1), 0) % seq_len
        has1 = t >= 1
        has2 = t >= 2

    def chunk_cols(j):
        return (slice(j * FF_CHUNK, (j + 1) * FF_CHUNK),
                slice(d_ff + j * FF_CHUNK, d_ff + (j + 1) * FF_CHUNK))

    def up_proj(j):
        return tuple(_dot(h, wup_ref[:, cols]) for cols in chunk_cols(j))

    def conv(up, cols, slot):
        buf = shift.at[slot]
        if seq_len is None:
            buf[:CARRY_ROWS, :] = carry[:, cols]
            carry[:, cols] = up[tm - CARRY_ROWS:, :]
        else:
            buf[:CARRY_ROWS, :] = jnp.zeros((CARRY_ROWS, FF_CHUNK), F32)
        buf[CARRY_ROWS:, :] = up
        tail_ref[0, :, cols] = up[tm - tail_rows:, :]
        prev1 = buf[CARRY_ROWS - 1:CARRY_ROWS - 1 + tm, :]
        prev2 = buf[CARRY_ROWS - 2:CARRY_ROWS - 2 + tm, :]
        if seq_len is not None:
            prev1 = jnp.where(has1, prev1, e1_ref[:, cols])
            prev2 = jnp.where(has2, prev2, e2_ref[:, cols])
        return (cb_ref[:, cols] + cw_ref[0:1, cols] * prev2 + cw_ref[1:2, cols] * prev1
                + cw_ref[2:3, cols] * up)

    acc = jnp.zeros_like(x1)
    n_chunks = d_ff // FF_CHUNK
    ups = up_proj(0)
    act = None
    for j in range(n_chunks + 1):
        nxt = up_proj(j + 1) if j + 1 < n_chunks else None
        if act is not None:
            acc = acc + _dot(act, wdn_ref[(j - 1) * FF_CHUNK:j * FF_CHUNK, :])
        if j < n_chunks:
            a, b = (conv(up, cols, 2 * (j % 2) + s) for s, (up, cols) in enumerate(zip(ups, chunk_cols(j))))
            act = (a * jax.nn.sigmoid(a) * b).astype(BF16)
        ups = nxt
    out_ref[...] = x1 + acc


def _out_ffn(x2d, mix, cross, w_out_bf, g_ffn, w_up_bf, conv_w, conv_b, w_down_bf, tm,
             tiles_per_seq=None, seq_len=None, e1=None, e2=None, tail_rows=CARRY_ROWS):
    rows, d = x2d.shape
    d_ff = w_down_bf.shape[0]
    n_tiles = rows // tm
    in_specs = [
        pl.BlockSpec((tm, d), lambda i: (i, 0)),
        pl.BlockSpec((tm, ATT_WIDTH), lambda i: (i, 0)),
        pl.BlockSpec((tm, ATT_WIDTH), lambda i: (i, 0)),
        _full(w_out_bf.shape), _full((1, d)), _full(w_up_bf.shape),
        _full((CONV_W, 2 * d_ff)), _full((1, 2 * d_ff)), _full(w_down_bf.shape),
    ]
    args = [x2d, mix, cross, w_out_bf, g_ffn.reshape(1, d), w_up_bf, conv_w, conv_b.reshape(1, 2 * d_ff),
            w_down_bf]
    scratch = [pltpu.VMEM((SHIFT_SLOTS, CARRY_ROWS + tm, FF_CHUNK), F32)]
    if seq_len is None:
        scratch += [pltpu.VMEM((CARRY_ROWS, 2 * d_ff), F32)]
    else:
        in_specs += [pl.BlockSpec((tm, 2 * d_ff), lambda i: (i, 0))] * 2
        args += [e1, e2]
    return pl.pallas_call(
        functools.partial(_out_ffn_kernel, seq_len=seq_len, tiles_per_seq=tiles_per_seq,
                          tail_rows=tail_rows),
        grid=(n_tiles,),
        in_specs=in_specs,
        out_specs=[pl.BlockSpec((tm, d), lambda i: (i, 0)),
                   pl.BlockSpec((1, tail_rows, 2 * d_ff), lambda i: (i, 0, 0))],
        out_shape=[jax.ShapeDtypeStruct((rows, d), F32),
                   jax.ShapeDtypeStruct((n_tiles, tail_rows, 2 * d_ff), F32)],
        scratch_shapes=scratch,
        compiler_params=_params("arbitrary"),
        name="out_ffn",
    )(*args)


def _swap_halves(x):
    return pltpu.roll(x, S5_STATE, 1)


def _s5_prep_kernel(lam_ref, logdt_ref, bt_ref, c_ref, d_ref, e_ref, cl_ref, toep_ref, lam_out_ref, *, sub):
    gc = lam_ref.shape[0]
    lane = lax.broadcasted_iota(jnp.int32, lam_ref.shape, 1)
    first = lane < S5_STATE
    sign = jnp.where(first, -1.0, 1.0)
    a = lam_ref[...]
    a_sw = _swap_halves(a)
    are = jnp.where(first, a, a_sw)
    aim = jnp.where(first, a_sw, a)
    dt = jnp.exp(logdt_ref[...])
    mag = jnp.exp(are * dt)
    lr = mag * jnp.cos(aim * dt)
    li = mag * jnp.sin(aim * dt)
    den = are * are + aim * aim
    xr = lr - 1.0
    f_re = (xr * are + li * aim) / den
    f_im = (li * are - xr * aim) / den
    lb = sign * li

    def cmul(x, m_re, m_sw):
        return x * m_re + _swap_halves(x) * m_sw

    c = c_ref[...]
    c_neg = c * -sign
    ri = lax.broadcasted_iota(jnp.int32, (gc, gc), 0)
    ci = lax.broadcasted_iota(jnp.int32, (gc, gc), 1)
    same_group = (ri // S5_GROUP) == (ci // S5_GROUP)
    e = cmul(bt_ref[...], f_re, sign * f_im)
    cl = c
    pw = jnp.where(first, 1.0, 0.0)
    for k in range(sub):
        kk = sub - 1 - k
        e_ref[kk] = e
        kmat = jnp.where(same_group, _dot_nt(e, c_neg, precision=HIGHEST), 0.0)
        if k == 0:
            kmat = kmat + jnp.where(ri == ci, d_ref[...], 0.0)
        for half in range(2):
            sl = slice(half * S5_HALF, (half + 1) * S5_HALF)
            toep_ref[half, kk * S5_HALF:(kk + 1) * S5_HALF, :] = kmat[sl, sl].astype(toep_ref.dtype)
        cl = cmul(cl, lr, lb)
        cl_ref[k] = cl * -sign
        e = cmul(e, lr, lb)
        pw = cmul(pw, lr, lb)
    lam_out_ref[...] = pw


def _expand_kernel(tab_ref, out_ref):
    out_ref[...] = jnp.zeros_like(out_ref)
    for g in range(S5_HALF // S5_GROUP):
        rows = slice(g * S5_GROUP, (g + 1) * S5_GROUP)
        out_ref[0, rows, g * 2 * S5_STATE:(g + 1) * 2 * S5_STATE] = tab_ref[0, rows, :].astype(out_ref.dtype)


def _expand(tab, sub):
    n_state = (S5_HALF // S5_GROUP) * 2 * S5_STATE
    return pl.pallas_call(
        _expand_kernel,
        grid=(2, sub),
        in_specs=[pl.BlockSpec((1, S5_HALF, 2 * S5_STATE), lambda h, t: (t, h, 0))],
        out_specs=pl.BlockSpec((1, S5_HALF, n_state), lambda h, t: (h, t, 0)),
        out_shape=jax.ShapeDtypeStruct((2, sub * S5_HALF, n_state), BF16),
        compiler_params=_params("arbitrary", "arbitrary"),
        name="s5_expand",
    )(tab)


def _s5_prep(lam_re, lam_im, log_dt, b_re, b_im, c_re, c_im, d_skip, sub):
    n_groups = lam_re.shape[0]
    gc = n_groups * S5_GROUP
    rep = lambda t: jnp.repeat(t, S5_GROUP, axis=0)
    lam_p = rep(jnp.concatenate([lam_re, lam_im], axis=-1))
    logdt = rep(log_dt.reshape(n_groups, 1))
    bt_p = jnp.concatenate([b_re.transpose(0, 2, 1), b_im.transpose(0, 2, 1)], axis=-1).reshape(gc, 2 * S5_STATE)
    c_p = jnp.concatenate([c_re, c_im], axis=-1).reshape(gc, 2 * S5_STATE)
    tab_shape = (sub, gc, 2 * S5_STATE)
    e_tab, cl_tab, toep, lam_sub = pl.pallas_call(
        functools.partial(_s5_prep_kernel, sub=sub),
        grid=(1,),
        in_specs=[_full(lam_p.shape), _full(logdt.shape), _full(bt_p.shape), _full(c_p.shape), _full((gc, 1))],
        out_specs=[_full(tab_shape), _full(tab_shape), _full((2, sub * S5_HALF, S5_HALF)),
                   _full((gc, 2 * S5_STATE))],
        out_shape=[jax.ShapeDtypeStruct(tab_shape, F32), jax.ShapeDtypeStruct(tab_shape, F32),
                   jax.ShapeDtypeStruct((2, sub * S5_HALF, S5_HALF), BF16),
                   jax.ShapeDtypeStruct((gc, 2 * S5_STATE), F32)],
        compiler_params=_params("arbitrary"),
        name="s5_prep",
    )(lam_p, logdt, bt_p, c_p, d_skip.reshape(gc, 1))
    lam_sub = lam_sub[::S5_GROUP].reshape(2, n_groups // 2, 2 * S5_STATE)
    return toep, _expand(e_tab, sub), _expand(cl_tab, sub), lam_sub


def _s5_toep_kernel(v_ref, w_ref, y_ref, *, sub):
    for tau in range(sub):
        y_ref[0, :, tau * S5_HALF:(tau + 1) * S5_HALF] = _dot(
            v_ref[0, :, :(tau + 1) * S5_HALF], w_ref[0, (sub - 1 - tau) * S5_HALF:, :])


def _matmul_kernel(a_ref, b_ref, o_ref):
    o_ref[0] = _dot(a_ref[0], b_ref[0])


def _matmul_nt_add_kernel(a_ref, b_ref, y_ref, o_ref):
    o_ref[0] = y_ref[0] + _dot_nt(a_ref[0].astype(BF16), b_ref[0])


def _s5_scan_kernel(s_ref, lam_ref, x0_ref, xprev_ref, xfin_ref, *, n_seq, n_sub):
    lane = lax.broadcasted_iota(jnp.int32, lam_ref.shape[1:], 1)
    first = lane < S5_STATE
    lam = lam_ref[0]
    lam_sw = _swap_halves(lam)
    m_re = jnp.where(first, lam, lam_sw)
    m_sw = jnp.where(first, -lam_sw, lam)
    x_init = tuple(x0_ref[0, b] for b in range(n_seq))
    xs_init = tuple(_swap_halves(x) for x in x_init)

    def body(n, carry):
        xs, xss = carry
        new_x, new_xs = [], []
        for b in range(n_seq):
            row = b * n_sub + n
            xprev_ref[0, row] = xs[b]
            s = s_ref[0, row]
            new_x.append(xs[b] * m_re + xss[b] * m_sw + s)
            new_xs.append(xss[b] * m_re - xs[b] * m_sw + _swap_halves(s))
        return tuple(new_x), tuple(new_xs)
    fin, _ = lax.fori_loop(0, n_sub, body, (x_init, xs_init))
    for b in range(n_seq):
        xfin_ref[0, b] = fin[b]


def _glu_kernel(y_ref, w_ref, b_ref, o_ref, stage, *, sub):
    tmv = y_ref.shape[1]
    n_cb = ATT_WIDTH // HEAD_DIM
    for cb in range(n_cb):
        half, off = divmod(cb * HEAD_DIM, S5_HALF)
        for tau in range(sub):
            stage[cb, pl.ds(tau, tmv, stride=sub), :] = (
                y_ref[half, :, tau * S5_HALF + off:tau * S5_HALF + off + HEAD_DIM])
    y = jax.nn.gelu(jnp.concatenate([stage[cb] for cb in range(n_cb)], axis=-1))
    z = _dot(y.astype(BF16), w_ref[...]) + b_ref[...]
    o_ref[...] = (y * jax.nn.sigmoid(z)).astype(o_ref.dtype)


def _s5_mixer(v, n_seq, seq, sub, x0, toep, b_all, c_all_t, lam_sub, w_glu_bf, b_glu):
    _, m, width = v.shape
    rows = m * sub
    n_sub = seq // sub
    n_state = b_all.shape[2]
    gh = n_state // (2 * S5_STATE)
    tms = min(m, 256)
    y_intra = pl.pallas_call(
        functools.partial(_s5_toep_kernel, sub=sub),
        grid=(2, m // tms),
        in_specs=[pl.BlockSpec((1, tms, width), lambda h, i: (h, i, 0)),
                  pl.BlockSpec((1, width, S5_HALF), lambda h, i: (h, 0, 0))],
        out_specs=pl.BlockSpec((1, tms, width), lambda h, i: (h, i, 0)),
        out_shape=jax.ShapeDtypeStruct((2, m, width), F32),
        compiler_params=_params("arbitrary", "arbitrary"),
        name="s5_toeplitz",
    )(v, toep)
    nc = 512
    s_loc = pl.pallas_call(
        _matmul_kernel,
        grid=(2, n_state // nc, m // tms),
        in_specs=[pl.BlockSpec((1, tms, width), lambda h, j, i: (h, i, 0)),
                  pl.BlockSpec((1, width, nc), lambda h, j, i: (h, 0, j))],
        out_specs=pl.BlockSpec((1, tms, nc), lambda h, j, i: (h, i, j)),
        out_shape=jax.ShapeDtypeStruct((2, m, n_state), F32),
        compiler_params=_params("arbitrary", "arbitrary", "arbitrary"),
        name="s5_state_in",
    )(v, b_all)
    st_shape = (2, m, gh, 2 * S5_STATE)
    x_prev, x_fin = pl.pallas_call(
        functools.partial(_s5_scan_kernel, n_seq=n_seq, n_sub=n_sub),
        grid=(2,),
        in_specs=[pl.BlockSpec((1,) + st_shape[1:], lambda h: (h, 0, 0, 0)),
                  pl.BlockSpec((1, gh, 2 * S5_STATE), lambda h: (h, 0, 0)),
                  pl.BlockSpec((1, n_seq, gh, 2 * S5_STATE), lambda h: (h, 0, 0, 0))],
        out_specs=[pl.BlockSpec((1,) + st_shape[1:], lambda h: (h, 0, 0, 0)),
                   pl.BlockSpec((1, n_seq, gh, 2 * S5_STATE), lambda h: (h, 0, 0, 0))],
        out_shape=[jax.ShapeDtypeStruct(st_shape, F32),
                   jax.ShapeDtypeStruct((2, n_seq, gh, 2 * S5_STATE), F32)],
        compiler_params=_params("arbitrary"),
        name="s5_scan",
    )(s_loc.reshape(st_shape), lam_sub, x0)
    wc = min(width, 512)
    y = pl.pallas_call(
        _matmul_nt_add_kernel,
        grid=(2, width // wc, m // tms),
        in_specs=[pl.BlockSpec((1, tms, n_state), lambda h, j, i: (h, i, 0)),
                  pl.BlockSpec((1, wc, n_state), lambda h, j, i: (h, j, 0)),
                  pl.BlockSpec((1, tms, wc), lambda h, j, i: (h, i, j))],
        out_specs=pl.BlockSpec((1, tms, wc), lambda h, j, i: (h, i, j)),
        out_shape=jax.ShapeDtypeStruct((2, m, width), F32),
        compiler_params=_params("arbitrary", "arbitrary", "arbitrary"),
        name="s5_state_out",
    )(x_prev.reshape(2, m, n_state), c_all_t, y_intra)
    tm = min(rows, 512)
    mix = pl.pallas_call(
        functools.partial(_glu_kernel, sub=sub),
        grid=(rows // tm,),
        in_specs=[pl.BlockSpec((2, tm // sub, width), lambda i: (0, i, 0)),
                  _full(w_glu_bf.shape), _full((1, ATT_WIDTH))],
        out_specs=pl.BlockSpec((tm, ATT_WIDTH), lambda i: (i, 0)),
        out_shape=jax.ShapeDtypeStruct((rows, ATT_WIDTH), BF16),
        scratch_shapes=[pltpu.VMEM((ATT_WIDTH // HEAD_DIM, tm, HEAD_DIM), F32)],
        compiler_params=_params("arbitrary"),
        name="s5_glu",
    )(y, w_glu_bf, b_glu.reshape(1, ATT_WIDTH))
    return mix, x_fin


def _pack_state(s):
    n = s.shape[0]
    gh = s.shape[2] // 2
    return s.reshape(n, 2, 2, gh, S5_STATE).transpose(2, 0, 3, 1, 4).reshape(2, n, gh, 2 * S5_STATE)


def _unpack_state(x):
    _, n, gh, _ = x.shape
    return x.reshape(2, n, gh, 2, S5_STATE).transpose(1, 3, 0, 2, 4).reshape(n, 2, 2 * gh, S5_STATE)


PROMPT_TILE = 512
PROMPT_SUB = 16
Q_PAD = 16


def _pad_rows(t, n):
    return jnp.pad(t, ((0, 0), (0, n - t.shape[1]), (0, 0)))


def kernel(x_prompt, x_sample, cache_win0_kv, cache_win1_kv, cache_win2_kv, cache_mem_kv, state_s5,
           state_ffn_conv, mem_prompt, g_mix, g_ffn, w_in_a, g_q_dil, g_k_dil, w_in_b, s5_lam_re,
           s5_lam_im, s5_log_dt, s5_b_re, s5_b_im, s5_c_re, s5_c_im, s5_d, w_glu, b_glu, g_mem,
           w_mem_kv, g_q_cross, g_k_cross, w_out, w_up, conv_w, conv_b, w_down):
    nb, seq, d = x_prompt.shape
    db, ts, _ = x_sample.shape
    depth = g_mix.shape[0]
    n_mem = mem_prompt.shape[1]
    d_ff2 = w_up.shape[2]
    assert ts >= CONV_W - 1 and seq % PROMPT_TILE == 0 and ts <= Q_PAD
    caches = (cache_win0_kv, cache_win1_kv, cache_win2_kv)
    w_mem_bf = w_mem_kv.astype(BF16)
    mem_cache = cache_mem_kv.reshape(depth * db, n_mem * KV_ROWS, HEAD_DIM)

    tab_p = _rope_tables(jnp.arange(seq, dtype=jnp.int32))
    tab_s = tuple(jnp.tile(t, (db, 1)) for t in _rope_tables(PAST_LEN + jnp.arange(ts, dtype=jnp.int32)))
    win_keep = tuple(min(w, seq) for w, _ in DIL_GROUPS)
    rows_s = db * ts
    kv_tail = (2, N_HEADS, HEAD_DIM)

    mem_p = _mem_kv(mem_prompt.reshape(nb * n_mem, d), g_mem, w_mem_bf, g_k_cross)

    xp = x_prompt.reshape(nb * seq, d)
    xs = x_sample.reshape(rows_s, d)
    p_win, s_win = [[] for _ in DIL_GROUPS], [[] for _ in DIL_GROUPS]
    p_s5, s_s5, p_conv, s_conv = [], [], [], []
    tiles_per_seq = seq // PROMPT_TILE
    for i in range(depth):
        mem_i = mem_p[i].reshape(nb, n_mem * KV_ROWS, HEAD_DIM)
        if i % 2 == 0:
            ia = i // 2
            w_in_a_bf = w_in_a[ia].astype(BF16)
            *qkv_groups, w0, w1, w2, cross_p = _in_proj_a(
                xp, seq, PROMPT_TILE, g_mix[i], w_in_a_bf, tab_p, g_q_dil[ia], g_k_dil[ia],
                g_q_cross[i], mem_i, win_keep, dilated=True)
            for g, w in enumerate((w0, w1, w2)):
                p_win[g].append(w.reshape((nb, win_keep[g]) + kv_tail))
            mix_p = _attn_prompt(qkv_groups, nb, seq).reshape(nb * seq, ATT_WIDTH)
            qkv_s, k0, k1, k2, qc_s = _in_proj_a(
                xs, rows_s, rows_s, g_mix[i], w_in_a_bf, tab_s, g_q_dil[ia], g_k_dil[ia],
                None, None, (rows_s,) * N_GROUPS, dilated=False)
            outs, lses = [], []
            for g, ((_, r), kv_new) in enumerate(zip(DIL_GROUPS, (k0, k1, k2))):
                q_g = qkv_s[g * N_HEADS:(g + 1) * N_HEADS]
                q_g = _pad_rows(q_g.transpose(1, 0, 2).reshape(db, ts, ATT_WIDTH), Q_PAD)
                win = caches[g][ia]
                o_g, lse_g, new_g = _attn_sample(q_g, kv_new, win.reshape(-1, HEAD_DIM), db, r)
                outs.append(o_g[:, :ts].reshape(rows_s, ATT_WIDTH))
                lses.append(lse_g[:, :ts].reshape(rows_s, ATT_WIDTH))
                s_win[g].append(new_g.reshape(win.shape))
            mix_s = _combine_groups(outs, lses)
        else:
            ib = i // 2
            prm = (s5_lam_re[ib], s5_lam_im[ib], s5_log_dt[ib], s5_b_re[ib], s5_b_im[ib], s5_c_re[ib],
                   s5_c_im[ib], s5_d[ib])
            gh = s5_lam_re.shape[1] // 2
            w_in_b_bf, w_glu_bf = w_in_b[ib].astype(BF16), w_glu[ib].astype(BF16)
            u_p, cross_p = _in_proj_b(xp, seq, PROMPT_TILE, PROMPT_SUB, g_mix[i], w_in_b_bf,
                                      g_q_cross[i], mem_i)
            x0 = jnp.zeros((2, nb, gh, 2 * S5_STATE), F32)
            mix_p, fin_p = _s5_mixer(u_p, nb, seq, PROMPT_SUB, x0, *_s5_prep(*prm, PROMPT_SUB),
                                     w_glu_bf, b_glu[ib])
            p_s5.append(_unpack_state(fin_p))
            u_s, qc_s = _in_proj_b(xs, rows_s, rows_s, ts, g_mix[i], w_in_b_bf, None, None)
            mix_s, fin_s = _s5_mixer(u_s, db, ts, ts, _pack_state(state_s5[ib]), *_s5_prep(*prm, ts),
                                     w_glu_bf, b_glu[ib])
            s_s5.append(_unpack_state(fin_s))
        cross_s = _cross_sample(_pad_rows(qc_s.reshape(db, ts, ATT_WIDTH), Q_PAD), g_q_cross[i], mem_cache,
                                first=i * db)
        cross_s = cross_s[:, :ts].reshape(rows_s, ATT_WIDTH)

        ffn_w = (w_out[i].astype(BF16), g_ffn[i], w_up[i].astype(BF16), conv_w[i], conv_b[i],
                 w_down[i].astype(BF16))
        xp, tails = _out_ffn(xp, mix_p, cross_p, *ffn_w, PROMPT_TILE, tiles_per_seq=tiles_per_seq)
        p_conv.append(tails.reshape(nb, tiles_per_seq, CARRY_ROWS, d_ff2)[:, -1, CARRY_ROWS - (CONV_W - 1):])
        buf = state_ffn_conv[i]
        zero = jnp.zeros((db, ts - 2, d_ff2), F32)
        e1 = jnp.concatenate([buf[:, 1:2], zero, zero[:, :1]], axis=1).reshape(rows_s, d_ff2)
        e2 = jnp.concatenate([buf, zero], axis=1).reshape(rows_s, d_ff2)
        xs, tails = _out_ffn(xs, mix_s, cross_s, *ffn_w, rows_s, seq_len=ts, e1=e1, e2=e2, tail_rows=rows_s)
        s_conv.append(tails.reshape(db, ts, d_ff2)[:, ts - (CONV_W - 1):])

    return (xp.reshape(nb, seq, d), xs.reshape(db, ts, d),
            jnp.stack(p_win[0]), jnp.stack(p_win[1]), jnp.stack(p_win[2]),
            mem_p.reshape((depth, nb, n_mem) + kv_tail),
            jnp.stack(p_s5), jnp.stack(p_conv),
            jnp.stack(s_win[0]), jnp.stack(s_win[1]), jnp.stack(s_win[2]),
            jnp.stack(s_s5), jnp.stack(s_conv))
```

```python
import functools
import math

import jax
import jax.numpy as jnp
from jax import lax
from jax.experimental import pallas as pl
from jax.experimental.pallas import tpu as pltpu

HEAD_DIM = 128
N_HEADS = 4
DIL_GROUPS = ((128, 1), (512, 4), (2048, 16))
N_GROUPS = len(DIL_GROUPS)
DIL_SPAN = 128
BLOCK = 128
ATT_WIDTH = N_HEADS * HEAD_DIM
KV_ROWS = 2 * N_HEADS
ROT_DIM = HEAD_DIM // 4
ROT_HALF = ROT_DIM // 2
ROPE_THETA = 500000.0
S5_GROUP = 16
S5_STATE = 64
S5_SUB = 16
S5_BLOCK = S5_SUB * S5_GROUP
CONV_W = 3
EPS = 1e-6
NEG = -1e30
SCALE = HEAD_DIM ** -0.5
PAST_LEN = 16384

VMEM_LIMIT_V7X = 56 * 1024 * 1024
BF16 = jnp.bfloat16
F32 = jnp.float32
HIGHEST = lax.Precision.HIGHEST


def _params(*sem):
    return pltpu.CompilerParams(dimension_semantics=sem, vmem_limit_bytes=VMEM_LIMIT_V7X)


def _rms(x, g):
    return x * lax.rsqrt(jnp.mean(x * x, axis=-1, keepdims=True) + EPS) * g


def _dot(a, b):
    return jnp.dot(a, b, preferred_element_type=F32)


def _dot_nt(a, b, precision=None):
    return lax.dot_general(a, b, (((1,), (1,)), ((), ())), precision=precision,
                           preferred_element_type=F32)


def _full(shape):
    nd = len(shape)
    return pl.BlockSpec(shape, lambda *_: (0,) * nd)


def _kv_rows(kv, head, n):
    return pl.ds(kv * N_HEADS + head, n, stride=KV_ROWS)


def _mem_kv_kernel(mem_ref, g_ref, w_ref, gk_ref, out_ref):
    tm = mem_ref.shape[0]
    h = _rms(mem_ref[...], g_ref[0]).astype(BF16)
    kv = _dot(h, w_ref[0])
    for hd in range(N_HEADS):
        sl = slice(hd * HEAD_DIM, (hd + 1) * HEAD_DIM)
        out_ref[0, _kv_rows(0, hd, tm), :] = _rms(kv[:, sl], gk_ref[0])
        out_ref[0, _kv_rows(1, hd, tm), :] = kv[:, ATT_WIDTH + hd * HEAD_DIM:ATT_WIDTH + (hd + 1) * HEAD_DIM]


def _mem_kv(mem2d, g_mem, w_kv_bf, g_k):
    depth, d, _ = w_kv_bf.shape
    rows = mem2d.shape[0]
    tm = min(rows, 512)
    return pl.pallas_call(
        _mem_kv_kernel,
        grid=(depth, rows // tm),
        in_specs=[
            pl.BlockSpec((tm, d), lambda l, i: (i, 0)),
            pl.BlockSpec((1, 1, d), lambda l, i: (l, 0, 0)),
            pl.BlockSpec((1, d, 2 * ATT_WIDTH), lambda l, i: (l, 0, 0)),
            pl.BlockSpec((1, 1, HEAD_DIM), lambda l, i: (l, 0, 0)),
        ],
        out_specs=pl.BlockSpec((1, tm * KV_ROWS, HEAD_DIM), lambda l, i: (l, i, 0)),
        out_shape=jax.ShapeDtypeStruct((depth, rows * KV_ROWS, HEAD_DIM), F32),
        compiler_params=_params("arbitrary", "arbitrary"),
        name="mem_kv",
    )(mem2d, g_mem.reshape(depth, 1, d), w_kv_bf, g_k.reshape(depth, 1, HEAD_DIM))


def _cross_heads(qc, mem_ref, gq):
    n_mem = mem_ref.shape[1] // KV_ROWS
    outs = []
    for hd in range(N_HEADS):
        q = _rms(qc[:, hd * HEAD_DIM:(hd + 1) * HEAD_DIM], gq).astype(BF16)
        k = mem_ref[0, _kv_rows(0, hd, n_mem), :].astype(BF16)
        v = mem_ref[0, _kv_rows(1, hd, n_mem), :].astype(BF16)
        s = _dot_nt(q, k) * SCALE
        m = jnp.max(s, axis=-1, keepdims=True)
        p = jnp.exp(s - m)
        l = jnp.sum(p, axis=-1, keepdims=True)
        outs.append(_dot(p.astype(BF16), v) / l)
    return outs


def _rope(x, cos_t, sin_lo, sin_hi):
    return (x * cos_t + pltpu.roll(x, HEAD_DIM - ROT_HALF, 1) * sin_lo
            + pltpu.roll(x, ROT_HALF, 1) * sin_hi)


def _in_proj_a_kernel(*refs, fuse_cross, dilated, win_rows, win_first, tiles_per_seq):
    n_in = 10 if fuse_cross else 8
    x_ref, g_ref, w_ref, cos_ref, slo_ref, shi_ref, gq_ref, gk_ref = refs[:8]
    n_qkv = N_GROUPS if dilated else 1
    qkv_refs = refs[n_in:n_in + n_qkv]
    win_refs = refs[n_in + n_qkv:n_in + n_qkv + N_GROUPS]
    cr_ref = refs[n_in + n_qkv + N_GROUPS]
    kv_stash = refs[n_in + n_qkv + N_GROUPS + 1]
    stage = refs[n_in + n_qkv + N_GROUPS + 2] if dilated else None
    tm = x_ref.shape[0]
    h = _rms(x_ref[...], g_ref[...]).astype(BF16)
    cos_t, sin_lo, sin_hi = cos_ref[...], slo_ref[...], shi_ref[...]
    def proj(c):
        return _dot(h, w_ref[:, c * ATT_WIDTH:(c + 1) * ATT_WIDTH])

    tile = pl.program_id(0) % tiles_per_seq
    in_window = [tile >= first for first in win_first]
    for g in range(N_GROUPS):
        @pl.when(jnp.logical_not(in_window[g]))
        def _(g=g):
            win_refs[g][...] = jnp.zeros_like(win_refs[g])

    y = proj(0)
    for c in range(3 * N_GROUPS):
        y_next = proj(c + 1)
        role, g = divmod(c, N_GROUPS)
        r = DIL_GROUPS[g][1]
        wr = win_rows[g]
        for hd in range(N_HEADS):
            yh = y[:, hd * HEAD_DIM:(hd + 1) * HEAD_DIM]
            if role == 0:
                yh = _rope(_rms(yh, gq_ref[g]), cos_t, sin_lo, sin_hi)
            elif role == 1:
                yh = _rope(_rms(yh, gk_ref[g]), cos_t, sin_lo, sin_hi)
            if role > 0:
                kv_stash[((role - 1) * N_GROUPS + g) * N_HEADS + hd] = yh
            if not dilated:
                qkv_refs[0][role * N_GROUPS * N_HEADS + g * N_HEADS + hd] = yh.astype(BF16)
            elif r == 1:
                qkv_refs[g][role * N_HEADS + hd, 0] = yh.astype(BF16)
            else:
                stage[hd] = yh
                for rho in range(r):
                    qkv_refs[g][role * N_HEADS + hd, 0, :, rho * HEAD_DIM:(rho + 1) * HEAD_DIM] = (
                        stage[hd, pl.ds(rho, tm // r, stride=r), :].astype(BF16))
        y = y_next
    qc = y
    if fuse_cross:
        outs = _cross_heads(qc, refs[9], refs[8][...])
        for hd in range(N_HEADS):
            cr_ref[:, hd * HEAD_DIM:(hd + 1) * HEAD_DIM] = outs[hd].astype(cr_ref.dtype)
    else:
        cr_ref[...] = qc
    for g in range(N_GROUPS):
        @pl.when(in_window[g])
        def _(g=g):
            wr = win_rows[g]
            for kv in range(2):
                for hd in range(N_HEADS):
                    win_refs[g][_kv_rows(kv, hd, wr), :] = kv_stash[(kv * N_GROUPS + g) * N_HEADS + hd,
                                                                    tm - wr:, :]


def _in_proj_b_kernel(*refs, fuse_cross, n_valid):
    if fuse_cross:
        x_ref, g_ref, w_ref, gqc_ref, mem_ref, u_ref, cr_ref, stage = refs
    else:
        x_ref, g_ref, w_ref, u_ref, cr_ref, stage = refs
    tm = x_ref.shape[0]
    n_rows = tm // n_valid
    h = _rms(x_ref[...], g_ref[...]).astype(BF16)
    u = _dot(h, w_ref[:, :ATT_WIDTH])
    slot = lax.broadcasted_iota(jnp.int32, (n_rows, HEAD_DIM), 1) // S5_GROUP
    per_tile = HEAD_DIM // S5_GROUP
    for cb in range(ATT_WIDTH // HEAD_DIM):
        stage[cb] = u[:, cb * HEAD_DIM:(cb + 1) * HEAD_DIM]
        toks = [stage[cb, pl.ds(tau, n_rows, stride=n_valid), :] for tau in range(n_valid)]
        for p in range(per_tile):
            g = cb * per_tile + p
            for half in range(S5_SUB // per_tile):
                acc = jnp.zeros((n_rows, HEAD_DIM), F32)
                for s in range(per_tile):
                    tau = half * per_tile + s
                    if tau < n_valid:
                        shift = ((s - p) * S5_GROUP) % HEAD_DIM
                        acc = jnp.where(slot == s, pltpu.roll(toks[tau], shift, 1) if shift else toks[tau], acc)
                lanes = (2 * g + half) * HEAD_DIM
                u_ref[:, lanes:lanes + HEAD_DIM] = acc.astype(BF16)
    qc = _dot(h, w_ref[:, ATT_WIDTH:])
    if fuse_cross:
        outs = _cross_heads(qc, mem_ref, gqc_ref[...])
        for hd in range(N_HEADS):
            cr_ref[:, hd * HEAD_DIM:(hd + 1) * HEAD_DIM] = outs[hd].astype(cr_ref.dtype)
    else:
        cr_ref[...] = qc


def _rope_tables(pos):
    inv = jnp.exp(-math.log(ROPE_THETA) * jnp.arange(ROT_HALF, dtype=F32) / ROT_HALF)
    ang = pos.astype(F32)[:, None] * inv[None, :]
    cos, sin = jnp.cos(ang), jnp.sin(ang)
    rows = pos.shape[0]
    ones = jnp.ones((rows, HEAD_DIM - ROT_DIM), F32)
    zeros = jnp.zeros((rows, HEAD_DIM - ROT_DIM), F32)
    z16 = jnp.zeros((rows, ROT_HALF), F32)
    cos_t = jnp.concatenate([cos, cos, ones], axis=1)
    sin_lo = jnp.concatenate([-sin, z16, zeros], axis=1)
    sin_hi = jnp.concatenate([z16, sin, zeros], axis=1)
    return cos_t, sin_lo, sin_hi


def _in_proj_a(x2d, seq, tm, g_mix, w_bf, tables, g_q, g_k, g_qc, mem_kv, win_keep, dilated):
    rows, d = x2d.shape
    n_in = w_bf.shape[1]
    tiles_per_seq = seq // tm
    n_seq = rows // seq
    fuse_cross = mem_kv is not None
    tab_tiles = tables[0].shape[0] // tm
    win_rows = tuple(min(k, tm) for k in win_keep)
    win_first = tuple(tiles_per_seq - k // wr for k, wr in zip(win_keep, win_rows))

    def win_spec(keep):
        wr = min(keep, tm)
        nblk = keep // wr
        first = tiles_per_seq - nblk

        def imap(i):
            b = i // tiles_per_seq
            t = i % tiles_per_seq
            return (b * nblk + jnp.maximum(t - first, 0), 0)
        return pl.BlockSpec((wr * KV_ROWS, HEAD_DIM), imap)

    in_specs = [
        pl.BlockSpec((tm, d), lambda i: (i, 0)),
        _full((1, d)),
        _full((d, n_in)),
        pl.BlockSpec((tm, HEAD_DIM), lambda i: (i % tab_tiles, 0)),
        pl.BlockSpec((tm, HEAD_DIM), lambda i: (i % tab_tiles, 0)),
        pl.BlockSpec((tm, HEAD_DIM), lambda i: (i % tab_tiles, 0)),
        _full((N_GROUPS, HEAD_DIM)),
        _full((N_GROUPS, HEAD_DIM)),
    ]
    args = [x2d, g_mix.reshape(1, d), w_bf, *tables, g_q, g_k]
    if fuse_cross:
        in_specs += [_full((1, HEAD_DIM)),
                     pl.BlockSpec((1,) + mem_kv.shape[1:], lambda i: (i // tiles_per_seq, 0, 0))]
        args += [g_qc.reshape(1, HEAD_DIM), mem_kv]
    if dilated:
        qkv_specs = [pl.BlockSpec((3 * N_HEADS, 1, tm // r, r * HEAD_DIM),
                                  lambda i: (0, i // tiles_per_seq, i % tiles_per_seq, 0))
                     for _, r in DIL_GROUPS]
        qkv_shapes = [jax.ShapeDtypeStruct((3 * N_HEADS, n_seq, seq // r, r * HEAD_DIM), BF16)
                      for _, r in DIL_GROUPS]
        scratch = [pltpu.VMEM((N_HEADS, tm, HEAD_DIM), F32)]
    else:
        qkv_specs = [pl.BlockSpec((3 * N_GROUPS * N_HEADS, tm, HEAD_DIM), lambda i: (0, i, 0))]
        qkv_shapes = [jax.ShapeDtypeStruct((3 * N_GROUPS * N_HEADS, rows, HEAD_DIM), BF16)]
        scratch = []
    scratch = [pltpu.VMEM((2 * N_GROUPS * N_HEADS, tm, HEAD_DIM), F32)] + scratch
    out_specs = [*qkv_specs, *[win_spec(k) for k in win_keep],
                 pl.BlockSpec((tm, ATT_WIDTH), lambda i: (i, 0))]
    out_shape = [*qkv_shapes,
                 *[jax.ShapeDtypeStruct((n_seq * k * KV_ROWS, HEAD_DIM), F32) for k in win_keep],
                 jax.ShapeDtypeStruct((rows, ATT_WIDTH), BF16 if fuse_cross else F32)]
    return pl.pallas_call(
        functools.partial(_in_proj_a_kernel, fuse_cross=fuse_cross, dilated=dilated, win_rows=win_rows,
                          win_first=win_first, tiles_per_seq=tiles_per_seq),
        grid=(rows // tm,),
        in_specs=in_specs, out_specs=out_specs, out_shape=out_shape, scratch_shapes=scratch,
        compiler_params=_params("arbitrary"),
        name="in_proj_a",
    )(*args)


def _in_proj_b(x2d, seq, tm, n_valid, g_mix, w_bf, g_qc, mem_kv):
    rows, d = x2d.shape
    width = (ATT_WIDTH // S5_GROUP) * S5_BLOCK
    n_in = w_bf.shape[1]
    tiles_per_seq = seq // tm
    fuse_cross = mem_kv is not None
    in_specs = [pl.BlockSpec((tm, d), lambda i: (i, 0)), _full((1, d)), _full((d, n_in))]
    args = [x2d, g_mix.reshape(1, d), w_bf]
    if fuse_cross:
        in_specs += [_full((1, HEAD_DIM)),
                     pl.BlockSpec((1,) + mem_kv.shape[1:], lambda i: (i // tiles_per_seq, 0, 0))]
        args += [g_qc.reshape(1, HEAD_DIM), mem_kv]
    return pl.pallas_call(
        functools.partial(_in_proj_b_kernel, fuse_cross=fuse_cross, n_valid=n_valid),
        grid=(rows // tm,),
        in_specs=in_specs,
        out_specs=[pl.BlockSpec((tm // n_valid, width), lambda i: (i, 0)),
                   pl.BlockSpec((tm, ATT_WIDTH), lambda i: (i, 0))],
        out_shape=[jax.ShapeDtypeStruct((rows // n_valid, width), BF16),
                   jax.ShapeDtypeStruct((rows, ATT_WIDTH), BF16 if fuse_cross else F32)],
        scratch_shapes=[pltpu.VMEM((ATT_WIDTH // HEAD_DIM, tm, HEAD_DIM), F32)],
        compiler_params=_params("arbitrary"),
        name="in_proj_b",
    )(*args)


CROSS_BATCH = 4


def _cross_sample_kernel(qc_ref, gq_ref, mem_ref, out_ref):
    for b in range(qc_ref.shape[0]):
        outs = _cross_heads(qc_ref[b], mem_ref.at[pl.ds(b, 1)], gq_ref[...])
        for hd in range(N_HEADS):
            out_ref[b, :, hd * HEAD_DIM:(hd + 1) * HEAD_DIM] = outs[hd].astype(out_ref.dtype)


def _cross_sample(qc, g_qc, mem_kv, first):
    nb, tq, _ = qc.shape
    cb = math.gcd(nb, CROSS_BATCH)
    assert first % cb == 0
    return pl.pallas_call(
        _cross_sample_kernel,
        grid=(nb // cb,),
        in_specs=[pl.BlockSpec((cb, tq, ATT_WIDTH), lambda b: (b, 0, 0)),
                  _full((1, HEAD_DIM)),
                  pl.BlockSpec((cb,) + mem_kv.shape[1:], lambda b: (first // cb + b, 0, 0))],
        out_specs=pl.BlockSpec((cb, tq, ATT_WIDTH), lambda b: (b, 0, 0)),
        out_shape=jax.ShapeDtypeStruct((nb, tq, ATT_WIDTH), BF16),
        compiler_params=_params("arbitrary"),
        name="cross_sample",
    )(qc, g_qc.reshape(1, HEAD_DIM), mem_kv)


INFLIGHT = 4


def _band_block(q, k, v):
    n = k.shape[0]
    dist = (n - BLOCK + lax.broadcasted_iota(jnp.int32, (BLOCK, n), 0)
            - lax.broadcasted_iota(jnp.int32, (BLOCK, n), 1))
    s = jnp.where((dist >= 0) & (dist <= DIL_SPAN), _dot_nt(q, k) * SCALE, NEG)
    m = jnp.max(s, axis=-1, keepdims=True)
    p = jnp.exp(s - m)
    l = jnp.sum(p, axis=-1, keepdims=True)
    return _dot(p.astype(BF16), v) / l, m + jnp.log(l)


def _attn_prompt_kernel(q0, q1, q2, k0, k1, k2, v0, v1, v2, out_ref, o_scr, l_scr, *, seq):
    qkv_refs = ((q0, k0, v0), (q1, k1, v1), (q2, k2, v2))
    for g, (_, r) in enumerate(DIL_GROUPS):
        q_ref, k_ref, v_ref = qkv_refs[g]
        nblk = seq // r // BLOCK

        def store(rho, blk, o, lse, g=g, r=r):
            start = blk * (BLOCK * r) + rho
            if r == 1:
                idx = pl.ds(pl.multiple_of(start, BLOCK), BLOCK)
            else:
                idx = pl.ds(start, BLOCK, stride=r)
            o_scr[g, idx, :] = o
            l_scr[g, idx, :] = jnp.broadcast_to(lse, (BLOCK, HEAD_DIM))

        def first(rho, q_ref=q_ref, k_ref=k_ref, v_ref=v_ref, store=store):
            lanes = slice(rho * HEAD_DIM, (rho + 1) * HEAD_DIM)
            o, lse = _band_block(q_ref[0, 0, :BLOCK, lanes], k_ref[0, 0, :BLOCK, lanes],
                                 v_ref[0, 0, :BLOCK, lanes])
            store(rho, 0, o, lse)

        def later(rho, blk, q_ref=q_ref, k_ref=k_ref, v_ref=v_ref, store=store):
            lanes = slice(rho * HEAD_DIM, (rho + 1) * HEAD_DIM)
            cur = pl.ds(pl.multiple_of(blk * BLOCK, BLOCK), BLOCK)
            both = pl.ds(pl.multiple_of((blk - 1) * BLOCK, BLOCK), 2 * BLOCK)
            o, lse = _band_block(q_ref[0, 0, cur, lanes], k_ref[0, 0, both, lanes], v_ref[0, 0, both, lanes])
            store(rho, blk, o, lse)

        for rho0 in range(0, r, INFLIGHT):
            rhos = range(rho0, min(r, rho0 + INFLIGHT))
            for rho in rhos:
                first(rho)

            def body(blk, carry, rhos=rhos, later=later):
                for rho in rhos:
                    later(rho, blk)
                return carry
            if nblk > 1:
                lax.fori_loop(1, nblk, body, 0, unroll=max(1, INFLIGHT // len(rhos)))

    def combine(c, carry):
        rows = pl.ds(pl.multiple_of(c * BLOCK, BLOCK), BLOCK)
        ls = [l_scr[g, rows, :] for g in range(N_GROUPS)]
        m = jnp.maximum(jnp.maximum(ls[0], ls[1]), ls[2])
        es = [jnp.exp(l - m) for l in ls]
        num = es[0] * o_scr[0, rows, :] + es[1] * o_scr[1, rows, :] + es[2] * o_scr[2, rows, :]
        out_ref[0, rows, :] = (num / (es[0] + es[1] + es[2])).astype(out_ref.dtype)
        return carry
    lax.fori_loop(0, seq // BLOCK, combine, 0)


def _attn_prompt(qkv_groups, n_seq, seq):
    in_specs, args = [], []
    for role in range(3):
        for g, (_, r) in enumerate(DIL_GROUPS):
            in_specs.append(pl.BlockSpec((1, 1, seq // r, r * HEAD_DIM),
                                         lambda b, h, role=role: (role * N_HEADS + h, b, 0, 0)))
            args.append(qkv_groups[g])
    return pl.pallas_call(
        functools.partial(_attn_prompt_kernel, seq=seq),
        grid=(n_seq, N_HEADS),
        in_specs=in_specs,
        out_specs=pl.BlockSpec((1, seq, HEAD_DIM), lambda b, h: (b, 0, h)),
        out_shape=jax.ShapeDtypeStruct((n_seq, seq, ATT_WIDTH), BF16),
        scratch_shapes=[pltpu.VMEM((N_GROUPS, seq, HEAD_DIM), F32),
                        pltpu.VMEM((N_GROUPS, seq, HEAD_DIM), F32)],
        compiler_params=_params("arbitrary", "arbitrary"),
        name="attn_prompt",
    )(*args)


COPY_ROWS = 1024


def _attn_sample_kernel(q_ref, kvn_ref, win_ref, o_ref, lse_ref, new_ref, *, r, t_new):
    lb = win_ref.shape[0] // KV_ROWS
    tq = q_ref.shape[1]
    shift = t_new * KV_ROWS
    n_keep = (lb - t_new) * KV_ROWS
    n_full = n_keep // COPY_ROWS

    def copy(c, carry):
        dst = pl.multiple_of(c * COPY_ROWS, COPY_ROWS)
        src = pl.multiple_of(c * COPY_ROWS + shift, KV_ROWS)
        new_ref[pl.ds(dst, COPY_ROWS), :] = win_ref[pl.ds(src, COPY_ROWS), :]
        return carry
    lax.fori_loop(0, n_full, copy, 0)
    rest = n_keep - n_full * COPY_ROWS
    if rest:
        new_ref[n_full * COPY_ROWS:n_keep, :] = win_ref[n_full * COPY_ROWS + shift:, :]
    new_ref[n_keep:, :] = kvn_ref[...]
    t_i = lax.broadcasted_iota(jnp.int32, (tq, lb), 0)
    k_i = lax.broadcasted_iota(jnp.int32, (tq, lb), 1)
    dist = lb + t_i - k_i
    ok_buf = (dist % r == 0) & (dist <= r * DIL_SPAN)
    t_n = lax.broadcasted_iota(jnp.int32, (tq, BLOCK), 0)
    k_n = lax.broadcasted_iota(jnp.int32, (tq, BLOCK), 1) - (BLOCK - t_new)
    dn = t_n - k_n
    ok_new = (k_n >= 0) & (dn >= 0) & (dn % r == 0) & (dn <= r * DIL_SPAN)
    tail0 = (lb - BLOCK) * KV_ROWS
    for hd in range(N_HEADS):
        sl = slice(hd * HEAD_DIM, (hd + 1) * HEAD_DIM)
        q = q_ref[0, :, sl]
        kb = win_ref[_kv_rows(0, hd, lb), :].astype(BF16)
        vb = win_ref[_kv_rows(1, hd, lb), :].astype(BF16)
        kn = new_ref[pl.ds(tail0 + hd, BLOCK, stride=KV_ROWS), :].astype(BF16)
        vn = new_ref[pl.ds(tail0 + N_HEADS + hd, BLOCK, stride=KV_ROWS), :].astype(BF16)
        sb = jnp.where(ok_buf, _dot_nt(q, kb) * SCALE, NEG)
        sn = jnp.where(ok_new, _dot_nt(q, kn) * SCALE, NEG)
        m = jnp.maximum(jnp.max(sb, axis=-1, keepdims=True), jnp.max(sn, axis=-1, keepdims=True))
        pb = jnp.exp(sb - m)
        pn = jnp.exp(sn - m)
        l = jnp.sum(pb, axis=-1, keepdims=True) + jnp.sum(pn, axis=-1, keepdims=True)
        o = _dot(pb.astype(BF16), vb) + _dot(pn.astype(BF16), vn)
        o_ref[0, :, sl] = o / l
        lse_ref[0, :, sl] = jnp.broadcast_to(m + jnp.log(l), (tq, HEAD_DIM))


def _attn_sample(q, kv_new, win, nb, r):
    tq = q.shape[1]
    rows_new = kv_new.shape[0] // nb
    rows_win = win.shape[0] // nb
    return pl.pallas_call(
        functools.partial(_attn_sample_kernel, r=r, t_new=rows_new // KV_ROWS),
        grid=(nb,),
        in_specs=[pl.BlockSpec((1, tq, ATT_WIDTH), lambda b: (b, 0, 0)),
                  pl.BlockSpec((rows_new, HEAD_DIM), lambda b: (b, 0)),
                  pl.BlockSpec((rows_win, HEAD_DIM), lambda b: (b, 0))],
        out_specs=[pl.BlockSpec((1, tq, ATT_WIDTH), lambda b: (b, 0, 0)),
                   pl.BlockSpec((1, tq, ATT_WIDTH), lambda b: (b, 0, 0)),
                   pl.BlockSpec((rows_win, HEAD_DIM), lambda b: (b, 0))],
        out_shape=[jax.ShapeDtypeStruct((nb, tq, ATT_WIDTH), F32),
                   jax.ShapeDtypeStruct((nb, tq, ATT_WIDTH), F32),
                   jax.ShapeDtypeStruct(win.shape, F32)],
        compiler_params=_params("arbitrary"),
        name=f"attn_sample_r{r}",
    )(q, kv_new, win)


def _combine_kernel(o0, o1, o2, l0, l1, l2, out_ref):
    ls = [l0[...], l1[...], l2[...]]
    m = jnp.maximum(jnp.maximum(ls[0], ls[1]), ls[2])
    es = [jnp.exp(l - m) for l in ls]
    num = es[0] * o0[...] + es[1] * o1[...] + es[2] * o2[...]
    out_ref[...] = (num / (es[0] + es[1] + es[2])).astype(out_ref.dtype)


def _combine_groups(outs, lses):
    shape = outs[0].shape
    return pl.pallas_call(
        _combine_kernel,
        in_specs=[_full(shape)] * 6,
        out_specs=_full(shape),
        out_shape=jax.ShapeDtypeStruct(shape, BF16),
        grid=(1,),
        compiler_params=_params("arbitrary"),
        name="combine_groups",
    )(*outs, *lses)


FF_CHUNK = 256
CARRY_ROWS = 8
SHIFT_SLOTS = 4


def _out_ffn_kernel(*refs, seq_len, tiles_per_seq, tail_rows):
    if seq_len is None:
        (x_ref, mix_ref, cr_ref, wo_ref, g_ref, wup_ref, cw_ref, cb_ref, wdn_ref,
         out_ref, tail_ref, shift, carry) = refs
    else:
        (x_ref, mix_ref, cr_ref, wo_ref, g_ref, wup_ref, cw_ref, cb_ref, wdn_ref, e1_ref, e2_ref,
         out_ref, tail_ref, shift) = refs
    tm = x_ref.shape[0]
    d_ff = wdn_ref.shape[0]
    x1 = (x_ref[...] + _dot(mix_ref[...], wo_ref[:ATT_WIDTH, :])
          + _dot(cr_ref[...], wo_ref[ATT_WIDTH:, :]))
    h = _rms(x1, g_ref[...]).astype(BF16)
    if seq_len is None:
        @pl.when(pl.program_id(0) % tiles_per_seq == 0)
        def _():
            carry[...] = jnp.zeros_like(carry)
    else:
        t = lax.broadcasted_iota(jnp.int32, (tm, 1), 0) % seq_len
        has1 = t >= 1
        has2 = t >= 2

    def chunk_cols(j):
        return (slice(j * FF_CHUNK, (j + 1) * FF_CHUNK),
                slice(d_ff + j * FF_CHUNK, d_ff + (j + 1) * FF_CHUNK))

    def up_proj(j):
        return tuple(_dot(h, wup_ref[:, cols]) for cols in chunk_cols(j))

    def conv(up, cols, slot):
        buf = shift.at[slot]
        if seq_len is None:
            buf[:CARRY_ROWS, :] = carry[:, cols]
            carry[:, cols] = up[tm - CARRY_ROWS:, :]
        else:
            buf[:CARRY_ROWS, :] = jnp.zeros((CARRY_ROWS, FF_CHUNK), F32)
        buf[CARRY_ROWS:, :] = up
        tail_ref[0, :, cols] = up[tm - tail_rows:, :]
        prev1 = buf[CARRY_ROWS - 1:CARRY_ROWS - 1 + tm, :]
        prev2 = buf[CARRY_ROWS - 2:CARRY_ROWS - 2 + tm, :]
        if seq_len is not None:
            prev1 = jnp.where(has1, prev1, e1_ref[:, cols])
            prev2 = jnp.where(has2, prev2, e2_ref[:, cols])
        return (cb_ref[:, cols] + cw_ref[0:1, cols] * prev2 + cw_ref[1:2, cols] * prev1
                + cw_ref[2:3, cols] * up)

    acc = jnp.zeros_like(x1)
    n_chunks = d_ff // FF_CHUNK
    ups = up_proj(0)
    act = None
    for j in range(n_chunks + 1):
        nxt = up_proj(j + 1) if j + 1 < n_chunks else None
        if act is not None:
            acc = acc + _dot(act, wdn_ref[(j - 1) * FF_CHUNK:j * FF_CHUNK, :])
        if j < n_chunks:
            a, b = (conv(up, cols, 2 * (j % 2) + s) for s, (up, cols) in enumerate(zip(ups, chunk_cols(j))))
            act = (a * jax.nn.sigmoid(a) * b).astype(BF16)
        ups = nxt
    out_ref[...] = x1 + acc


def _out_ffn(x2d, mix, cross, w_out_bf, g_ffn, w_up_bf, conv_w, conv_b, w_down_bf, tm,
             tiles_per_seq=None, seq_len=None, e1=None, e2=None, tail_rows=CARRY_ROWS):
    rows, d = x2d.shape
    d_ff = w_down_bf.shape[0]
    n_tiles = rows // tm
    in_specs = [
        pl.BlockSpec((tm, d), lambda i: (i, 0)),
        pl.BlockSpec((tm, ATT_WIDTH), lambda i: (i, 0)),
        pl.BlockSpec((tm, ATT_WIDTH), lambda i: (i, 0)),
        _full(w_out_bf.shape), _full((1, d)), _full(w_up_bf.shape),
        _full((CONV_W, 2 * d_ff)), _full((1, 2 * d_ff)), _full(w_down_bf.shape),
    ]
    args = [x2d, mix, cross, w_out_bf, g_ffn.reshape(1, d), w_up_bf, conv_w, conv_b.reshape(1, 2 * d_ff),
            w_down_bf]
    scratch = [pltpu.VMEM((SHIFT_SLOTS, CARRY_ROWS + tm, FF_CHUNK), F32)]
    if seq_len is None:
        scratch += [pltpu.VMEM((CARRY_ROWS, 2 * d_ff), F32)]
    else:
        in_specs += [pl.BlockSpec((tm, 2 * d_ff), lambda i: (i, 0))] * 2
        args += [e1, e2]
    return pl.pallas_call(
        functools.partial(_out_ffn_kernel, seq_len=seq_len, tiles_per_seq=tiles_per_seq,
                          tail_rows=tail_rows),
        grid=(n_tiles,),
        in_specs=in_specs,
        out_specs=[pl.BlockSpec((tm, d), lambda i: (i, 0)),
                   pl.BlockSpec((1, tail_rows, 2 * d_ff), lambda i: (i, 0, 0))],
        out_shape=[jax.ShapeDtypeStruct((rows, d), F32),
                   jax.ShapeDtypeStruct((n_tiles, tail_rows, 2 * d_ff), F32)],
        scratch_shapes=scratch,
        compiler_params=_params("arbitrary"),
        name="out_ffn",
    )(*args)


def _swap_halves(x):
    return pltpu.roll(x, S5_STATE, 1)


def _s5_prep_kernel(lam_ref, logdt_ref, bt_ref, c_ref, d_ref, t_ref, bp_ref, cpt_ref, lam_out_ref, *, n_valid):
    gc = lam_ref.shape[0]
    n_groups = gc // S5_GROUP
    lane = lax.broadcasted_iota(jnp.int32, lam_ref.shape, 1)
    first = lane < S5_STATE
    sign = jnp.where(first, -1.0, 1.0)
    a = lam_ref[...]
    a_sw = _swap_halves(a)
    are = jnp.where(first, a, a_sw)
    aim = jnp.where(first, a_sw, a)
    dt = jnp.exp(logdt_ref[...])
    mag = jnp.exp(are * dt)
    lr = mag * jnp.cos(aim * dt)
    li = mag * jnp.sin(aim * dt)
    den = are * are + aim * aim
    xr = lr - 1.0
    f_re = (xr * are + li * aim) / den
    f_im = (li * are - xr * aim) / den
    lb = sign * li

    def cmul(x, m_re, m_sw):
        return x * m_re + _swap_halves(x) * m_sw

    c = c_ref[...]
    c_neg = c * -sign
    ri = lax.broadcasted_iota(jnp.int32, (gc, gc), 0)
    ci = lax.broadcasted_iota(jnp.int32, (gc, gc), 1)
    same_group = (ri // S5_GROUP) == (ci // S5_GROUP)
    e = cmul(bt_ref[...], f_re, sign * f_im)
    cl = c
    pw = jnp.where(first, 1.0, 0.0)
    zr = lax.broadcasted_iota(jnp.int32, (gc, S5_BLOCK), 0)
    zc = lax.broadcasted_iota(jnp.int32, (gc, S5_BLOCK), 1)
    same_out = (zr % S5_GROUP) == (zc % S5_GROUP)
    lags = jnp.zeros((gc, S5_BLOCK), F32)
    for k in range(S5_SUB):
        kmat = jnp.where(same_group, _dot_nt(e, c_neg, precision=HIGHEST), 0.0)
        if k == 0:
            kmat = kmat + jnp.where(ri == ci, d_ref[...], 0.0)
        place = jnp.where(same_out & (zc // S5_GROUP == k), 1.0, 0.0)
        lags = lags + jnp.dot(kmat, place, precision=HIGHEST, preferred_element_type=F32)
        tau = n_valid - 1 - k
        if tau >= 0:
            both = jnp.concatenate([e, _swap_halves(e)], axis=-1)
            bp_ref[:, tau] = both.reshape(n_groups, S5_GROUP, 4 * S5_STATE).astype(bp_ref.dtype)
        cl = cmul(cl, lr, lb)
        cpt_ref[:, k] = (cl * -sign).reshape(n_groups, S5_GROUP, 2 * S5_STATE).astype(cpt_ref.dtype)
        e = cmul(e, lr, lb)
        pw = cmul(pw, lr, lb)
        if k + 1 == n_valid:
            lam_out_ref[...] = pw
    for tau in range(n_valid, S5_SUB):
        bp_ref[:, tau] = jnp.zeros((n_groups, S5_GROUP, 4 * S5_STATE), bp_ref.dtype)
    for taup in range(S5_SUB):
        moved = lags if taup == 0 else jnp.where(zc >= taup * S5_GROUP, pltpu.roll(lags, taup * S5_GROUP, 1), 0.0)
        t_ref[:, taup] = moved.reshape(n_groups, S5_GROUP, S5_BLOCK).astype(t_ref.dtype)


def _s5_prep(lam_re, lam_im, log_dt, b_re, b_im, c_re, c_im, d_skip, n_valid):
    n_groups = lam_re.shape[0]
    gc = n_groups * S5_GROUP
    rep = lambda t: jnp.repeat(t, S5_GROUP, axis=0)
    lam_p = rep(jnp.concatenate([lam_re, lam_im], axis=-1))
    logdt = rep(log_dt.reshape(n_groups, 1))
    bt_p = jnp.concatenate([b_re.transpose(0, 2, 1), b_im.transpose(0, 2, 1)], axis=-1).reshape(gc, 2 * S5_STATE)
    c_p = jnp.concatenate([c_re, c_im], axis=-1).reshape(gc, 2 * S5_STATE)
    shapes = [(n_groups, S5_SUB, S5_GROUP, S5_BLOCK), (n_groups, S5_SUB, S5_GROUP, 4 * S5_STATE),
              (n_groups, S5_SUB, S5_GROUP, 2 * S5_STATE)]
    t, bp, cpt, lam_n = pl.pallas_call(
        functools.partial(_s5_prep_kernel, n_valid=n_valid),
        grid=(1,),
        in_specs=[_full(lam_p.shape), _full(logdt.shape), _full(bt_p.shape), _full(c_p.shape), _full((gc, 1))],
        out_specs=[*[_full(s) for s in shapes], _full((gc, 2 * S5_STATE))],
        out_shape=[*[jax.ShapeDtypeStruct(s, BF16) for s in shapes],
                   jax.ShapeDtypeStruct((gc, 2 * S5_STATE), F32)],
        compiler_params=_params("arbitrary"),
        name="s5_prep",
    )(lam_p, logdt, bt_p, c_p, d_skip.reshape(gc, 1))
    merge = lambda a: a.reshape(n_groups, S5_BLOCK, a.shape[-1])
    return merge(t), merge(bp), merge(cpt), lam_n[::S5_GROUP]


SCAN_GROUPS = 8
SUBLANES = 8
GLU_ROWS = 512


def _s5_core_kernel(v_ref, x0_ref, x0s_ref, t_ref, bp_ref, cpt_ref, lam_ref, wglu_ref, bglu_ref,
                    mix_ref, xfin_ref, s_scr, xprev_scr, m_scr, stage, *, n_valid, n_sub):
    rows = v_ref.shape[0]
    n_groups = t_ref.shape[0]
    lane = lax.broadcasted_iota(jnp.int32, (1, 2 * S5_STATE), 1)
    first = lane < S5_STATE
    blk = lambda g: slice(g * S5_BLOCK, (g + 1) * S5_BLOCK)
    tile = lambda i: slice(i * 2 * S5_STATE, (i + 1) * 2 * S5_STATE)
    for g in range(n_groups):
        s_scr[:, blk(g)] = _dot(v_ref[:, blk(g)], bp_ref[g])
        lam = lam_ref[g:g + 1, :]
        lam_sw = _swap_halves(lam)
        m_scr[0:1, tile(g)] = jnp.where(first, lam, lam_sw)
        m_scr[1:2, tile(g)] = jnp.where(first, -lam_sw, lam)
    if n_sub == 1:
        for g in range(n_groups):
            x, xs = x0_ref[:, tile(g)], x0s_ref[:, tile(g)]
            xprev_scr[:, tile(g)] = x
            xfin_ref[:, tile(g)] = (x * m_scr[0:1, tile(g)] + xs * m_scr[1:2, tile(g)]
                                    + s_scr[:, tile(2 * g)])
    else:
        assert rows == n_sub
        for g0 in range(0, n_groups, SCAN_GROUPS):
            gs = range(g0, min(n_groups, g0 + SCAN_GROUPS))

            def body(i, carry, gs=gs):
                rows8 = pl.ds(pl.multiple_of(i * SUBLANES, SUBLANES), SUBLANES)
                sub_i = lax.broadcasted_iota(jnp.int32, (SUBLANES, 2 * S5_STATE), 0)
                out = []
                for g, (x, xs) in zip(gs, carry):
                    s8, ssw8 = s_scr[rows8, tile(2 * g)], s_scr[rows8, tile(2 * g + 1)]
                    prev8 = jnp.zeros((SUBLANES, 2 * S5_STATE), F32)
                    for r in range(SUBLANES):
                        prev8 = jnp.where(sub_i == r, x, prev8)
                        m_re, m_sw = m_scr[0:1, tile(g)], m_scr[1:2, tile(g)]
                        x, xs = (x * m_re + xs * m_sw + s8[r:r + 1, :],
                                 xs * m_re - x * m_sw + ssw8[r:r + 1, :])
                    xprev_scr[rows8, tile(g)] = prev8
                    out.append((x, xs))
                return tuple(out)
            init = tuple((x0_ref[0:1, tile(g)], x0s_ref[0:1, tile(g)]) for g in gs)
            fin = lax.fori_loop(0, n_sub // SUBLANES, body, init)
            for g, (x, _) in zip(gs, fin):
                xfin_ref[:, tile(g)] = jnp.broadcast_to(x, (xfin_ref.shape[0], 2 * S5_STATE))
    for g in range(n_groups):
        s_scr[:, blk(g)] = (_dot(v_ref[:, blk(g)], t_ref[g])
                            + _dot_nt(xprev_scr[:, tile(g)].astype(BF16), cpt_ref[g]))
    slot = lax.broadcasted_iota(jnp.int32, (rows, HEAD_DIM), 1) // S5_GROUP
    per_tile = HEAD_DIM // S5_GROUP
    for cb in range(ATT_WIDTH // HEAD_DIM):
        for tau in range(n_valid):
            half, s = divmod(tau, per_tile)
            acc = jnp.zeros((rows, HEAD_DIM), F32)
            for p in range(per_tile):
                src = s_scr[:, tile(2 * (cb * per_tile + p) + half)]
                shift = ((p - s) * S5_GROUP) % HEAD_DIM
                acc = jnp.where(slot == p, pltpu.roll(src, shift, 1) if shift else src, acc)
            stage[cb, pl.ds(tau, rows, stride=n_valid), :] = acc
    n_tok = rows * n_valid
    chunk = min(n_tok, GLU_ROWS)

    def glu(i, carry):
        r = pl.ds(pl.multiple_of(i * chunk, chunk), chunk)
        y = jax.nn.gelu(jnp.concatenate([stage[cb, r, :] for cb in range(ATT_WIDTH // HEAD_DIM)], axis=-1))
        z = _dot(y.astype(BF16), wglu_ref[...]) + bglu_ref[...]
        mix_ref[r, :] = (y * jax.nn.sigmoid(z)).astype(mix_ref.dtype)
        return carry
    lax.fori_loop(0, n_tok // chunk, glu, 0)


def _s5_mixer(v, rows_per_tile, n_valid, n_sub, x0, x0s, t, bp, cpt, lam_n, w_glu_bf, b_glu):
    m, width = v.shape
    n_tiles = m // rows_per_tile
    r0 = x0.shape[0] // n_tiles
    n_state = x0.shape[1]
    n_tok = rows_per_tile * n_valid
    return pl.pallas_call(
        functools.partial(_s5_core_kernel, n_valid=n_valid, n_sub=n_sub),
        grid=(n_tiles,),
        in_specs=[pl.BlockSpec((rows_per_tile, width), lambda i: (i, 0)),
                  pl.BlockSpec((r0, n_state), lambda i: (i, 0)),
                  pl.BlockSpec((r0, n_state), lambda i: (i, 0)),
                  _full(t.shape), _full(bp.shape), _full(cpt.shape), _full(lam_n.shape),
                  _full(w_glu_bf.shape), _full((1, ATT_WIDTH))],
        out_specs=[pl.BlockSpec((n_tok, ATT_WIDTH), lambda i: (i, 0)),
                   pl.BlockSpec((r0, n_state), lambda i: (i, 0))],
        out_shape=[jax.ShapeDtypeStruct((m * n_valid, ATT_WIDTH), BF16),
                   jax.ShapeDtypeStruct(x0.shape, F32)],
        scratch_shapes=[pltpu.VMEM((rows_per_tile, width), F32),
                        pltpu.VMEM((rows_per_tile, n_state), F32),
                        pltpu.VMEM((8, n_state), F32),
                        pltpu.VMEM((ATT_WIDTH // HEAD_DIM, n_tok, HEAD_DIM), F32)],
        compiler_params=_params("arbitrary"),
        name="s5_core",
    )(v, x0, x0s, t, bp, cpt, lam_n, w_glu_bf, b_glu.reshape(1, ATT_WIDTH))


def _pack_state(s):
    n = s.shape[0]
    packed = s.transpose(0, 2, 1, 3).reshape(n, -1)
    swapped = jnp.stack([s[:, 1], s[:, 0]], axis=1).transpose(0, 2, 1, 3).reshape(n, -1)
    return packed, swapped


def _unpack_state(x):
    n = x.shape[0]
    return x.reshape(n, -1, 2, S5_STATE).transpose(0, 2, 1, 3)


PROMPT_TILE = 512
Q_PAD = 16


def _pad_rows(t, n):
    return jnp.pad(t, ((0, 0), (0, n - t.shape[1]), (0, 0)))


def kernel(x_prompt, x_sample, cache_win0_kv, cache_win1_kv, cache_win2_kv, cache_mem_kv, state_s5,
           state_ffn_conv, mem_prompt, g_mix, g_ffn, w_in_a, g_q_dil, g_k_dil, w_in_b, s5_lam_re,
           s5_lam_im, s5_log_dt, s5_b_re, s5_b_im, s5_c_re, s5_c_im, s5_d, w_glu, b_glu, g_mem,
           w_mem_kv, g_q_cross, g_k_cross, w_out, w_up, conv_w, conv_b, w_down):
    nb, seq, d = x_prompt.shape
    db, ts, _ = x_sample.shape
    depth = g_mix.shape[0]
    n_mem = mem_prompt.shape[1]
    d_ff2 = w_up.shape[2]
    assert ts >= CONV_W - 1 and seq % PROMPT_TILE == 0 and ts <= Q_PAD
    caches = (cache_win0_kv, cache_win1_kv, cache_win2_kv)
    w_mem_bf = w_mem_kv.astype(BF16)
    mem_cache = cache_mem_kv.reshape(depth * db, n_mem * KV_ROWS, HEAD_DIM)

    tab_p = _rope_tables(jnp.arange(seq, dtype=jnp.int32))
    tab_s = tuple(jnp.tile(t, (db, 1)) for t in _rope_tables(PAST_LEN + jnp.arange(ts, dtype=jnp.int32)))
    win_keep = tuple(min(w, seq) for w, _ in DIL_GROUPS)
    rows_s = db * ts
    kv_tail = (2, N_HEADS, HEAD_DIM)

    mem_p = _mem_kv(mem_prompt.reshape(nb * n_mem, d), g_mem, w_mem_bf, g_k_cross)

    xp = x_prompt.reshape(nb * seq, d)
    xs = x_sample.reshape(rows_s, d)
    p_win, s_win = [[] for _ in DIL_GROUPS], [[] for _ in DIL_GROUPS]
    p_s5, s_s5, p_conv, s_conv = [], [], [], []
    tiles_per_seq = seq // PROMPT_TILE
    for i in range(depth):
        mem_i = mem_p[i].reshape(nb, n_mem * KV_ROWS, HEAD_DIM)
        if i % 2 == 0:
            ia = i // 2
            w_in_a_bf = w_in_a[ia].astype(BF16)
            *qkv_groups, w0, w1, w2, cross_p = _in_proj_a(
                xp, seq, PROMPT_TILE, g_mix[i], w_in_a_bf, tab_p, g_q_dil[ia], g_k_dil[ia],
                g_q_cross[i], mem_i, win_keep, dilated=True)
            for g, w in enumerate((w0, w1, w2)):
                p_win[g].append(w.reshape((nb, win_keep[g]) + kv_tail))
            mix_p = _attn_prompt(qkv_groups, nb, seq).reshape(nb * seq, ATT_WIDTH)
            qkv_s, k0, k1, k2, qc_s = _in_proj_a(
                xs, rows_s, rows_s, g_mix[i], w_in_a_bf, tab_s, g_q_dil[ia], g_k_dil[ia],
                None, None, (rows_s,) * N_GROUPS, dilated=False)
            outs, lses = [], []
            for g, ((_, r), kv_new) in enumerate(zip(DIL_GROUPS, (k0, k1, k2))):
                q_g = qkv_s[g * N_HEADS:(g + 1) * N_HEADS]
                q_g = _pad_rows(q_g.transpose(1, 0, 2).reshape(db, ts, ATT_WIDTH), Q_PAD)
                win = caches[g][ia]
                o_g, lse_g, new_g = _attn_sample(q_g, kv_new, win.reshape(-1, HEAD_DIM), db, r)
                outs.append(o_g[:, :ts].reshape(rows_s, ATT_WIDTH))
                lses.append(lse_g[:, :ts].reshape(rows_s, ATT_WIDTH))
                s_win[g].append(new_g.reshape(win.shape))
            mix_s = _combine_groups(outs, lses)
        else:
            ib = i // 2
            prm = (s5_lam_re[ib], s5_lam_im[ib], s5_log_dt[ib], s5_b_re[ib], s5_b_im[ib], s5_c_re[ib],
                   s5_c_im[ib], s5_d[ib])
            n_state = s5_lam_re.shape[1] * 2 * S5_STATE
            w_in_b_bf, w_glu_bf = w_in_b[ib].astype(BF16), w_glu[ib].astype(BF16)
            u_p, cross_p = _in_proj_b(xp, seq, PROMPT_TILE, S5_SUB, g_mix[i], w_in_b_bf, g_q_cross[i], mem_i)
            zero = jnp.zeros((nb * CARRY_ROWS, n_state), F32)
            mix_p, fin_p = _s5_mixer(u_p, seq // S5_SUB, S5_SUB, seq // S5_SUB, zero, zero,
                                     *_s5_prep(*prm, S5_SUB), w_glu_bf, b_glu[ib])
            p_s5.append(_unpack_state(fin_p.reshape(nb, CARRY_ROWS, n_state)[:, 0]))
            u_s, qc_s = _in_proj_b(xs, rows_s, rows_s, ts, g_mix[i], w_in_b_bf, None, None)
            mix_s, fin_s = _s5_mixer(u_s, db, ts, 1, *_pack_state(state_s5[ib]), *_s5_prep(*prm, ts),
                                     w_glu_bf, b_glu[ib])
            s_s5.append(_unpack_state(fin_s))
        cross_s = _cross_sample(_pad_rows(qc_s.reshape(db, ts, ATT_WIDTH), Q_PAD), g_q_cross[i], mem_cache,
                                first=i * db)
        cross_s = cross_s[:, :ts].reshape(rows_s, ATT_WIDTH)

        ffn_w = (w_out[i].astype(BF16), g_ffn[i], w_up[i].astype(BF16), conv_w[i], conv_b[i],
                 w_down[i].astype(BF16))
        xp, tails = _out_ffn(xp, mix_p, cross_p, *ffn_w, PROMPT_TILE, tiles_per_seq=tiles_per_seq)
        p_conv.append(tails.reshape(nb, tiles_per_seq, CARRY_ROWS, d_ff2)[:, -1, CARRY_ROWS - (CONV_W - 1):])
        buf = state_ffn_conv[i]
        zero = jnp.zeros((db, ts - 2, d_ff2), F32)
        e1 = jnp.concatenate([buf[:, 1:2], zero, zero[:, :1]], axis=1).reshape(rows_s, d_ff2)
        e2 = jnp.concatenate([buf, zero], axis=1).reshape(rows_s, d_ff2)
        xs, tails = _out_ffn(xs, mix_s, cross_s, *ffn_w, rows_s, seq_len=ts, e1=e1, e2=e2, tail_rows=rows_s)
        s_conv.append(tails.reshape(db, ts, d_ff2)[:, ts - (CONV_W - 1):])

    return (xp.reshape(nb, seq, d), xs.reshape(db, ts, d),
            jnp.stack(p_win[0]), jnp.stack(p_win[1]), jnp.stack(p_win[2]),
            mem_p.reshape((depth, nb, n_mem) + kv_tail),
            jnp.stack(p_s5), jnp.stack(p_conv),
            jnp.stack(s_win[0]), jnp.stack(s_win[1]), jnp.stack(s_win[2]),
            jnp.stack(s_s5), jnp.stack(s_conv))
```

```python
import functools
import math

import jax
import jax.numpy as jnp
from jax import lax
from jax.experimental import pallas as pl
from jax.experimental.pallas import tpu as pltpu

HEAD_DIM = 128
N_HEADS = 4
DIL_GROUPS = ((128, 1), (512, 4), (2048, 16))
N_GROUPS = len(DIL_GROUPS)
DIL_SPAN = 128
BLOCK = 128
ATT_WIDTH = N_HEADS * HEAD_DIM
KV_ROWS = 2 * N_HEADS
ROT_DIM = HEAD_DIM // 4
ROT_HALF = ROT_DIM // 2
ROPE_THETA = 500000.0
S5_GROUP = 16
S5_STATE = 64
S5_SUB = 16
S5_BLOCK = S5_SUB * S5_GROUP
CONV_W = 3
EPS = 1e-6
NEG = -1e30
SCALE = HEAD_DIM ** -0.5
PAST_LEN = 16384

VMEM_LIMIT_V7X = 56 * 1024 * 1024
BF16 = jnp.bfloat16
F32 = jnp.float32
HIGHEST = lax.Precision.HIGHEST


def _params(*sem):
    return pltpu.CompilerParams(dimension_semantics=sem, vmem_limit_bytes=VMEM_LIMIT_V7X)


def _rms(x, g):
    return x * lax.rsqrt(jnp.mean(x * x, axis=-1, keepdims=True) + EPS) * g


def _dot(a, b):
    return jnp.dot(a, b, preferred_element_type=F32)


def _dot_nt(a, b, precision=None):
    return lax.dot_general(a, b, (((1,), (1,)), ((), ())), precision=precision,
                           preferred_element_type=F32)


def _full(shape):
    nd = len(shape)
    return pl.BlockSpec(shape, lambda *_: (0,) * nd)


def _kv_rows(kv, head, n):
    return pl.ds(kv * N_HEADS + head, n, stride=KV_ROWS)


def _mem_kv_kernel(mem_ref, g_ref, w_ref, gk_ref, out_ref):
    tm = mem_ref.shape[0]
    h = _rms(mem_ref[...], g_ref[0]).astype(BF16)
    kv = _dot(h, w_ref[0])
    for hd in range(N_HEADS):
        sl = slice(hd * HEAD_DIM, (hd + 1) * HEAD_DIM)
        out_ref[0, _kv_rows(0, hd, tm), :] = _rms(kv[:, sl], gk_ref[0])
        out_ref[0, _kv_rows(1, hd, tm), :] = kv[:, ATT_WIDTH + hd * HEAD_DIM:ATT_WIDTH + (hd + 1) * HEAD_DIM]


def _mem_kv(mem2d, g_mem, w_kv_bf, g_k):
    depth, d, _ = w_kv_bf.shape
    rows = mem2d.shape[0]
    tm = min(rows, 512)
    return pl.pallas_call(
        _mem_kv_kernel,
        grid=(depth, rows // tm),
        in_specs=[
            pl.BlockSpec((tm, d), lambda l, i: (i, 0)),
            pl.BlockSpec((1, 1, d), lambda l, i: (l, 0, 0)),
            pl.BlockSpec((1, d, 2 * ATT_WIDTH), lambda l, i: (l, 0, 0)),
            pl.BlockSpec((1, 1, HEAD_DIM), lambda l, i: (l, 0, 0)),
        ],
        out_specs=pl.BlockSpec((1, tm * KV_ROWS, HEAD_DIM), lambda l, i: (l, i, 0)),
        out_shape=jax.ShapeDtypeStruct((depth, rows * KV_ROWS, HEAD_DIM), F32),
        compiler_params=_params("arbitrary", "arbitrary"),
        name="mem_kv",
    )(mem2d, g_mem.reshape(depth, 1, d), w_kv_bf, g_k.reshape(depth, 1, HEAD_DIM))


def _cross_heads(qc, mem_ref, gq):
    n_mem = mem_ref.shape[1] // KV_ROWS
    outs = []
    for hd in range(N_HEADS):
        q = _rms(qc[:, hd * HEAD_DIM:(hd + 1) * HEAD_DIM], gq).astype(BF16)
        k = mem_ref[0, _kv_rows(0, hd, n_mem), :].astype(BF16)
        v = mem_ref[0, _kv_rows(1, hd, n_mem), :].astype(BF16)
        s = _dot_nt(q, k) * SCALE
        m = jnp.max(s, axis=-1, keepdims=True)
        p = jnp.exp(s - m)
        l = jnp.sum(p, axis=-1, keepdims=True)
        outs.append(_dot(p.astype(BF16), v) / l)
    return outs


def _rope(x, cos_t, sin_lo, sin_hi):
    return (x * cos_t + pltpu.roll(x, HEAD_DIM - ROT_HALF, 1) * sin_lo
            + pltpu.roll(x, ROT_HALF, 1) * sin_hi)


def _in_proj_a_kernel(*refs, fuse_cross, dilated, win_rows, win_first, tiles_per_seq):
    n_in = 10 if fuse_cross else 8
    x_ref, g_ref, w_ref, cos_ref, slo_ref, shi_ref, gq_ref, gk_ref = refs[:8]
    n_qkv = N_GROUPS if dilated else 1
    qkv_refs = refs[n_in:n_in + n_qkv]
    win_refs = refs[n_in + n_qkv:n_in + n_qkv + N_GROUPS]
    cr_ref = refs[n_in + n_qkv + N_GROUPS]
    kv_stash = refs[n_in + n_qkv + N_GROUPS + 1]
    stage = refs[n_in + n_qkv + N_GROUPS + 2] if dilated else None
    tm = x_ref.shape[0]
    h = _rms(x_ref[...], g_ref[...]).astype(BF16)
    cos_t, sin_lo, sin_hi = cos_ref[...], slo_ref[...], shi_ref[...]
    def proj(c):
        return _dot(h, w_ref[:, c * ATT_WIDTH:(c + 1) * ATT_WIDTH])

    tile = pl.program_id(0) % tiles_per_seq
    in_window = [tile >= first for first in win_first]
    for g in range(N_GROUPS):
        @pl.when(jnp.logical_not(in_window[g]))
        def _(g=g):
            win_refs[g][...] = jnp.zeros_like(win_refs[g])

    y = proj(0)
    for c in range(3 * N_GROUPS):
        y_next = proj(c + 1)
        role, g = divmod(c, N_GROUPS)
        r = DIL_GROUPS[g][1]
        wr = win_rows[g]
        for hd in range(N_HEADS):
            yh = y[:, hd * HEAD_DIM:(hd + 1) * HEAD_DIM]
            if role == 0:
                yh = _rope(_rms(yh, gq_ref[g]), cos_t, sin_lo, sin_hi)
            elif role == 1:
                yh = _rope(_rms(yh, gk_ref[g]), cos_t, sin_lo, sin_hi)
            if role > 0:
                kv_stash[((role - 1) * N_GROUPS + g) * N_HEADS + hd] = yh
            if not dilated:
                qkv_refs[0][role * N_GROUPS * N_HEADS + g * N_HEADS + hd] = yh.astype(BF16)
            elif r == 1:
                qkv_refs[g][role * N_HEADS + hd, 0] = yh.astype(BF16)
            else:
                stage[hd] = yh
                for rho in range(r):
                    qkv_refs[g][role * N_HEADS + hd, 0, :, rho * HEAD_DIM:(rho + 1) * HEAD_DIM] = (
                        stage[hd, pl.ds(rho, tm // r, stride=r), :].astype(BF16))
        y = y_next
    qc = y
    if fuse_cross:
        outs = _cross_heads(qc, refs[9], refs[8][...])
        for hd in range(N_HEADS):
            cr_ref[:, hd * HEAD_DIM:(hd + 1) * HEAD_DIM] = outs[hd].astype(cr_ref.dtype)
    else:
        cr_ref[...] = qc
    for g in range(N_GROUPS):
        @pl.when(in_window[g])
        def _(g=g):
            wr = win_rows[g]
            for kv in range(2):
                for hd in range(N_HEADS):
                    win_refs[g][_kv_rows(kv, hd, wr), :] = kv_stash[(kv * N_GROUPS + g) * N_HEADS + hd,
                                                                    tm - wr:, :]


SLOTS = HEAD_DIM // S5_GROUP


def _slot_transpose(vs):
    slot = lax.broadcasted_iota(jnp.int32, vs[0].shape, 1) // S5_GROUP
    d = SLOTS // 2
    while d:
        low = (slot & d) == 0
        new = list(vs)
        for i in range(SLOTS):
            if not i & d:
                new[i] = jnp.where(low, vs[i], pltpu.roll(vs[i | d], d * S5_GROUP, 1))
                new[i | d] = jnp.where(low, pltpu.roll(vs[i], HEAD_DIM - d * S5_GROUP, 1), vs[i | d])
        vs = new
        d //= 2
    return vs


def _in_proj_b_kernel(*refs, fuse_cross, n_valid):
    if fuse_cross:
        x_ref, g_ref, w_ref, gqc_ref, mem_ref, u_ref, cr_ref, stage = refs
    else:
        x_ref, g_ref, w_ref, u_ref, cr_ref, stage = refs
    tm = x_ref.shape[0]
    n_rows = tm // n_valid
    h = _rms(x_ref[...], g_ref[...]).astype(BF16)
    u = _dot(h, w_ref[:, :ATT_WIDTH])
    zero = jnp.zeros((n_rows, HEAD_DIM), F32)
    for cb in range(ATT_WIDTH // HEAD_DIM):
        stage[cb] = u[:, cb * HEAD_DIM:(cb + 1) * HEAD_DIM]
        for half in range(S5_SUB // SLOTS):
            taus = range(half * SLOTS, (half + 1) * SLOTS)
            groups = _slot_transpose([stage[cb, pl.ds(tau, n_rows, stride=n_valid), :] if tau < n_valid else zero
                                      for tau in taus])
            for p in range(SLOTS):
                lanes = (2 * (cb * SLOTS + p) + half) * HEAD_DIM
                u_ref[:, lanes:lanes + HEAD_DIM] = groups[p].astype(BF16)
    qc = _dot(h, w_ref[:, ATT_WIDTH:])
    if fuse_cross:
        outs = _cross_heads(qc, mem_ref, gqc_ref[...])
        for hd in range(N_HEADS):
            cr_ref[:, hd * HEAD_DIM:(hd + 1) * HEAD_DIM] = outs[hd].astype(cr_ref.dtype)
    else:
        cr_ref[...] = qc


def _rope_tables(pos):
    inv = jnp.exp(-math.log(ROPE_THETA) * jnp.arange(ROT_HALF, dtype=F32) / ROT_HALF)
    ang = pos.astype(F32)[:, None] * inv[None, :]
    cos, sin = jnp.cos(ang), jnp.sin(ang)
    rows = pos.shape[0]
    ones = jnp.ones((rows, HEAD_DIM - ROT_DIM), F32)
    zeros = jnp.zeros((rows, HEAD_DIM - ROT_DIM), F32)
    z16 = jnp.zeros((rows, ROT_HALF), F32)
    cos_t = jnp.concatenate([cos, cos, ones], axis=1)
    sin_lo = jnp.concatenate([-sin, z16, zeros], axis=1)
    sin_hi = jnp.concatenate([z16, sin, zeros], axis=1)
    return cos_t, sin_lo, sin_hi


def _in_proj_a(x2d, seq, tm, g_mix, w_bf, tables, g_q, g_k, g_qc, mem_kv, win_keep, dilated):
    rows, d = x2d.shape
    n_in = w_bf.shape[1]
    tiles_per_seq = seq // tm
    n_seq = rows // seq
    fuse_cross = mem_kv is not None
    tab_tiles = tables[0].shape[0] // tm
    win_rows = tuple(min(k, tm) for k in win_keep)
    win_first = tuple(tiles_per_seq - k // wr for k, wr in zip(win_keep, win_rows))

    def win_spec(keep):
        wr = min(keep, tm)
        nblk = keep // wr
        first = tiles_per_seq - nblk

        def imap(i):
            b = i // tiles_per_seq
            t = i % tiles_per_seq
            return (b * nblk + jnp.maximum(t - first, 0), 0)
        return pl.BlockSpec((wr * KV_ROWS, HEAD_DIM), imap)

    in_specs = [
        pl.BlockSpec((tm, d), lambda i: (i, 0)),
        _full((1, d)),
        _full((d, n_in)),
        pl.BlockSpec((tm, HEAD_DIM), lambda i: (i % tab_tiles, 0)),
        pl.BlockSpec((tm, HEAD_DIM), lambda i: (i % tab_tiles, 0)),
        pl.BlockSpec((tm, HEAD_DIM), lambda i: (i % tab_tiles, 0)),
        _full((N_GROUPS, HEAD_DIM)),
        _full((N_GROUPS, HEAD_DIM)),
    ]
    args = [x2d, g_mix.reshape(1, d), w_bf, *tables, g_q, g_k]
    if fuse_cross:
        in_specs += [_full((1, HEAD_DIM)),
                     pl.BlockSpec((1,) + mem_kv.shape[1:], lambda i: (i // tiles_per_seq, 0, 0))]
        args += [g_qc.reshape(1, HEAD_DIM), mem_kv]
    if dilated:
        qkv_specs = [pl.BlockSpec((3 * N_HEADS, 1, tm // r, r * HEAD_DIM),
                                  lambda i: (0, i // tiles_per_seq, i % tiles_per_seq, 0))
                     for _, r in DIL_GROUPS]
        qkv_shapes = [jax.ShapeDtypeStruct((3 * N_HEADS, n_seq, seq // r, r * HEAD_DIM), BF16)
                      for _, r in DIL_GROUPS]
        scratch = [pltpu.VMEM((N_HEADS, tm, HEAD_DIM), F32)]
    else:
        qkv_specs = [pl.BlockSpec((3 * N_GROUPS * N_HEADS, tm, HEAD_DIM), lambda i: (0, i, 0))]
        qkv_shapes = [jax.ShapeDtypeStruct((3 * N_GROUPS * N_HEADS, rows, HEAD_DIM), BF16)]
        scratch = []
    scratch = [pltpu.VMEM((2 * N_GROUPS * N_HEADS, tm, HEAD_DIM), F32)] + scratch
    out_specs = [*qkv_specs, *[win_spec(k) for k in win_keep],
                 pl.BlockSpec((tm, ATT_WIDTH), lambda i: (i, 0))]
    out_shape = [*qkv_shapes,
                 *[jax.ShapeDtypeStruct((n_seq * k * KV_ROWS, HEAD_DIM), F32) for k in win_keep],
                 jax.ShapeDtypeStruct((rows, ATT_WIDTH), BF16 if fuse_cross else F32)]
    return pl.pallas_call(
        functools.partial(_in_proj_a_kernel, fuse_cross=fuse_cross, dilated=dilated, win_rows=win_rows,
                          win_first=win_first, tiles_per_seq=tiles_per_seq),
        grid=(rows // tm,),
        in_specs=in_specs, out_specs=out_specs, out_shape=out_shape, scratch_shapes=scratch,
        compiler_params=_params("arbitrary"),
        name="in_proj_a",
    )(*args)


def _in_proj_b(x2d, seq, tm, n_valid, g_mix, w_bf, g_qc, mem_kv):
    rows, d = x2d.shape
    width = (ATT_WIDTH // S5_GROUP) * S5_BLOCK
    n_in = w_bf.shape[1]
    tiles_per_seq = seq // tm
    fuse_cross = mem_kv is not None
    in_specs = [pl.BlockSpec((tm, d), lambda i: (i, 0)), _full((1, d)), _full((d, n_in))]
    args = [x2d, g_mix.reshape(1, d), w_bf]
    if fuse_cross:
        in_specs += [_full((1, HEAD_DIM)),
                     pl.BlockSpec((1,) + mem_kv.shape[1:], lambda i: (i // tiles_per_seq, 0, 0))]
        args += [g_qc.reshape(1, HEAD_DIM), mem_kv]
    return pl.pallas_call(
        functools.partial(_in_proj_b_kernel, fuse_cross=fuse_cross, n_valid=n_valid),
        grid=(rows // tm,),
        in_specs=in_specs,
        out_specs=[pl.BlockSpec((tm // n_valid, width), lambda i: (i, 0)),
                   pl.BlockSpec((tm, ATT_WIDTH), lambda i: (i, 0))],
        out_shape=[jax.ShapeDtypeStruct((rows // n_valid, width), BF16),
                   jax.ShapeDtypeStruct((rows, ATT_WIDTH), BF16 if fuse_cross else F32)],
        scratch_shapes=[pltpu.VMEM((ATT_WIDTH // HEAD_DIM, tm, HEAD_DIM), F32)],
        compiler_params=_params("arbitrary"),
        name="in_proj_b",
    )(*args)


CROSS_BATCH = 4


def _cross_sample_kernel(qc_ref, gq_ref, mem_ref, out_ref):
    for b in range(qc_ref.shape[0]):
        outs = _cross_heads(qc_ref[b], mem_ref.at[pl.ds(b, 1)], gq_ref[...])
        for hd in range(N_HEADS):
            out_ref[b, :, hd * HEAD_DIM:(hd + 1) * HEAD_DIM] = outs[hd].astype(out_ref.dtype)


def _cross_sample(qc, g_qc, mem_kv, first):
    nb, tq, _ = qc.shape
    cb = math.gcd(nb, CROSS_BATCH)
    assert first % cb == 0
    return pl.pallas_call(
        _cross_sample_kernel,
        grid=(nb // cb,),
        in_specs=[pl.BlockSpec((cb, tq, ATT_WIDTH), lambda b: (b, 0, 0)),
                  _full((1, HEAD_DIM)),
                  pl.BlockSpec((cb,) + mem_kv.shape[1:], lambda b: (first // cb + b, 0, 0))],
        out_specs=pl.BlockSpec((cb, tq, ATT_WIDTH), lambda b: (b, 0, 0)),
        out_shape=jax.ShapeDtypeStruct((nb, tq, ATT_WIDTH), BF16),
        compiler_params=_params("arbitrary"),
        name="cross_sample",
    )(qc, g_qc.reshape(1, HEAD_DIM), mem_kv)


INFLIGHT = 4


def _band_block(q, k, v):
    n = k.shape[0]
    dist = (n - BLOCK + lax.broadcasted_iota(jnp.int32, (BLOCK, n), 0)
            - lax.broadcasted_iota(jnp.int32, (BLOCK, n), 1))
    s = jnp.where((dist >= 0) & (dist <= DIL_SPAN), _dot_nt(q, k) * SCALE, NEG)
    m = jnp.max(s, axis=-1, keepdims=True)
    p = jnp.exp(s - m)
    l = jnp.sum(p, axis=-1, keepdims=True)
    return _dot(p.astype(BF16), v) / l, m + jnp.log(l)


def _attn_prompt_kernel(q0, q1, q2, k0, k1, k2, v0, v1, v2, out_ref, o_scr, l_scr, *, seq):
    qkv_refs = ((q0, k0, v0), (q1, k1, v1), (q2, k2, v2))
    for g, (_, r) in enumerate(DIL_GROUPS):
        q_ref, k_ref, v_ref = qkv_refs[g]
        nblk = seq // r // BLOCK

        def store(rho, blk, o, lse, g=g, r=r):
            start = blk * (BLOCK * r) + rho
            if r == 1:
                idx = pl.ds(pl.multiple_of(start, BLOCK), BLOCK)
            else:
                idx = pl.ds(start, BLOCK, stride=r)
            o_scr[g, idx, :] = o
            l_scr[g, idx, :] = jnp.broadcast_to(lse, (BLOCK, HEAD_DIM))

        def first(rho, q_ref=q_ref, k_ref=k_ref, v_ref=v_ref, store=store):
            lanes = slice(rho * HEAD_DIM, (rho + 1) * HEAD_DIM)
            o, lse = _band_block(q_ref[0, 0, :BLOCK, lanes], k_ref[0, 0, :BLOCK, lanes],
                                 v_ref[0, 0, :BLOCK, lanes])
            store(rho, 0, o, lse)

        def later(rho, blk, q_ref=q_ref, k_ref=k_ref, v_ref=v_ref, store=store):
            lanes = slice(rho * HEAD_DIM, (rho + 1) * HEAD_DIM)
            cur = pl.ds(pl.multiple_of(blk * BLOCK, BLOCK), BLOCK)
            both = pl.ds(pl.multiple_of((blk - 1) * BLOCK, BLOCK), 2 * BLOCK)
            o, lse = _band_block(q_ref[0, 0, cur, lanes], k_ref[0, 0, both, lanes], v_ref[0, 0, both, lanes])
            store(rho, blk, o, lse)

        for rho0 in range(0, r, INFLIGHT):
            rhos = range(rho0, min(r, rho0 + INFLIGHT))
            for rho in rhos:
                first(rho)

            def body(blk, carry, rhos=rhos, later=later):
                for rho in rhos:
                    later(rho, blk)
                return carry
            if nblk > 1:
                lax.fori_loop(1, nblk, body, 0, unroll=max(1, INFLIGHT // len(rhos)))

    def combine(c, carry):
        rows = pl.ds(pl.multiple_of(c * BLOCK, BLOCK), BLOCK)
        ls = [l_scr[g, rows, :] for g in range(N_GROUPS)]
        m = jnp.maximum(jnp.maximum(ls[0], ls[1]), ls[2])
        es = [jnp.exp(l - m) for l in ls]
        num = es[0] * o_scr[0, rows, :] + es[1] * o_scr[1, rows, :] + es[2] * o_scr[2, rows, :]
        out_ref[0, rows, :] = (num / (es[0] + es[1] + es[2])).astype(out_ref.dtype)
        return carry
    lax.fori_loop(0, seq // BLOCK, combine, 0)


def _attn_prompt(qkv_groups, n_seq, seq):
    in_specs, args = [], []
    for role in range(3):
        for g, (_, r) in enumerate(DIL_GROUPS):
            in_specs.append(pl.BlockSpec((1, 1, seq // r, r * HEAD_DIM),
                                         lambda b, h, role=role: (role * N_HEADS + h, b, 0, 0)))
            args.append(qkv_groups[g])
    return pl.pallas_call(
        functools.partial(_attn_prompt_kernel, seq=seq),
        grid=(n_seq, N_HEADS),
        in_specs=in_specs,
        out_specs=pl.BlockSpec((1, seq, HEAD_DIM), lambda b, h: (b, 0, h)),
        out_shape=jax.ShapeDtypeStruct((n_seq, seq, ATT_WIDTH), BF16),
        scratch_shapes=[pltpu.VMEM((N_GROUPS, seq, HEAD_DIM), F32),
                        pltpu.VMEM((N_GROUPS, seq, HEAD_DIM), F32)],
        compiler_params=_params("arbitrary", "arbitrary"),
        name="attn_prompt",
    )(*args)


COPY_ROWS = 1024


def _attn_sample_kernel(q_ref, kvn_ref, win_ref, o_ref, lse_ref, new_ref, *, r, t_new):
    lb = win_ref.shape[0] // KV_ROWS
    tq = q_ref.shape[1]
    shift = t_new * KV_ROWS
    n_keep = (lb - t_new) * KV_ROWS
    n_full = n_keep // COPY_ROWS

    def copy(c, carry):
        dst = pl.multiple_of(c * COPY_ROWS, COPY_ROWS)
        src = pl.multiple_of(c * COPY_ROWS + shift, KV_ROWS)
        new_ref[pl.ds(dst, COPY_ROWS), :] = win_ref[pl.ds(src, COPY_ROWS), :]
        return carry
    lax.fori_loop(0, n_full, copy, 0)
    rest = n_keep - n_full * COPY_ROWS
    if rest:
        new_ref[n_full * COPY_ROWS:n_keep, :] = win_ref[n_full * COPY_ROWS + shift:, :]
    new_ref[n_keep:, :] = kvn_ref[...]
    t_i = lax.broadcasted_iota(jnp.int32, (tq, lb), 0)
    k_i = lax.broadcasted_iota(jnp.int32, (tq, lb), 1)
    dist = lb + t_i - k_i
    ok_buf = (dist % r == 0) & (dist <= r * DIL_SPAN)
    t_n = lax.broadcasted_iota(jnp.int32, (tq, BLOCK), 0)
    k_n = lax.broadcasted_iota(jnp.int32, (tq, BLOCK), 1) - (BLOCK - t_new)
    dn = t_n - k_n
    ok_new = (k_n >= 0) & (dn >= 0) & (dn % r == 0) & (dn <= r * DIL_SPAN)
    tail0 = (lb - BLOCK) * KV_ROWS
    for hd in range(N_HEADS):
        sl = slice(hd * HEAD_DIM, (hd + 1) * HEAD_DIM)
        q = q_ref[0, :, sl]
        kb = win_ref[_kv_rows(0, hd, lb), :].astype(BF16)
        vb = win_ref[_kv_rows(1, hd, lb), :].astype(BF16)
        kn = new_ref[pl.ds(tail0 + hd, BLOCK, stride=KV_ROWS), :].astype(BF16)
        vn = new_ref[pl.ds(tail0 + N_HEADS + hd, BLOCK, stride=KV_ROWS), :].astype(BF16)
        sb = jnp.where(ok_buf, _dot_nt(q, kb) * SCALE, NEG)
        sn = jnp.where(ok_new, _dot_nt(q, kn) * SCALE, NEG)
        m = jnp.maximum(jnp.max(sb, axis=-1, keepdims=True), jnp.max(sn, axis=-1, keepdims=True))
        pb = jnp.exp(sb - m)
        pn = jnp.exp(sn - m)
        l = jnp.sum(pb, axis=-1, keepdims=True) + jnp.sum(pn, axis=-1, keepdims=True)
        o = _dot(pb.astype(BF16), vb) + _dot(pn.astype(BF16), vn)
        o_ref[0, :, sl] = o / l
        lse_ref[0, :, sl] = jnp.broadcast_to(m + jnp.log(l), (tq, HEAD_DIM))


def _attn_sample(q, kv_new, win, nb, r):
    tq = q.shape[1]
    rows_new = kv_new.shape[0] // nb
    rows_win = win.shape[0] // nb
    return pl.pallas_call(
        functools.partial(_attn_sample_kernel, r=r, t_new=rows_new // KV_ROWS),
        grid=(nb,),
        in_specs=[pl.BlockSpec((1, tq, ATT_WIDTH), lambda b: (b, 0, 0)),
                  pl.BlockSpec((rows_new, HEAD_DIM), lambda b: (b, 0)),
                  pl.BlockSpec((rows_win, HEAD_DIM), lambda b: (b, 0))],
        out_specs=[pl.BlockSpec((1, tq, ATT_WIDTH), lambda b: (b, 0, 0)),
                   pl.BlockSpec((1, tq, ATT_WIDTH), lambda b: (b, 0, 0)),
                   pl.BlockSpec((rows_win, HEAD_DIM), lambda b: (b, 0))],
        out_shape=[jax.ShapeDtypeStruct((nb, tq, ATT_WIDTH), F32),
                   jax.ShapeDtypeStruct((nb, tq, ATT_WIDTH), F32),
                   jax.ShapeDtypeStruct(win.shape, F32)],
        compiler_params=_params("arbitrary"),
        name=f"attn_sample_r{r}",
    )(q, kv_new, win)


def _combine_kernel(o0, o1, o2, l0, l1, l2, out_ref):
    ls = [l0[...], l1[...], l2[...]]
    m = jnp.maximum(jnp.maximum(ls[0], ls[1]), ls[2])
    es = [jnp.exp(l - m) for l in ls]
    num = es[0] * o0[...] + es[1] * o1[...] + es[2] * o2[...]
    out_ref[...] = (num / (es[0] + es[1] + es[2])).astype(out_ref.dtype)


def _combine_groups(outs, lses):
    shape = outs[0].shape
    return pl.pallas_call(
        _combine_kernel,
        in_specs=[_full(shape)] * 6,
        out_specs=_full(shape),
        out_shape=jax.ShapeDtypeStruct(shape, BF16),
        grid=(1,),
        compiler_params=_params("arbitrary"),
        name="combine_groups",
    )(*outs, *lses)


FF_CHUNK = 256
CARRY_ROWS = 8
SHIFT_SLOTS = 4


def _out_ffn_kernel(*refs, seq_len, tiles_per_seq, tail_rows):
    if seq_len is None:
        (x_ref, mix_ref, cr_ref, wo_ref, g_ref, wup_ref, cw_ref, cb_ref, wdn_ref,
         out_ref, tail_ref, shift, act_scr, carry) = refs
    else:
        (x_ref, mix_ref, cr_ref, wo_ref, g_ref, wup_ref, cw_ref, cb_ref, wdn_ref, e1_ref, e2_ref,
         out_ref, tail_ref, shift, act_scr) = refs
    tm = x_ref.shape[0]
    d_ff = wdn_ref.shape[0]
    x1 = x_ref[...] + _dot(jnp.concatenate([mix_ref[...], cr_ref[...]], axis=-1), wo_ref[...])
    h = _rms(x1, g_ref[...]).astype(BF16)
    if seq_len is None:
        @pl.when(pl.program_id(0) % tiles_per_seq == 0)
        def _():
            carry[...] = jnp.zeros_like(carry)
    else:
        t = lax.broadcasted_iota(jnp.int32, (tm, 1), 0) % seq_len
        has1 = t >= 1
        has2 = t >= 2

    def chunk_cols(j):
        return (slice(j * FF_CHUNK, (j + 1) * FF_CHUNK),
                slice(d_ff + j * FF_CHUNK, d_ff + (j + 1) * FF_CHUNK))

    def up_proj(j):
        return tuple(_dot(h, wup_ref[:, cols]) for cols in chunk_cols(j))

    def conv(up, cols, slot):
        buf = shift.at[slot]
        if seq_len is None:
            buf[:CARRY_ROWS, :] = carry[:, cols]
            carry[:, cols] = up[tm - CARRY_ROWS:, :]
        else:
            buf[:CARRY_ROWS, :] = jnp.zeros((CARRY_ROWS, FF_CHUNK), F32)
        buf[CARRY_ROWS:, :] = up
        tail_ref[0, :, cols] = up[tm - tail_rows:, :]
        prev1 = buf[CARRY_ROWS - 1:CARRY_ROWS - 1 + tm, :]
        prev2 = buf[CARRY_ROWS - 2:CARRY_ROWS - 2 + tm, :]
        if seq_len is not None:
            prev1 = jnp.where(has1, prev1, e1_ref[:, cols])
            prev2 = jnp.where(has2, prev2, e2_ref[:, cols])
        return (cb_ref[:, cols] + cw_ref[0:1, cols] * prev2 + cw_ref[1:2, cols] * prev1
                + cw_ref[2:3, cols] * up)

    n_chunks = d_ff // FF_CHUNK
    ups = up_proj(0)
    for j in range(n_chunks):
        nxt = up_proj(j + 1) if j + 1 < n_chunks else None
        a, b = (conv(up, cols, 2 * (j % 2) + s) for s, (up, cols) in enumerate(zip(ups, chunk_cols(j))))
        act_scr[:, j * FF_CHUNK:(j + 1) * FF_CHUNK] = (a * jax.nn.sigmoid(a) * b).astype(BF16)
        ups = nxt
    out_ref[...] = x1 + _dot(act_scr[...], wdn_ref[...])


def _out_ffn(x2d, mix, cross, w_out_bf, g_ffn, w_up_bf, conv_w, conv_b, w_down_bf, tm,
             tiles_per_seq=None, seq_len=None, e1=None, e2=None, tail_rows=CARRY_ROWS):
    rows, d = x2d.shape
    d_ff = w_down_bf.shape[0]
    n_tiles = rows // tm
    in_specs = [
        pl.BlockSpec((tm, d), lambda i: (i, 0)),
        pl.BlockSpec((tm, ATT_WIDTH), lambda i: (i, 0)),
        pl.BlockSpec((tm, ATT_WIDTH), lambda i: (i, 0)),
        _full(w_out_bf.shape), _full((1, d)), _full(w_up_bf.shape),
        _full((CONV_W, 2 * d_ff)), _full((1, 2 * d_ff)), _full(w_down_bf.shape),
    ]
    args = [x2d, mix, cross, w_out_bf, g_ffn.reshape(1, d), w_up_bf, conv_w, conv_b.reshape(1, 2 * d_ff),
            w_down_bf]
    scratch = [pltpu.VMEM((SHIFT_SLOTS, CARRY_ROWS + tm, FF_CHUNK), F32), pltpu.VMEM((tm, d_ff), BF16)]
    if seq_len is None:
        scratch += [pltpu.VMEM((CARRY_ROWS, 2 * d_ff), F32)]
    else:
        in_specs += [pl.BlockSpec((tm, 2 * d_ff), lambda i: (i, 0))] * 2
        args += [e1, e2]
    return pl.pallas_call(
        functools.partial(_out_ffn_kernel, seq_len=seq_len, tiles_per_seq=tiles_per_seq,
                          tail_rows=tail_rows),
        grid=(n_tiles,),
        in_specs=in_specs,
        out_specs=[pl.BlockSpec((tm, d), lambda i: (i, 0)),
                   pl.BlockSpec((1, tail_rows, 2 * d_ff), lambda i: (i, 0, 0))],
        out_shape=[jax.ShapeDtypeStruct((rows, d), F32),
                   jax.ShapeDtypeStruct((n_tiles, tail_rows, 2 * d_ff), F32)],
        scratch_shapes=scratch,
        compiler_params=_params("arbitrary"),
        name="out_ffn",
    )(*args)


def _swap_halves(x):
    return pltpu.roll(x, S5_STATE, 1)


def _s5_prep_kernel(lam_ref, logdt_ref, bt_ref, c_ref, d_ref, t_ref, cpt_ref, *var_refs, n_valids):
    gc = lam_ref.shape[0]
    n_groups = gc // S5_GROUP
    lane = lax.broadcasted_iota(jnp.int32, lam_ref.shape, 1)
    first = lane < S5_STATE
    sign = jnp.where(first, -1.0, 1.0)
    a = lam_ref[...]
    a_sw = _swap_halves(a)
    are = jnp.where(first, a, a_sw)
    aim = jnp.where(first, a_sw, a)
    dt = jnp.exp(logdt_ref[...])
    mag = jnp.exp(are * dt)
    lr = mag * jnp.cos(aim * dt)
    li = mag * jnp.sin(aim * dt)
    den = are * are + aim * aim
    xr = lr - 1.0
    f_re = (xr * are + li * aim) / den
    f_im = (li * are - xr * aim) / den
    lb = sign * li

    def cmul(x, m_re, m_sw):
        return x * m_re + _swap_halves(x) * m_sw

    c = c_ref[...]
    c_neg = c * -sign
    ri = lax.broadcasted_iota(jnp.int32, (gc, gc), 0)
    ci = lax.broadcasted_iota(jnp.int32, (gc, gc), 1)
    same_group = (ri // S5_GROUP) == (ci // S5_GROUP)
    e = cmul(bt_ref[...], f_re, sign * f_im)
    cl = c
    pw = jnp.where(first, 1.0, 0.0)
    zr = lax.broadcasted_iota(jnp.int32, (gc, S5_BLOCK), 0)
    zc = lax.broadcasted_iota(jnp.int32, (gc, S5_BLOCK), 1)
    same_out = (zr % S5_GROUP) == (zc % S5_GROUP)
    lags = jnp.zeros((gc, S5_BLOCK), F32)
    for k in range(S5_SUB):
        kmat = jnp.where(same_group, _dot_nt(e, c_neg, precision=HIGHEST), 0.0)
        if k == 0:
            kmat = kmat + jnp.where(ri == ci, d_ref[...], 0.0)
        place = jnp.where(same_out & (zc // S5_GROUP == k), 1.0, 0.0)
        lags = lags + jnp.dot(kmat, place, precision=HIGHEST, preferred_element_type=F32)
        both = jnp.concatenate([e, _swap_halves(e)], axis=-1).reshape(n_groups, S5_GROUP, 4 * S5_STATE)
        for n_valid, bp_ref in zip(n_valids, var_refs[0::2]):
            if n_valid - 1 - k >= 0:
                bp_ref[:, n_valid - 1 - k] = both.astype(bp_ref.dtype)
        cl = cmul(cl, lr, lb)
        cpt_ref[:, k] = (cl * -sign).reshape(n_groups, S5_GROUP, 2 * S5_STATE).astype(cpt_ref.dtype)
        e = cmul(e, lr, lb)
        pw = cmul(pw, lr, lb)
        for n_valid, lam_out_ref in zip(n_valids, var_refs[1::2]):
            if k + 1 == n_valid:
                lam_out_ref[...] = pw
    for n_valid, bp_ref in zip(n_valids, var_refs[0::2]):
        for tau in range(n_valid, S5_SUB):
            bp_ref[:, tau] = jnp.zeros((n_groups, S5_GROUP, 4 * S5_STATE), bp_ref.dtype)
    for taup in range(S5_SUB):
        moved = lags if taup == 0 else jnp.where(zc >= taup * S5_GROUP, pltpu.roll(lags, taup * S5_GROUP, 1), 0.0)
        t_ref[:, taup] = moved.reshape(n_groups, S5_GROUP, S5_BLOCK).astype(t_ref.dtype)


def _s5_prep(lam_re, lam_im, log_dt, b_re, b_im, c_re, c_im, d_skip, n_valids):
    n_groups = lam_re.shape[0]
    gc = n_groups * S5_GROUP
    rep = lambda t: jnp.repeat(t, S5_GROUP, axis=0)
    lam_p = rep(jnp.concatenate([lam_re, lam_im], axis=-1))
    logdt = rep(log_dt.reshape(n_groups, 1))
    bt_p = jnp.concatenate([b_re.transpose(0, 2, 1), b_im.transpose(0, 2, 1)], axis=-1).reshape(gc, 2 * S5_STATE)
    c_p = jnp.concatenate([c_re, c_im], axis=-1).reshape(gc, 2 * S5_STATE)
    t_shape = (n_groups, S5_SUB, S5_GROUP, S5_BLOCK)
    cpt_shape = (n_groups, S5_SUB, S5_GROUP, 2 * S5_STATE)
    bp_shape = (n_groups, S5_SUB, S5_GROUP, 4 * S5_STATE)
    lam_shape = (gc, 2 * S5_STATE)
    var = [(bp_shape, BF16), (lam_shape, F32)] * len(n_valids)
    t, cpt, *rest = pl.pallas_call(
        functools.partial(_s5_prep_kernel, n_valids=tuple(n_valids)),
        grid=(1,),
        in_specs=[_full(lam_p.shape), _full(logdt.shape), _full(bt_p.shape), _full(c_p.shape), _full((gc, 1))],
        out_specs=[_full(t_shape), _full(cpt_shape), *[_full(s) for s, _ in var]],
        out_shape=[jax.ShapeDtypeStruct(t_shape, BF16), jax.ShapeDtypeStruct(cpt_shape, BF16),
                   *[jax.ShapeDtypeStruct(s, dt) for s, dt in var]],
        compiler_params=_params("arbitrary"),
        name="s5_prep",
    )(lam_p, logdt, bt_p, c_p, d_skip.reshape(gc, 1))
    merge = lambda a: a.reshape(n_groups, S5_BLOCK, a.shape[-1])
    return [(merge(t), merge(bp), merge(cpt), lam_n[::S5_GROUP]) for bp, lam_n in zip(rest[0::2], rest[1::2])]


SCAN_GROUPS = 8
SUBLANES = 8
GLU_ROWS = 512


def _s5_core_kernel(v_ref, x0_ref, x0s_ref, t_ref, bp_ref, cpt_ref, lam_ref, wglu_ref, bglu_ref,
                    mix_ref, xfin_ref, s_scr, xprev_scr, m_scr, stage, *, n_valid, n_sub):
    rows = v_ref.shape[0]
    n_groups = t_ref.shape[0]
    lane = lax.broadcasted_iota(jnp.int32, (1, 2 * S5_STATE), 1)
    first = lane < S5_STATE
    blk = lambda g: slice(g * S5_BLOCK, (g + 1) * S5_BLOCK)
    tile = lambda i: slice(i * 2 * S5_STATE, (i + 1) * 2 * S5_STATE)
    for g in range(n_groups):
        s_scr[:, blk(g)] = _dot(v_ref[:, blk(g)], bp_ref[g])
        lam = lam_ref[g:g + 1, :]
        lam_sw = _swap_halves(lam)
        m_scr[0:1, tile(g)] = jnp.where(first, lam, lam_sw)
        m_scr[1:2, tile(g)] = jnp.where(first, -lam_sw, lam)
    if n_sub == 1:
        for g in range(n_groups):
            x, xs = x0_ref[:, tile(g)], x0s_ref[:, tile(g)]
            xprev_scr[:, tile(g)] = x
            xfin_ref[:, tile(g)] = (x * m_scr[0:1, tile(g)] + xs * m_scr[1:2, tile(g)]
                                    + s_scr[:, tile(2 * g)])
    else:
        assert rows == n_sub
        for g0 in range(0, n_groups, SCAN_GROUPS):
            gs = range(g0, min(n_groups, g0 + SCAN_GROUPS))

            def body(i, carry, gs=gs):
                rows8 = pl.ds(pl.multiple_of(i * SUBLANES, SUBLANES), SUBLANES)
                sub_i = lax.broadcasted_iota(jnp.int32, (SUBLANES, 2 * S5_STATE), 0)
                out = []
                for g, (x, xs) in zip(gs, carry):
                    s8, ssw8 = s_scr[rows8, tile(2 * g)], s_scr[rows8, tile(2 * g + 1)]
                    prev8 = jnp.zeros((SUBLANES, 2 * S5_STATE), F32)
                    for r in range(SUBLANES):
                        prev8 = jnp.where(sub_i == r, x, prev8)
                        m_re, m_sw = m_scr[0:1, tile(g)], m_scr[1:2, tile(g)]
                        x, xs = (x * m_re + xs * m_sw + s8[r:r + 1, :],
                                 xs * m_re - x * m_sw + ssw8[r:r + 1, :])
                    xprev_scr[rows8, tile(g)] = prev8
                    out.append((x, xs))
                return tuple(out)
            init = tuple((x0_ref[0:1, tile(g)], x0s_ref[0:1, tile(g)]) for g in gs)
            fin = lax.fori_loop(0, n_sub // SUBLANES, body, init)
            for g, (x, _) in zip(gs, fin):
                xfin_ref[:, tile(g)] = jnp.broadcast_to(x, (xfin_ref.shape[0], 2 * S5_STATE))
    for g in range(n_groups):
        s_scr[:, blk(g)] = (_dot(v_ref[:, blk(g)], t_ref[g])
                            + _dot_nt(xprev_scr[:, tile(g)].astype(BF16), cpt_ref[g]))
    for cb in range(ATT_WIDTH // HEAD_DIM):
        for half in range(-(-n_valid // SLOTS)):
            toks = _slot_transpose([s_scr[:, tile(2 * (cb * SLOTS + p) + half)] for p in range(SLOTS)])
            for s in range(min(SLOTS, n_valid - half * SLOTS)):
                stage[cb, pl.ds(half * SLOTS + s, rows, stride=n_valid), :] = toks[s]
    n_tok = rows * n_valid
    chunk = min(n_tok, GLU_ROWS)

    def glu(i, carry):
        r = pl.ds(pl.multiple_of(i * chunk, chunk), chunk)
        y = jax.nn.gelu(jnp.concatenate([stage[cb, r, :] for cb in range(ATT_WIDTH // HEAD_DIM)], axis=-1))
        z = _dot(y.astype(BF16), wglu_ref[...]) + bglu_ref[...]
        mix_ref[r, :] = (y * jax.nn.sigmoid(z)).astype(mix_ref.dtype)
        return carry
    lax.fori_loop(0, n_tok // chunk, glu, 0)


def _s5_mixer(v, rows_per_tile, n_valid, n_sub, x0, x0s, t, bp, cpt, lam_n, w_glu_bf, b_glu):
    m, width = v.shape
    n_tiles = m // rows_per_tile
    r0 = x0.shape[0] // n_tiles
    n_state = x0.shape[1]
    n_tok = rows_per_tile * n_valid
    return pl.pallas_call(
        functools.partial(_s5_core_kernel, n_valid=n_valid, n_sub=n_sub),
        grid=(n_tiles,),
        in_specs=[pl.BlockSpec((rows_per_tile, width), lambda i: (i, 0)),
                  pl.BlockSpec((r0, n_state), lambda i: (i, 0)),
                  pl.BlockSpec((r0, n_state), lambda i: (i, 0)),
                  _full(t.shape), _full(bp.shape), _full(cpt.shape), _full(lam_n.shape),
                  _full(w_glu_bf.shape), _full((1, ATT_WIDTH))],
        out_specs=[pl.BlockSpec((n_tok, ATT_WIDTH), lambda i: (i, 0)),
                   pl.BlockSpec((r0, n_state), lambda i: (i, 0))],
        out_shape=[jax.ShapeDtypeStruct((m * n_valid, ATT_WIDTH), BF16),
                   jax.ShapeDtypeStruct(x0.shape, F32)],
        scratch_shapes=[pltpu.VMEM((rows_per_tile, width), F32),
                        pltpu.VMEM((rows_per_tile, n_state), F32),
                        pltpu.VMEM((8, n_state), F32),
                        pltpu.VMEM((ATT_WIDTH // HEAD_DIM, n_tok, HEAD_DIM), F32)],
        compiler_params=_params("arbitrary"),
        name="s5_core",
    )(v, x0, x0s, t, bp, cpt, lam_n, w_glu_bf, b_glu.reshape(1, ATT_WIDTH))


def _pack_state(s):
    n = s.shape[0]
    packed = s.transpose(0, 2, 1, 3).reshape(n, -1)
    swapped = jnp.stack([s[:, 1], s[:, 0]], axis=1).transpose(0, 2, 1, 3).reshape(n, -1)
    return packed, swapped


def _unpack_state(x):
    n = x.shape[0]
    return x.reshape(n, -1, 2, S5_STATE).transpose(0, 2, 1, 3)


PROMPT_TILE = 512
Q_PAD = 16


def _pad_rows(t, n):
    return jnp.pad(t, ((0, 0), (0, n - t.shape[1]), (0, 0)))


def kernel(x_prompt, x_sample, cache_win0_kv, cache_win1_kv, cache_win2_kv, cache_mem_kv, state_s5,
           state_ffn_conv, mem_prompt, g_mix, g_ffn, w_in_a, g_q_dil, g_k_dil, w_in_b, s5_lam_re,
           s5_lam_im, s5_log_dt, s5_b_re, s5_b_im, s5_c_re, s5_c_im, s5_d, w_glu, b_glu, g_mem,
           w_mem_kv, g_q_cross, g_k_cross, w_out, w_up, conv_w, conv_b, w_down):
    nb, seq, d = x_prompt.shape
    db, ts, _ = x_sample.shape
    depth = g_mix.shape[0]
    n_mem = mem_prompt.shape[1]
    d_ff2 = w_up.shape[2]
    assert ts >= CONV_W - 1 and seq % PROMPT_TILE == 0 and ts <= Q_PAD
    caches = (cache_win0_kv, cache_win1_kv, cache_win2_kv)
    w_mem_bf = w_mem_kv.astype(BF16)
    mem_cache = cache_mem_kv.reshape(depth * db, n_mem * KV_ROWS, HEAD_DIM)

    tab_p = _rope_tables(jnp.arange(seq, dtype=jnp.int32))
    tab_s = tuple(jnp.tile(t, (db, 1)) for t in _rope_tables(PAST_LEN + jnp.arange(ts, dtype=jnp.int32)))
    win_keep = tuple(min(w, seq) for w, _ in DIL_GROUPS)
    rows_s = db * ts
    kv_tail = (2, N_HEADS, HEAD_DIM)

    mem_p = _mem_kv(mem_prompt.reshape(nb * n_mem, d), g_mem, w_mem_bf, g_k_cross)

    xp = x_prompt.reshape(nb * seq, d)
    xs = x_sample.reshape(rows_s, d)
    p_win, s_win = [[] for _ in DIL_GROUPS], [[] for _ in DIL_GROUPS]
    p_s5, s_s5, p_conv, s_conv = [], [], [], []
    tiles_per_seq = seq // PROMPT_TILE
    for i in range(depth):
        mem_i = mem_p[i].reshape(nb, n_mem * KV_ROWS, HEAD_DIM)
        if i % 2 == 0:
            ia = i // 2
            w_in_a_bf = w_in_a[ia].astype(BF16)
            *qkv_groups, w0, w1, w2, cross_p = _in_proj_a(
                xp, seq, PROMPT_TILE, g_mix[i], w_in_a_bf, tab_p, g_q_dil[ia], g_k_dil[ia],
                g_q_cross[i], mem_i, win_keep, dilated=True)
            for g, w in enumerate((w0, w1, w2)):
                p_win[g].append(w.reshape((nb, win_keep[g]) + kv_tail))
            mix_p = _attn_prompt(qkv_groups, nb, seq).reshape(nb * seq, ATT_WIDTH)
            qkv_s, k0, k1, k2, qc_s = _in_proj_a(
                xs, rows_s, rows_s, g_mix[i], w_in_a_bf, tab_s, g_q_dil[ia], g_k_dil[ia],
                None, None, (rows_s,) * N_GROUPS, dilated=False)
            outs, lses = [], []
            for g, ((_, r), kv_new) in enumerate(zip(DIL_GROUPS, (k0, k1, k2))):
                q_g = qkv_s[g * N_HEADS:(g + 1) * N_HEADS]
                q_g = _pad_rows(q_g.transpose(1, 0, 2).reshape(db, ts, ATT_WIDTH), Q_PAD)
                win = caches[g][ia]
                o_g, lse_g, new_g = _attn_sample(q_g, kv_new, win.reshape(-1, HEAD_DIM), db, r)
                outs.append(o_g[:, :ts].reshape(rows_s, ATT_WIDTH))
                lses.append(lse_g[:, :ts].reshape(rows_s, ATT_WIDTH))
                s_win[g].append(new_g.reshape(win.shape))
            mix_s = _combine_groups(outs, lses)
        else:
            ib = i // 2
            prm = (s5_lam_re[ib], s5_lam_im[ib], s5_log_dt[ib], s5_b_re[ib], s5_b_im[ib], s5_c_re[ib],
                   s5_c_im[ib], s5_d[ib])
            n_state = s5_lam_re.shape[1] * 2 * S5_STATE
            w_in_b_bf, w_glu_bf = w_in_b[ib].astype(BF16), w_glu[ib].astype(BF16)
            u_p, cross_p = _in_proj_b(xp, seq, PROMPT_TILE, S5_SUB, g_mix[i], w_in_b_bf, g_q_cross[i], mem_i)
            zero = jnp.zeros((nb * CARRY_ROWS, n_state), F32)
            prep_p, prep_s = _s5_prep(*prm, (S5_SUB, ts))
            mix_p, fin_p = _s5_mixer(u_p, seq // S5_SUB, S5_SUB, seq // S5_SUB, zero, zero,
                                     *prep_p, w_glu_bf, b_glu[ib])
            p_s5.append(_unpack_state(fin_p.reshape(nb, CARRY_ROWS, n_state)[:, 0]))
            u_s, qc_s = _in_proj_b(xs, rows_s, rows_s, ts, g_mix[i], w_in_b_bf, None, None)
            mix_s, fin_s = _s5_mixer(u_s, db, ts, 1, *_pack_state(state_s5[ib]), *prep_s, w_glu_bf, b_glu[ib])
            s_s5.append(_unpack_state(fin_s))
        cross_s = _cross_sample(_pad_rows(qc_s.reshape(db, ts, ATT_WIDTH), Q_PAD), g_q_cross[i], mem_cache,
                                first=i * db)
        cross_s = cross_s[:, :ts].reshape(rows_s, ATT_WIDTH)

        ffn_w = (w_out[i].astype(BF16), g_ffn[i], w_up[i].astype(BF16), conv_w[i], conv_b[i],
                 w_down[i].astype(BF16))
        xp, tails = _out_ffn(xp, mix_p, cross_p, *ffn_w, PROMPT_TILE, tiles_per_seq=tiles_per_seq)
        p_conv.append(tails.reshape(nb, tiles_per_seq, CARRY_ROWS, d_ff2)[:, -1, CARRY_ROWS - (CONV_W - 1):])
        buf = state_ffn_conv[i]
        zero = jnp.zeros((db, ts - 2, d_ff2), F32)
        e1 = jnp.concatenate([buf[:, 1:2], zero, zero[:, :1]], axis=1).reshape(rows_s, d_ff2)
        e2 = jnp.concatenate([buf, zero], axis=1).reshape(rows_s, d_ff2)
        xs, tails = _out_ffn(xs, mix_s, cross_s, *ffn_w, rows_s, seq_len=ts, e1=e1, e2=e2, tail_rows=rows_s)
        s_conv.append(tails.reshape(db, ts, d_ff2)[:, ts - (CONV_W - 1):])

    return (xp.reshape(nb, seq, d), xs.reshape(db, ts, d),
            jnp.stack(p_win[0]), jnp.stack(p_win[1]), jnp.stack(p_win[2]),
            mem_p.reshape((depth, nb, n_mem) + kv_tail),
            jnp.stack(p_s5), jnp.stack(p_conv),
            jnp.stack(s_win[0]), jnp.stack(s_win[1]), jnp.stack(s_win[2]),
            jnp.stack(s_s5), jnp.stack(s_conv))
```

```python
import functools
import math

import jax
import jax.numpy as jnp
from jax import lax
from jax.experimental import pallas as pl
from jax.experimental.pallas import tpu as pltpu

HEAD_DIM = 128
N_HEADS = 4
DIL_GROUPS = ((128, 1), (512, 4), (2048, 16))
N_GROUPS = len(DIL_GROUPS)
DIL_SPAN = 128
BLOCK = 128
ATT_WIDTH = N_HEADS * HEAD_DIM
KV_ROWS = 2 * N_HEADS
ROT_DIM = HEAD_DIM // 4
ROT_HALF = ROT_DIM // 2
ROPE_THETA = 500000.0
S5_GROUP = 16
S5_STATE = 64
S5_SUB = 16
S5_BLOCK = S5_SUB * S5_GROUP
CONV_W = 3
EPS = 1e-6
NEG = -1e30
SCALE = HEAD_DIM ** -0.5
PAST_LEN = 16384

VMEM_LIMIT_V7X = 56 * 1024 * 1024
BF16 = jnp.bfloat16
F32 = jnp.float32
HIGHEST = lax.Precision.HIGHEST


def _params(*sem):
    return pltpu.CompilerParams(dimension_semantics=sem, vmem_limit_bytes=VMEM_LIMIT_V7X)


def _rms(x, g):
    return x * lax.rsqrt(jnp.mean(x * x, axis=-1, keepdims=True) + EPS) * g


def _dot(a, b):
    return jnp.dot(a, b, preferred_element_type=F32)


def _dot_nt(a, b, precision=None):
    return lax.dot_general(a, b, (((1,), (1,)), ((), ())), precision=precision,
                           preferred_element_type=F32)


def _full(shape):
    nd = len(shape)
    return pl.BlockSpec(shape, lambda *_: (0,) * nd)


def _kv_rows(kv, head, n):
    return pl.ds(kv * N_HEADS + head, n, stride=KV_ROWS)


def _mem_kv_kernel(mem_ref, g_ref, w_ref, gk_ref, out_ref):
    tm = mem_ref.shape[0]
    h = _rms(mem_ref[...], g_ref[0]).astype(BF16)
    kv = _dot(h, w_ref[0])
    for hd in range(N_HEADS):
        sl = slice(hd * HEAD_DIM, (hd + 1) * HEAD_DIM)
        out_ref[0, _kv_rows(0, hd, tm), :] = _rms(kv[:, sl], gk_ref[0])
        out_ref[0, _kv_rows(1, hd, tm), :] = kv[:, ATT_WIDTH + hd * HEAD_DIM:ATT_WIDTH + (hd + 1) * HEAD_DIM]


def _mem_kv(mem2d, g_mem, w_kv_bf, g_k):
    depth, d, _ = w_kv_bf.shape
    rows = mem2d.shape[0]
    tm = min(rows, 512)
    return pl.pallas_call(
        _mem_kv_kernel,
        grid=(depth, rows // tm),
        in_specs=[
            pl.BlockSpec((tm, d), lambda l, i: (i, 0)),
            pl.BlockSpec((1, 1, d), lambda l, i: (l, 0, 0)),
            pl.BlockSpec((1, d, 2 * ATT_WIDTH), lambda l, i: (l, 0, 0)),
            pl.BlockSpec((1, 1, HEAD_DIM), lambda l, i: (l, 0, 0)),
        ],
        out_specs=pl.BlockSpec((1, tm * KV_ROWS, HEAD_DIM), lambda l, i: (l, i, 0)),
        out_shape=jax.ShapeDtypeStruct((depth, rows * KV_ROWS, HEAD_DIM), F32),
        compiler_params=_params("arbitrary", "arbitrary"),
        name="mem_kv",
    )(mem2d, g_mem.reshape(depth, 1, d), w_kv_bf, g_k.reshape(depth, 1, HEAD_DIM))


def _cross_heads(qc, mem_ref, gq):
    n_mem = mem_ref.shape[1] // KV_ROWS
    outs = []
    for hd in range(N_HEADS):
        q = _rms(qc[:, hd * HEAD_DIM:(hd + 1) * HEAD_DIM], gq).astype(BF16)
        k = mem_ref[0, _kv_rows(0, hd, n_mem), :].astype(BF16)
        v = mem_ref[0, _kv_rows(1, hd, n_mem), :].astype(BF16)
        s = _dot_nt(q, k) * SCALE
        m = jnp.max(s, axis=-1, keepdims=True)
        p = jnp.exp(s - m)
        l = jnp.sum(p, axis=-1, keepdims=True)
        outs.append(_dot(p.astype(BF16), v) / l)
    return outs


def _rope(x, cos_t, sin_lo, sin_hi):
    return (x * cos_t + pltpu.roll(x, HEAD_DIM - ROT_HALF, 1) * sin_lo
            + pltpu.roll(x, ROT_HALF, 1) * sin_hi)


def _in_proj_a_kernel(*refs, fuse_cross, dilated, win_rows, win_first, tiles_per_seq):
    n_in = 10 if fuse_cross else 8
    x_ref, g_ref, w_ref, cos_ref, slo_ref, shi_ref, gq_ref, gk_ref = refs[:8]
    n_qkv = N_GROUPS if dilated else 1
    qkv_refs = refs[n_in:n_in + n_qkv]
    win_refs = refs[n_in + n_qkv:n_in + n_qkv + N_GROUPS]
    cr_ref = refs[n_in + n_qkv + N_GROUPS]
    kv_stash = refs[n_in + n_qkv + N_GROUPS + 1]
    stage = refs[n_in + n_qkv + N_GROUPS + 2] if dilated else None
    tm = x_ref.shape[0]
    h = _rms(x_ref[...], g_ref[...]).astype(BF16)
    cos_t, sin_lo, sin_hi = cos_ref[...], slo_ref[...], shi_ref[...]
    def proj(c):
        return _dot(h, w_ref[:, c * ATT_WIDTH:(c + 1) * ATT_WIDTH])

    tile = pl.program_id(0) % tiles_per_seq
    in_window = [tile >= first for first in win_first]
    for g in range(N_GROUPS):
        @pl.when(jnp.logical_not(in_window[g]))
        def _(g=g):
            win_refs[g][...] = jnp.zeros_like(win_refs[g])

    y = proj(0)
    for c in range(3 * N_GROUPS):
        y_next = proj(c + 1)
        role, g = divmod(c, N_GROUPS)
        r = DIL_GROUPS[g][1]
        wr = win_rows[g]
        for hd in range(N_HEADS):
            yh = y[:, hd * HEAD_DIM:(hd + 1) * HEAD_DIM]
            if role == 0:
                yh = _rope(_rms(yh, gq_ref[g]), cos_t, sin_lo, sin_hi)
            elif role == 1:
                yh = _rope(_rms(yh, gk_ref[g]), cos_t, sin_lo, sin_hi)
            if role > 0:
                kv_stash[((role - 1) * N_GROUPS + g) * N_HEADS + hd] = yh
            if not dilated:
                qkv_refs[0][role * N_GROUPS * N_HEADS + g * N_HEADS + hd] = yh.astype(BF16)
            elif r == 1:
                qkv_refs[g][role * N_HEADS + hd, 0] = yh.astype(BF16)
            else:
                stage[hd] = yh
                for rho in range(r):
                    qkv_refs[g][role * N_HEADS + hd, 0, :, rho * HEAD_DIM:(rho + 1) * HEAD_DIM] = (
                        stage[hd, pl.ds(rho, tm // r, stride=r), :].astype(BF16))
        y = y_next
    qc = y
    if fuse_cross:
        outs = _cross_heads(qc, refs[9], refs[8][...])
        for hd in range(N_HEADS):
            cr_ref[:, hd * HEAD_DIM:(hd + 1) * HEAD_DIM] = outs[hd].astype(cr_ref.dtype)
    else:
        cr_ref[...] = qc
    for g in range(N_GROUPS):
        @pl.when(in_window[g])
        def _(g=g):
            wr = win_rows[g]
            for kv in range(2):
                for hd in range(N_HEADS):
                    win_refs[g][_kv_rows(kv, hd, wr), :] = kv_stash[(kv * N_GROUPS + g) * N_HEADS + hd,
                                                                    tm - wr:, :]


SLOTS = HEAD_DIM // S5_GROUP


def _slot_transpose(vs):
    slot = lax.broadcasted_iota(jnp.int32, vs[0].shape, 1) // S5_GROUP
    d = SLOTS // 2
    while d:
        low = (slot & d) == 0
        new = list(vs)
        for i in range(SLOTS):
            if not i & d:
                new[i] = jnp.where(low, vs[i], pltpu.roll(vs[i | d], d * S5_GROUP, 1))
                new[i | d] = jnp.where(low, pltpu.roll(vs[i], HEAD_DIM - d * S5_GROUP, 1), vs[i | d])
        vs = new
        d //= 2
    return vs


def _in_proj_b_kernel(*refs, fuse_cross, n_valid):
    if fuse_cross:
        x_ref, g_ref, w_ref, gqc_ref, mem_ref, u_ref, cr_ref, stage = refs
    else:
        x_ref, g_ref, w_ref, u_ref, cr_ref, stage = refs
    tm = x_ref.shape[0]
    n_rows = tm // n_valid
    h = _rms(x_ref[...], g_ref[...]).astype(BF16)
    u = _dot(h, w_ref[:, :ATT_WIDTH])
    zero = jnp.zeros((n_rows, HEAD_DIM), F32)
    for cb in range(ATT_WIDTH // HEAD_DIM):
        stage[cb] = u[:, cb * HEAD_DIM:(cb + 1) * HEAD_DIM]
        for half in range(S5_SUB // SLOTS):
            taus = range(half * SLOTS, (half + 1) * SLOTS)
            groups = _slot_transpose([stage[cb, pl.ds(tau, n_rows, stride=n_valid), :] if tau < n_valid else zero
                                      for tau in taus])
            for p in range(SLOTS):
                lanes = (2 * (cb * SLOTS + p) + half) * HEAD_DIM
                u_ref[:, lanes:lanes + HEAD_DIM] = groups[p].astype(BF16)
    qc = _dot(h, w_ref[:, ATT_WIDTH:])
    if fuse_cross:
        outs = _cross_heads(qc, mem_ref, gqc_ref[...])
        for hd in range(N_HEADS):
            cr_ref[:, hd * HEAD_DIM:(hd + 1) * HEAD_DIM] = outs[hd].astype(cr_ref.dtype)
    else:
        cr_ref[...] = qc


def _rope_tables(pos):
    inv = jnp.exp(-math.log(ROPE_THETA) * jnp.arange(ROT_HALF, dtype=F32) / ROT_HALF)
    ang = pos.astype(F32)[:, None] * inv[None, :]
    cos, sin = jnp.cos(ang), jnp.sin(ang)
    rows = pos.shape[0]
    ones = jnp.ones((rows, HEAD_DIM - ROT_DIM), F32)
    zeros = jnp.zeros((rows, HEAD_DIM - ROT_DIM), F32)
    z16 = jnp.zeros((rows, ROT_HALF), F32)
    cos_t = jnp.concatenate([cos, cos, ones], axis=1)
    sin_lo = jnp.concatenate([-sin, z16, zeros], axis=1)
    sin_hi = jnp.concatenate([z16, sin, zeros], axis=1)
    return cos_t, sin_lo, sin_hi


def _in_proj_a(x2d, seq, tm, g_mix, w_bf, tables, g_q, g_k, g_qc, mem_kv, win_keep, dilated):
    rows, d = x2d.shape
    n_in = w_bf.shape[1]
    tiles_per_seq = seq // tm
    n_seq = rows // seq
    fuse_cross = mem_kv is not None
    tab_tiles = tables[0].shape[0] // tm
    win_rows = tuple(min(k, tm) for k in win_keep)
    win_first = tuple(tiles_per_seq - k // wr for k, wr in zip(win_keep, win_rows))

    def win_spec(keep):
        wr = min(keep, tm)
        nblk = keep // wr
        first = tiles_per_seq - nblk

        def imap(i):
            b = i // tiles_per_seq
            t = i % tiles_per_seq
            return (b * nblk + jnp.maximum(t - first, 0), 0)
        return pl.BlockSpec((wr * KV_ROWS, HEAD_DIM), imap)

    in_specs = [
        pl.BlockSpec((tm, d), lambda i: (i, 0)),
        _full((1, d)),
        _full((d, n_in)),
        pl.BlockSpec((tm, HEAD_DIM), lambda i: (i % tab_tiles, 0)),
        pl.BlockSpec((tm, HEAD_DIM), lambda i: (i % tab_tiles, 0)),
        pl.BlockSpec((tm, HEAD_DIM), lambda i: (i % tab_tiles, 0)),
        _full((N_GROUPS, HEAD_DIM)),
        _full((N_GROUPS, HEAD_DIM)),
    ]
    args = [x2d, g_mix.reshape(1, d), w_bf, *tables, g_q, g_k]
    if fuse_cross:
        in_specs += [_full((1, HEAD_DIM)),
                     pl.BlockSpec((1,) + mem_kv.shape[1:], lambda i: (i // tiles_per_seq, 0, 0))]
        args += [g_qc.reshape(1, HEAD_DIM), mem_kv]
    if dilated:
        qkv_specs = [pl.BlockSpec((3 * N_HEADS, 1, tm // r, r * HEAD_DIM),
                                  lambda i: (0, i // tiles_per_seq, i % tiles_per_seq, 0))
                     for _, r in DIL_GROUPS]
        qkv_shapes = [jax.ShapeDtypeStruct((3 * N_HEADS, n_seq, seq // r, r * HEAD_DIM), BF16)
                      for _, r in DIL_GROUPS]
        scratch = [pltpu.VMEM((N_HEADS, tm, HEAD_DIM), F32)]
    else:
        qkv_specs = [pl.BlockSpec((3 * N_GROUPS * N_HEADS, tm, HEAD_DIM), lambda i: (0, i, 0))]
        qkv_shapes = [jax.ShapeDtypeStruct((3 * N_GROUPS * N_HEADS, rows, HEAD_DIM), BF16)]
        scratch = []
    scratch = [pltpu.VMEM((2 * N_GROUPS * N_HEADS, tm, HEAD_DIM), F32)] + scratch
    out_specs = [*qkv_specs, *[win_spec(k) for k in win_keep],
                 pl.BlockSpec((tm, ATT_WIDTH), lambda i: (i, 0))]
    out_shape = [*qkv_shapes,
                 *[jax.ShapeDtypeStruct((n_seq * k * KV_ROWS, HEAD_DIM), F32) for k in win_keep],
                 jax.ShapeDtypeStruct((rows, ATT_WIDTH), BF16 if fuse_cross else F32)]
    return pl.pallas_call(
        functools.partial(_in_proj_a_kernel, fuse_cross=fuse_cross, dilated=dilated, win_rows=win_rows,
                          win_first=win_first, tiles_per_seq=tiles_per_seq),
        grid=(rows // tm,),
        in_specs=in_specs, out_specs=out_specs, out_shape=out_shape, scratch_shapes=scratch,
        compiler_params=_params("arbitrary"),
        name="in_proj_a",
    )(*args)


def _in_proj_b(x2d, seq, tm, n_valid, g_mix, w_bf, g_qc, mem_kv):
    rows, d = x2d.shape
    width = (ATT_WIDTH // S5_GROUP) * S5_BLOCK
    n_in = w_bf.shape[1]
    tiles_per_seq = seq // tm
    fuse_cross = mem_kv is not None
    in_specs = [pl.BlockSpec((tm, d), lambda i: (i, 0)), _full((1, d)), _full((d, n_in))]
    args = [x2d, g_mix.reshape(1, d), w_bf]
    if fuse_cross:
        in_specs += [_full((1, HEAD_DIM)),
                     pl.BlockSpec((1,) + mem_kv.shape[1:], lambda i: (i // tiles_per_seq, 0, 0))]
        args += [g_qc.reshape(1, HEAD_DIM), mem_kv]
    return pl.pallas_call(
        functools.partial(_in_proj_b_kernel, fuse_cross=fuse_cross, n_valid=n_valid),
        grid=(rows // tm,),
        in_specs=in_specs,
        out_specs=[pl.BlockSpec((tm // n_valid, width), lambda i: (i, 0)),
                   pl.BlockSpec((tm, ATT_WIDTH), lambda i: (i, 0))],
        out_shape=[jax.ShapeDtypeStruct((rows // n_valid, width), BF16),
                   jax.ShapeDtypeStruct((rows, ATT_WIDTH), BF16 if fuse_cross else F32)],
        scratch_shapes=[pltpu.VMEM((ATT_WIDTH // HEAD_DIM, tm, HEAD_DIM), F32)],
        compiler_params=_params("arbitrary"),
        name="in_proj_b",
    )(*args)


CROSS_BATCH = 4


def _cross_sample_kernel(qc_ref, gq_ref, mem_ref, out_ref):
    for b in range(qc_ref.shape[0]):
        outs = _cross_heads(qc_ref[b], mem_ref.at[pl.ds(b, 1)], gq_ref[...])
        for hd in range(N_HEADS):
            out_ref[b, :, hd * HEAD_DIM:(hd + 1) * HEAD_DIM] = outs[hd].astype(out_ref.dtype)


def _cross_sample(qc, g_qc, mem_kv, first):
    nb, tq, _ = qc.shape
    cb = math.gcd(nb, CROSS_BATCH)
    assert first % cb == 0
    return pl.pallas_call(
        _cross_sample_kernel,
        grid=(nb // cb,),
        in_specs=[pl.BlockSpec((cb, tq, ATT_WIDTH), lambda b: (b, 0, 0)),
                  _full((1, HEAD_DIM)),
                  pl.BlockSpec((cb,) + mem_kv.shape[1:], lambda b: (first // cb + b, 0, 0))],
        out_specs=pl.BlockSpec((cb, tq, ATT_WIDTH), lambda b: (b, 0, 0)),
        out_shape=jax.ShapeDtypeStruct((nb, tq, ATT_WIDTH), BF16),
        compiler_params=_params("arbitrary"),
        name="cross_sample",
    )(qc, g_qc.reshape(1, HEAD_DIM), mem_kv)


INFLIGHT = 4


def _band_block(q, k, v):
    n = k.shape[0]
    dist = (n - BLOCK + lax.broadcasted_iota(jnp.int32, (BLOCK, n), 0)
            - lax.broadcasted_iota(jnp.int32, (BLOCK, n), 1))
    s = jnp.where((dist >= 0) & (dist <= DIL_SPAN), _dot_nt(q, k) * SCALE, NEG)
    m = jnp.max(s, axis=-1, keepdims=True)
    p = jnp.exp(s - m)
    l = jnp.sum(p, axis=-1, keepdims=True)
    return _dot(p.astype(BF16), v) / l, m + jnp.log(l)


def _attn_prompt_kernel(q0, q1, q2, k0, k1, k2, v0, v1, v2, out_ref, o_scr, l_scr, *, seq):
    qkv_refs = ((q0, k0, v0), (q1, k1, v1), (q2, k2, v2))
    for g, (_, r) in enumerate(DIL_GROUPS):
        q_ref, k_ref, v_ref = qkv_refs[g]
        nblk = seq // r // BLOCK

        def store(rho, blk, o, lse, g=g, r=r):
            start = blk * (BLOCK * r) + rho
            if r == 1:
                idx = pl.ds(pl.multiple_of(start, BLOCK), BLOCK)
            else:
                idx = pl.ds(start, BLOCK, stride=r)
            o_scr[g, idx, :] = o
            l_scr[g, idx, :] = jnp.broadcast_to(lse, (BLOCK, HEAD_DIM))

        def first(rho, q_ref=q_ref, k_ref=k_ref, v_ref=v_ref, store=store):
            lanes = slice(rho * HEAD_DIM, (rho + 1) * HEAD_DIM)
            o, lse = _band_block(q_ref[0, 0, :BLOCK, lanes], k_ref[0, 0, :BLOCK, lanes],
                                 v_ref[0, 0, :BLOCK, lanes])
            store(rho, 0, o, lse)

        def later(rho, blk, q_ref=q_ref, k_ref=k_ref, v_ref=v_ref, store=store):
            lanes = slice(rho * HEAD_DIM, (rho + 1) * HEAD_DIM)
            cur = pl.ds(pl.multiple_of(blk * BLOCK, BLOCK), BLOCK)
            both = pl.ds(pl.multiple_of((blk - 1) * BLOCK, BLOCK), 2 * BLOCK)
            o, lse = _band_block(q_ref[0, 0, cur, lanes], k_ref[0, 0, both, lanes], v_ref[0, 0, both, lanes])
            store(rho, blk, o, lse)

        for rho0 in range(0, r, INFLIGHT):
            rhos = range(rho0, min(r, rho0 + INFLIGHT))
            for rho in rhos:
                first(rho)

            def body(blk, carry, rhos=rhos, later=later):
                for rho in rhos:
                    later(rho, blk)
                return carry
            if nblk > 1:
                lax.fori_loop(1, nblk, body, 0, unroll=max(1, INFLIGHT // len(rhos)))

    def combine(c, carry):
        rows = pl.ds(pl.multiple_of(c * BLOCK, BLOCK), BLOCK)
        ls = [l_scr[g, rows, :] for g in range(N_GROUPS)]
        m = jnp.maximum(jnp.maximum(ls[0], ls[1]), ls[2])
        es = [jnp.exp(l - m) for l in ls]
        num = es[0] * o_scr[0, rows, :] + es[1] * o_scr[1, rows, :] + es[2] * o_scr[2, rows, :]
        out_ref[0, rows, :] = (num / (es[0] + es[1] + es[2])).astype(out_ref.dtype)
        return carry
    lax.fori_loop(0, seq // BLOCK, combine, 0)


def _attn_prompt(qkv_groups, n_seq, seq):
    in_specs, args = [], []
    for role in range(3):
        for g, (_, r) in enumerate(DIL_GROUPS):
            in_specs.append(pl.BlockSpec((1, 1, seq // r, r * HEAD_DIM),
                                         lambda b, h, role=role: (role * N_HEADS + h, b, 0, 0)))
            args.append(qkv_groups[g])
    return pl.pallas_call(
        functools.partial(_attn_prompt_kernel, seq=seq),
        grid=(n_seq, N_HEADS),
        in_specs=in_specs,
        out_specs=pl.BlockSpec((1, seq, HEAD_DIM), lambda b, h: (b, 0, h)),
        out_shape=jax.ShapeDtypeStruct((n_seq, seq, ATT_WIDTH), BF16),
        scratch_shapes=[pltpu.VMEM((N_GROUPS, seq, HEAD_DIM), F32),
                        pltpu.VMEM((N_GROUPS, seq, HEAD_DIM), F32)],
        compiler_params=_params("arbitrary", "arbitrary"),
        name="attn_prompt",
    )(*args)


def _attn_sample_kernel(q_ref, new_ref, old_ref, o_ref, lse_ref, *scratch, r, t_new, lb, compact):
    qf = q_ref[0].astype(F32)
    if compact:
        (flat,) = scratch
        n_key = lb // r
        for u in range(t_new):
            flat[u] = new_ref[:, u * KV_ROWS:(u + 1) * KV_ROWS, :].reshape(n_key * KV_ROWS, HEAD_DIM)
    else:
        n_key = lb
    i_key = lax.broadcasted_iota(jnp.int32, (n_key, 1), 0)
    j_old = lax.broadcasted_iota(jnp.int32, (t_new, 1), 0)
    for t in range(t_new):
        if compact:
            idx = i_key * r + (r - t_new + t)
            rows = lambda kvh, t=t: flat[t, pl.ds(kvh, n_key, stride=KV_ROWS), :]
        else:
            idx = i_key
            rows = lambda kvh: new_ref[pl.ds(kvh, n_key, stride=KV_ROWS), :]
        dist = lb - t_new + t - idx
        ok = (dist >= 0) & (dist % r == 0) & (dist <= r * DIL_SPAN)
        dist_old = lb + t - j_old
        ok_old = (dist_old % r == 0) & (dist_old <= r * DIL_SPAN)
        for hd in range(N_HEADS):
            sl = slice(hd * HEAD_DIM, (hd + 1) * HEAD_DIM)
            q = qf[t:t + 1, sl]
            k, v = rows(hd), rows(N_HEADS + hd)
            k_old = old_ref[pl.ds(hd, t_new, stride=KV_ROWS), :]
            v_old = old_ref[pl.ds(N_HEADS + hd, t_new, stride=KV_ROWS), :]
            s = jnp.where(ok, jnp.sum(k * q, axis=-1, keepdims=True) * SCALE, NEG)
            s_old = jnp.where(ok_old, jnp.sum(k_old * q, axis=-1, keepdims=True) * SCALE, NEG)
            m = jnp.maximum(jnp.max(s, axis=0, keepdims=True), jnp.max(s_old, axis=0, keepdims=True))
            p, p_old = jnp.exp(s - m), jnp.exp(s_old - m)
            l = jnp.sum(p, axis=0, keepdims=True) + jnp.sum(p_old, axis=0, keepdims=True)
            o = jnp.sum(p * v, axis=0, keepdims=True) + jnp.sum(p_old * v_old, axis=0, keepdims=True)
            o_ref[0, t:t + 1, sl] = o / l
            lse_ref[0, t:t + 1, sl] = jnp.broadcast_to(m + jnp.log(l), (1, HEAD_DIM))


def _attn_sample(q, new_buf, old_buf, r):
    nb, t_new, _ = q.shape
    lb = new_buf.shape[1] // KV_ROWS
    compact = r % t_new == 0
    if compact:
        new_view = new_buf.reshape(nb, lb // r, r // t_new, t_new * KV_ROWS, HEAD_DIM)
        new_spec = pl.BlockSpec((None, lb // r, None, t_new * KV_ROWS, HEAD_DIM),
                                lambda b: (b, 0, r // t_new - 1, 0, 0))
        scratch = [pltpu.VMEM((t_new, lb // r * KV_ROWS, HEAD_DIM), F32)]
    else:
        new_view = new_buf
        new_spec = pl.BlockSpec((None, lb * KV_ROWS, HEAD_DIM), lambda b: (b, 0, 0))
        scratch = []
    return pl.pallas_call(
        functools.partial(_attn_sample_kernel, r=r, t_new=t_new, lb=lb, compact=compact),
        grid=(nb,),
        in_specs=[pl.BlockSpec((1, t_new, ATT_WIDTH), lambda b: (b, 0, 0)),
                  new_spec,
                  pl.BlockSpec((None, t_new * KV_ROWS, HEAD_DIM), lambda b: (b, 0, 0))],
        out_specs=[pl.BlockSpec((1, t_new, ATT_WIDTH), lambda b: (b, 0, 0)),
                   pl.BlockSpec((1, t_new, ATT_WIDTH), lambda b: (b, 0, 0))],
        out_shape=[jax.ShapeDtypeStruct((nb, t_new, ATT_WIDTH), F32),
                   jax.ShapeDtypeStruct((nb, t_new, ATT_WIDTH), F32)],
        scratch_shapes=scratch,
        compiler_params=_params("arbitrary"),
        name=f"attn_sample_r{r}",
    )(q, new_view, old_buf)


def _combine_kernel(o0, o1, o2, l0, l1, l2, out_ref):
    ls = [l0[...], l1[...], l2[...]]
    m = jnp.maximum(jnp.maximum(ls[0], ls[1]), ls[2])
    es = [jnp.exp(l - m) for l in ls]
    num = es[0] * o0[...] + es[1] * o1[...] + es[2] * o2[...]
    out_ref[...] = (num / (es[0] + es[1] + es[2])).astype(out_ref.dtype)


def _combine_groups(outs, lses):
    shape = outs[0].shape
    return pl.pallas_call(
        _combine_kernel,
        in_specs=[_full(shape)] * 6,
        out_specs=_full(shape),
        out_shape=jax.ShapeDtypeStruct(shape, BF16),
        grid=(1,),
        compiler_params=_params("arbitrary"),
        name="combine_groups",
    )(*outs, *lses)


FF_CHUNK = 256
CARRY_ROWS = 8
SHIFT_SLOTS = 4


MOVE_ROWS = 8192


def _move_regions(keeps):
    regions = [(g, start, min(MOVE_ROWS, keep - start)) for g, keep in enumerate(keeps)
               for start in range(0, keep, MOVE_ROWS)]
    halves = ([], [])
    for reg in sorted(regions, key=lambda reg: -reg[2]):
        min(halves, key=lambda h: sum(r[2] for r in h)).append(reg)
    return halves


def _buffer_move(old_refs, kvn_refs, new_refs, bufs, tails, sems, b, n_steps):
    n_new = [k.shape[1] for k in kvn_refs]
    halves = _move_regions([o.shape[1] - n for o, n in zip(old_refs, n_new)])

    def copies(batch):
        out, sem = [], 0
        for buf, half in zip(bufs, halves):
            row, cps = 0, []
            for g, start, size in half:
                stage = buf.at[pl.ds(row, size)]
                cps.append((pltpu.make_async_copy(old_refs[g].at[batch, pl.ds(n_new[g] + start, size)], stage,
                                                  sems.at[sem]),
                            pltpu.make_async_copy(stage, new_refs[g].at[batch, pl.ds(start, size)],
                                                  sems.at[sem + 1])))
                row += size
                sem += 2
            out.append(cps)
        tail = []
        for g in range(N_GROUPS):
            keep = old_refs[g].shape[1] - n_new[g]
            tail.append((pltpu.make_async_copy(kvn_refs[g].at[batch], tails.at[g], sems.at[sem]),
                         pltpu.make_async_copy(tails.at[g], new_refs[g].at[batch, pl.ds(keep, n_new[g])],
                                               sems.at[sem + 1])))
            sem += 2
        return out[0], out[1], tail
    first, second, tail = copies(b)
    _, prev_second, _ = copies(jnp.maximum(b - 1, 0))

    def top():
        for cp_in, _ in first + tail:
            cp_in.start()

    def mid():
        for cp_in, cp_out in first + tail:
            cp_in.wait()
            cp_out.start()

        @pl.when(b > 0)
        def _():
            for _, cp_out in prev_second:
                cp_out.wait()
        for cp_in, _ in second:
            cp_in.start()

    def end():
        for cp_in, cp_out in second:
            cp_in.wait()
            cp_out.start()
        for _, cp_out in first + tail:
            cp_out.wait()

        @pl.when(b == n_steps - 1)
        def _():
            for _, cp_out in second:
                cp_out.wait()
    return top, mid, end


def _move_scratch(old, new_rows):
    halves = _move_regions([o.shape[1] - n.shape[1] for o, n in zip(old, new_rows)])
    n_copies = sum(len(h) for h in halves) + N_GROUPS
    return ([pltpu.VMEM((sum(r[2] for r in h), HEAD_DIM), F32) for h in halves]
            + [pltpu.VMEM((N_GROUPS,) + new_rows[0].shape[1:], F32), pltpu.SemaphoreType.DMA((2 * n_copies,))])


def _out_ffn_kernel(*refs, seq_len, tiles_per_seq, tail_rows, move):
    move_top = move_mid = move_end = lambda: None
    if move:
        (x_ref, mix_ref, cr_ref, wo_ref, g_ref, wup_ref, cw_ref, cb_ref, wdn_ref, o0, o1, o2, k0, k1, k2,
         out_ref, tail_ref, n0, n1, n2, shift, act_scr, carry, buf_a, buf_b, tails, sems) = refs
        move_top, move_mid, move_end = _buffer_move(
            (o0, o1, o2), (k0, k1, k2), (n0, n1, n2), (buf_a, buf_b), tails, sems,
            pl.program_id(0), pl.num_programs(0))
    elif seq_len is None:
        (x_ref, mix_ref, cr_ref, wo_ref, g_ref, wup_ref, cw_ref, cb_ref, wdn_ref,
         out_ref, tail_ref, shift, act_scr, carry) = refs
    else:
        (x_ref, mix_ref, cr_ref, wo_ref, g_ref, wup_ref, cw_ref, cb_ref, wdn_ref, e1_ref, e2_ref,
         out_ref, tail_ref, shift, act_scr) = refs
    tm = x_ref.shape[0]
    d_ff = wdn_ref.shape[0]
    move_top()
    x1 = x_ref[...] + _dot(jnp.concatenate([mix_ref[...], cr_ref[...]], axis=-1), wo_ref[...])
    h = _rms(x1, g_ref[...]).astype(BF16)
    if seq_len is None:
        @pl.when(pl.program_id(0) % tiles_per_seq == 0)
        def _():
            carry[...] = jnp.zeros_like(carry)
    else:
        t = lax.broadcasted_iota(jnp.int32, (tm, 1), 0) % seq_len
        has1 = t >= 1
        has2 = t >= 2

    def chunk_cols(j):
        return (slice(j * FF_CHUNK, (j + 1) * FF_CHUNK),
                slice(d_ff + j * FF_CHUNK, d_ff + (j + 1) * FF_CHUNK))

    def up_proj(j):
        return tuple(_dot(h, wup_ref[:, cols]) for cols in chunk_cols(j))

    def conv(up, cols, slot):
        buf = shift.at[slot]
        if seq_len is None:
            buf[:CARRY_ROWS, :] = carry[:, cols]
            carry[:, cols] = up[tm - CARRY_ROWS:, :]
        else:
            buf[:CARRY_ROWS, :] = jnp.zeros((CARRY_ROWS, FF_CHUNK), F32)
        buf[CARRY_ROWS:, :] = up
        tail_ref[0, :, cols] = up[tm - tail_rows:, :]
        prev1 = buf[CARRY_ROWS - 1:CARRY_ROWS - 1 + tm, :]
        prev2 = buf[CARRY_ROWS - 2:CARRY_ROWS - 2 + tm, :]
        if seq_len is not None:
            prev1 = jnp.where(has1, prev1, e1_ref[:, cols])
            prev2 = jnp.where(has2, prev2, e2_ref[:, cols])
        return (cb_ref[:, cols] + cw_ref[0:1, cols] * prev2 + cw_ref[1:2, cols] * prev1
                + cw_ref[2:3, cols] * up)

    n_chunks = d_ff // FF_CHUNK
    ups = up_proj(0)
    for j in range(n_chunks):
        nxt = up_proj(j + 1) if j + 1 < n_chunks else None
        a, b = (conv(up, cols, 2 * (j % 2) + s) for s, (up, cols) in enumerate(zip(ups, chunk_cols(j))))
        act_scr[:, j * FF_CHUNK:(j + 1) * FF_CHUNK] = (a * jax.nn.sigmoid(a) * b).astype(BF16)
        ups = nxt
        if j == n_chunks // 2:
            move_mid()
    out_ref[...] = x1 + _dot(act_scr[...], wdn_ref[...])
    move_end()


def _out_ffn(x2d, mix, cross, w_out_bf, g_ffn, w_up_bf, conv_w, conv_b, w_down_bf, tm,
             tiles_per_seq=None, seq_len=None, e1=None, e2=None, tail_rows=CARRY_ROWS, moves=None):
    rows, d = x2d.shape
    d_ff = w_down_bf.shape[0]
    n_tiles = rows // tm
    in_specs = [
        pl.BlockSpec((tm, d), lambda i: (i, 0)),
        pl.BlockSpec((tm, ATT_WIDTH), lambda i: (i, 0)),
        pl.BlockSpec((tm, ATT_WIDTH), lambda i: (i, 0)),
        _full(w_out_bf.shape), _full((1, d)), _full(w_up_bf.shape),
        _full((CONV_W, 2 * d_ff)), _full((1, 2 * d_ff)), _full(w_down_bf.shape),
    ]
    args = [x2d, mix, cross, w_out_bf, g_ffn.reshape(1, d), w_up_bf, conv_w, conv_b.reshape(1, 2 * d_ff),
            w_down_bf]
    scratch = [pltpu.VMEM((SHIFT_SLOTS, CARRY_ROWS + tm, FF_CHUNK), F32), pltpu.VMEM((tm, d_ff), BF16)]
    if seq_len is None:
        scratch += [pltpu.VMEM((CARRY_ROWS, 2 * d_ff), F32)]
    else:
        in_specs += [pl.BlockSpec((tm, 2 * d_ff), lambda i: (i, 0))] * 2
        args += [e1, e2]
    out_specs = [pl.BlockSpec((tm, d), lambda i: (i, 0)),
                 pl.BlockSpec((1, tail_rows, 2 * d_ff), lambda i: (i, 0, 0))]
    out_shape = [jax.ShapeDtypeStruct((rows, d), F32),
                 jax.ShapeDtypeStruct((n_tiles, tail_rows, 2 * d_ff), F32)]
    if moves is not None:
        old, new_rows = moves
        assert seq_len is None and all(o.shape[0] == n_tiles for o in old)
        any_spec = pl.BlockSpec(memory_space=pl.ANY)
        in_specs += [any_spec] * (2 * N_GROUPS)
        args += [*old, *new_rows]
        out_specs += [any_spec] * N_GROUPS
        out_shape += [jax.ShapeDtypeStruct(o.shape, o.dtype) for o in old]
        scratch += _move_scratch(old, new_rows)
    return pl.pallas_call(
        functools.partial(_out_ffn_kernel, seq_len=seq_len, tiles_per_seq=tiles_per_seq,
                          tail_rows=tail_rows, move=moves is not None),
        grid=(n_tiles,),
        in_specs=in_specs,
        out_specs=out_specs,
        out_shape=out_shape,
        scratch_shapes=scratch,
        compiler_params=_params("arbitrary"),
        name="out_ffn",
    )(*args)


def _swap_halves(x):
    return pltpu.roll(x, S5_STATE, 1)


def _s5_prep_kernel(lam_ref, logdt_ref, bt_ref, c_ref, d_ref, t_ref, cpt_ref, *var_refs, n_valids):
    gc = lam_ref.shape[0]
    n_groups = gc // S5_GROUP
    lane = lax.broadcasted_iota(jnp.int32, lam_ref.shape, 1)
    first = lane < S5_STATE
    sign = jnp.where(first, -1.0, 1.0)
    a = lam_ref[...]
    a_sw = _swap_halves(a)
    are = jnp.where(first, a, a_sw)
    aim = jnp.where(first, a_sw, a)
    dt = jnp.exp(logdt_ref[...])
    mag = jnp.exp(are * dt)
    lr = mag * jnp.cos(aim * dt)
    li = mag * jnp.sin(aim * dt)
    den = are * are + aim * aim
    xr = lr - 1.0
    f_re = (xr * are + li * aim) / den
    f_im = (li * are - xr * aim) / den
    lb = sign * li

    def cmul(x, m_re, m_sw):
        return x * m_re + _swap_halves(x) * m_sw

    c = c_ref[...]
    c_neg = c * -sign
    ri = lax.broadcasted_iota(jnp.int32, (gc, gc), 0)
    ci = lax.broadcasted_iota(jnp.int32, (gc, gc), 1)
    same_group = (ri // S5_GROUP) == (ci // S5_GROUP)
    e = cmul(bt_ref[...], f_re, sign * f_im)
    cl = c
    pw = jnp.where(first, 1.0, 0.0)
    zr = lax.broadcasted_iota(jnp.int32, (gc, S5_BLOCK), 0)
    zc = lax.broadcasted_iota(jnp.int32, (gc, S5_BLOCK), 1)
    same_out = (zr % S5_GROUP) == (zc % S5_GROUP)
    lags = jnp.zeros((gc, S5_BLOCK), F32)
    for k in range(S5_SUB):
        kmat = jnp.where(same_group, _dot_nt(e, c_neg, precision=HIGHEST), 0.0)
        if k == 0:
            kmat = kmat + jnp.where(ri == ci, d_ref[...], 0.0)
        place = jnp.where(same_out & (zc // S5_GROUP == k), 1.0, 0.0)
        lags = lags + jnp.dot(kmat, place, precision=HIGHEST, preferred_element_type=F32)
        both = jnp.concatenate([e, _swap_halves(e)], axis=-1).reshape(n_groups, S5_GROUP, 4 * S5_STATE)
        for n_valid, bp_ref in zip(n_valids, var_refs[0::2]):
            if n_valid - 1 - k >= 0:
                bp_ref[:, n_valid - 1 - k] = both.astype(bp_ref.dtype)
        cl = cmul(cl, lr, lb)
        cpt_ref[:, k] = (cl * -sign).reshape(n_groups, S5_GROUP, 2 * S5_STATE).astype(cpt_ref.dtype)
        e = cmul(e, lr, lb)
        pw = cmul(pw, lr, lb)
        for n_valid, lam_out_ref in zip(n_valids, var_refs[1::2]):
            if k + 1 == n_valid:
                lam_out_ref[...] = pw
    for n_valid, bp_ref in zip(n_valids, var_refs[0::2]):
        for tau in range(n_valid, S5_SUB):
            bp_ref[:, tau] = jnp.zeros((n_groups, S5_GROUP, 4 * S5_STATE), bp_ref.dtype)
    for taup in range(S5_SUB):
        moved = lags if taup == 0 else jnp.where(zc >= taup * S5_GROUP, pltpu.roll(lags, taup * S5_GROUP, 1), 0.0)
        t_ref[:, taup] = moved.reshape(n_groups, S5_GROUP, S5_BLOCK).astype(t_ref.dtype)


def _s5_prep(lam_re, lam_im, log_dt, b_re, b_im, c_re, c_im, d_skip, n_valids):
    n_groups = lam_re.shape[0]
    gc = n_groups * S5_GROUP
    rep = lambda t: jnp.repeat(t, S5_GROUP, axis=0)
    lam_p = rep(jnp.concatenate([lam_re, lam_im], axis=-1))
    logdt = rep(log_dt.reshape(n_groups, 1))
    bt_p = jnp.concatenate([b_re.transpose(0, 2, 1), b_im.transpose(0, 2, 1)], axis=-1).reshape(gc, 2 * S5_STATE)
    c_p = jnp.concatenate([c_re, c_im], axis=-1).reshape(gc, 2 * S5_STATE)
    t_shape = (n_groups, S5_SUB, S5_GROUP, S5_BLOCK)
    cpt_shape = (n_groups, S5_SUB, S5_GROUP, 2 * S5_STATE)
    bp_shape = (n_groups, S5_SUB, S5_GROUP, 4 * S5_STATE)
    lam_shape = (gc, 2 * S5_STATE)
    var = [(bp_shape, BF16), (lam_shape, F32)] * len(n_valids)
    t, cpt, *rest = pl.pallas_call(
        functools.partial(_s5_prep_kernel, n_valids=tuple(n_valids)),
        grid=(1,),
        in_specs=[_full(lam_p.shape), _full(logdt.shape), _full(bt_p.shape), _full(c_p.shape), _full((gc, 1))],
        out_specs=[_full(t_shape), _full(cpt_shape), *[_full(s) for s, _ in var]],
        out_shape=[jax.ShapeDtypeStruct(t_shape, BF16), jax.ShapeDtypeStruct(cpt_shape, BF16),
                   *[jax.ShapeDtypeStruct(s, dt) for s, dt in var]],
        compiler_params=_params("arbitrary"),
        name="s5_prep",
    )(lam_p, logdt, bt_p, c_p, d_skip.reshape(gc, 1))
    merge = lambda a: a.reshape(n_groups, S5_BLOCK, a.shape[-1])
    return [(merge(t), merge(bp), merge(cpt), lam_n[::S5_GROUP]) for bp, lam_n in zip(rest[0::2], rest[1::2])]


SCAN_GROUPS = 8
SUBLANES = 8
GLU_ROWS = 512


def _s5_core_kernel(v_ref, x0_ref, x0s_ref, t_ref, bp_ref, cpt_ref, lam_ref, wglu_ref, bglu_ref,
                    mix_ref, xfin_ref, s_scr, xprev_scr, m_scr, stage, *, n_valid, n_sub):
    rows = v_ref.shape[0]
    n_groups = t_ref.shape[0]
    lane = lax.broadcasted_iota(jnp.int32, (1, 2 * S5_STATE), 1)
    first = lane < S5_STATE
    blk = lambda g: slice(g * S5_BLOCK, (g + 1) * S5_BLOCK)
    tile = lambda i: slice(i * 2 * S5_STATE, (i + 1) * 2 * S5_STATE)
    for g in range(n_groups):
        s_scr[:, blk(g)] = _dot(v_ref[:, blk(g)], bp_ref[g])
        lam = lam_ref[g:g + 1, :]
        lam_sw = _swap_halves(lam)
        m_scr[0:1, tile(g)] = jnp.where(first, lam, lam_sw)
        m_scr[1:2, tile(g)] = jnp.where(first, -lam_sw, lam)
    if n_sub == 1:
        for g in range(n_groups):
            x, xs = x0_ref[:, tile(g)], x0s_ref[:, tile(g)]
            xprev_scr[:, tile(g)] = x
            xfin_ref[:, tile(g)] = (x * m_scr[0:1, tile(g)] + xs * m_scr[1:2, tile(g)]
                                    + s_scr[:, tile(2 * g)])
    else:
        assert rows == n_sub
        for g0 in range(0, n_groups, SCAN_GROUPS):
            gs = range(g0, min(n_groups, g0 + SCAN_GROUPS))

            def body(i, carry, gs=gs):
                rows8 = pl.ds(pl.multiple_of(i * SUBLANES, SUBLANES), SUBLANES)
                sub_i = lax.broadcasted_iota(jnp.int32, (SUBLANES, 2 * S5_STATE), 0)
                out = []
                for g, (x, xs) in zip(gs, carry):
                    s8, ssw8 = s_scr[rows8, tile(2 * g)], s_scr[rows8, tile(2 * g + 1)]
                    prev8 = jnp.zeros((SUBLANES, 2 * S5_STATE), F32)
                    for r in range(SUBLANES):
                        prev8 = jnp.where(sub_i == r, x, prev8)
                        m_re, m_sw = m_scr[0:1, tile(g)], m_scr[1:2, tile(g)]
                        x, xs = (x * m_re + xs * m_sw + s8[r:r + 1, :],
                                 xs * m_re - x * m_sw + ssw8[r:r + 1, :])
                    xprev_scr[rows8, tile(g)] = prev8
                    out.append((x, xs))
                return tuple(out)
            init = tuple((x0_ref[0:1, tile(g)], x0s_ref[0:1, tile(g)]) for g in gs)
            fin = lax.fori_loop(0, n_sub // SUBLANES, body, init)
            for g, (x, _) in zip(gs, fin):
                xfin_ref[:, tile(g)] = jnp.broadcast_to(x, (xfin_ref.shape[0], 2 * S5_STATE))
    for g in range(n_groups):
        s_scr[:, blk(g)] = (_dot(v_ref[:, blk(g)], t_ref[g])
                            + _dot_nt(xprev_scr[:, tile(g)].astype(BF16), cpt_ref[g]))
    for cb in range(ATT_WIDTH // HEAD_DIM):
        for half in range(-(-n_valid // SLOTS)):
            toks = _slot_transpose([s_scr[:, tile(2 * (cb * SLOTS + p) + half)] for p in range(SLOTS)])
            for s in range(min(SLOTS, n_valid - half * SLOTS)):
                stage[cb, pl.ds(half * SLOTS + s, rows, stride=n_valid), :] = toks[s]
    n_tok = rows * n_valid
    chunk = min(n_tok, GLU_ROWS)

    def glu(i, carry):
        r = pl.ds(pl.multiple_of(i * chunk, chunk), chunk)
        y = jax.nn.gelu(jnp.concatenate([stage[cb, r, :] for cb in range(ATT_WIDTH // HEAD_DIM)], axis=-1))
        z = _dot(y.astype(BF16), wglu_ref[...]) + bglu_ref[...]
        mix_ref[r, :] = (y * jax.nn.sigmoid(z)).astype(mix_ref.dtype)
        return carry
    lax.fori_loop(0, n_tok // chunk, glu, 0)


def _s5_mixer(v, rows_per_tile, n_valid, n_sub, x0, x0s, t, bp, cpt, lam_n, w_glu_bf, b_glu):
    m, width = v.shape
    n_tiles = m // rows_per_tile
    r0 = x0.shape[0] // n_tiles
    n_state = x0.shape[1]
    n_tok = rows_per_tile * n_valid
    return pl.pallas_call(
        functools.partial(_s5_core_kernel, n_valid=n_valid, n_sub=n_sub),
        grid=(n_tiles,),
        in_specs=[pl.BlockSpec((rows_per_tile, width), lambda i: (i, 0)),
                  pl.BlockSpec((r0, n_state), lambda i: (i, 0)),
                  pl.BlockSpec((r0, n_state), lambda i: (i, 0)),
                  _full(t.shape), _full(bp.shape), _full(cpt.shape), _full(lam_n.shape),
                  _full(w_glu_bf.shape), _full((1, ATT_WIDTH))],
        out_specs=[pl.BlockSpec((n_tok, ATT_WIDTH), lambda i: (i, 0)),
                   pl.BlockSpec((r0, n_state), lambda i: (i, 0))],
        out_shape=[jax.ShapeDtypeStruct((m * n_valid, ATT_WIDTH), BF16),
                   jax.ShapeDtypeStruct(x0.shape, F32)],
        scratch_shapes=[pltpu.VMEM((rows_per_tile, width), F32),
                        pltpu.VMEM((rows_per_tile, n_state), F32),
                        pltpu.VMEM((8, n_state), F32),
                        pltpu.VMEM((ATT_WIDTH // HEAD_DIM, n_tok, HEAD_DIM), F32)],
        compiler_params=_params("arbitrary"),
        name="s5_core",
    )(v, x0, x0s, t, bp, cpt, lam_n, w_glu_bf, b_glu.reshape(1, ATT_WIDTH))


def _pack_state(s):
    n = s.shape[0]
    packed = s.transpose(0, 2, 1, 3).reshape(n, -1)
    swapped = jnp.stack([s[:, 1], s[:, 0]], axis=1).transpose(0, 2, 1, 3).reshape(n, -1)
    return packed, swapped


def _unpack_state(x):
    n = x.shape[0]
    return x.reshape(n, -1, 2, S5_STATE).transpose(0, 2, 1, 3)


PROMPT_TILE = 512
Q_PAD = 16


def _pad_rows(t, n):
    return jnp.pad(t, ((0, 0), (0, n - t.shape[1]), (0, 0)))


def kernel(x_prompt, x_sample, cache_win0_kv, cache_win1_kv, cache_win2_kv, cache_mem_kv, state_s5,
           state_ffn_conv, mem_prompt, g_mix, g_ffn, w_in_a, g_q_dil, g_k_dil, w_in_b, s5_lam_re,
           s5_lam_im, s5_log_dt, s5_b_re, s5_b_im, s5_c_re, s5_c_im, s5_d, w_glu, b_glu, g_mem,
           w_mem_kv, g_q_cross, g_k_cross, w_out, w_up, conv_w, conv_b, w_down):
    nb, seq, d = x_prompt.shape
    db, ts, _ = x_sample.shape
    depth = g_mix.shape[0]
    n_mem = mem_prompt.shape[1]
    d_ff2 = w_up.shape[2]
    assert ts >= CONV_W - 1 and seq % PROMPT_TILE == 0 and ts <= Q_PAD
    caches = (cache_win0_kv, cache_win1_kv, cache_win2_kv)
    w_mem_bf = w_mem_kv.astype(BF16)
    mem_cache = cache_mem_kv.reshape(depth * db, n_mem * KV_ROWS, HEAD_DIM)

    tab_p = _rope_tables(jnp.arange(seq, dtype=jnp.int32))
    tab_s = tuple(jnp.tile(t, (db, 1)) for t in _rope_tables(PAST_LEN + jnp.arange(ts, dtype=jnp.int32)))
    win_keep = tuple(min(w, seq) for w, _ in DIL_GROUPS)
    rows_s = db * ts
    kv_tail = (2, N_HEADS, HEAD_DIM)

    mem_p = _mem_kv(mem_prompt.reshape(nb * n_mem, d), g_mem, w_mem_bf, g_k_cross)

    xp = x_prompt.reshape(nb * seq, d)
    xs = x_sample.reshape(rows_s, d)
    p_win, s_win = [[] for _ in DIL_GROUPS], [[] for _ in DIL_GROUPS]
    p_s5, s_s5, p_conv, s_conv = [], [], [], []
    tiles_per_seq = seq // PROMPT_TILE
    for i in range(depth):
        mem_i = mem_p[i].reshape(nb, n_mem * KV_ROWS, HEAD_DIM)
        if i % 2 == 0:
            ia = i // 2
            w_in_a_bf = w_in_a[ia].astype(BF16)
            *qkv_groups, w0, w1, w2, cross_p = _in_proj_a(
                xp, seq, PROMPT_TILE, g_mix[i], w_in_a_bf, tab_p, g_q_dil[ia], g_k_dil[ia],
                g_q_cross[i], mem_i, win_keep, dilated=True)
            for g, w in enumerate((w0, w1, w2)):
                p_win[g].append(w.reshape((nb, win_keep[g]) + kv_tail))
            mix_p = _attn_prompt(qkv_groups, nb, seq).reshape(nb * seq, ATT_WIDTH)
            qkv_s, *kv_new, qc_s = _in_proj_a(
                xs, rows_s, rows_s, g_mix[i], w_in_a_bf, tab_s, g_q_dil[ia], g_k_dil[ia],
                None, None, (rows_s,) * N_GROUPS, dilated=False)
            old_bufs = [c[ia].reshape(db, -1, HEAD_DIM) for c in caches]
            moves = (old_bufs, [k.reshape(db, ts * KV_ROWS, HEAD_DIM) for k in kv_new])
        else:
            moves = None
            ib = i // 2
            prm = (s5_lam_re[ib], s5_lam_im[ib], s5_log_dt[ib], s5_b_re[ib], s5_b_im[ib], s5_c_re[ib],
                   s5_c_im[ib], s5_d[ib])
            n_state = s5_lam_re.shape[1] * 2 * S5_STATE
            w_in_b_bf, w_glu_bf = w_in_b[ib].astype(BF16), w_glu[ib].astype(BF16)
            u_p, cross_p = _in_proj_b(xp, seq, PROMPT_TILE, S5_SUB, g_mix[i], w_in_b_bf, g_q_cross[i], mem_i)
            zero = jnp.zeros((nb * CARRY_ROWS, n_state), F32)
            prep_p, prep_s = _s5_prep(*prm, (S5_SUB, ts))
            mix_p, fin_p = _s5_mixer(u_p, seq // S5_SUB, S5_SUB, seq // S5_SUB, zero, zero,
                                     *prep_p, w_glu_bf, b_glu[ib])
            p_s5.append(_unpack_state(fin_p.reshape(nb, CARRY_ROWS, n_state)[:, 0]))
            u_s, qc_s = _in_proj_b(xs, rows_s, rows_s, ts, g_mix[i], w_in_b_bf, None, None)
            mix_s, fin_s = _s5_mixer(u_s, db, ts, 1, *_pack_state(state_s5[ib]), *prep_s, w_glu_bf, b_glu[ib])
            s_s5.append(_unpack_state(fin_s))
        cross_s = _cross_sample(_pad_rows(qc_s.reshape(db, ts, ATT_WIDTH), Q_PAD), g_q_cross[i], mem_cache,
                                first=i * db)
        cross_s = cross_s[:, :ts].reshape(rows_s, ATT_WIDTH)

        ffn_w = (w_out[i].astype(BF16), g_ffn[i], w_up[i].astype(BF16), conv_w[i], conv_b[i],
                 w_down[i].astype(BF16))
        xp, tails, *new_bufs = _out_ffn(xp, mix_p, cross_p, *ffn_w, PROMPT_TILE, tiles_per_seq=tiles_per_seq,
                                        moves=moves)
        p_conv.append(tails.reshape(nb, tiles_per_seq, CARRY_ROWS, d_ff2)[:, -1, CARRY_ROWS - (CONV_W - 1):])
        if moves is not None:
            outs, lses = [], []
            for g, (_, r) in enumerate(DIL_GROUPS):
                q_g = qkv_s[g * N_HEADS:(g + 1) * N_HEADS]
                q_g = q_g.transpose(1, 0, 2).reshape(db, ts, ATT_WIDTH)
                o_g, lse_g = _attn_sample(q_g, new_bufs[g], moves[0][g], r)
                outs.append(o_g.reshape(rows_s, ATT_WIDTH))
                lses.append(lse_g.reshape(rows_s, ATT_WIDTH))
                s_win[g].append(new_bufs[g].reshape(caches[g][i // 2].shape))
            mix_s = _combine_groups(outs, lses)
        buf = state_ffn_conv[i]
        zero = jnp.zeros((db, ts - 2, d_ff2), F32)
        e1 = jnp.concatenate([buf[:, 1:2], zero, zero[:, :1]], axis=1).reshape(rows_s, d_ff2)
        e2 = jnp.concatenate([buf, zero], axis=1).reshape(rows_s, d_ff2)
        xs, tails = _out_ffn(xs, mix_s, cross_s, *ffn_w, rows_s, seq_len=ts, e1=e1, e2=e2, tail_rows=rows_s)
        s_conv.append(tails.reshape(db, ts, d_ff2)[:, ts - (CONV_W - 1):])

    return (xp.reshape(nb, seq, d), xs.reshape(db, ts, d),
            jnp.stack(p_win[0]), jnp.stack(p_win[1]), jnp.stack(p_win[2]),
            mem_p.reshape((depth, nb, n_mem) + kv_tail),
            jnp.stack(p_s5), jnp.stack(p_conv),
            jnp.stack(s_win[0]), jnp.stack(s_win[1]), jnp.stack(s_win[2]),
            jnp.stack(s_s5), jnp.stack(s_conv))
```

```python
import functools
import math

import jax
import jax.numpy as jnp
from jax import lax
from jax.experimental import pallas as pl
from jax.experimental.pallas import tpu as pltpu

HEAD_DIM = 128
N_HEADS = 4
DIL_GROUPS = ((128, 1), (512, 4), (2048, 16))
N_GROUPS = len(DIL_GROUPS)
DIL_SPAN = 128
BLOCK = 128
ATT_WIDTH = N_HEADS * HEAD_DIM
KV_ROWS = 2 * N_HEADS
ROT_DIM = HEAD_DIM // 4
ROT_HALF = ROT_DIM // 2
ROPE_THETA = 500000.0
S5_GROUP = 16
S5_STATE = 64
S5_SUB = 16
S5_BLOCK = S5_SUB * S5_GROUP
CONV_W = 3
EPS = 1e-6
NEG = -1e30
SCALE = HEAD_DIM ** -0.5
PAST_LEN = 16384

VMEM_LIMIT_V7X = 56 * 1024 * 1024
BF16 = jnp.bfloat16
F32 = jnp.float32
HIGHEST = lax.Precision.HIGHEST


def _params(*sem):
    return pltpu.CompilerParams(dimension_semantics=sem, vmem_limit_bytes=VMEM_LIMIT_V7X)


def _rms(x, g):
    return x * lax.rsqrt(jnp.mean(x * x, axis=-1, keepdims=True) + EPS) * g


def _rms_head(x, g):
    ones = jnp.ones((HEAD_DIM, HEAD_DIM), BF16)
    ssq = _dot((x * x).astype(BF16), ones)
    return x * lax.rsqrt(ssq * (1.0 / HEAD_DIM) + EPS) * g


def _dot(a, b):
    return jnp.dot(a, b, preferred_element_type=F32)


def _dot_nt(a, b, precision=None):
    return lax.dot_general(a, b, (((1,), (1,)), ((), ())), precision=precision,
                           preferred_element_type=F32)


def _full(shape):
    nd = len(shape)
    return pl.BlockSpec(shape, lambda *_: (0,) * nd)


def _kv_rows(kv, head, n):
    return pl.ds(kv * N_HEADS + head, n, stride=KV_ROWS)


def _mem_kv_kernel(mem_ref, g_ref, w_ref, gk_ref, out_ref):
    tm = mem_ref.shape[0]
    h = _rms(mem_ref[...], g_ref[0]).astype(BF16)
    kv = _dot(h, w_ref[0])
    for hd in range(N_HEADS):
        sl = slice(hd * HEAD_DIM, (hd + 1) * HEAD_DIM)
        out_ref[0, _kv_rows(0, hd, tm), :] = _rms(kv[:, sl], gk_ref[0])
        out_ref[0, _kv_rows(1, hd, tm), :] = kv[:, ATT_WIDTH + hd * HEAD_DIM:ATT_WIDTH + (hd + 1) * HEAD_DIM]


def _mem_kv(mem2d, g_mem, w_kv_bf, g_k):
    depth, d, _ = w_kv_bf.shape
    rows = mem2d.shape[0]
    tm = min(rows, 512)
    return pl.pallas_call(
        _mem_kv_kernel,
        grid=(depth, rows // tm),
        in_specs=[
            pl.BlockSpec((tm, d), lambda l, i: (i, 0)),
            pl.BlockSpec((1, 1, d), lambda l, i: (l, 0, 0)),
            pl.BlockSpec((1, d, 2 * ATT_WIDTH), lambda l, i: (l, 0, 0)),
            pl.BlockSpec((1, 1, HEAD_DIM), lambda l, i: (l, 0, 0)),
        ],
        out_specs=pl.BlockSpec((1, tm * KV_ROWS, HEAD_DIM), lambda l, i: (l, i, 0)),
        out_shape=jax.ShapeDtypeStruct((depth, rows * KV_ROWS, HEAD_DIM), F32),
        compiler_params=_params("arbitrary", "arbitrary"),
        name="mem_kv",
    )(mem2d, g_mem.reshape(depth, 1, d), w_kv_bf, g_k.reshape(depth, 1, HEAD_DIM))


def _cross_heads(qc, mem_ref, gq):
    n_mem = mem_ref.shape[1] // KV_ROWS
    outs = []
    for hd in range(N_HEADS):
        q = _rms_head(qc[:, hd * HEAD_DIM:(hd + 1) * HEAD_DIM], gq).astype(BF16)
        k = mem_ref[0, _kv_rows(0, hd, n_mem), :].astype(BF16)
        v = mem_ref[0, _kv_rows(1, hd, n_mem), :].astype(BF16)
        s = _dot_nt(q, k) * SCALE
        m = jnp.max(s, axis=-1, keepdims=True)
        p = jnp.exp(s - m)
        l = jnp.sum(p, axis=-1, keepdims=True)
        outs.append(_dot(p.astype(BF16), v) / l)
    return outs


def _rope(x, cos_t, sin_lo, sin_hi):
    return (x * cos_t + pltpu.roll(x, HEAD_DIM - ROT_HALF, 1) * sin_lo
            + pltpu.roll(x, ROT_HALF, 1) * sin_hi)


def _in_proj_a_kernel(*refs, fuse_cross, dilated, win_rows, win_first, tiles_per_seq):
    n_in = 10 if fuse_cross else 8
    x_ref, g_ref, w_ref, cos_ref, slo_ref, shi_ref, gq_ref, gk_ref = refs[:8]
    n_qkv = N_GROUPS if dilated else 1
    qkv_refs = refs[n_in:n_in + n_qkv]
    win_refs = refs[n_in + n_qkv:n_in + n_qkv + N_GROUPS]
    cr_ref = refs[n_in + n_qkv + N_GROUPS]
    kv_stash = refs[n_in + n_qkv + N_GROUPS + 1]
    stage = refs[n_in + n_qkv + N_GROUPS + 2] if dilated else None
    tm = x_ref.shape[0]
    h = _rms(x_ref[...], g_ref[...]).astype(BF16)
    cos_t, sin_lo, sin_hi = cos_ref[...], slo_ref[...], shi_ref[...]
    def proj(c):
        return _dot(h, w_ref[:, c * ATT_WIDTH:(c + 1) * ATT_WIDTH])

    tile = pl.program_id(0) % tiles_per_seq
    in_window = [tile >= first for first in win_first]
    for g in range(N_GROUPS):
        @pl.when(jnp.logical_not(in_window[g]))
        def _(g=g):
            win_refs[g][...] = jnp.zeros_like(win_refs[g])

    y = proj(0)
    for c in range(3 * N_GROUPS):
        y_next = proj(c + 1)
        role, g = divmod(c, N_GROUPS)
        r = DIL_GROUPS[g][1]
        wr = win_rows[g]
        for hd in range(N_HEADS):
            yh = y[:, hd * HEAD_DIM:(hd + 1) * HEAD_DIM]
            if role == 0:
                yh = _rope(_rms_head(yh, gq_ref[g]), cos_t, sin_lo, sin_hi)
            elif role == 1:
                yh = _rope(_rms_head(yh, gk_ref[g]), cos_t, sin_lo, sin_hi)
            if role > 0:
                kv_stash[((role - 1) * N_GROUPS + g) * N_HEADS + hd] = yh
            if not dilated:
                qkv_refs[0][role * N_GROUPS * N_HEADS + g * N_HEADS + hd] = yh.astype(BF16)
            elif r == 1:
                qkv_refs[g][role * N_HEADS + hd, 0] = yh.astype(BF16)
            else:
                stage[hd] = yh
                for rho in range(r):
                    qkv_refs[g][role * N_HEADS + hd, 0, :, rho * HEAD_DIM:(rho + 1) * HEAD_DIM] = (
                        stage[hd, pl.ds(rho, tm // r, stride=r), :].astype(BF16))
        y = y_next
    qc = y
    if fuse_cross:
        outs = _cross_heads(qc, refs[9], refs[8][...])
        for hd in range(N_HEADS):
            cr_ref[:, hd * HEAD_DIM:(hd + 1) * HEAD_DIM] = outs[hd].astype(cr_ref.dtype)
    else:
        cr_ref[...] = qc
    for g in range(N_GROUPS):
        @pl.when(in_window[g])
        def _(g=g):
            wr = win_rows[g]
            for kv in range(2):
                for hd in range(N_HEADS):
                    win_refs[g][_kv_rows(kv, hd, wr), :] = kv_stash[(kv * N_GROUPS + g) * N_HEADS + hd,
                                                                    tm - wr:, :]


SLOTS = HEAD_DIM // S5_GROUP


def _slot_transpose(vs):
    slot = lax.broadcasted_iota(jnp.int32, vs[0].shape, 1) // S5_GROUP
    d = SLOTS // 2
    while d:
        low = (slot & d) == 0
        new = list(vs)
        for i in range(SLOTS):
            if not i & d:
                new[i] = jnp.where(low, vs[i], pltpu.roll(vs[i | d], d * S5_GROUP, 1))
                new[i | d] = jnp.where(low, pltpu.roll(vs[i], HEAD_DIM - d * S5_GROUP, 1), vs[i | d])
        vs = new
        d //= 2
    return vs


def _in_proj_b_kernel(*refs, fuse_cross, n_valid):
    if fuse_cross:
        x_ref, g_ref, w_ref, gqc_ref, mem_ref, u_ref, cr_ref, stage = refs
    else:
        x_ref, g_ref, w_ref, u_ref, cr_ref, stage = refs
    tm = x_ref.shape[0]
    n_rows = tm // n_valid
    h = _rms(x_ref[...], g_ref[...]).astype(BF16)
    u = _dot(h, w_ref[:, :ATT_WIDTH])
    zero = jnp.zeros((n_rows, HEAD_DIM), F32)
    for cb in range(ATT_WIDTH // HEAD_DIM):
        stage[cb] = u[:, cb * HEAD_DIM:(cb + 1) * HEAD_DIM]
        for half in range(S5_SUB // SLOTS):
            taus = range(half * SLOTS, (half + 1) * SLOTS)
            groups = _slot_transpose([stage[cb, pl.ds(tau, n_rows, stride=n_valid), :] if tau < n_valid else zero
                                      for tau in taus])
            for p in range(SLOTS):
                lanes = (2 * (cb * SLOTS + p) + half) * HEAD_DIM
                u_ref[:, lanes:lanes + HEAD_DIM] = groups[p].astype(BF16)
    qc = _dot(h, w_ref[:, ATT_WIDTH:])
    if fuse_cross:
        outs = _cross_heads(qc, mem_ref, gqc_ref[...])
        for hd in range(N_HEADS):
            cr_ref[:, hd * HEAD_DIM:(hd + 1) * HEAD_DIM] = outs[hd].astype(cr_ref.dtype)
    else:
        cr_ref[...] = qc


def _rope_tables(pos):
    inv = jnp.exp(-math.log(ROPE_THETA) * jnp.arange(ROT_HALF, dtype=F32) / ROT_HALF)
    ang = pos.astype(F32)[:, None] * inv[None, :]
    cos, sin = jnp.cos(ang), jnp.sin(ang)
    rows = pos.shape[0]
    ones = jnp.ones((rows, HEAD_DIM - ROT_DIM), F32)
    zeros = jnp.zeros((rows, HEAD_DIM - ROT_DIM), F32)
    z16 = jnp.zeros((rows, ROT_HALF), F32)
    cos_t = jnp.concatenate([cos, cos, ones], axis=1)
    sin_lo = jnp.concatenate([-sin, z16, zeros], axis=1)
    sin_hi = jnp.concatenate([z16, sin, zeros], axis=1)
    return cos_t, sin_lo, sin_hi


def _in_proj_a(x2d, seq, tm, g_mix, w_bf, tables, g_q, g_k, g_qc, mem_kv, win_keep, dilated):
    rows, d = x2d.shape
    n_in = w_bf.shape[1]
    tiles_per_seq = seq // tm
    n_seq = rows // seq
    fuse_cross = mem_kv is not None
    tab_tiles = tables[0].shape[0] // tm
    win_rows = tuple(min(k, tm) for k in win_keep)
    win_first = tuple(tiles_per_seq - k // wr for k, wr in zip(win_keep, win_rows))

    def win_spec(keep):
        wr = min(keep, tm)
        nblk = keep // wr
        first = tiles_per_seq - nblk

        def imap(i):
            b = i // tiles_per_seq
            t = i % tiles_per_seq
            return (b * nblk + jnp.maximum(t - first, 0), 0)
        return pl.BlockSpec((wr * KV_ROWS, HEAD_DIM), imap)

    in_specs = [
        pl.BlockSpec((tm, d), lambda i: (i, 0)),
        _full((1, d)),
        _full((d, n_in)),
        pl.BlockSpec((tm, HEAD_DIM), lambda i: (i % tab_tiles, 0)),
        pl.BlockSpec((tm, HEAD_DIM), lambda i: (i % tab_tiles, 0)),
        pl.BlockSpec((tm, HEAD_DIM), lambda i: (i % tab_tiles, 0)),
        _full((N_GROUPS, HEAD_DIM)),
        _full((N_GROUPS, HEAD_DIM)),
    ]
    args = [x2d, g_mix.reshape(1, d), w_bf, *tables, g_q, g_k]
    if fuse_cross:
        in_specs += [_full((1, HEAD_DIM)),
                     pl.BlockSpec((1,) + mem_kv.shape[1:], lambda i: (i // tiles_per_seq, 0, 0))]
        args += [g_qc.reshape(1, HEAD_DIM), mem_kv]
    if dilated:
        qkv_specs = [pl.BlockSpec((3 * N_HEADS, 1, tm // r, r * HEAD_DIM),
                                  lambda i: (0, i // tiles_per_seq, i % tiles_per_seq, 0))
                     for _, r in DIL_GROUPS]
        qkv_shapes = [jax.ShapeDtypeStruct((3 * N_HEADS, n_seq, seq // r, r * HEAD_DIM), BF16)
                      for _, r in DIL_GROUPS]
        scratch = [pltpu.VMEM((N_HEADS, tm, HEAD_DIM), F32)]
    else:
        qkv_specs = [pl.BlockSpec((3 * N_GROUPS * N_HEADS, tm, HEAD_DIM), lambda i: (0, i, 0))]
        qkv_shapes = [jax.ShapeDtypeStruct((3 * N_GROUPS * N_HEADS, rows, HEAD_DIM), BF16)]
        scratch = []
    scratch = [pltpu.VMEM((2 * N_GROUPS * N_HEADS, tm, HEAD_DIM), F32)] + scratch
    out_specs = [*qkv_specs, *[win_spec(k) for k in win_keep],
                 pl.BlockSpec((tm, ATT_WIDTH), lambda i: (i, 0))]
    out_shape = [*qkv_shapes,
                 *[jax.ShapeDtypeStruct((n_seq * k * KV_ROWS, HEAD_DIM), F32) for k in win_keep],
                 jax.ShapeDtypeStruct((rows, ATT_WIDTH), BF16 if fuse_cross else F32)]
    return pl.pallas_call(
        functools.partial(_in_proj_a_kernel, fuse_cross=fuse_cross, dilated=dilated, win_rows=win_rows,
                          win_first=win_first, tiles_per_seq=tiles_per_seq),
        grid=(rows // tm,),
        in_specs=in_specs, out_specs=out_specs, out_shape=out_shape, scratch_shapes=scratch,
        compiler_params=_params("arbitrary"),
        name="in_proj_a",
    )(*args)


def _in_proj_b(x2d, seq, tm, n_valid, g_mix, w_bf, g_qc, mem_kv):
    rows, d = x2d.shape
    width = (ATT_WIDTH // S5_GROUP) * S5_BLOCK
    n_in = w_bf.shape[1]
    tiles_per_seq = seq // tm
    fuse_cross = mem_kv is not None
    in_specs = [pl.BlockSpec((tm, d), lambda i: (i, 0)), _full((1, d)), _full((d, n_in))]
    args = [x2d, g_mix.reshape(1, d), w_bf]
    if fuse_cross:
        in_specs += [_full((1, HEAD_DIM)),
                     pl.BlockSpec((1,) + mem_kv.shape[1:], lambda i: (i // tiles_per_seq, 0, 0))]
        args += [g_qc.reshape(1, HEAD_DIM), mem_kv]
    return pl.pallas_call(
        functools.partial(_in_proj_b_kernel, fuse_cross=fuse_cross, n_valid=n_valid),
        grid=(rows // tm,),
        in_specs=in_specs,
        out_specs=[pl.BlockSpec((tm // n_valid, width), lambda i: (i, 0)),
                   pl.BlockSpec((tm, ATT_WIDTH), lambda i: (i, 0))],
        out_shape=[jax.ShapeDtypeStruct((rows // n_valid, width), BF16),
                   jax.ShapeDtypeStruct((rows, ATT_WIDTH), BF16 if fuse_cross else F32)],
        scratch_shapes=[pltpu.VMEM((ATT_WIDTH // HEAD_DIM, tm, HEAD_DIM), F32)],
        compiler_params=_params("arbitrary"),
        name="in_proj_b",
    )(*args)


CROSS_BATCH = 4


def _cross_sample_kernel(qc_ref, gq_ref, mem_ref, out_ref):
    for b in range(qc_ref.shape[0]):
        outs = _cross_heads(qc_ref[b], mem_ref.at[pl.ds(b, 1)], gq_ref[...])
        for hd in range(N_HEADS):
            out_ref[b, :, hd * HEAD_DIM:(hd + 1) * HEAD_DIM] = outs[hd].astype(out_ref.dtype)


def _cross_sample(qc, g_qc, mem_kv, first):
    nb, tq, _ = qc.shape
    cb = math.gcd(nb, CROSS_BATCH)
    assert first % cb == 0
    return pl.pallas_call(
        _cross_sample_kernel,
        grid=(nb // cb,),
        in_specs=[pl.BlockSpec((cb, tq, ATT_WIDTH), lambda b: (b, 0, 0)),
                  _full((1, HEAD_DIM)),
                  pl.BlockSpec((cb,) + mem_kv.shape[1:], lambda b: (first // cb + b, 0, 0))],
        out_specs=pl.BlockSpec((cb, tq, ATT_WIDTH), lambda b: (b, 0, 0)),
        out_shape=jax.ShapeDtypeStruct((nb, tq, ATT_WIDTH), BF16),
        compiler_params=_params("arbitrary"),
        name="cross_sample",
    )(qc, g_qc.reshape(1, HEAD_DIM), mem_kv)


INFLIGHT = 4


def _band_block(q, k, v):
    n = k.shape[0]
    dist = (n - BLOCK + lax.broadcasted_iota(jnp.int32, (BLOCK, n), 0)
            - lax.broadcasted_iota(jnp.int32, (BLOCK, n), 1))
    s = jnp.where((dist >= 0) & (dist <= DIL_SPAN), _dot_nt(q, k) * SCALE, NEG)
    m = jnp.max(s, axis=-1, keepdims=True)
    p = jnp.exp(s - m).astype(BF16)
    l = _dot(p, jnp.ones((n, HEAD_DIM), BF16))
    return _dot(p, v) / l, m + jnp.log(l)


def _attn_prompt_kernel(q0, q1, q2, k0, k1, k2, v0, v1, v2, out_ref, o_scr, l_scr, *, seq):
    qkv_refs = ((q0, k0, v0), (q1, k1, v1), (q2, k2, v2))
    for g, (_, r) in enumerate(DIL_GROUPS):
        q_ref, k_ref, v_ref = qkv_refs[g]
        nblk = seq // r // BLOCK

        def store(rho, blk, o, lse, g=g, r=r):
            start = blk * (BLOCK * r) + rho
            if r == 1:
                idx = pl.ds(pl.multiple_of(start, BLOCK), BLOCK)
            else:
                idx = pl.ds(start, BLOCK, stride=r)
            o_scr[g, idx, :] = o
            l_scr[g, idx, :] = jnp.broadcast_to(lse, (BLOCK, HEAD_DIM))

        def first(rho, q_ref=q_ref, k_ref=k_ref, v_ref=v_ref, store=store):
            lanes = slice(rho * HEAD_DIM, (rho + 1) * HEAD_DIM)
            o, lse = _band_block(q_ref[0, 0, :BLOCK, lanes], k_ref[0, 0, :BLOCK, lanes],
                                 v_ref[0, 0, :BLOCK, lanes])
            store(rho, 0, o, lse)

        def later(rho, blk, q_ref=q_ref, k_ref=k_ref, v_ref=v_ref, store=store):
            lanes = slice(rho * HEAD_DIM, (rho + 1) * HEAD_DIM)
            cur = pl.ds(pl.multiple_of(blk * BLOCK, BLOCK), BLOCK)
            both = pl.ds(pl.multiple_of((blk - 1) * BLOCK, BLOCK), 2 * BLOCK)
            o, lse = _band_block(q_ref[0, 0, cur, lanes], k_ref[0, 0, both, lanes], v_ref[0, 0, both, lanes])
            store(rho, blk, o, lse)

        for rho0 in range(0, r, INFLIGHT):
            rhos = range(rho0, min(r, rho0 + INFLIGHT))
            for rho in rhos:
                first(rho)

            def body(blk, carry, rhos=rhos, later=later):
                for rho in rhos:
                    later(rho, blk)
                return carry
            if nblk > 1:
                lax.fori_loop(1, nblk, body, 0, unroll=max(1, INFLIGHT // len(rhos)))

    def combine(c, carry):
        rows = pl.ds(pl.multiple_of(c * BLOCK, BLOCK), BLOCK)
        ls = [l_scr[g, rows, :] for g in range(N_GROUPS)]
        m = jnp.maximum(jnp.maximum(ls[0], ls[1]), ls[2])
        es = [jnp.exp(l - m) for l in ls]
        num = es[0] * o_scr[0, rows, :] + es[1] * o_scr[1, rows, :] + es[2] * o_scr[2, rows, :]
        out_ref[0, rows, :] = (num / (es[0] + es[1] + es[2])).astype(out_ref.dtype)
        return carry
    lax.fori_loop(0, seq // BLOCK, combine, 0)


def _attn_prompt(qkv_groups, n_seq, seq):
    in_specs, args = [], []
    for role in range(3):
        for g, (_, r) in enumerate(DIL_GROUPS):
            in_specs.append(pl.BlockSpec((1, 1, seq // r, r * HEAD_DIM),
                                         lambda b, h, role=role: (role * N_HEADS + h, b, 0, 0)))
            args.append(qkv_groups[g])
    return pl.pallas_call(
        functools.partial(_attn_prompt_kernel, seq=seq),
        grid=(n_seq, N_HEADS),
        in_specs=in_specs,
        out_specs=pl.BlockSpec((1, seq, HEAD_DIM), lambda b, h: (b, 0, h)),
        out_shape=jax.ShapeDtypeStruct((n_seq, seq, ATT_WIDTH), BF16),
        scratch_shapes=[pltpu.VMEM((N_GROUPS, seq, HEAD_DIM), F32),
                        pltpu.VMEM((N_GROUPS, seq, HEAD_DIM), F32)],
        compiler_params=_params("arbitrary", "arbitrary"),
        name="attn_prompt",
    )(*args)


def _attn_sample_kernel(q_ref, new_ref, old_ref, o_ref, lse_ref, *scratch, r, t_new, lb, compact):
    q_pad = q_ref[0]
    qf = q_pad.astype(F32)
    if compact:
        (flat,) = scratch
        n_key = lb // r
        for u in range(t_new):
            flat[u] = new_ref[:, u * KV_ROWS:(u + 1) * KV_ROWS, :].reshape(n_key * KV_ROWS, HEAD_DIM)
    else:
        n_key = lb
    i_key = lax.broadcasted_iota(jnp.int32, (n_key, 1), 0)
    j_old = lax.broadcasted_iota(jnp.int32, (t_new, 1), 0)
    for t in range(t_new):
        if compact:
            idx = i_key * r + (r - t_new + t)
            rows = lambda kvh, t=t: flat[t, pl.ds(kvh, n_key, stride=KV_ROWS), :]
        else:
            idx = i_key
            rows = lambda kvh: new_ref[pl.ds(kvh, n_key, stride=KV_ROWS), :]
        dist = lb - t_new + t - idx
        ok = (dist >= 0) & (dist % r == 0) & (dist <= r * DIL_SPAN)
        dist_old = lb + t - j_old
        ok_old = (dist_old % r == 0) & (dist_old <= r * DIL_SPAN)
        for hd in range(N_HEADS):
            sl = slice(hd * HEAD_DIM, (hd + 1) * HEAD_DIM)
            q = qf[t:t + 1, sl]
            k, v = rows(hd), rows(N_HEADS + hd)
            k_old = old_ref[pl.ds(hd, t_new, stride=KV_ROWS), :]
            v_old = old_ref[pl.ds(N_HEADS + hd, t_new, stride=KV_ROWS), :]
            s_all = _dot_nt(k.astype(BF16), q_pad[:, sl])
            s = jnp.where(ok, s_all[:, t:t + 1] * SCALE, NEG)
            s_old = jnp.where(ok_old, jnp.sum(k_old * q, axis=-1, keepdims=True) * SCALE, NEG)
            m = jnp.maximum(jnp.max(s, axis=0, keepdims=True), jnp.max(s_old, axis=0, keepdims=True))
            p, p_old = jnp.exp(s - m), jnp.exp(s_old - m)
            l = jnp.sum(p, axis=0, keepdims=True) + jnp.sum(p_old, axis=0, keepdims=True)
            o = jnp.sum(p * v, axis=0, keepdims=True) + jnp.sum(p_old * v_old, axis=0, keepdims=True)
            o_ref[0, t:t + 1, sl] = o / l
            lse_ref[0, t:t + 1, sl] = jnp.broadcast_to(m + jnp.log(l), (1, HEAD_DIM))


def _attn_sample(q, new_buf, old_buf, r, t_new):
    nb, tq, _ = q.shape
    lb = new_buf.shape[1] // KV_ROWS
    compact = r % t_new == 0
    if compact:
        new_view = new_buf.reshape(nb, lb // r, r // t_new, t_new * KV_ROWS, HEAD_DIM)
        new_spec = pl.BlockSpec((None, lb // r, None, t_new * KV_ROWS, HEAD_DIM),
                                lambda b: (b, 0, r // t_new - 1, 0, 0))
        scratch = [pltpu.VMEM((t_new, lb // r * KV_ROWS, HEAD_DIM), F32)]
    else:
        new_view = new_buf
        new_spec = pl.BlockSpec((None, lb * KV_ROWS, HEAD_DIM), lambda b: (b, 0, 0))
        scratch = []
    return pl.pallas_call(
        functools.partial(_attn_sample_kernel, r=r, t_new=t_new, lb=lb, compact=compact),
        grid=(nb,),
        in_specs=[pl.BlockSpec((1, tq, ATT_WIDTH), lambda b: (b, 0, 0)),
                  new_spec,
                  pl.BlockSpec((None, t_new * KV_ROWS, HEAD_DIM), lambda b: (b, 0, 0))],
        out_specs=[pl.BlockSpec((1, t_new, ATT_WIDTH), lambda b: (b, 0, 0)),
                   pl.BlockSpec((1, t_new, ATT_WIDTH), lambda b: (b, 0, 0))],
        out_shape=[jax.ShapeDtypeStruct((nb, t_new, ATT_WIDTH), F32),
                   jax.ShapeDtypeStruct((nb, t_new, ATT_WIDTH), F32)],
        scratch_shapes=scratch,
        compiler_params=_params("arbitrary"),
        name=f"attn_sample_r{r}",
    )(q, new_view, old_buf)


def _combine_kernel(o0, o1, o2, l0, l1, l2, out_ref):
    ls = [l0[...], l1[...], l2[...]]
    m = jnp.maximum(jnp.maximum(ls[0], ls[1]), ls[2])
    es = [jnp.exp(l - m) for l in ls]
    num = es[0] * o0[...] + es[1] * o1[...] + es[2] * o2[...]
    out_ref[...] = (num / (es[0] + es[1] + es[2])).astype(out_ref.dtype)


def _combine_groups(outs, lses):
    shape = outs[0].shape
    return pl.pallas_call(
        _combine_kernel,
        in_specs=[_full(shape)] * 6,
        out_specs=_full(shape),
        out_shape=jax.ShapeDtypeStruct(shape, BF16),
        grid=(1,),
        compiler_params=_params("arbitrary"),
        name="combine_groups",
    )(*outs, *lses)


FF_CHUNK = 256
CARRY_ROWS = 8
SHIFT_SLOTS = 4


MOVE_ROWS = 8192


def _move_regions(keeps):
    regions = [(g, start, min(MOVE_ROWS, keep - start)) for g, keep in enumerate(keeps)
               for start in range(0, keep, MOVE_ROWS)]
    halves = ([], [])
    for reg in sorted(regions, key=lambda reg: -reg[2]):
        min(halves, key=lambda h: sum(r[2] for r in h)).append(reg)
    return halves


def _buffer_move(old_refs, kvn_refs, new_refs, bufs, tails, sems, b, n_steps):
    n_new = [k.shape[1] for k in kvn_refs]
    halves = _move_regions([o.shape[1] - n for o, n in zip(old_refs, n_new)])

    def copies(batch):
        out, sem = [], 0
        for buf, half in zip(bufs, halves):
            row, cps = 0, []
            for g, start, size in half:
                stage = buf.at[pl.ds(row, size)]
                cps.append((pltpu.make_async_copy(old_refs[g].at[batch, pl.ds(n_new[g] + start, size)], stage,
                                                  sems.at[sem]),
                            pltpu.make_async_copy(stage, new_refs[g].at[batch, pl.ds(start, size)],
                                                  sems.at[sem + 1])))
                row += size
                sem += 2
            out.append(cps)
        tail = []
        for g in range(N_GROUPS):
            keep = old_refs[g].shape[1] - n_new[g]
            tail.append((pltpu.make_async_copy(kvn_refs[g].at[batch], tails.at[g], sems.at[sem]),
                         pltpu.make_async_copy(tails.at[g], new_refs[g].at[batch, pl.ds(keep, n_new[g])],
                                               sems.at[sem + 1])))
            sem += 2
        return out[0], out[1], tail
    first, second, tail = copies(b)
    _, prev_second, _ = copies(jnp.maximum(b - 1, 0))

    def top():
        for cp_in, _ in first + tail:
            cp_in.start()

    def mid():
        for cp_in, cp_out in first + tail:
            cp_in.wait()
            cp_out.start()

        @pl.when(b > 0)
        def _():
            for _, cp_out in prev_second:
                cp_out.wait()
        for cp_in, _ in second:
            cp_in.start()

    def end():
        for cp_in, cp_out in second:
            cp_in.wait()
            cp_out.start()
        for _, cp_out in first + tail:
            cp_out.wait()

        @pl.when(b == n_steps - 1)
        def _():
            for _, cp_out in second:
                cp_out.wait()
    return top, mid, end


def _move_scratch(old, new_rows):
    halves = _move_regions([o.shape[1] - n.shape[1] for o, n in zip(old, new_rows)])
    n_copies = sum(len(h) for h in halves) + N_GROUPS
    return ([pltpu.VMEM((sum(r[2] for r in h), HEAD_DIM), F32) for h in halves]
            + [pltpu.VMEM((N_GROUPS,) + new_rows[0].shape[1:], F32), pltpu.SemaphoreType.DMA((2 * n_copies,))])


def _out_ffn_kernel(*refs, seq_len, tiles_per_seq, tail_rows, move):
    move_top = move_mid = move_end = lambda: None
    if move:
        (x_ref, mix_ref, cr_ref, wo_ref, g_ref, wup_ref, cw_ref, cb_ref, wdn_ref, o0, o1, o2, k0, k1, k2,
         out_ref, tail_ref, n0, n1, n2, shift, act_scr, carry, buf_a, buf_b, tails, sems) = refs
        move_top, move_mid, move_end = _buffer_move(
            (o0, o1, o2), (k0, k1, k2), (n0, n1, n2), (buf_a, buf_b), tails, sems,
            pl.program_id(0), pl.num_programs(0))
    elif seq_len is None:
        (x_ref, mix_ref, cr_ref, wo_ref, g_ref, wup_ref, cw_ref, cb_ref, wdn_ref,
         out_ref, tail_ref, shift, act_scr, carry) = refs
    else:
        (x_ref, mix_ref, cr_ref, wo_ref, g_ref, wup_ref, cw_ref, cb_ref, wdn_ref, e1_ref, e2_ref,
         out_ref, tail_ref, shift, act_scr) = refs
    tm = x_ref.shape[0]
    d_ff = wdn_ref.shape[0]
    move_top()
    x1 = x_ref[...] + _dot(jnp.concatenate([mix_ref[...], cr_ref[...]], axis=-1), wo_ref[...])
    h = _rms(x1, g_ref[...]).astype(BF16)
    if seq_len is None:
        @pl.when(pl.program_id(0) % tiles_per_seq == 0)
        def _():
            carry[...] = jnp.zeros_like(carry)
    else:
        t = lax.broadcasted_iota(jnp.int32, (tm, 1), 0) % seq_len
        has1 = t >= 1
        has2 = t >= 2

    def chunk_cols(j):
        return (slice(j * FF_CHUNK, (j + 1) * FF_CHUNK),
                slice(d_ff + j * FF_CHUNK, d_ff + (j + 1) * FF_CHUNK))

    def up_proj(j):
        return tuple(_dot(h, wup_ref[:, cols]) for cols in chunk_cols(j))

    def conv(up, cols, slot):
        buf = shift.at[slot]
        if seq_len is None:
            buf[:CARRY_ROWS, :] = carry[:, cols]
            carry[:, cols] = up[tm - CARRY_ROWS:, :]
        else:
            buf[:CARRY_ROWS, :] = jnp.zeros((CARRY_ROWS, FF_CHUNK), F32)
        buf[CARRY_ROWS:, :] = up
        tail_ref[0, :, cols] = up[tm - tail_rows:, :]
        prev1 = buf[CARRY_ROWS - 1:CARRY_ROWS - 1 + tm, :]
        prev2 = buf[CARRY_ROWS - 2:CARRY_ROWS - 2 + tm, :]
        if seq_len is not None:
            prev1 = jnp.where(has1, prev1, e1_ref[:, cols])
            prev2 = jnp.where(has2, prev2, e2_ref[:, cols])
        return (cb_ref[:, cols] + cw_ref[0:1, cols] * prev2 + cw_ref[1:2, cols] * prev1
                + cw_ref[2:3, cols] * up)

    n_chunks = d_ff // FF_CHUNK
    ups = up_proj(0)
    for j in range(n_chunks):
        nxt = up_proj(j + 1) if j + 1 < n_chunks else None
        a, b = (conv(up, cols, 2 * (j % 2) + s) for s, (up, cols) in enumerate(zip(ups, chunk_cols(j))))
        act_scr[:, j * FF_CHUNK:(j + 1) * FF_CHUNK] = (a * jax.nn.sigmoid(a) * b).astype(BF16)
        ups = nxt
        if j == n_chunks // 2:
            move_mid()
    out_ref[...] = x1 + _dot(act_scr[...], wdn_ref[...])
    move_end()


def _out_ffn(x2d, mix, cross, w_out_bf, g_ffn, w_up_bf, conv_w, conv_b, w_down_bf, tm,
             tiles_per_seq=None, seq_len=None, e1=None, e2=None, tail_rows=CARRY_ROWS, moves=None):
    rows, d = x2d.shape
    d_ff = w_down_bf.shape[0]
    n_tiles = rows // tm
    in_specs = [
        pl.BlockSpec((tm, d), lambda i: (i, 0)),
        pl.BlockSpec((tm, ATT_WIDTH), lambda i: (i, 0)),
        pl.BlockSpec((tm, ATT_WIDTH), lambda i: (i, 0)),
        _full(w_out_bf.shape), _full((1, d)), _full(w_up_bf.shape),
        _full((CONV_W, 2 * d_ff)), _full((1, 2 * d_ff)), _full(w_down_bf.shape),
    ]
    args = [x2d, mix, cross, w_out_bf, g_ffn.reshape(1, d), w_up_bf, conv_w, conv_b.reshape(1, 2 * d_ff),
            w_down_bf]
    scratch = [pltpu.VMEM((SHIFT_SLOTS, CARRY_ROWS + tm, FF_CHUNK), F32), pltpu.VMEM((tm, d_ff), BF16)]
    if seq_len is None:
        scratch += [pltpu.VMEM((CARRY_ROWS, 2 * d_ff), F32)]
    else:
        in_specs += [pl.BlockSpec((tm, 2 * d_ff), lambda i: (i, 0))] * 2
        args += [e1, e2]
    out_specs = [pl.BlockSpec((tm, d), lambda i: (i, 0)),
                 pl.BlockSpec((1, tail_rows, 2 * d_ff), lambda i: (i, 0, 0))]
    out_shape = [jax.ShapeDtypeStruct((rows, d), F32),
                 jax.ShapeDtypeStruct((n_tiles, tail_rows, 2 * d_ff), F32)]
    if moves is not None:
        old, new_rows = moves
        assert seq_len is None and all(o.shape[0] == n_tiles for o in old)
        any_spec = pl.BlockSpec(memory_space=pl.ANY)
        in_specs += [any_spec] * (2 * N_GROUPS)
        args += [*old, *new_rows]
        out_specs += [any_spec] * N_GROUPS
        out_shape += [jax.ShapeDtypeStruct(o.shape, o.dtype) for o in old]
        scratch += _move_scratch(old, new_rows)
    return pl.pallas_call(
        functools.partial(_out_ffn_kernel, seq_len=seq_len, tiles_per_seq=tiles_per_seq,
                          tail_rows=tail_rows, move=moves is not None),
        grid=(n_tiles,),
        in_specs=in_specs,
        out_specs=out_specs,
        out_shape=out_shape,
        scratch_shapes=scratch,
        compiler_params=_params("arbitrary"),
        name="out_ffn",
    )(*args)


def _swap_halves(x):
    return pltpu.roll(x, S5_STATE, 1)


def _s5_prep_kernel(lam_ref, logdt_ref, bt_ref, c_ref, d_ref, t_ref, cpt_ref, *var_refs, n_valids):
    gc = lam_ref.shape[0]
    n_groups = gc // S5_GROUP
    lane = lax.broadcasted_iota(jnp.int32, lam_ref.shape, 1)
    first = lane < S5_STATE
    sign = jnp.where(first, -1.0, 1.0)
    a = lam_ref[...]
    a_sw = _swap_halves(a)
    are = jnp.where(first, a, a_sw)
    aim = jnp.where(first, a_sw, a)
    dt = jnp.exp(logdt_ref[...])
    mag = jnp.exp(are * dt)
    lr = mag * jnp.cos(aim * dt)
    li = mag * jnp.sin(aim * dt)
    den = are * are + aim * aim
    xr = lr - 1.0
    f_re = (xr * are + li * aim) / den
    f_im = (li * are - xr * aim) / den
    lb = sign * li

    def cmul(x, m_re, m_sw):
        return x * m_re + _swap_halves(x) * m_sw

    c = c_ref[...]
    c_neg = c * -sign
    ri = lax.broadcasted_iota(jnp.int32, (gc, gc), 0)
    ci = lax.broadcasted_iota(jnp.int32, (gc, gc), 1)
    same_group = (ri // S5_GROUP) == (ci // S5_GROUP)
    e = cmul(bt_ref[...], f_re, sign * f_im)
    cl = c
    pw = jnp.where(first, 1.0, 0.0)
    zr = lax.broadcasted_iota(jnp.int32, (gc, S5_BLOCK), 0)
    zc = lax.broadcasted_iota(jnp.int32, (gc, S5_BLOCK), 1)
    same_out = (zr % S5_GROUP) == (zc % S5_GROUP)
    lags = jnp.zeros((gc, S5_BLOCK), F32)
    for k in range(S5_SUB):
        kmat = jnp.where(same_group, _dot_nt(e, c_neg, precision=HIGHEST), 0.0)
        if k == 0:
            kmat = kmat + jnp.where(ri == ci, d_ref[...], 0.0)
        place = jnp.where(same_out & (zc // S5_GROUP == k), 1.0, 0.0)
        lags = lags + jnp.dot(kmat, place, precision=HIGHEST, preferred_element_type=F32)
        both = jnp.concatenate([e, _swap_halves(e)], axis=-1).reshape(n_groups, S5_GROUP, 4 * S5_STATE)
        for n_valid, bp_ref in zip(n_valids, var_refs[0::2]):
            if n_valid - 1 - k >= 0:
                bp_ref[:, n_valid - 1 - k] = both.astype(bp_ref.dtype)
        cl = cmul(cl, lr, lb)
        cpt_ref[:, k] = (cl * -sign).reshape(n_groups, S5_GROUP, 2 * S5_STATE).astype(cpt_ref.dtype)
        e = cmul(e, lr, lb)
        pw = cmul(pw, lr, lb)
        for n_valid, lam_out_ref in zip(n_valids, var_refs[1::2]):
            if k + 1 == n_valid:
                lam_out_ref[...] = pw
    for n_valid, bp_ref in zip(n_valids, var_refs[0::2]):
        for tau in range(n_valid, S5_SUB):
            bp_ref[:, tau] = jnp.zeros((n_groups, S5_GROUP, 4 * S5_STATE), bp_ref.dtype)
    for taup in range(S5_SUB):
        moved = lags if taup == 0 else jnp.where(zc >= taup * S5_GROUP, pltpu.roll(lags, taup * S5_GROUP, 1), 0.0)
        t_ref[:, taup] = moved.reshape(n_groups, S5_GROUP, S5_BLOCK).astype(t_ref.dtype)


def _s5_prep(lam_re, lam_im, log_dt, b_re, b_im, c_re, c_im, d_skip, n_valids):
    n_groups = lam_re.shape[0]
    gc = n_groups * S5_GROUP
    rep = lambda t: jnp.repeat(t, S5_GROUP, axis=0)
    lam_p = rep(jnp.concatenate([lam_re, lam_im], axis=-1))
    logdt = rep(log_dt.reshape(n_groups, 1))
    bt_p = jnp.concatenate([b_re.transpose(0, 2, 1), b_im.transpose(0, 2, 1)], axis=-1).reshape(gc, 2 * S5_STATE)
    c_p = jnp.concatenate([c_re, c_im], axis=-1).reshape(gc, 2 * S5_STATE)
    t_shape = (n_groups, S5_SUB, S5_GROUP, S5_BLOCK)
    cpt_shape = (n_groups, S5_SUB, S5_GROUP, 2 * S5_STATE)
    bp_shape = (n_groups, S5_SUB, S5_GROUP, 4 * S5_STATE)
    lam_shape = (gc, 2 * S5_STATE)
    var = [(bp_shape, BF16), (lam_shape, F32)] * len(n_valids)
    t, cpt, *rest = pl.pallas_call(
        functools.partial(_s5_prep_kernel, n_valids=tuple(n_valids)),
        grid=(1,),
        in_specs=[_full(lam_p.shape), _full(logdt.shape), _full(bt_p.shape), _full(c_p.shape), _full((gc, 1))],
        out_specs=[_full(t_shape), _full(cpt_shape), *[_full(s) for s, _ in var]],
        out_shape=[jax.ShapeDtypeStruct(t_shape, BF16), jax.ShapeDtypeStruct(cpt_shape, BF16),
                   *[jax.ShapeDtypeStruct(s, dt) for s, dt in var]],
        compiler_params=_params("arbitrary"),
        name="s5_prep",
    )(lam_p, logdt, bt_p, c_p, d_skip.reshape(gc, 1))
    merge = lambda a: a.reshape(n_groups, S5_BLOCK, a.shape[-1])
    return [(merge(t), merge(bp), merge(cpt), lam_n[::S5_GROUP]) for bp, lam_n in zip(rest[0::2], rest[1::2])]


SCAN_GROUPS = 8
SUBLANES = 8
GLU_ROWS = 512


def _s5_core_kernel(v_ref, x0_ref, x0s_ref, t_ref, bp_ref, cpt_ref, lam_ref, wglu_ref, bglu_ref,
                    mix_ref, xfin_ref, s_scr, xprev_scr, m_scr, stage, *, n_valid, n_sub):
    rows = v_ref.shape[0]
    n_groups = t_ref.shape[0]
    lane = lax.broadcasted_iota(jnp.int32, (1, 2 * S5_STATE), 1)
    first = lane < S5_STATE
    blk = lambda g: slice(g * S5_BLOCK, (g + 1) * S5_BLOCK)
    tile = lambda i: slice(i * 2 * S5_STATE, (i + 1) * 2 * S5_STATE)
    for g in range(n_groups):
        s_scr[:, blk(g)] = _dot(v_ref[:, blk(g)], bp_ref[g])
        lam = lam_ref[g:g + 1, :]
        lam_sw = _swap_halves(lam)
        m_scr[0:1, tile(g)] = jnp.where(first, lam, lam_sw)
        m_scr[1:2, tile(g)] = jnp.where(first, -lam_sw, lam)
    if n_sub == 1:
        for g in range(n_groups):
            x, xs = x0_ref[:, tile(g)], x0s_ref[:, tile(g)]
            xprev_scr[:, tile(g)] = x
            xfin_ref[:, tile(g)] = (x * m_scr[0:1, tile(g)] + xs * m_scr[1:2, tile(g)]
                                    + s_scr[:, tile(2 * g)])
    else:
        assert rows == n_sub
        for g0 in range(0, n_groups, SCAN_GROUPS):
            gs = range(g0, min(n_groups, g0 + SCAN_GROUPS))

            def body(i, carry, gs=gs):
                rows8 = pl.ds(pl.multiple_of(i * SUBLANES, SUBLANES), SUBLANES)
                sub_i = lax.broadcasted_iota(jnp.int32, (SUBLANES, 2 * S5_STATE), 0)
                out = []
                for g, (x, xs) in zip(gs, carry):
                    s8, ssw8 = s_scr[rows8, tile(2 * g)], s_scr[rows8, tile(2 * g + 1)]
                    prev8 = jnp.zeros((SUBLANES, 2 * S5_STATE), F32)
                    for r in range(SUBLANES):
                        prev8 = jnp.where(sub_i == r, x, prev8)
                        m_re, m_sw = m_scr[0:1, tile(g)], m_scr[1:2, tile(g)]
                        x, xs = (x * m_re + xs * m_sw + s8[r:r + 1, :],
                                 xs * m_re - x * m_sw + ssw8[r:r + 1, :])
                    xprev_scr[rows8, tile(g)] = prev8
                    out.append((x, xs))
                return tuple(out)
            init = tuple((x0_ref[0:1, tile(g)], x0s_ref[0:1, tile(g)]) for g in gs)
            fin = lax.fori_loop(0, n_sub // SUBLANES, body, init)
            for g, (x, _) in zip(gs, fin):
                xfin_ref[:, tile(g)] = jnp.broadcast_to(x, (xfin_ref.shape[0], 2 * S5_STATE))
    for g in range(n_groups):
        s_scr[:, blk(g)] = (_dot(v_ref[:, blk(g)], t_ref[g])
                            + _dot_nt(xprev_scr[:, tile(g)].astype(BF16), cpt_ref[g]))
    for cb in range(ATT_WIDTH // HEAD_DIM):
        for half in range(-(-n_valid // SLOTS)):
            toks = _slot_transpose([s_scr[:, tile(2 * (cb * SLOTS + p) + half)] for p in range(SLOTS)])
            for s in range(min(SLOTS, n_valid - half * SLOTS)):
                stage[cb, pl.ds(half * SLOTS + s, rows, stride=n_valid), :] = toks[s]
    n_tok = rows * n_valid
    chunk = min(n_tok, GLU_ROWS)

    def glu(i, carry):
        r = pl.ds(pl.multiple_of(i * chunk, chunk), chunk)
        y = jax.nn.gelu(jnp.concatenate([stage[cb, r, :] for cb in range(ATT_WIDTH // HEAD_DIM)], axis=-1))
        z = _dot(y.astype(BF16), wglu_ref[...]) + bglu_ref[...]
        mix_ref[r, :] = (y * jax.nn.sigmoid(z)).astype(mix_ref.dtype)
        return carry
    lax.fori_loop(0, n_tok // chunk, glu, 0)


def _s5_mixer(v, rows_per_tile, n_valid, n_sub, x0, x0s, t, bp, cpt, lam_n, w_glu_bf, b_glu):
    m, width = v.shape
    n_tiles = m // rows_per_tile
    r0 = x0.shape[0] // n_tiles
    n_state = x0.shape[1]
    n_tok = rows_per_tile * n_valid
    return pl.pallas_call(
        functools.partial(_s5_core_kernel, n_valid=n_valid, n_sub=n_sub),
        grid=(n_tiles,),
        in_specs=[pl.BlockSpec((rows_per_tile, width), lambda i: (i, 0)),
                  pl.BlockSpec((r0, n_state), lambda i: (i, 0)),
                  pl.BlockSpec((r0, n_state), lambda i: (i, 0)),
                  _full(t.shape), _full(bp.shape), _full(cpt.shape), _full(lam_n.shape),
                  _full(w_glu_bf.shape), _full((1, ATT_WIDTH))],
        out_specs=[pl.BlockSpec((n_tok, ATT_WIDTH), lambda i: (i, 0)),
                   pl.BlockSpec((r0, n_state), lambda i: (i, 0))],
        out_shape=[jax.ShapeDtypeStruct((m * n_valid, ATT_WIDTH), BF16),
                   jax.ShapeDtypeStruct(x0.shape, F32)],
        scratch_shapes=[pltpu.VMEM((rows_per_tile, width), F32),
                        pltpu.VMEM((rows_per_tile, n_state), F32),
                        pltpu.VMEM((8, n_state), F32),
                        pltpu.VMEM((ATT_WIDTH // HEAD_DIM, n_tok, HEAD_DIM), F32)],
        compiler_params=_params("arbitrary"),
        name="s5_core",
    )(v, x0, x0s, t, bp, cpt, lam_n, w_glu_bf, b_glu.reshape(1, ATT_WIDTH))


def _pack_state(s):
    n = s.shape[0]
    packed = s.transpose(0, 2, 1, 3).reshape(n, -1)
    swapped = jnp.stack([s[:, 1], s[:, 0]], axis=1).transpose(0, 2, 1, 3).reshape(n, -1)
    return packed, swapped


def _unpack_state(x):
    n = x.shape[0]
    return x.reshape(n, -1, 2, S5_STATE).transpose(0, 2, 1, 3)


PROMPT_TILE = 512
Q_PAD = 16


def _pad_rows(t, n):
    return jnp.pad(t, ((0, 0), (0, n - t.shape[1]), (0, 0)))


def kernel(x_prompt, x_sample, cache_win0_kv, cache_win1_kv, cache_win2_kv, cache_mem_kv, state_s5,
           state_ffn_conv, mem_prompt, g_mix, g_ffn, w_in_a, g_q_dil, g_k_dil, w_in_b, s5_lam_re,
           s5_lam_im, s5_log_dt, s5_b_re, s5_b_im, s5_c_re, s5_c_im, s5_d, w_glu, b_glu, g_mem,
           w_mem_kv, g_q_cross, g_k_cross, w_out, w_up, conv_w, conv_b, w_down):
    nb, seq, d = x_prompt.shape
    db, ts, _ = x_sample.shape
    depth = g_mix.shape[0]
    n_mem = mem_prompt.shape[1]
    d_ff2 = w_up.shape[2]
    assert ts >= CONV_W - 1 and seq % PROMPT_TILE == 0 and ts <= Q_PAD
    caches = (cache_win0_kv, cache_win1_kv, cache_win2_kv)
    w_mem_bf = w_mem_kv.astype(BF16)
    mem_cache = cache_mem_kv.reshape(depth * db, n_mem * KV_ROWS, HEAD_DIM)

    tab_p = _rope_tables(jnp.arange(seq, dtype=jnp.int32))
    tab_s = tuple(jnp.tile(t, (db, 1)) for t in _rope_tables(PAST_LEN + jnp.arange(ts, dtype=jnp.int32)))
    win_keep = tuple(min(w, seq) for w, _ in DIL_GROUPS)
    rows_s = db * ts
    kv_tail = (2, N_HEADS, HEAD_DIM)

    mem_p = _mem_kv(mem_prompt.reshape(nb * n_mem, d), g_mem, w_mem_bf, g_k_cross)

    xp = x_prompt.reshape(nb * seq, d)
    xs = x_sample.reshape(rows_s, d)
    p_win, s_win = [[] for _ in DIL_GROUPS], [[] for _ in DIL_GROUPS]
    p_s5, s_s5, p_conv, s_conv = [], [], [], []
    tiles_per_seq = seq // PROMPT_TILE
    for i in range(depth):
        mem_i = mem_p[i].reshape(nb, n_mem * KV_ROWS, HEAD_DIM)
        if i % 2 == 0:
            ia = i // 2
            w_in_a_bf = w_in_a[ia].astype(BF16)
            *qkv_groups, w0, w1, w2, cross_p = _in_proj_a(
                xp, seq, PROMPT_TILE, g_mix[i], w_in_a_bf, tab_p, g_q_dil[ia], g_k_dil[ia],
                g_q_cross[i], mem_i, win_keep, dilated=True)
            for g, w in enumerate((w0, w1, w2)):
                p_win[g].append(w.reshape((nb, win_keep[g]) + kv_tail))
            mix_p = _attn_prompt(qkv_groups, nb, seq).reshape(nb * seq, ATT_WIDTH)
            qkv_s, *kv_new, qc_s = _in_proj_a(
                xs, rows_s, rows_s, g_mix[i], w_in_a_bf, tab_s, g_q_dil[ia], g_k_dil[ia],
                None, None, (rows_s,) * N_GROUPS, dilated=False)
            old_bufs = [c[ia].reshape(db, -1, HEAD_DIM) for c in caches]
            moves = (old_bufs, [k.reshape(db, ts * KV_ROWS, HEAD_DIM) for k in kv_new])
        else:
            moves = None
            ib = i // 2
            prm = (s5_lam_re[ib], s5_lam_im[ib], s5_log_dt[ib], s5_b_re[ib], s5_b_im[ib], s5_c_re[ib],
                   s5_c_im[ib], s5_d[ib])
            n_state = s5_lam_re.shape[1] * 2 * S5_STATE
            w_in_b_bf, w_glu_bf = w_in_b[ib].astype(BF16), w_glu[ib].astype(BF16)
            u_p, cross_p = _in_proj_b(xp, seq, PROMPT_TILE, S5_SUB, g_mix[i], w_in_b_bf, g_q_cross[i], mem_i)
            zero = jnp.zeros((nb * CARRY_ROWS, n_state), F32)
            prep_p, prep_s = _s5_prep(*prm, (S5_SUB, ts))
            mix_p, fin_p = _s5_mixer(u_p, seq // S5_SUB, S5_SUB, seq // S5_SUB, zero, zero,
                                     *prep_p, w_glu_bf, b_glu[ib])
            p_s5.append(_unpack_state(fin_p.reshape(nb, CARRY_ROWS, n_state)[:, 0]))
            u_s, qc_s = _in_proj_b(xs, rows_s, rows_s, ts, g_mix[i], w_in_b_bf, None, None)
            mix_s, fin_s = _s5_mixer(u_s, db, ts, 1, *_pack_state(state_s5[ib]), *prep_s, w_glu_bf, b_glu[ib])
            s_s5.append(_unpack_state(fin_s))
        cross_s = _cross_sample(_pad_rows(qc_s.reshape(db, ts, ATT_WIDTH), Q_PAD), g_q_cross[i], mem_cache,
                                first=i * db)
        cross_s = cross_s[:, :ts].reshape(rows_s, ATT_WIDTH)

        ffn_w = (w_out[i].astype(BF16), g_ffn[i], w_up[i].astype(BF16), conv_w[i], conv_b[i],
                 w_down[i].astype(BF16))
        xp, tails, *new_bufs = _out_ffn(xp, mix_p, cross_p, *ffn_w, PROMPT_TILE, tiles_per_seq=tiles_per_seq,
                                        moves=moves)
        p_conv.append(tails.reshape(nb, tiles_per_seq, CARRY_ROWS, d_ff2)[:, -1, CARRY_ROWS - (CONV_W - 1):])
        if moves is not None:
            outs, lses = [], []
            for g, (_, r) in enumerate(DIL_GROUPS):
                q_g = qkv_s[g * N_HEADS:(g + 1) * N_HEADS]
                q_g = _pad_rows(q_g.transpose(1, 0, 2).reshape(db, ts, ATT_WIDTH), Q_PAD)
                o_g, lse_g = _attn_sample(q_g, new_bufs[g], moves[0][g], r, ts)
                outs.append(o_g.reshape(rows_s, ATT_WIDTH))
                lses.append(lse_g.reshape(rows_s, ATT_WIDTH))
                s_win[g].append(new_bufs[g].reshape(caches[g][i // 2].shape))
            mix_s = _combine_groups(outs, lses)
        buf = state_ffn_conv[i]
        zero = jnp.zeros((db, ts - 2, d_ff2), F32)
        e1 = jnp.concatenate([buf[:, 1:2], zero, zero[:, :1]], axis=1).reshape(rows_s, d_ff2)
        e2 = jnp.concatenate([buf, zero], axis=1).reshape(rows_s, d_ff2)
        xs, tails = _out_ffn(xs, mix_s, cross_s, *ffn_w, rows_s, seq_len=ts, e1=e1, e2=e2, tail_rows=rows_s)
        s_conv.append(tails.reshape(db, ts, d_ff2)[:, ts - (CONV_W - 1):])

    return (xp.reshape(nb, seq, d), xs.reshape(db, ts, d),
            jnp.stack(p_win[0]), jnp.stack(p_win[1]), jnp.stack(p_win[2]),
            mem_p.reshape((depth, nb, n_mem) + kv_tail),
            jnp.stack(p_s5), jnp.stack(p_conv),
            jnp.stack(s_win[0]), jnp.stack(s_win[1]), jnp.stack(s_win[2]),
            jnp.stack(s_s5), jnp.stack(s_conv))
```

```python
import functools
import math

import jax
import jax.numpy as jnp
from jax import lax
from jax.experimental import pallas as pl
from jax.experimental.pallas import tpu as pltpu

HEAD_DIM = 128
N_HEADS = 4
DIL_GROUPS = ((128, 1), (512, 4), (2048, 16))
N_GROUPS = len(DIL_GROUPS)
DIL_SPAN = 128
BLOCK = 128
ATT_WIDTH = N_HEADS * HEAD_DIM
KV_ROWS = 2 * N_HEADS
ROT_DIM = HEAD_DIM // 4
ROT_HALF = ROT_DIM // 2
ROPE_THETA = 500000.0
S5_GROUP = 16
S5_STATE = 64
S5_SUB = 16
S5_BLOCK = S5_SUB * S5_GROUP
CONV_W = 3
EPS = 1e-6
NEG = -1e30
SCALE = HEAD_DIM ** -0.5
PAST_LEN = 16384

VMEM_LIMIT_V7X = 56 * 1024 * 1024
BF16 = jnp.bfloat16
F32 = jnp.float32
HIGHEST = lax.Precision.HIGHEST


def _params(*sem):
    return pltpu.CompilerParams(dimension_semantics=sem, vmem_limit_bytes=VMEM_LIMIT_V7X)


def _rms(x, g):
    return x * lax.rsqrt(jnp.mean(x * x, axis=-1, keepdims=True) + EPS) * g


def _rms_head(x, g):
    ones = jnp.ones((HEAD_DIM, HEAD_DIM), BF16)
    ssq = _dot((x * x).astype(BF16), ones)
    return x * lax.rsqrt(ssq * (1.0 / HEAD_DIM) + EPS) * g


def _dot(a, b):
    return jnp.dot(a, b, preferred_element_type=F32)


def _dot_nt(a, b, precision=None):
    return lax.dot_general(a, b, (((1,), (1,)), ((), ())), precision=precision,
                           preferred_element_type=F32)


def _full(shape):
    nd = len(shape)
    return pl.BlockSpec(shape, lambda *_: (0,) * nd)


def _kv_rows(kv, head, n):
    return pl.ds(kv * N_HEADS + head, n, stride=KV_ROWS)


def _mem_kv_kernel(mem_ref, g_ref, w_ref, gk_ref, out_ref):
    tm = mem_ref.shape[0]
    h = _rms(mem_ref[...], g_ref[0]).astype(BF16)
    kv = _dot(h, w_ref[0])
    for hd in range(N_HEADS):
        sl = slice(hd * HEAD_DIM, (hd + 1) * HEAD_DIM)
        out_ref[0, _kv_rows(0, hd, tm), :] = _rms(kv[:, sl], gk_ref[0])
        out_ref[0, _kv_rows(1, hd, tm), :] = kv[:, ATT_WIDTH + hd * HEAD_DIM:ATT_WIDTH + (hd + 1) * HEAD_DIM]


def _mem_kv(mem2d, g_mem, w_kv_bf, g_k):
    depth, d, _ = w_kv_bf.shape
    rows = mem2d.shape[0]
    tm = min(rows, 512)
    return pl.pallas_call(
        _mem_kv_kernel,
        grid=(depth, rows // tm),
        in_specs=[
            pl.BlockSpec((tm, d), lambda l, i: (i, 0)),
            pl.BlockSpec((1, 1, d), lambda l, i: (l, 0, 0)),
            pl.BlockSpec((1, d, 2 * ATT_WIDTH), lambda l, i: (l, 0, 0)),
            pl.BlockSpec((1, 1, HEAD_DIM), lambda l, i: (l, 0, 0)),
        ],
        out_specs=pl.BlockSpec((1, tm * KV_ROWS, HEAD_DIM), lambda l, i: (l, i, 0)),
        out_shape=jax.ShapeDtypeStruct((depth, rows * KV_ROWS, HEAD_DIM), F32),
        compiler_params=_params("arbitrary", "arbitrary"),
        name="mem_kv",
    )(mem2d, g_mem.reshape(depth, 1, d), w_kv_bf, g_k.reshape(depth, 1, HEAD_DIM))


def _cross_heads(qc, mem_ref, gq):
    n_mem = mem_ref.shape[1] // KV_ROWS
    outs = []
    for hd in range(N_HEADS):
        q = _rms(qc[:, hd * HEAD_DIM:(hd + 1) * HEAD_DIM], gq).astype(BF16)
        k = mem_ref[0, _kv_rows(0, hd, n_mem), :].astype(BF16)
        v = mem_ref[0, _kv_rows(1, hd, n_mem), :].astype(BF16)
        s = _dot_nt(q, k) * SCALE
        m = jnp.max(s, axis=-1, keepdims=True)
        p = jnp.exp(s - m)
        l = jnp.sum(p, axis=-1, keepdims=True)
        outs.append(_dot(p.astype(BF16), v) / l)
    return outs


def _rope(x, cos_t, sin_lo, sin_hi):
    return (x * cos_t + pltpu.roll(x, HEAD_DIM - ROT_HALF, 1) * sin_lo
            + pltpu.roll(x, ROT_HALF, 1) * sin_hi)


def _in_proj_a_kernel(*refs, fuse_cross, dilated, win_rows, win_first, tiles_per_seq):
    n_in = 10 if fuse_cross else 8
    x_ref, g_ref, w_ref, cos_ref, slo_ref, shi_ref, gq_ref, gk_ref = refs[:8]
    n_qkv = N_GROUPS if dilated else 1
    qkv_refs = refs[n_in:n_in + n_qkv]
    win_refs = refs[n_in + n_qkv:n_in + n_qkv + N_GROUPS]
    cr_ref = refs[n_in + n_qkv + N_GROUPS]
    kv_stash = refs[n_in + n_qkv + N_GROUPS + 1]
    stage = refs[n_in + n_qkv + N_GROUPS + 2] if dilated else None
    tm = x_ref.shape[0]
    h = _rms(x_ref[...], g_ref[...]).astype(BF16)
    cos_t, sin_lo, sin_hi = cos_ref[...], slo_ref[...], shi_ref[...]
    def proj(c):
        return _dot(h, w_ref[:, c * ATT_WIDTH:(c + 1) * ATT_WIDTH])

    tile = pl.program_id(0) % tiles_per_seq
    in_window = [tile >= first for first in win_first]
    for g in range(N_GROUPS):
        @pl.when(jnp.logical_not(in_window[g]))
        def _(g=g):
            win_refs[g][...] = jnp.zeros_like(win_refs[g])

    y = proj(0)
    for c in range(3 * N_GROUPS):
        y_next = proj(c + 1)
        role, g = divmod(c, N_GROUPS)
        r = DIL_GROUPS[g][1]
        wr = win_rows[g]
        for hd in range(N_HEADS):
            yh = y[:, hd * HEAD_DIM:(hd + 1) * HEAD_DIM]
            if role == 0:
                yh = _rope(_rms_head(yh, gq_ref[g]), cos_t, sin_lo, sin_hi)
            elif role == 1:
                yh = _rope(_rms_head(yh, gk_ref[g]), cos_t, sin_lo, sin_hi)
            if role > 0:
                kv_stash[((role - 1) * N_GROUPS + g) * N_HEADS + hd] = yh
            if not dilated:
                qkv_refs[0][role * N_GROUPS * N_HEADS + g * N_HEADS + hd] = yh.astype(BF16)
            elif r == 1:
                qkv_refs[g][role * N_HEADS + hd, 0] = yh.astype(BF16)
            else:
                stage[hd] = yh
                for rho in range(r):
                    qkv_refs[g][role * N_HEADS + hd, 0, :, rho * HEAD_DIM:(rho + 1) * HEAD_DIM] = (
                        stage[hd, pl.ds(rho, tm // r, stride=r), :].astype(BF16))
        y = y_next
    qc = y
    if fuse_cross:
        outs = _cross_heads(qc, refs[9], refs[8][...])
        for hd in range(N_HEADS):
            cr_ref[:, hd * HEAD_DIM:(hd + 1) * HEAD_DIM] = outs[hd].astype(cr_ref.dtype)
    else:
        cr_ref[...] = qc
    for g in range(N_GROUPS):
        @pl.when(in_window[g])
        def _(g=g):
            wr = win_rows[g]
            for kv in range(2):
                for hd in range(N_HEADS):
                    win_refs[g][_kv_rows(kv, hd, wr), :] = kv_stash[(kv * N_GROUPS + g) * N_HEADS + hd,
                                                                    tm - wr:, :]


SLOTS = HEAD_DIM // S5_GROUP


def _slot_transpose(vs):
    slot = lax.broadcasted_iota(jnp.int32, vs[0].shape, 1) // S5_GROUP
    d = SLOTS // 2
    while d:
        low = (slot & d) == 0
        new = list(vs)
        for i in range(SLOTS):
            if not i & d:
                new[i] = jnp.where(low, vs[i], pltpu.roll(vs[i | d], d * S5_GROUP, 1))
                new[i | d] = jnp.where(low, pltpu.roll(vs[i], HEAD_DIM - d * S5_GROUP, 1), vs[i | d])
        vs = new
        d //= 2
    return vs


def _in_proj_b_kernel(*refs, fuse_cross, n_valid):
    if fuse_cross:
        x_ref, g_ref, w_ref, gqc_ref, mem_ref, u_ref, cr_ref, stage = refs
    else:
        x_ref, g_ref, w_ref, u_ref, cr_ref, stage = refs
    tm = x_ref.shape[0]
    n_rows = tm // n_valid
    h = _rms(x_ref[...], g_ref[...]).astype(BF16)
    u = _dot(h, w_ref[:, :ATT_WIDTH])
    zero = jnp.zeros((n_rows, HEAD_DIM), F32)
    for cb in range(ATT_WIDTH // HEAD_DIM):
        stage[cb] = u[:, cb * HEAD_DIM:(cb + 1) * HEAD_DIM]
        for half in range(S5_SUB // SLOTS):
            taus = range(half * SLOTS, (half + 1) * SLOTS)
            groups = _slot_transpose([stage[cb, pl.ds(tau, n_rows, stride=n_valid), :] if tau < n_valid else zero
                                      for tau in taus])
            for p in range(SLOTS):
                lanes = (2 * (cb * SLOTS + p) + half) * HEAD_DIM
                u_ref[:, lanes:lanes + HEAD_DIM] = groups[p].astype(BF16)
    qc = _dot(h, w_ref[:, ATT_WIDTH:])
    if fuse_cross:
        outs = _cross_heads(qc, mem_ref, gqc_ref[...])
        for hd in range(N_HEADS):
            cr_ref[:, hd * HEAD_DIM:(hd + 1) * HEAD_DIM] = outs[hd].astype(cr_ref.dtype)
    else:
        cr_ref[...] = qc


def _rope_tables(pos):
    inv = jnp.exp(-math.log(ROPE_THETA) * jnp.arange(ROT_HALF, dtype=F32) / ROT_HALF)
    ang = pos.astype(F32)[:, None] * inv[None, :]
    cos, sin = jnp.cos(ang), jnp.sin(ang)
    rows = pos.shape[0]
    ones = jnp.ones((rows, HEAD_DIM - ROT_DIM), F32)
    zeros = jnp.zeros((rows, HEAD_DIM - ROT_DIM), F32)
    z16 = jnp.zeros((rows, ROT_HALF), F32)
    cos_t = jnp.concatenate([cos, cos, ones], axis=1)
    sin_lo = jnp.concatenate([-sin, z16, zeros], axis=1)
    sin_hi = jnp.concatenate([z16, sin, zeros], axis=1)
    return cos_t, sin_lo, sin_hi


def _in_proj_a(x2d, seq, tm, g_mix, w_bf, tables, g_q, g_k, g_qc, mem_kv, win_keep, dilated):
    rows, d = x2d.shape
    n_in = w_bf.shape[1]
    tiles_per_seq = seq // tm
    n_seq = rows // seq
    fuse_cross = mem_kv is not None
    tab_tiles = tables[0].shape[0] // tm
    win_rows = tuple(min(k, tm) for k in win_keep)
    win_first = tuple(tiles_per_seq - k // wr for k, wr in zip(win_keep, win_rows))

    def win_spec(keep):
        wr = min(keep, tm)
        nblk = keep // wr
        first = tiles_per_seq - nblk

        def imap(i):
            b = i // tiles_per_seq
            t = i % tiles_per_seq
            return (b * nblk + jnp.maximum(t - first, 0), 0)
        return pl.BlockSpec((wr * KV_ROWS, HEAD_DIM), imap)

    in_specs = [
        pl.BlockSpec((tm, d), lambda i: (i, 0)),
        _full((1, d)),
        _full((d, n_in)),
        pl.BlockSpec((tm, HEAD_DIM), lambda i: (i % tab_tiles, 0)),
        pl.BlockSpec((tm, HEAD_DIM), lambda i: (i % tab_tiles, 0)),
        pl.BlockSpec((tm, HEAD_DIM), lambda i: (i % tab_tiles, 0)),
        _full((N_GROUPS, HEAD_DIM)),
        _full((N_GROUPS, HEAD_DIM)),
    ]
    args = [x2d, g_mix.reshape(1, d), w_bf, *tables, g_q, g_k]
    if fuse_cross:
        in_specs += [_full((1, HEAD_DIM)),
                     pl.BlockSpec((1,) + mem_kv.shape[1:], lambda i: (i // tiles_per_seq, 0, 0))]
        args += [g_qc.reshape(1, HEAD_DIM), mem_kv]
    if dilated:
        qkv_specs = [pl.BlockSpec((3 * N_HEADS, 1, tm // r, r * HEAD_DIM),
                                  lambda i: (0, i // tiles_per_seq, i % tiles_per_seq, 0))
                     for _, r in DIL_GROUPS]
        qkv_shapes = [jax.ShapeDtypeStruct((3 * N_HEADS, n_seq, seq // r, r * HEAD_DIM), BF16)
                      for _, r in DIL_GROUPS]
        scratch = [pltpu.VMEM((N_HEADS, tm, HEAD_DIM), F32)]
    else:
        qkv_specs = [pl.BlockSpec((3 * N_GROUPS * N_HEADS, tm, HEAD_DIM), lambda i: (0, i, 0))]
        qkv_shapes = [jax.ShapeDtypeStruct((3 * N_GROUPS * N_HEADS, rows, HEAD_DIM), BF16)]
        scratch = []
    scratch = [pltpu.VMEM((2 * N_GROUPS * N_HEADS, tm, HEAD_DIM), F32)] + scratch
    out_specs = [*qkv_specs, *[win_spec(k) for k in win_keep],
                 pl.BlockSpec((tm, ATT_WIDTH), lambda i: (i, 0))]
    out_shape = [*qkv_shapes,
                 *[jax.ShapeDtypeStruct((n_seq * k * KV_ROWS, HEAD_DIM), F32) for k in win_keep],
                 jax.ShapeDtypeStruct((rows, ATT_WIDTH), BF16 if fuse_cross else F32)]
    return pl.pallas_call(
        functools.partial(_in_proj_a_kernel, fuse_cross=fuse_cross, dilated=dilated, win_rows=win_rows,
                          win_first=win_first, tiles_per_seq=tiles_per_seq),
        grid=(rows // tm,),
        in_specs=in_specs, out_specs=out_specs, out_shape=out_shape, scratch_shapes=scratch,
        compiler_params=_params("arbitrary"),
        name="in_proj_a",
    )(*args)


def _in_proj_b(x2d, seq, tm, n_valid, g_mix, w_bf, g_qc, mem_kv):
    rows, d = x2d.shape
    width = (ATT_WIDTH // S5_GROUP) * S5_BLOCK
    n_in = w_bf.shape[1]
    tiles_per_seq = seq // tm
    fuse_cross = mem_kv is not None
    in_specs = [pl.BlockSpec((tm, d), lambda i: (i, 0)), _full((1, d)), _full((d, n_in))]
    args = [x2d, g_mix.reshape(1, d), w_bf]
    if fuse_cross:
        in_specs += [_full((1, HEAD_DIM)),
                     pl.BlockSpec((1,) + mem_kv.shape[1:], lambda i: (i // tiles_per_seq, 0, 0))]
        args += [g_qc.reshape(1, HEAD_DIM), mem_kv]
    return pl.pallas_call(
        functools.partial(_in_proj_b_kernel, fuse_cross=fuse_cross, n_valid=n_valid),
        grid=(rows // tm,),
        in_specs=in_specs,
        out_specs=[pl.BlockSpec((tm // n_valid, width), lambda i: (i, 0)),
                   pl.BlockSpec((tm, ATT_WIDTH), lambda i: (i, 0))],
        out_shape=[jax.ShapeDtypeStruct((rows // n_valid, width), BF16),
                   jax.ShapeDtypeStruct((rows, ATT_WIDTH), BF16 if fuse_cross else F32)],
        scratch_shapes=[pltpu.VMEM((ATT_WIDTH // HEAD_DIM, tm, HEAD_DIM), F32)],
        compiler_params=_params("arbitrary"),
        name="in_proj_b",
    )(*args)


CROSS_BATCH = 4


def _cross_sample_kernel(qc_ref, gq_ref, mem_ref, out_ref):
    for b in range(qc_ref.shape[0]):
        outs = _cross_heads(qc_ref[b], mem_ref.at[pl.ds(b, 1)], gq_ref[...])
        for hd in range(N_HEADS):
            out_ref[b, :, hd * HEAD_DIM:(hd + 1) * HEAD_DIM] = outs[hd].astype(out_ref.dtype)


def _cross_sample(qc, g_qc, mem_kv, first):
    nb, tq, _ = qc.shape
    cb = math.gcd(nb, CROSS_BATCH)
    assert first % cb == 0
    return pl.pallas_call(
        _cross_sample_kernel,
        grid=(nb // cb,),
        in_specs=[pl.BlockSpec((cb, tq, ATT_WIDTH), lambda b: (b, 0, 0)),
                  _full((1, HEAD_DIM)),
                  pl.BlockSpec((cb,) + mem_kv.shape[1:], lambda b: (first // cb + b, 0, 0))],
        out_specs=pl.BlockSpec((cb, tq, ATT_WIDTH), lambda b: (b, 0, 0)),
        out_shape=jax.ShapeDtypeStruct((nb, tq, ATT_WIDTH), BF16),
        compiler_params=_params("arbitrary"),
        name="cross_sample",
    )(qc, g_qc.reshape(1, HEAD_DIM), mem_kv)


INFLIGHT = 4


def _band_block(q, k, v):
    n = k.shape[0]
    dist = (n - BLOCK + lax.broadcasted_iota(jnp.int32, (BLOCK, n), 0)
            - lax.broadcasted_iota(jnp.int32, (BLOCK, n), 1))
    s = jnp.where((dist >= 0) & (dist <= DIL_SPAN), _dot_nt(q, k) * SCALE, NEG)
    m = jnp.max(s, axis=-1, keepdims=True)
    p = jnp.exp(s - m)
    l = jnp.sum(p, axis=-1, keepdims=True)
    return _dot(p.astype(BF16), v) / l, m + jnp.log(l)


def _attn_prompt_kernel(q0, q1, q2, k0, k1, k2, v0, v1, v2, out_ref, o_scr, l_scr, *, seq):
    qkv_refs = ((q0, k0, v0), (q1, k1, v1), (q2, k2, v2))
    for g, (_, r) in enumerate(DIL_GROUPS):
        q_ref, k_ref, v_ref = qkv_refs[g]
        nblk = seq // r // BLOCK

        def store(rho, blk, o, lse, g=g, r=r):
            start = blk * (BLOCK * r) + rho
            if r == 1:
                idx = pl.ds(pl.multiple_of(start, BLOCK), BLOCK)
            else:
                idx = pl.ds(start, BLOCK, stride=r)
            o_scr[g, idx, :] = o
            l_scr[g, idx, :] = jnp.broadcast_to(lse, (BLOCK, HEAD_DIM))

        def first(rho, q_ref=q_ref, k_ref=k_ref, v_ref=v_ref, store=store):
            lanes = slice(rho * HEAD_DIM, (rho + 1) * HEAD_DIM)
            o, lse = _band_block(q_ref[0, 0, :BLOCK, lanes], k_ref[0, 0, :BLOCK, lanes],
                                 v_ref[0, 0, :BLOCK, lanes])
            store(rho, 0, o, lse)

        def later(rho, blk, q_ref=q_ref, k_ref=k_ref, v_ref=v_ref, store=store):
            lanes = slice(rho * HEAD_DIM, (rho + 1) * HEAD_DIM)
            cur = pl.ds(pl.multiple_of(blk * BLOCK, BLOCK), BLOCK)
            both = pl.ds(pl.multiple_of((blk - 1) * BLOCK, BLOCK), 2 * BLOCK)
            o, lse = _band_block(q_ref[0, 0, cur, lanes], k_ref[0, 0, both, lanes], v_ref[0, 0, both, lanes])
            store(rho, blk, o, lse)

        for rho0 in range(0, r, INFLIGHT):
            rhos = range(rho0, min(r, rho0 + INFLIGHT))
            for rho in rhos:
                first(rho)

            def body(blk, carry, rhos=rhos, later=later):
                for rho in rhos:
                    later(rho, blk)
                return carry
            if nblk > 1:
                lax.fori_loop(1, nblk, body, 0, unroll=max(1, INFLIGHT // len(rhos)))

    def combine(c, carry):
        rows = pl.ds(pl.multiple_of(c * BLOCK, BLOCK), BLOCK)
        ls = [l_scr[g, rows, :] for g in range(N_GROUPS)]
        m = jnp.maximum(jnp.maximum(ls[0], ls[1]), ls[2])
        es = [jnp.exp(l - m) for l in ls]
        num = es[0] * o_scr[0, rows, :] + es[1] * o_scr[1, rows, :] + es[2] * o_scr[2, rows, :]
        out_ref[0, rows, :] = (num / (es[0] + es[1] + es[2])).astype(out_ref.dtype)
        return carry
    lax.fori_loop(0, seq // BLOCK, combine, 0)


def _attn_prompt(qkv_groups, n_seq, seq):
    in_specs, args = [], []
    for role in range(3):
        for g, (_, r) in enumerate(DIL_GROUPS):
            in_specs.append(pl.BlockSpec((1, 1, seq // r, r * HEAD_DIM),
                                         lambda b, h, role=role: (role * N_HEADS + h, b, 0, 0)))
            args.append(qkv_groups[g])
    return pl.pallas_call(
        functools.partial(_attn_prompt_kernel, seq=seq),
        grid=(n_seq, N_HEADS),
        in_specs=in_specs,
        out_specs=pl.BlockSpec((1, seq, HEAD_DIM), lambda b, h: (b, 0, h)),
        out_shape=jax.ShapeDtypeStruct((n_seq, seq, ATT_WIDTH), BF16),
        scratch_shapes=[pltpu.VMEM((N_GROUPS, seq, HEAD_DIM), F32),
                        pltpu.VMEM((N_GROUPS, seq, HEAD_DIM), F32)],
        compiler_params=_params("arbitrary", "arbitrary"),
        name="attn_prompt",
    )(*args)


def _attn_sample_kernel(q_ref, new_ref, old_ref, o_ref, lse_ref, *scratch, r, t_new, lb, compact):
    qf = q_ref[0].astype(F32)
    if compact:
        (flat,) = scratch
        n_key = lb // r
        for u in range(t_new):
            flat[u] = new_ref[:, u * KV_ROWS:(u + 1) * KV_ROWS, :].reshape(n_key * KV_ROWS, HEAD_DIM)
    else:
        n_key = lb
    i_key = lax.broadcasted_iota(jnp.int32, (n_key, 1), 0)
    j_old = lax.broadcasted_iota(jnp.int32, (t_new, 1), 0)
    for t in range(t_new):
        if compact:
            idx = i_key * r + (r - t_new + t)
            rows = lambda kvh, t=t: flat[t, pl.ds(kvh, n_key, stride=KV_ROWS), :]
        else:
            idx = i_key
            rows = lambda kvh: new_ref[pl.ds(kvh, n_key, stride=KV_ROWS), :]
        dist = lb - t_new + t - idx
        ok = (dist >= 0) & (dist % r == 0) & (dist <= r * DIL_SPAN)
        dist_old = lb + t - j_old
        ok_old = (dist_old % r == 0) & (dist_old <= r * DIL_SPAN)
        for hd in range(N_HEADS):
            sl = slice(hd * HEAD_DIM, (hd + 1) * HEAD_DIM)
            q = qf[t:t + 1, sl]
            k, v = rows(hd), rows(N_HEADS + hd)
            k_old = old_ref[pl.ds(hd, t_new, stride=KV_ROWS), :]
            v_old = old_ref[pl.ds(N_HEADS + hd, t_new, stride=KV_ROWS), :]
            s = jnp.where(ok, jnp.sum(k * q, axis=-1, keepdims=True) * SCALE, NEG)
            s_old = jnp.where(ok_old, jnp.sum(k_old * q, axis=-1, keepdims=True) * SCALE, NEG)
            m = jnp.maximum(jnp.max(s, axis=0, keepdims=True), jnp.max(s_old, axis=0, keepdims=True))
            p, p_old = jnp.exp(s - m), jnp.exp(s_old - m)
            l = jnp.sum(p, axis=0, keepdims=True) + jnp.sum(p_old, axis=0, keepdims=True)
            o = jnp.sum(p * v, axis=0, keepdims=True) + jnp.sum(p_old * v_old, axis=0, keepdims=True)
            o_ref[0, t:t + 1, sl] = o / l
            lse_ref[0, t:t + 1, sl] = jnp.broadcast_to(m + jnp.log(l), (1, HEAD_DIM))


def _attn_sample(q, new_buf, old_buf, r, t_new):
    nb, tq, _ = q.shape
    lb = new_buf.shape[1] // KV_ROWS
    compact = r % t_new == 0
    if compact:
        new_view = new_buf.reshape(nb, lb // r, r // t_new, t_new * KV_ROWS, HEAD_DIM)
        new_spec = pl.BlockSpec((None, lb // r, None, t_new * KV_ROWS, HEAD_DIM),
                                lambda b: (b, 0, r // t_new - 1, 0, 0))
        scratch = [pltpu.VMEM((t_new, lb // r * KV_ROWS, HEAD_DIM), F32)]
    else:
        new_view = new_buf
        new_spec = pl.BlockSpec((None, lb * KV_ROWS, HEAD_DIM), lambda b: (b, 0, 0))
        scratch = []
    return pl.pallas_call(
        functools.partial(_attn_sample_kernel, r=r, t_new=t_new, lb=lb, compact=compact),
        grid=(nb,),
        in_specs=[pl.BlockSpec((1, tq, ATT_WIDTH), lambda b: (b, 0, 0)),
                  new_spec,
                  pl.BlockSpec((None, t_new * KV_ROWS, HEAD_DIM), lambda b: (b, 0, 0))],
        out_specs=[pl.BlockSpec((1, t_new, ATT_WIDTH), lambda b: (b, 0, 0)),
                   pl.BlockSpec((1, t_new, ATT_WIDTH), lambda b: (b, 0, 0))],
        out_shape=[jax.ShapeDtypeStruct((nb, t_new, ATT_WIDTH), F32),
                   jax.ShapeDtypeStruct((nb, t_new, ATT_WIDTH), F32)],
        scratch_shapes=scratch,
        compiler_params=_params("arbitrary"),
        name=f"attn_sample_r{r}",
    )(q, new_view, old_buf)


def _combine_kernel(o0, o1, o2, l0, l1, l2, out_ref):
    ls = [l0[...], l1[...], l2[...]]
    m = jnp.maximum(jnp.maximum(ls[0], ls[1]), ls[2])
    es = [jnp.exp(l - m) for l in ls]
    num = es[0] * o0[...] + es[1] * o1[...] + es[2] * o2[...]
    out_ref[...] = (num / (es[0] + es[1] + es[2])).astype(out_ref.dtype)


def _combine_groups(outs, lses):
    shape = outs[0].shape
    return pl.pallas_call(
        _combine_kernel,
        in_specs=[_full(shape)] * 6,
        out_specs=_full(shape),
        out_shape=jax.ShapeDtypeStruct(shape, BF16),
        grid=(1,),
        compiler_params=_params("arbitrary"),
        name="combine_groups",
    )(*outs, *lses)


FF_CHUNK = 256
CARRY_ROWS = 8
SHIFT_SLOTS = 4


MOVE_ROWS = 8192


def _move_regions(keeps):
    regions = [(g, start, min(MOVE_ROWS, keep - start)) for g, keep in enumerate(keeps)
               for start in range(0, keep, MOVE_ROWS)]
    halves = ([], [])
    for reg in sorted(regions, key=lambda reg: -reg[2]):
        min(halves, key=lambda h: sum(r[2] for r in h)).append(reg)
    return halves


def _buffer_move(old_refs, kvn_refs, new_refs, bufs, tails, sems, b, n_steps):
    n_new = [k.shape[1] for k in kvn_refs]
    halves = _move_regions([o.shape[1] - n for o, n in zip(old_refs, n_new)])

    def copies(batch):
        out, sem = [], 0
        for buf, half in zip(bufs, halves):
            row, cps = 0, []
            for g, start, size in half:
                stage = buf.at[pl.ds(row, size)]
                cps.append((pltpu.make_async_copy(old_refs[g].at[batch, pl.ds(n_new[g] + start, size)], stage,
                                                  sems.at[sem]),
                            pltpu.make_async_copy(stage, new_refs[g].at[batch, pl.ds(start, size)],
                                                  sems.at[sem + 1])))
                row += size
                sem += 2
            out.append(cps)
        tail = []
        for g in range(N_GROUPS):
            keep = old_refs[g].shape[1] - n_new[g]
            tail.append((pltpu.make_async_copy(kvn_refs[g].at[batch], tails.at[g], sems.at[sem]),
                         pltpu.make_async_copy(tails.at[g], new_refs[g].at[batch, pl.ds(keep, n_new[g])],
                                               sems.at[sem + 1])))
            sem += 2
        return out[0], out[1], tail
    first, second, tail = copies(b)
    _, prev_second, _ = copies(jnp.maximum(b - 1, 0))

    def top():
        for cp_in, _ in first + tail:
            cp_in.start()

    def mid():
        for cp_in, cp_out in first + tail:
            cp_in.wait()
            cp_out.start()

        @pl.when(b > 0)
        def _():
            for _, cp_out in prev_second:
                cp_out.wait()
        for cp_in, _ in second:
            cp_in.start()

    def end():
        for cp_in, cp_out in second:
            cp_in.wait()
            cp_out.start()
        for _, cp_out in first + tail:
            cp_out.wait()

        @pl.when(b == n_steps - 1)
        def _():
            for _, cp_out in second:
                cp_out.wait()
    return top, mid, end


def _move_scratch(old, new_rows):
    halves = _move_regions([o.shape[1] - n.shape[1] for o, n in zip(old, new_rows)])
    n_copies = sum(len(h) for h in halves) + N_GROUPS
    return ([pltpu.VMEM((sum(r[2] for r in h), HEAD_DIM), F32) for h in halves]
            + [pltpu.VMEM((N_GROUPS,) + new_rows[0].shape[1:], F32), pltpu.SemaphoreType.DMA((2 * n_copies,))])


def _out_ffn_kernel(*refs, seq_len, tiles_per_seq, tail_rows, move):
    move_top = move_mid = move_end = lambda: None
    if move:
        (x_ref, mix_ref, cr_ref, wo_ref, g_ref, wup_ref, cw_ref, cb_ref, wdn_ref, o0, o1, o2, k0, k1, k2,
         out_ref, tail_ref, n0, n1, n2, shift, act_scr, carry, buf_a, buf_b, tails, sems) = refs
        move_top, move_mid, move_end = _buffer_move(
            (o0, o1, o2), (k0, k1, k2), (n0, n1, n2), (buf_a, buf_b), tails, sems,
            pl.program_id(0), pl.num_programs(0))
    elif seq_len is None:
        (x_ref, mix_ref, cr_ref, wo_ref, g_ref, wup_ref, cw_ref, cb_ref, wdn_ref,
         out_ref, tail_ref, shift, act_scr, carry) = refs
    else:
        (x_ref, mix_ref, cr_ref, wo_ref, g_ref, wup_ref, cw_ref, cb_ref, wdn_ref, e1_ref, e2_ref,
         out_ref, tail_ref, shift, act_scr) = refs
    tm = x_ref.shape[0]
    d_ff = wdn_ref.shape[0]
    move_top()
    x1 = x_ref[...] + _dot(jnp.concatenate([mix_ref[...], cr_ref[...]], axis=-1), wo_ref[...])
    h = _rms(x1, g_ref[...]).astype(BF16)
    if seq_len is None:
        @pl.when(pl.program_id(0) % tiles_per_seq == 0)
        def _():
            carry[...] = jnp.zeros_like(carry)
    else:
        t = lax.broadcasted_iota(jnp.int32, (tm, 1), 0) % seq_len
        has1 = t >= 1
        has2 = t >= 2

    def chunk_cols(j):
        return (slice(j * FF_CHUNK, (j + 1) * FF_CHUNK),
                slice(d_ff + j * FF_CHUNK, d_ff + (j + 1) * FF_CHUNK))

    def up_proj(j):
        return tuple(_dot(h, wup_ref[:, cols]) for cols in chunk_cols(j))

    def conv(up, cols, slot):
        buf = shift.at[slot]
        if seq_len is None:
            buf[:CARRY_ROWS, :] = carry[:, cols]
            carry[:, cols] = up[tm - CARRY_ROWS:, :]
        else:
            buf[:CARRY_ROWS, :] = jnp.zeros((CARRY_ROWS, FF_CHUNK), F32)
        buf[CARRY_ROWS:, :] = up
        tail_ref[0, :, cols] = up[tm - tail_rows:, :]
        prev1 = buf[CARRY_ROWS - 1:CARRY_ROWS - 1 + tm, :]
        prev2 = buf[CARRY_ROWS - 2:CARRY_ROWS - 2 + tm, :]
        if seq_len is not None:
            prev1 = jnp.where(has1, prev1, e1_ref[:, cols])
            prev2 = jnp.where(has2, prev2, e2_ref[:, cols])
        return (cb_ref[:, cols] + cw_ref[0:1, cols] * prev2 + cw_ref[1:2, cols] * prev1
                + cw_ref[2:3, cols] * up)

    n_chunks = d_ff // FF_CHUNK
    ups = up_proj(0)
    for j in range(n_chunks):
        nxt = up_proj(j + 1) if j + 1 < n_chunks else None
        a, b = (conv(up, cols, 2 * (j % 2) + s) for s, (up, cols) in enumerate(zip(ups, chunk_cols(j))))
        act_scr[:, j * FF_CHUNK:(j + 1) * FF_CHUNK] = (a * jax.nn.sigmoid(a) * b).astype(BF16)
        ups = nxt
        if j == n_chunks // 2:
            move_mid()
    out_ref[...] = x1 + _dot(act_scr[...], wdn_ref[...])
    move_end()


def _out_ffn(x2d, mix, cross, w_out_bf, g_ffn, w_up_bf, conv_w, conv_b, w_down_bf, tm,
             tiles_per_seq=None, seq_len=None, e1=None, e2=None, tail_rows=CARRY_ROWS, moves=None):
    rows, d = x2d.shape
    d_ff = w_down_bf.shape[0]
    n_tiles = rows // tm
    in_specs = [
        pl.BlockSpec((tm, d), lambda i: (i, 0)),
        pl.BlockSpec((tm, ATT_WIDTH), lambda i: (i, 0)),
        pl.BlockSpec((tm, ATT_WIDTH), lambda i: (i, 0)),
        _full(w_out_bf.shape), _full((1, d)), _full(w_up_bf.shape),
        _full((CONV_W, 2 * d_ff)), _full((1, 2 * d_ff)), _full(w_down_bf.shape),
    ]
    args = [x2d, mix, cross, w_out_bf, g_ffn.reshape(1, d), w_up_bf, conv_w, conv_b.reshape(1, 2 * d_ff),
            w_down_bf]
    scratch = [pltpu.VMEM((SHIFT_SLOTS, CARRY_ROWS + tm, FF_CHUNK), F32), pltpu.VMEM((tm, d_ff), BF16)]
    if seq_len is None:
        scratch += [pltpu.VMEM((CARRY_ROWS, 2 * d_ff), F32)]
    else:
        in_specs += [pl.BlockSpec((tm, 2 * d_ff), lambda i: (i, 0))] * 2
        args += [e1, e2]
    out_specs = [pl.BlockSpec((tm, d), lambda i: (i, 0)),
                 pl.BlockSpec((1, tail_rows, 2 * d_ff), lambda i: (i, 0, 0))]
    out_shape = [jax.ShapeDtypeStruct((rows, d), F32),
                 jax.ShapeDtypeStruct((n_tiles, tail_rows, 2 * d_ff), F32)]
    if moves is not None:
        old, new_rows = moves
        assert seq_len is None and all(o.shape[0] == n_tiles for o in old)
        any_spec = pl.BlockSpec(memory_space=pl.ANY)
        in_specs += [any_spec] * (2 * N_GROUPS)
        args += [*old, *new_rows]
        out_specs += [any_spec] * N_GROUPS
        out_shape += [jax.ShapeDtypeStruct(o.shape, o.dtype) for o in old]
        scratch += _move_scratch(old, new_rows)
    return pl.pallas_call(
        functools.partial(_out_ffn_kernel, seq_len=seq_len, tiles_per_seq=tiles_per_seq,
                          tail_rows=tail_rows, move=moves is not None),
        grid=(n_tiles,),
        in_specs=in_specs,
        out_specs=out_specs,
        out_shape=out_shape,
        scratch_shapes=scratch,
        compiler_params=_params("arbitrary"),
        name="out_ffn",
    )(*args)


def _swap_halves(x):
    return pltpu.roll(x, S5_STATE, 1)


def _s5_prep_kernel(lam_ref, logdt_ref, bt_ref, c_ref, d_ref, t_ref, cpt_ref, *var_refs, n_valids):
    gc = lam_ref.shape[0]
    n_groups = gc // S5_GROUP
    lane = lax.broadcasted_iota(jnp.int32, lam_ref.shape, 1)
    first = lane < S5_STATE
    sign = jnp.where(first, -1.0, 1.0)
    a = lam_ref[...]
    a_sw = _swap_halves(a)
    are = jnp.where(first, a, a_sw)
    aim = jnp.where(first, a_sw, a)
    dt = jnp.exp(logdt_ref[...])
    mag = jnp.exp(are * dt)
    lr = mag * jnp.cos(aim * dt)
    li = mag * jnp.sin(aim * dt)
    den = are * are + aim * aim
    xr = lr - 1.0
    f_re = (xr * are + li * aim) / den
    f_im = (li * are - xr * aim) / den
    lb = sign * li

    def cmul(x, m_re, m_sw):
        return x * m_re + _swap_halves(x) * m_sw

    c = c_ref[...]
    c_neg = c * -sign
    ri = lax.broadcasted_iota(jnp.int32, (gc, gc), 0)
    ci = lax.broadcasted_iota(jnp.int32, (gc, gc), 1)
    same_group = (ri // S5_GROUP) == (ci // S5_GROUP)
    e = cmul(bt_ref[...], f_re, sign * f_im)
    cl = c
    pw = jnp.where(first, 1.0, 0.0)
    zr = lax.broadcasted_iota(jnp.int32, (gc, S5_BLOCK), 0)
    zc = lax.broadcasted_iota(jnp.int32, (gc, S5_BLOCK), 1)
    same_out = (zr % S5_GROUP) == (zc % S5_GROUP)
    lags = jnp.zeros((gc, S5_BLOCK), F32)
    for k in range(S5_SUB):
        kmat = jnp.where(same_group, _dot_nt(e, c_neg, precision=HIGHEST), 0.0)
        if k == 0:
            kmat = kmat + jnp.where(ri == ci, d_ref[...], 0.0)
        place = jnp.where(same_out & (zc // S5_GROUP == k), 1.0, 0.0)
        lags = lags + jnp.dot(kmat, place, precision=HIGHEST, preferred_element_type=F32)
        both = jnp.concatenate([e, _swap_halves(e)], axis=-1).reshape(n_groups, S5_GROUP, 4 * S5_STATE)
        for n_valid, bp_ref in zip(n_valids, var_refs[0::2]):
            if n_valid - 1 - k >= 0:
                bp_ref[:, n_valid - 1 - k] = both.astype(bp_ref.dtype)
        cl = cmul(cl, lr, lb)
        cpt_ref[:, k] = (cl * -sign).reshape(n_groups, S5_GROUP, 2 * S5_STATE).astype(cpt_ref.dtype)
        e = cmul(e, lr, lb)
        pw = cmul(pw, lr, lb)
        for n_valid, lam_out_ref in zip(n_valids, var_refs[1::2]):
            if k + 1 == n_valid:
                lam_out_ref[...] = pw
    for n_valid, bp_ref in zip(n_valids, var_refs[0::2]):
        for tau in range(n_valid, S5_SUB):
            bp_ref[:, tau] = jnp.zeros((n_groups, S5_GROUP, 4 * S5_STATE), bp_ref.dtype)
    for taup in range(S5_SUB):
        moved = lags if taup == 0 else jnp.where(zc >= taup * S5_GROUP, pltpu.roll(lags, taup * S5_GROUP, 1), 0.0)
        t_ref[:, taup] = moved.reshape(n_groups, S5_GROUP, S5_BLOCK).astype(t_ref.dtype)


def _s5_prep(lam_re, lam_im, log_dt, b_re, b_im, c_re, c_im, d_skip, n_valids):
    n_groups = lam_re.shape[0]
    gc = n_groups * S5_GROUP
    rep = lambda t: jnp.repeat(t, S5_GROUP, axis=0)
    lam_p = rep(jnp.concatenate([lam_re, lam_im], axis=-1))
    logdt = rep(log_dt.reshape(n_groups, 1))
    bt_p = jnp.concatenate([b_re.transpose(0, 2, 1), b_im.transpose(0, 2, 1)], axis=-1).reshape(gc, 2 * S5_STATE)
    c_p = jnp.concatenate([c_re, c_im], axis=-1).reshape(gc, 2 * S5_STATE)
    t_shape = (n_groups, S5_SUB, S5_GROUP, S5_BLOCK)
    cpt_shape = (n_groups, S5_SUB, S5_GROUP, 2 * S5_STATE)
    bp_shape = (n_groups, S5_SUB, S5_GROUP, 4 * S5_STATE)
    lam_shape = (gc, 2 * S5_STATE)
    var = [(bp_shape, BF16), (lam_shape, F32)] * len(n_valids)
    t, cpt, *rest = pl.pallas_call(
        functools.partial(_s5_prep_kernel, n_valids=tuple(n_valids)),
        grid=(1,),
        in_specs=[_full(lam_p.shape), _full(logdt.shape), _full(bt_p.shape), _full(c_p.shape), _full((gc, 1))],
        out_specs=[_full(t_shape), _full(cpt_shape), *[_full(s) for s, _ in var]],
        out_shape=[jax.ShapeDtypeStruct(t_shape, BF16), jax.ShapeDtypeStruct(cpt_shape, BF16),
                   *[jax.ShapeDtypeStruct(s, dt) for s, dt in var]],
        compiler_params=_params("arbitrary"),
        name="s5_prep",
    )(lam_p, logdt, bt_p, c_p, d_skip.reshape(gc, 1))
    merge = lambda a: a.reshape(n_groups, S5_BLOCK, a.shape[-1])
    return [(merge(t), merge(bp), merge(cpt), lam_n[::S5_GROUP]) for bp, lam_n in zip(rest[0::2], rest[1::2])]


SCAN_GROUPS = 8
SUBLANES = 8
GLU_ROWS = 512


def _s5_core_kernel(v_ref, x0_ref, x0s_ref, t_ref, bp_ref, cpt_ref, lam_ref, wglu_ref, bglu_ref,
                    mix_ref, xfin_ref, s_scr, xprev_scr, m_scr, stage, *, n_valid, n_sub):
    rows = v_ref.shape[0]
    n_groups = t_ref.shape[0]
    lane = lax.broadcasted_iota(jnp.int32, (1, 2 * S5_STATE), 1)
    first = lane < S5_STATE
    blk = lambda g: slice(g * S5_BLOCK, (g + 1) * S5_BLOCK)
    tile = lambda i: slice(i * 2 * S5_STATE, (i + 1) * 2 * S5_STATE)
    for g in range(n_groups):
        s_scr[:, blk(g)] = _dot(v_ref[:, blk(g)], bp_ref[g])
        lam = lam_ref[g:g + 1, :]
        lam_sw = _swap_halves(lam)
        m_scr[0:1, tile(g)] = jnp.where(first, lam, lam_sw)
        m_scr[1:2, tile(g)] = jnp.where(first, -lam_sw, lam)
    if n_sub == 1:
        for g in range(n_groups):
            x, xs = x0_ref[:, tile(g)], x0s_ref[:, tile(g)]
            xprev_scr[:, tile(g)] = x
            xfin_ref[:, tile(g)] = (x * m_scr[0:1, tile(g)] + xs * m_scr[1:2, tile(g)]
                                    + s_scr[:, tile(2 * g)])
    else:
        assert rows == n_sub
        for g0 in range(0, n_groups, SCAN_GROUPS):
            gs = range(g0, min(n_groups, g0 + SCAN_GROUPS))

            def body(i, carry, gs=gs):
                rows8 = pl.ds(pl.multiple_of(i * SUBLANES, SUBLANES), SUBLANES)
                sub_i = lax.broadcasted_iota(jnp.int32, (SUBLANES, 2 * S5_STATE), 0)
                out = []
                for g, (x, xs) in zip(gs, carry):
                    s8, ssw8 = s_scr[rows8, tile(2 * g)], s_scr[rows8, tile(2 * g + 1)]
                    prev8 = jnp.zeros((SUBLANES, 2 * S5_STATE), F32)
                    for r in range(SUBLANES):
                        prev8 = jnp.where(sub_i == r, x, prev8)
                        m_re, m_sw = m_scr[0:1, tile(g)], m_scr[1:2, tile(g)]
                        x, xs = (x * m_re + xs * m_sw + s8[r:r + 1, :],
                                 xs * m_re - x * m_sw + ssw8[r:r + 1, :])
                    xprev_scr[rows8, tile(g)] = prev8
                    out.append((x, xs))
                return tuple(out)
            init = tuple((x0_ref[0:1, tile(g)], x0s_ref[0:1, tile(g)]) for g in gs)
            fin = lax.fori_loop(0, n_sub // SUBLANES, body, init)
            for g, (x, _) in zip(gs, fin):
                xfin_ref[:, tile(g)] = jnp.broadcast_to(x, (xfin_ref.shape[0], 2 * S5_STATE))
    for g in range(n_groups):
        s_scr[:, blk(g)] = (_dot(v_ref[:, blk(g)], t_ref[g])
                            + _dot_nt(xprev_scr[:, tile(g)].astype(BF16), cpt_ref[g]))
    for cb in range(ATT_WIDTH // HEAD_DIM):
        for half in range(-(-n_valid // SLOTS)):
            toks = _slot_transpose([s_scr[:, tile(2 * (cb * SLOTS + p) + half)] for p in range(SLOTS)])
            for s in range(min(SLOTS, n_valid - half * SLOTS)):
                stage[cb, pl.ds(half * SLOTS + s, rows, stride=n_valid), :] = toks[s]
    n_tok = rows * n_valid
    chunk = min(n_tok, GLU_ROWS)

    def glu(i, carry):
        r = pl.ds(pl.multiple_of(i * chunk, chunk), chunk)
        y = jax.nn.gelu(jnp.concatenate([stage[cb, r, :] for cb in range(ATT_WIDTH // HEAD_DIM)], axis=-1))
        z = _dot(y.astype(BF16), wglu_ref[...]) + bglu_ref[...]
        mix_ref[r, :] = (y * jax.nn.sigmoid(z)).astype(mix_ref.dtype)
        return carry
    lax.fori_loop(0, n_tok // chunk, glu, 0)


def _s5_mixer(v, rows_per_tile, n_valid, n_sub, x0, x0s, t, bp, cpt, lam_n, w_glu_bf, b_glu):
    m, width = v.shape
    n_tiles = m // rows_per_tile
    r0 = x0.shape[0] // n_tiles
    n_state = x0.shape[1]
    n_tok = rows_per_tile * n_valid
    return pl.pallas_call(
        functools.partial(_s5_core_kernel, n_valid=n_valid, n_sub=n_sub),
        grid=(n_tiles,),
        in_specs=[pl.BlockSpec((rows_per_tile, width), lambda i: (i, 0)),
                  pl.BlockSpec((r0, n_state), lambda i: (i, 0)),
                  pl.BlockSpec((r0, n_state), lambda i: (i, 0)),
                  _full(t.shape), _full(bp.shape), _full(cpt.shape), _full(lam_n.shape),
                  _full(w_glu_bf.shape), _full((1, ATT_WIDTH))],
        out_specs=[pl.BlockSpec((n_tok, ATT_WIDTH), lambda i: (i, 0)),
                   pl.BlockSpec((r0, n_state), lambda i: (i, 0))],
        out_shape=[jax.ShapeDtypeStruct((m * n_valid, ATT_WIDTH), BF16),
                   jax.ShapeDtypeStruct(x0.shape, F32)],
        scratch_shapes=[pltpu.VMEM((rows_per_tile, width), F32),
                        pltpu.VMEM((rows_per_tile, n_state), F32),
                        pltpu.VMEM((8, n_state), F32),
                        pltpu.VMEM((ATT_WIDTH // HEAD_DIM, n_tok, HEAD_DIM), F32)],
        compiler_params=_params("arbitrary"),
        name="s5_core",
    )(v, x0, x0s, t, bp, cpt, lam_n, w_glu_bf, b_glu.reshape(1, ATT_WIDTH))


def _pack_state(s):
    n = s.shape[0]
    packed = s.transpose(0, 2, 1, 3).reshape(n, -1)
    swapped = jnp.stack([s[:, 1], s[:, 0]], axis=1).transpose(0, 2, 1, 3).reshape(n, -1)
    return packed, swapped


def _unpack_state(x):
    n = x.shape[0]
    return x.reshape(n, -1, 2, S5_STATE).transpose(0, 2, 1, 3)


PROMPT_TILE = 512
Q_PAD = 16


def _pad_rows(t, n):
    return jnp.pad(t, ((0, 0), (0, n - t.shape[1]), (0, 0)))


def kernel(x_prompt, x_sample, cache_win0_kv, cache_win1_kv, cache_win2_kv, cache_mem_kv, state_s5,
           state_ffn_conv, mem_prompt, g_mix, g_ffn, w_in_a, g_q_dil, g_k_dil, w_in_b, s5_lam_re,
           s5_lam_im, s5_log_dt, s5_b_re, s5_b_im, s5_c_re, s5_c_im, s5_d, w_glu, b_glu, g_mem,
           w_mem_kv, g_q_cross, g_k_cross, w_out, w_up, conv_w, conv_b, w_down):
    nb, seq, d = x_prompt.shape
    db, ts, _ = x_sample.shape
    depth = g_mix.shape[0]
    n_mem = mem_prompt.shape[1]
    d_ff2 = w_up.shape[2]
    assert ts >= CONV_W - 1 and seq % PROMPT_TILE == 0 and ts <= Q_PAD
    caches = (cache_win0_kv, cache_win1_kv, cache_win2_kv)
    w_mem_bf = w_mem_kv.astype(BF16)
    mem_cache = cache_mem_kv.reshape(depth * db, n_mem * KV_ROWS, HEAD_DIM)

    tab_p = _rope_tables(jnp.arange(seq, dtype=jnp.int32))
    tab_s = tuple(jnp.tile(t, (db, 1)) for t in _rope_tables(PAST_LEN + jnp.arange(ts, dtype=jnp.int32)))
    win_keep = tuple(min(w, seq) for w, _ in DIL_GROUPS)
    rows_s = db * ts
    kv_tail = (2, N_HEADS, HEAD_DIM)

    mem_p = _mem_kv(mem_prompt.reshape(nb * n_mem, d), g_mem, w_mem_bf, g_k_cross)

    xp = x_prompt.reshape(nb * seq, d)
    xs = x_sample.reshape(rows_s, d)
    p_win, s_win = [[] for _ in DIL_GROUPS], [[] for _ in DIL_GROUPS]
    p_s5, s_s5, p_conv, s_conv = [], [], [], []
    tiles_per_seq = seq // PROMPT_TILE
    for i in range(depth):
        mem_i = mem_p[i].reshape(nb, n_mem * KV_ROWS, HEAD_DIM)
        if i % 2 == 0:
            ia = i // 2
            w_in_a_bf = w_in_a[ia].astype(BF16)
            *qkv_groups, w0, w1, w2, cross_p = _in_proj_a(
                xp, seq, PROMPT_TILE, g_mix[i], w_in_a_bf, tab_p, g_q_dil[ia], g_k_dil[ia],
                g_q_cross[i], mem_i, win_keep, dilated=True)
            for g, w in enumerate((w0, w1, w2)):
                p_win[g].append(w.reshape((nb, win_keep[g]) + kv_tail))
            mix_p = _attn_prompt(qkv_groups, nb, seq).reshape(nb * seq, ATT_WIDTH)
            qkv_s, *kv_new, qc_s = _in_proj_a(
                xs, rows_s, rows_s, g_mix[i], w_in_a_bf, tab_s, g_q_dil[ia], g_k_dil[ia],
                None, None, (rows_s,) * N_GROUPS, dilated=False)
            old_bufs = [c[ia].reshape(db, -1, HEAD_DIM) for c in caches]
            moves = (old_bufs, [k.reshape(db, ts * KV_ROWS, HEAD_DIM) for k in kv_new])
        else:
            moves = None
            ib = i // 2
            prm = (s5_lam_re[ib], s5_lam_im[ib], s5_log_dt[ib], s5_b_re[ib], s5_b_im[ib], s5_c_re[ib],
                   s5_c_im[ib], s5_d[ib])
            n_state = s5_lam_re.shape[1] * 2 * S5_STATE
            w_in_b_bf, w_glu_bf = w_in_b[ib].astype(BF16), w_glu[ib].astype(BF16)
            u_p, cross_p = _in_proj_b(xp, seq, PROMPT_TILE, S5_SUB, g_mix[i], w_in_b_bf, g_q_cross[i], mem_i)
            zero = jnp.zeros((nb * CARRY_ROWS, n_state), F32)
            prep_p, prep_s = _s5_prep(*prm, (S5_SUB, ts))
            mix_p, fin_p = _s5_mixer(u_p, seq // S5_SUB, S5_SUB, seq // S5_SUB, zero, zero,
                                     *prep_p, w_glu_bf, b_glu[ib])
            p_s5.append(_unpack_state(fin_p.reshape(nb, CARRY_ROWS, n_state)[:, 0]))
            u_s, qc_s = _in_proj_b(xs, rows_s, rows_s, ts, g_mix[i], w_in_b_bf, None, None)
            mix_s, fin_s = _s5_mixer(u_s, db, ts, 1, *_pack_state(state_s5[ib]), *prep_s, w_glu_bf, b_glu[ib])
            s_s5.append(_unpack_state(fin_s))
        cross_s = _cross_sample(_pad_rows(qc_s.reshape(db, ts, ATT_WIDTH), Q_PAD), g_q_cross[i], mem_cache,
                                first=i * db)
        cross_s = cross_s[:, :ts].reshape(rows_s, ATT_WIDTH)

        ffn_w = (w_out[i].astype(BF16), g_ffn[i], w_up[i].astype(BF16), conv_w[i], conv_b[i],
                 w_down[i].astype(BF16))
        xp, tails, *new_bufs = _out_ffn(xp, mix_p, cross_p, *ffn_w, PROMPT_TILE, tiles_per_seq=tiles_per_seq,
                                        moves=moves)
        p_conv.append(tails.reshape(nb, tiles_per_seq, CARRY_ROWS, d_ff2)[:, -1, CARRY_ROWS - (CONV_W - 1):])
        if moves is not None:
            outs, lses = [], []
            for g, (_, r) in enumerate(DIL_GROUPS):
                q_g = qkv_s[g * N_HEADS:(g + 1) * N_HEADS]
                q_g = _pad_rows(q_g.transpose(1, 0, 2).reshape(db, ts, ATT_WIDTH), Q_PAD)
                o_g, lse_g = _attn_sample(q_g, new_bufs[g], moves[0][g], r, ts)
                outs.append(o_g.reshape(rows_s, ATT_WIDTH))
                lses.append(lse_g.reshape(rows_s, ATT_WIDTH))
                s_win[g].append(new_bufs[g].reshape(caches[g][i // 2].shape))
            mix_s = _combine_groups(outs, lses)
        buf = state_ffn_conv[i]
        zero = jnp.zeros((db, ts - 2, d_ff2), F32)
        e1 = jnp.concatenate([buf[:, 1:2], zero, zero[:, :1]], axis=1).reshape(rows_s, d_ff2)
        e2 = jnp.concatenate([buf, zero], axis=1).reshape(rows_s, d_ff2)
        xs, tails = _out_ffn(xs, mix_s, cross_s, *ffn_w, rows_s, seq_len=ts, e1=e1, e2=e2, tail_rows=rows_s)
        s_conv.append(tails.reshape(db, ts, d_ff2)[:, ts - (CONV_W - 1):])

    return (xp.reshape(nb, seq, d), xs.reshape(db, ts, d),
            jnp.stack(p_win[0]), jnp.stack(p_win[1]), jnp.stack(p_win[2]),
            mem_p.reshape((depth, nb, n_mem) + kv_tail),
            jnp.stack(p_s5), jnp.stack(p_conv),
            jnp.stack(s_win[0]), jnp.stack(s_win[1]), jnp.stack(s_win[2]),
            jnp.stack(s_s5), jnp.stack(s_conv))
```

```python
import functools
import math

import jax
import jax.numpy as jnp
from jax import lax
from jax.experimental import pallas as pl
from jax.experimental.pallas import tpu as pltpu

HEAD_DIM = 128
N_HEADS = 4
DIL_GROUPS = ((128, 1), (512, 4), (2048, 16))
N_GROUPS = len(DIL_GROUPS)
DIL_SPAN = 128
BLOCK = 128
ATT_WIDTH = N_HEADS * HEAD_DIM
KV_ROWS = 2 * N_HEADS
ROT_DIM = HEAD_DIM // 4
ROT_HALF = ROT_DIM // 2
ROPE_THETA = 500000.0
S5_GROUP = 16
S5_STATE = 64
S5_SUB = 16
S5_BLOCK = S5_SUB * S5_GROUP
CONV_W = 3
EPS = 1e-6
NEG = -1e30
SCALE = HEAD_DIM ** -0.5
PAST_LEN = 16384

VMEM_LIMIT_V7X = 56 * 1024 * 1024
BF16 = jnp.bfloat16
F32 = jnp.float32
PREP_PRECISION = lax.Precision.HIGHEST


def _params(*sem):
    return pltpu.CompilerParams(dimension_semantics=sem, vmem_limit_bytes=VMEM_LIMIT_V7X)


def _rms(x, g):
    return x * lax.rsqrt(jnp.mean(x * x, axis=-1, keepdims=True) + EPS) * g


def _rms_head(x, g):
    ones = jnp.ones((HEAD_DIM, HEAD_DIM), BF16)
    ssq = _dot((x * x).astype(BF16), ones)
    return x * lax.rsqrt(ssq * (1.0 / HEAD_DIM) + EPS) * g


def _dot(a, b):
    return jnp.dot(a, b, preferred_element_type=F32)


def _dot_nt(a, b, precision=None):
    return lax.dot_general(a, b, (((1,), (1,)), ((), ())), precision=precision,
                           preferred_element_type=F32)


def _full(shape):
    nd = len(shape)
    return pl.BlockSpec(shape, lambda *_: (0,) * nd)


def _kv_rows(kv, head, n):
    return pl.ds(kv * N_HEADS + head, n, stride=KV_ROWS)


def _mem_kv_kernel(mem_ref, g_ref, w_ref, gk_ref, out_ref):
    tm = mem_ref.shape[0]
    h = _rms(mem_ref[...], g_ref[0]).astype(BF16)
    kv = _dot(h, w_ref[0])
    for hd in range(N_HEADS):
        sl = slice(hd * HEAD_DIM, (hd + 1) * HEAD_DIM)
        out_ref[0, _kv_rows(0, hd, tm), :] = _rms(kv[:, sl], gk_ref[0])
        out_ref[0, _kv_rows(1, hd, tm), :] = kv[:, ATT_WIDTH + hd * HEAD_DIM:ATT_WIDTH + (hd + 1) * HEAD_DIM]


def _mem_kv(mem2d, g_mem, w_kv_bf, g_k):
    depth, d, _ = w_kv_bf.shape
    rows = mem2d.shape[0]
    tm = min(rows, 512)
    return pl.pallas_call(
        _mem_kv_kernel,
        grid=(depth, rows // tm),
        in_specs=[
            pl.BlockSpec((tm, d), lambda l, i: (i, 0)),
            pl.BlockSpec((1, 1, d), lambda l, i: (l, 0, 0)),
            pl.BlockSpec((1, d, 2 * ATT_WIDTH), lambda l, i: (l, 0, 0)),
            pl.BlockSpec((1, 1, HEAD_DIM), lambda l, i: (l, 0, 0)),
        ],
        out_specs=pl.BlockSpec((1, tm * KV_ROWS, HEAD_DIM), lambda l, i: (l, i, 0)),
        out_shape=jax.ShapeDtypeStruct((depth, rows * KV_ROWS, HEAD_DIM), F32),
        compiler_params=_params("arbitrary", "arbitrary"),
        name="mem_kv",
    )(mem2d, g_mem.reshape(depth, 1, d), w_kv_bf, g_k.reshape(depth, 1, HEAD_DIM))


def _cross_heads(qc, mem_ref, gq):
    n_mem = mem_ref.shape[1] // KV_ROWS
    outs = []
    for hd in range(N_HEADS):
        q = _rms(qc[:, hd * HEAD_DIM:(hd + 1) * HEAD_DIM], gq).astype(BF16)
        k = mem_ref[0, _kv_rows(0, hd, n_mem), :].astype(BF16)
        v = mem_ref[0, _kv_rows(1, hd, n_mem), :].astype(BF16)
        s = _dot_nt(q, k) * SCALE
        m = jnp.max(s, axis=-1, keepdims=True)
        p = jnp.exp(s - m)
        l = jnp.sum(p, axis=-1, keepdims=True)
        outs.append(_dot(p.astype(BF16), v) / l)
    return outs


def _rope(x, cos_t, sin_lo, sin_hi):
    return (x * cos_t + pltpu.roll(x, HEAD_DIM - ROT_HALF, 1) * sin_lo
            + pltpu.roll(x, ROT_HALF, 1) * sin_hi)


def _in_proj_a_kernel(*refs, fuse_cross, dilated, win_rows, win_first, tiles_per_seq):
    n_in = 10 if fuse_cross else 8
    x_ref, g_ref, w_ref, cos_ref, slo_ref, shi_ref, gq_ref, gk_ref = refs[:8]
    n_qkv = N_GROUPS if dilated else 1
    qkv_refs = refs[n_in:n_in + n_qkv]
    win_refs = refs[n_in + n_qkv:n_in + n_qkv + N_GROUPS]
    cr_ref = refs[n_in + n_qkv + N_GROUPS]
    kv_stash = refs[n_in + n_qkv + N_GROUPS + 1]
    stage = refs[n_in + n_qkv + N_GROUPS + 2] if dilated else None
    tm = x_ref.shape[0]
    h = _rms(x_ref[...], g_ref[...]).astype(BF16)
    cos_t, sin_lo, sin_hi = cos_ref[...], slo_ref[...], shi_ref[...]
    def proj(c):
        return _dot(h, w_ref[:, c * ATT_WIDTH:(c + 1) * ATT_WIDTH])

    tile = pl.program_id(0) % tiles_per_seq
    in_window = [tile >= first for first in win_first]
    for g in range(N_GROUPS):
        @pl.when(jnp.logical_not(in_window[g]))
        def _(g=g):
            win_refs[g][...] = jnp.zeros_like(win_refs[g])

    y = proj(0)
    for c in range(3 * N_GROUPS):
        y_next = proj(c + 1)
        role, g = divmod(c, N_GROUPS)
        r = DIL_GROUPS[g][1]
        wr = win_rows[g]
        for hd in range(N_HEADS):
            yh = y[:, hd * HEAD_DIM:(hd + 1) * HEAD_DIM]
            if role == 0:
                yh = _rope(_rms_head(yh, gq_ref[g]), cos_t, sin_lo, sin_hi)
            elif role == 1:
                yh = _rope(_rms_head(yh, gk_ref[g]), cos_t, sin_lo, sin_hi)
            if role > 0:
                kv_stash[((role - 1) * N_GROUPS + g) * N_HEADS + hd] = yh
            if not dilated:
                qkv_refs[0][role * N_GROUPS * N_HEADS + g * N_HEADS + hd] = yh.astype(BF16)
            elif r == 1:
                qkv_refs[g][role * N_HEADS + hd, 0] = yh.astype(BF16)
            else:
                stage[hd] = yh
                for rho in range(r):
                    qkv_refs[g][role * N_HEADS + hd, 0, :, rho * HEAD_DIM:(rho + 1) * HEAD_DIM] = (
                        stage[hd, pl.ds(rho, tm // r, stride=r), :].astype(BF16))
        y = y_next
    qc = y
    if fuse_cross:
        outs = _cross_heads(qc, refs[9], refs[8][...])
        for hd in range(N_HEADS):
            cr_ref[:, hd * HEAD_DIM:(hd + 1) * HEAD_DIM] = outs[hd].astype(cr_ref.dtype)
    else:
        cr_ref[...] = qc
    for g in range(N_GROUPS):
        @pl.when(in_window[g])
        def _(g=g):
            wr = win_rows[g]
            for kv in range(2):
                for hd in range(N_HEADS):
                    win_refs[g][_kv_rows(kv, hd, wr), :] = kv_stash[(kv * N_GROUPS + g) * N_HEADS + hd,
                                                                    tm - wr:, :]


SLOTS = HEAD_DIM // S5_GROUP


def _slot_transpose(vs):
    slot = lax.broadcasted_iota(jnp.int32, vs[0].shape, 1) // S5_GROUP
    d = SLOTS // 2
    while d:
        low = (slot & d) == 0
        new = list(vs)
        for i in range(SLOTS):
            if not i & d:
                new[i] = jnp.where(low, vs[i], pltpu.roll(vs[i | d], d * S5_GROUP, 1))
                new[i | d] = jnp.where(low, pltpu.roll(vs[i], HEAD_DIM - d * S5_GROUP, 1), vs[i | d])
        vs = new
        d //= 2
    return vs


def _in_proj_b_kernel(*refs, fuse_cross, n_valid):
    if fuse_cross:
        x_ref, g_ref, w_ref, gqc_ref, mem_ref, u_ref, cr_ref, stage = refs
    else:
        x_ref, g_ref, w_ref, u_ref, cr_ref, stage = refs
    tm = x_ref.shape[0]
    n_rows = tm // n_valid
    h = _rms(x_ref[...], g_ref[...]).astype(BF16)
    u = _dot(h, w_ref[:, :ATT_WIDTH])
    zero = jnp.zeros((n_rows, HEAD_DIM), F32)
    for cb in range(ATT_WIDTH // HEAD_DIM):
        stage[cb] = u[:, cb * HEAD_DIM:(cb + 1) * HEAD_DIM]
        for half in range(S5_SUB // SLOTS):
            taus = range(half * SLOTS, (half + 1) * SLOTS)
            groups = _slot_transpose([stage[cb, pl.ds(tau, n_rows, stride=n_valid), :] if tau < n_valid else zero
                                      for tau in taus])
            for p in range(SLOTS):
                lanes = (2 * (cb * SLOTS + p) + half) * HEAD_DIM
                u_ref[:, lanes:lanes + HEAD_DIM] = groups[p].astype(BF16)
    qc = _dot(h, w_ref[:, ATT_WIDTH:])
    if fuse_cross:
        outs = _cross_heads(qc, mem_ref, gqc_ref[...])
        for hd in range(N_HEADS):
            cr_ref[:, hd * HEAD_DIM:(hd + 1) * HEAD_DIM] = outs[hd].astype(cr_ref.dtype)
    else:
        cr_ref[...] = qc


def _rope_tables(pos):
    inv = jnp.exp(-math.log(ROPE_THETA) * jnp.arange(ROT_HALF, dtype=F32) / ROT_HALF)
    ang = pos.astype(F32)[:, None] * inv[None, :]
    cos, sin = jnp.cos(ang), jnp.sin(ang)
    rows = pos.shape[0]
    ones = jnp.ones((rows, HEAD_DIM - ROT_DIM), F32)
    zeros = jnp.zeros((rows, HEAD_DIM - ROT_DIM), F32)
    z16 = jnp.zeros((rows, ROT_HALF), F32)
    cos_t = jnp.concatenate([cos, cos, ones], axis=1)
    sin_lo = jnp.concatenate([-sin, z16, zeros], axis=1)
    sin_hi = jnp.concatenate([z16, sin, zeros], axis=1)
    return cos_t, sin_lo, sin_hi


def _in_proj_a(x2d, seq, tm, g_mix, w_bf, tables, g_q, g_k, g_qc, mem_kv, win_keep, dilated):
    rows, d = x2d.shape
    n_in = w_bf.shape[1]
    tiles_per_seq = seq // tm
    n_seq = rows // seq
    fuse_cross = mem_kv is not None
    tab_tiles = tables[0].shape[0] // tm
    win_rows = tuple(min(k, tm) for k in win_keep)
    win_first = tuple(tiles_per_seq - k // wr for k, wr in zip(win_keep, win_rows))

    def win_spec(keep):
        wr = min(keep, tm)
        nblk = keep // wr
        first = tiles_per_seq - nblk

        def imap(i):
            b = i // tiles_per_seq
            t = i % tiles_per_seq
            return (b * nblk + jnp.maximum(t - first, 0), 0)
        return pl.BlockSpec((wr * KV_ROWS, HEAD_DIM), imap)

    in_specs = [
        pl.BlockSpec((tm, d), lambda i: (i, 0)),
        _full((1, d)),
        _full((d, n_in)),
        pl.BlockSpec((tm, HEAD_DIM), lambda i: (i % tab_tiles, 0)),
        pl.BlockSpec((tm, HEAD_DIM), lambda i: (i % tab_tiles, 0)),
        pl.BlockSpec((tm, HEAD_DIM), lambda i: (i % tab_tiles, 0)),
        _full((N_GROUPS, HEAD_DIM)),
        _full((N_GROUPS, HEAD_DIM)),
    ]
    args = [x2d, g_mix.reshape(1, d), w_bf, *tables, g_q, g_k]
    if fuse_cross:
        in_specs += [_full((1, HEAD_DIM)),
                     pl.BlockSpec((1,) + mem_kv.shape[1:], lambda i: (i // tiles_per_seq, 0, 0))]
        args += [g_qc.reshape(1, HEAD_DIM), mem_kv]
    if dilated:
        qkv_specs = [pl.BlockSpec((3 * N_HEADS, 1, tm // r, r * HEAD_DIM),
                                  lambda i: (0, i // tiles_per_seq, i % tiles_per_seq, 0))
                     for _, r in DIL_GROUPS]
        qkv_shapes = [jax.ShapeDtypeStruct((3 * N_HEADS, n_seq, seq // r, r * HEAD_DIM), BF16)
                      for _, r in DIL_GROUPS]
        scratch = [pltpu.VMEM((N_HEADS, tm, HEAD_DIM), F32)]
    else:
        qkv_specs = [pl.BlockSpec((3 * N_GROUPS * N_HEADS, tm, HEAD_DIM), lambda i: (0, i, 0))]
        qkv_shapes = [jax.ShapeDtypeStruct((3 * N_GROUPS * N_HEADS, rows, HEAD_DIM), BF16)]
        scratch = []
    scratch = [pltpu.VMEM((2 * N_GROUPS * N_HEADS, tm, HEAD_DIM), F32)] + scratch
    out_specs = [*qkv_specs, *[win_spec(k) for k in win_keep],
                 pl.BlockSpec((tm, ATT_WIDTH), lambda i: (i, 0))]
    out_shape = [*qkv_shapes,
                 *[jax.ShapeDtypeStruct((n_seq * k * KV_ROWS, HEAD_DIM), F32) for k in win_keep],
                 jax.ShapeDtypeStruct((rows, ATT_WIDTH), BF16 if fuse_cross else F32)]
    return pl.pallas_call(
        functools.partial(_in_proj_a_kernel, fuse_cross=fuse_cross, dilated=dilated, win_rows=win_rows,
                          win_first=win_first, tiles_per_seq=tiles_per_seq),
        grid=(rows // tm,),
        in_specs=in_specs, out_specs=out_specs, out_shape=out_shape, scratch_shapes=scratch,
        compiler_params=_params("arbitrary"),
        name="in_proj_a",
    )(*args)


def _in_proj_b(x2d, seq, tm, n_valid, g_mix, w_bf, g_qc, mem_kv):
    rows, d = x2d.shape
    width = (ATT_WIDTH // S5_GROUP) * S5_BLOCK
    n_in = w_bf.shape[1]
    tiles_per_seq = seq // tm
    fuse_cross = mem_kv is not None
    in_specs = [pl.BlockSpec((tm, d), lambda i: (i, 0)), _full((1, d)), _full((d, n_in))]
    args = [x2d, g_mix.reshape(1, d), w_bf]
    if fuse_cross:
        in_specs += [_full((1, HEAD_DIM)),
                     pl.BlockSpec((1,) + mem_kv.shape[1:], lambda i: (i // tiles_per_seq, 0, 0))]
        args += [g_qc.reshape(1, HEAD_DIM), mem_kv]
    return pl.pallas_call(
        functools.partial(_in_proj_b_kernel, fuse_cross=fuse_cross, n_valid=n_valid),
        grid=(rows // tm,),
        in_specs=in_specs,
        out_specs=[pl.BlockSpec((tm // n_valid, width), lambda i: (i, 0)),
                   pl.BlockSpec((tm, ATT_WIDTH), lambda i: (i, 0))],
        out_shape=[jax.ShapeDtypeStruct((rows // n_valid, width), BF16),
                   jax.ShapeDtypeStruct((rows, ATT_WIDTH), BF16 if fuse_cross else F32)],
        scratch_shapes=[pltpu.VMEM((ATT_WIDTH // HEAD_DIM, tm, HEAD_DIM), F32)],
        compiler_params=_params("arbitrary"),
        name="in_proj_b",
    )(*args)


CROSS_BATCH = 4


def _cross_sample_kernel(qc_ref, gq_ref, mem_ref, out_ref):
    for b in range(qc_ref.shape[0]):
        outs = _cross_heads(qc_ref[b], mem_ref.at[pl.ds(b, 1)], gq_ref[...])
        for hd in range(N_HEADS):
            out_ref[b, :, hd * HEAD_DIM:(hd + 1) * HEAD_DIM] = outs[hd].astype(out_ref.dtype)


def _cross_sample(qc, g_qc, mem_kv, first):
    nb, tq, _ = qc.shape
    cb = math.gcd(nb, CROSS_BATCH)
    assert first % cb == 0
    return pl.pallas_call(
        _cross_sample_kernel,
        grid=(nb // cb,),
        in_specs=[pl.BlockSpec((cb, tq, ATT_WIDTH), lambda b: (b, 0, 0)),
                  _full((1, HEAD_DIM)),
                  pl.BlockSpec((cb,) + mem_kv.shape[1:], lambda b: (first // cb + b, 0, 0))],
        out_specs=pl.BlockSpec((cb, tq, ATT_WIDTH), lambda b: (b, 0, 0)),
        out_shape=jax.ShapeDtypeStruct((nb, tq, ATT_WIDTH), BF16),
        compiler_params=_params("arbitrary"),
        name="cross_sample",
    )(qc, g_qc.reshape(1, HEAD_DIM), mem_kv)


INFLIGHT = 4


def _band_block(q, k, v):
    n = k.shape[0]
    dist = (n - BLOCK + lax.broadcasted_iota(jnp.int32, (BLOCK, n), 0)
            - lax.broadcasted_iota(jnp.int32, (BLOCK, n), 1))
    s = jnp.where((dist >= 0) & (dist <= DIL_SPAN), _dot_nt(q, k) * SCALE, NEG)
    m = jnp.max(s, axis=-1, keepdims=True)
    p = jnp.exp(s - m)
    l = jnp.sum(p, axis=-1, keepdims=True)
    return _dot(p.astype(BF16), v) / l, m + jnp.log(l)


def _attn_prompt_kernel(q0, q1, q2, k0, k1, k2, v0, v1, v2, out_ref, o_scr, l_scr, *, seq):
    qkv_refs = ((q0, k0, v0), (q1, k1, v1), (q2, k2, v2))
    for g, (_, r) in enumerate(DIL_GROUPS):
        q_ref, k_ref, v_ref = qkv_refs[g]
        nblk = seq // r // BLOCK

        def store(rho, blk, o, lse, g=g, r=r):
            start = blk * (BLOCK * r) + rho
            if r == 1:
                idx = pl.ds(pl.multiple_of(start, BLOCK), BLOCK)
            else:
                idx = pl.ds(start, BLOCK, stride=r)
            o_scr[g, idx, :] = o
            l_scr[g, idx, :] = jnp.broadcast_to(lse, (BLOCK, HEAD_DIM))

        def first(rho, q_ref=q_ref, k_ref=k_ref, v_ref=v_ref, store=store):
            lanes = slice(rho * HEAD_DIM, (rho + 1) * HEAD_DIM)
            o, lse = _band_block(q_ref[0, 0, :BLOCK, lanes], k_ref[0, 0, :BLOCK, lanes],
                                 v_ref[0, 0, :BLOCK, lanes])
            store(rho, 0, o, lse)

        def later(rho, blk, q_ref=q_ref, k_ref=k_ref, v_ref=v_ref, store=store):
            lanes = slice(rho * HEAD_DIM, (rho + 1) * HEAD_DIM)
            cur = pl.ds(pl.multiple_of(blk * BLOCK, BLOCK), BLOCK)
            both = pl.ds(pl.multiple_of((blk - 1) * BLOCK, BLOCK), 2 * BLOCK)
            o, lse = _band_block(q_ref[0, 0, cur, lanes], k_ref[0, 0, both, lanes], v_ref[0, 0, both, lanes])
            store(rho, blk, o, lse)

        for rho0 in range(0, r, INFLIGHT):
            rhos = range(rho0, min(r, rho0 + INFLIGHT))
            for rho in rhos:
                first(rho)

            def body(blk, carry, rhos=rhos, later=later):
                for rho in rhos:
                    later(rho, blk)
                return carry
            if nblk > 1:
                lax.fori_loop(1, nblk, body, 0, unroll=max(1, INFLIGHT // len(rhos)))

    def combine(c, carry):
        rows = pl.ds(pl.multiple_of(c * BLOCK, BLOCK), BLOCK)
        ls = [l_scr[g, rows, :] for g in range(N_GROUPS)]
        m = jnp.maximum(jnp.maximum(ls[0], ls[1]), ls[2])
        es = [jnp.exp(l - m) for l in ls]
        num = es[0] * o_scr[0, rows, :] + es[1] * o_scr[1, rows, :] + es[2] * o_scr[2, rows, :]
        out_ref[0, rows, :] = (num / (es[0] + es[1] + es[2])).astype(out_ref.dtype)
        return carry
    lax.fori_loop(0, seq // BLOCK, combine, 0)


def _attn_prompt(qkv_groups, n_seq, seq):
    in_specs, args = [], []
    for role in range(3):
        for g, (_, r) in enumerate(DIL_GROUPS):
            in_specs.append(pl.BlockSpec((1, 1, seq // r, r * HEAD_DIM),
                                         lambda b, h, role=role: (role * N_HEADS + h, b, 0, 0)))
            args.append(qkv_groups[g])
    return pl.pallas_call(
        functools.partial(_attn_prompt_kernel, seq=seq),
        grid=(n_seq, N_HEADS),
        in_specs=in_specs,
        out_specs=pl.BlockSpec((1, seq, HEAD_DIM), lambda b, h: (b, 0, h)),
        out_shape=jax.ShapeDtypeStruct((n_seq, seq, ATT_WIDTH), BF16),
        scratch_shapes=[pltpu.VMEM((N_GROUPS, seq, HEAD_DIM), F32),
                        pltpu.VMEM((N_GROUPS, seq, HEAD_DIM), F32)],
        compiler_params=_params("arbitrary", "arbitrary"),
        name="attn_prompt",
    )(*args)


def _attn_sample_kernel(q_ref, new_ref, old_ref, o_ref, lse_ref, *scratch, r, t_new, lb, compact):
    qf = q_ref[0].astype(F32)
    if compact:
        (flat,) = scratch
        n_key = lb // r
        for u in range(t_new):
            flat[u] = new_ref[:, u * KV_ROWS:(u + 1) * KV_ROWS, :].reshape(n_key * KV_ROWS, HEAD_DIM)
    else:
        n_key = lb
    i_key = lax.broadcasted_iota(jnp.int32, (n_key, 1), 0)
    j_old = lax.broadcasted_iota(jnp.int32, (t_new, 1), 0)
    for t in range(t_new):
        if compact:
            idx = i_key * r + (r - t_new + t)
            rows = lambda kvh, t=t: flat[t, pl.ds(kvh, n_key, stride=KV_ROWS), :]
        else:
            idx = i_key
            rows = lambda kvh: new_ref[pl.ds(kvh, n_key, stride=KV_ROWS), :]
        dist = lb - t_new + t - idx
        ok = (dist >= 0) & (dist % r == 0) & (dist <= r * DIL_SPAN)
        dist_old = lb + t - j_old
        ok_old = (dist_old % r == 0) & (dist_old <= r * DIL_SPAN)
        for hd in range(N_HEADS):
            sl = slice(hd * HEAD_DIM, (hd + 1) * HEAD_DIM)
            q = qf[t:t + 1, sl]
            k, v = rows(hd), rows(N_HEADS + hd)
            k_old = old_ref[pl.ds(hd, t_new, stride=KV_ROWS), :]
            v_old = old_ref[pl.ds(N_HEADS + hd, t_new, stride=KV_ROWS), :]
            s = jnp.where(ok, jnp.sum(k * q, axis=-1, keepdims=True) * SCALE, NEG)
            s_old = jnp.where(ok_old, jnp.sum(k_old * q, axis=-1, keepdims=True) * SCALE, NEG)
            m = jnp.maximum(jnp.max(s, axis=0, keepdims=True), jnp.max(s_old, axis=0, keepdims=True))
            p, p_old = jnp.exp(s - m), jnp.exp(s_old - m)
            l = jnp.sum(p, axis=0, keepdims=True) + jnp.sum(p_old, axis=0, keepdims=True)
            o = jnp.sum(p * v, axis=0, keepdims=True) + jnp.sum(p_old * v_old, axis=0, keepdims=True)
            o_ref[0, t:t + 1, sl] = o / l
            lse_ref[0, t:t + 1, sl] = jnp.broadcast_to(m + jnp.log(l), (1, HEAD_DIM))


def _attn_sample(q, new_buf, old_buf, r, t_new):
    nb, tq, _ = q.shape
    lb = new_buf.shape[1] // KV_ROWS
    compact = r % t_new == 0
    if compact:
        new_view = new_buf.reshape(nb, lb // r, r // t_new, t_new * KV_ROWS, HEAD_DIM)
        new_spec = pl.BlockSpec((None, lb // r, None, t_new * KV_ROWS, HEAD_DIM),
                                lambda b: (b, 0, r // t_new - 1, 0, 0))
        scratch = [pltpu.VMEM((t_new, lb // r * KV_ROWS, HEAD_DIM), F32)]
    else:
        new_view = new_buf
        new_spec = pl.BlockSpec((None, lb * KV_ROWS, HEAD_DIM), lambda b: (b, 0, 0))
        scratch = []
    return pl.pallas_call(
        functools.partial(_attn_sample_kernel, r=r, t_new=t_new, lb=lb, compact=compact),
        grid=(nb,),
        in_specs=[pl.BlockSpec((1, tq, ATT_WIDTH), lambda b: (b, 0, 0)),
                  new_spec,
                  pl.BlockSpec((None, t_new * KV_ROWS, HEAD_DIM), lambda b: (b, 0, 0))],
        out_specs=[pl.BlockSpec((1, t_new, ATT_WIDTH), lambda b: (b, 0, 0)),
                   pl.BlockSpec((1, t_new, ATT_WIDTH), lambda b: (b, 0, 0))],
        out_shape=[jax.ShapeDtypeStruct((nb, t_new, ATT_WIDTH), F32),
                   jax.ShapeDtypeStruct((nb, t_new, ATT_WIDTH), F32)],
        scratch_shapes=scratch,
        compiler_params=_params("arbitrary"),
        name=f"attn_sample_r{r}",
    )(q, new_view, old_buf)


def _combine_kernel(o0, o1, o2, l0, l1, l2, out_ref):
    ls = [l0[...], l1[...], l2[...]]
    m = jnp.maximum(jnp.maximum(ls[0], ls[1]), ls[2])
    es = [jnp.exp(l - m) for l in ls]
    num = es[0] * o0[...] + es[1] * o1[...] + es[2] * o2[...]
    out_ref[...] = (num / (es[0] + es[1] + es[2])).astype(out_ref.dtype)


def _combine_groups(outs, lses):
    shape = outs[0].shape
    return pl.pallas_call(
        _combine_kernel,
        in_specs=[_full(shape)] * 6,
        out_specs=_full(shape),
        out_shape=jax.ShapeDtypeStruct(shape, BF16),
        grid=(1,),
        compiler_params=_params("arbitrary"),
        name="combine_groups",
    )(*outs, *lses)


FF_CHUNK = 256
CARRY_ROWS = 8
SHIFT_SLOTS = 4


MOVE_ROWS = 8192


def _move_regions(keeps):
    regions = [(g, start, min(MOVE_ROWS, keep - start)) for g, keep in enumerate(keeps)
               for start in range(0, keep, MOVE_ROWS)]
    halves = ([], [])
    for reg in sorted(regions, key=lambda reg: -reg[2]):
        min(halves, key=lambda h: sum(r[2] for r in h)).append(reg)
    return halves


def _buffer_move(old_refs, kvn_refs, new_refs, bufs, tails, sems, b, n_steps):
    n_new = [k.shape[1] for k in kvn_refs]
    halves = _move_regions([o.shape[1] - n for o, n in zip(old_refs, n_new)])

    def copies(batch):
        out, sem = [], 0
        for buf, half in zip(bufs, halves):
            row, cps = 0, []
            for g, start, size in half:
                stage = buf.at[pl.ds(row, size)]
                cps.append((pltpu.make_async_copy(old_refs[g].at[batch, pl.ds(n_new[g] + start, size)], stage,
                                                  sems.at[sem]),
                            pltpu.make_async_copy(stage, new_refs[g].at[batch, pl.ds(start, size)],
                                                  sems.at[sem + 1])))
                row += size
                sem += 2
            out.append(cps)
        tail = []
        for g in range(N_GROUPS):
            keep = old_refs[g].shape[1] - n_new[g]
            tail.append((pltpu.make_async_copy(kvn_refs[g].at[batch], tails.at[g], sems.at[sem]),
                         pltpu.make_async_copy(tails.at[g], new_refs[g].at[batch, pl.ds(keep, n_new[g])],
                                               sems.at[sem + 1])))
            sem += 2
        return out[0], out[1], tail
    first, second, tail = copies(b)
    _, prev_second, _ = copies(jnp.maximum(b - 1, 0))

    def top():
        for cp_in, _ in first + tail:
            cp_in.start()

    def mid():
        for cp_in, cp_out in first + tail:
            cp_in.wait()
            cp_out.start()

        @pl.when(b > 0)
        def _():
            for _, cp_out in prev_second:
                cp_out.wait()
        for cp_in, _ in second:
            cp_in.start()

    def end():
        for cp_in, cp_out in second:
            cp_in.wait()
            cp_out.start()
        for _, cp_out in first + tail:
            cp_out.wait()

        @pl.when(b == n_steps - 1)
        def _():
            for _, cp_out in second:
                cp_out.wait()
    return top, mid, end


def _move_scratch(old, new_rows):
    halves = _move_regions([o.shape[1] - n.shape[1] for o, n in zip(old, new_rows)])
    n_copies = sum(len(h) for h in halves) + N_GROUPS
    return ([pltpu.VMEM((sum(r[2] for r in h), HEAD_DIM), F32) for h in halves]
            + [pltpu.VMEM((N_GROUPS,) + new_rows[0].shape[1:], F32), pltpu.SemaphoreType.DMA((2 * n_copies,))])


def _out_ffn_kernel(*refs, seq_len, tiles_per_seq, tail_rows, move):
    move_top = move_mid = move_end = lambda: None
    if move:
        (x_ref, mix_ref, cr_ref, wo_ref, g_ref, wup_ref, cw_ref, cb_ref, wdn_ref, o0, o1, o2, k0, k1, k2,
         out_ref, tail_ref, n0, n1, n2, shift, act_scr, carry, buf_a, buf_b, tails, sems) = refs
        move_top, move_mid, move_end = _buffer_move(
            (o0, o1, o2), (k0, k1, k2), (n0, n1, n2), (buf_a, buf_b), tails, sems,
            pl.program_id(0), pl.num_programs(0))
    elif seq_len is None:
        (x_ref, mix_ref, cr_ref, wo_ref, g_ref, wup_ref, cw_ref, cb_ref, wdn_ref,
         out_ref, tail_ref, shift, act_scr, carry) = refs
    else:
        (x_ref, mix_ref, cr_ref, wo_ref, g_ref, wup_ref, cw_ref, cb_ref, wdn_ref, e1_ref, e2_ref,
         out_ref, tail_ref, shift, act_scr) = refs
    tm = x_ref.shape[0]
    d_ff = wdn_ref.shape[0]
    move_top()
    x1 = x_ref[...] + _dot(jnp.concatenate([mix_ref[...], cr_ref[...]], axis=-1), wo_ref[...])
    h = _rms(x1, g_ref[...]).astype(BF16)
    if seq_len is None:
        @pl.when(pl.program_id(0) % tiles_per_seq == 0)
        def _():
            carry[...] = jnp.zeros_like(carry)
    else:
        t = lax.broadcasted_iota(jnp.int32, (tm, 1), 0) % seq_len
        has1 = t >= 1
        has2 = t >= 2

    def chunk_cols(j):
        return (slice(j * FF_CHUNK, (j + 1) * FF_CHUNK),
                slice(d_ff + j * FF_CHUNK, d_ff + (j + 1) * FF_CHUNK))

    def up_proj(j):
        return tuple(_dot(h, wup_ref[:, cols]) for cols in chunk_cols(j))

    def conv(up, cols, slot):
        buf = shift.at[slot]
        if seq_len is None:
            buf[:CARRY_ROWS, :] = carry[:, cols]
            carry[:, cols] = up[tm - CARRY_ROWS:, :]
        else:
            buf[:CARRY_ROWS, :] = jnp.zeros((CARRY_ROWS, FF_CHUNK), F32)
        buf[CARRY_ROWS:, :] = up
        tail_ref[0, :, cols] = up[tm - tail_rows:, :]
        prev1 = buf[CARRY_ROWS - 1:CARRY_ROWS - 1 + tm, :]
        prev2 = buf[CARRY_ROWS - 2:CARRY_ROWS - 2 + tm, :]
        if seq_len is not None:
            prev1 = jnp.where(has1, prev1, e1_ref[:, cols])
            prev2 = jnp.where(has2, prev2, e2_ref[:, cols])
        return (cb_ref[:, cols] + cw_ref[0:1, cols] * prev2 + cw_ref[1:2, cols] * prev1
                + cw_ref[2:3, cols] * up)

    n_chunks = d_ff // FF_CHUNK
    ups = up_proj(0)
    for j in range(n_chunks):
        nxt = up_proj(j + 1) if j + 1 < n_chunks else None
        a, b = (conv(up, cols, 2 * (j % 2) + s) for s, (up, cols) in enumerate(zip(ups, chunk_cols(j))))
        act_scr[:, j * FF_CHUNK:(j + 1) * FF_CHUNK] = (a * jax.nn.sigmoid(a) * b).astype(BF16)
        ups = nxt
        if j == n_chunks // 2:
            move_mid()
    out_ref[...] = x1 + _dot(act_scr[...], wdn_ref[...])
    move_end()


def _layer_spec(stacked, layer):
    nd = stacked.ndim - 1
    return pl.BlockSpec((None,) + stacked.shape[1:], lambda *_: (layer,) + (0,) * nd,
                        pipeline_mode=pl.Buffered(1))


def _out_ffn(x2d, mix, cross, layer, w_out_bf, g_ffn, w_up_bf, conv_w, conv_b, w_down_bf, tm,
             tiles_per_seq=None, seq_len=None, e1=None, e2=None, tail_rows=CARRY_ROWS, moves=None):
    rows, d = x2d.shape
    depth, d_ff, _ = w_down_bf.shape
    n_tiles = rows // tm
    args = [x2d, mix, cross, w_out_bf, g_ffn.reshape(depth, 1, d), w_up_bf, conv_w,
            conv_b.reshape(depth, 1, 2 * d_ff), w_down_bf]
    in_specs = [
        pl.BlockSpec((tm, d), lambda i: (i, 0)),
        pl.BlockSpec((tm, ATT_WIDTH), lambda i: (i, 0)),
        pl.BlockSpec((tm, ATT_WIDTH), lambda i: (i, 0)),
        *[_layer_spec(a, layer) for a in args[3:]],
    ]
    scratch = [pltpu.VMEM((SHIFT_SLOTS, CARRY_ROWS + tm, FF_CHUNK), F32), pltpu.VMEM((tm, d_ff), BF16)]
    if seq_len is None:
        scratch += [pltpu.VMEM((CARRY_ROWS, 2 * d_ff), F32)]
    else:
        in_specs += [pl.BlockSpec((tm, 2 * d_ff), lambda i: (i, 0))] * 2
        args += [e1, e2]
    out_specs = [pl.BlockSpec((tm, d), lambda i: (i, 0)),
                 pl.BlockSpec((1, tail_rows, 2 * d_ff), lambda i: (i, 0, 0))]
    out_shape = [jax.ShapeDtypeStruct((rows, d), F32),
                 jax.ShapeDtypeStruct((n_tiles, tail_rows, 2 * d_ff), F32)]
    if moves is not None:
        old, new_rows = moves
        assert seq_len is None and all(o.shape[0] == n_tiles for o in old)
        any_spec = pl.BlockSpec(memory_space=pl.ANY)
        in_specs += [any_spec] * (2 * N_GROUPS)
        args += [*old, *new_rows]
        out_specs += [any_spec] * N_GROUPS
        out_shape += [jax.ShapeDtypeStruct(o.shape, o.dtype) for o in old]
        scratch += _move_scratch(old, new_rows)
    return pl.pallas_call(
        functools.partial(_out_ffn_kernel, seq_len=seq_len, tiles_per_seq=tiles_per_seq,
                          tail_rows=tail_rows, move=moves is not None),
        grid=(n_tiles,),
        in_specs=in_specs,
        out_specs=out_specs,
        out_shape=out_shape,
        scratch_shapes=scratch,
        compiler_params=_params("arbitrary"),
        name="out_ffn",
    )(*args)


def _swap_halves(x):
    return pltpu.roll(x, S5_STATE, 1)


def _s5_prep_kernel(lam_ref, logdt_ref, bt_ref, c_ref, d_ref, t_ref, cpt_ref, *var_refs, n_valids):
    gc = lam_ref.shape[0]
    n_groups = gc // S5_GROUP
    lane = lax.broadcasted_iota(jnp.int32, lam_ref.shape, 1)
    first = lane < S5_STATE
    sign = jnp.where(first, -1.0, 1.0)
    a = lam_ref[...]
    a_sw = _swap_halves(a)
    are = jnp.where(first, a, a_sw)
    aim = jnp.where(first, a_sw, a)
    dt = jnp.exp(logdt_ref[...])
    mag = jnp.exp(are * dt)
    lr = mag * jnp.cos(aim * dt)
    li = mag * jnp.sin(aim * dt)
    den = are * are + aim * aim
    xr = lr - 1.0
    f_re = (xr * are + li * aim) / den
    f_im = (li * are - xr * aim) / den
    lb = sign * li

    def cmul(x, m_re, m_sw):
        return x * m_re + _swap_halves(x) * m_sw

    c = c_ref[...]
    c_neg = c * -sign
    ri = lax.broadcasted_iota(jnp.int32, (gc, gc), 0)
    ci = lax.broadcasted_iota(jnp.int32, (gc, gc), 1)
    same_group = (ri // S5_GROUP) == (ci // S5_GROUP)
    e = cmul(bt_ref[...], f_re, sign * f_im)
    cl = c
    pw = jnp.where(first, 1.0, 0.0)
    zr = lax.broadcasted_iota(jnp.int32, (gc, S5_BLOCK), 0)
    zc = lax.broadcasted_iota(jnp.int32, (gc, S5_BLOCK), 1)
    same_out = (zr % S5_GROUP) == (zc % S5_GROUP)
    lags = jnp.zeros((gc, S5_BLOCK), F32)
    for k in range(S5_SUB):
        kmat = jnp.where(same_group, _dot_nt(e, c_neg, precision=PREP_PRECISION), 0.0)
        if k == 0:
            kmat = kmat + jnp.where(ri == ci, d_ref[...], 0.0)
        place = jnp.where(same_out & (zc // S5_GROUP == k), 1.0, 0.0)
        lags = lags + jnp.dot(kmat, place, precision=PREP_PRECISION, preferred_element_type=F32)
        both = jnp.concatenate([e, _swap_halves(e)], axis=-1).reshape(n_groups, S5_GROUP, 4 * S5_STATE)
        for n_valid, bp_ref in zip(n_valids, var_refs[0::2]):
            if n_valid - 1 - k >= 0:
                bp_ref[:, n_valid - 1 - k] = both.astype(bp_ref.dtype)
        cl = cmul(cl, lr, lb)
        cpt_ref[:, k] = (cl * -sign).reshape(n_groups, S5_GROUP, 2 * S5_STATE).astype(cpt_ref.dtype)
        e = cmul(e, lr, lb)
        pw = cmul(pw, lr, lb)
        for n_valid, lam_out_ref in zip(n_valids, var_refs[1::2]):
            if k + 1 == n_valid:
                lam_out_ref[...] = pw
    for n_valid, bp_ref in zip(n_valids, var_refs[0::2]):
        for tau in range(n_valid, S5_SUB):
            bp_ref[:, tau] = jnp.zeros((n_groups, S5_GROUP, 4 * S5_STATE), bp_ref.dtype)
    for taup in range(S5_SUB):
        moved = lags if taup == 0 else jnp.where(zc >= taup * S5_GROUP, pltpu.roll(lags, taup * S5_GROUP, 1), 0.0)
        t_ref[:, taup] = moved.reshape(n_groups, S5_GROUP, S5_BLOCK).astype(t_ref.dtype)


def _s5_prep(lam_re, lam_im, log_dt, b_re, b_im, c_re, c_im, d_skip, n_valids):
    n_groups = lam_re.shape[0]
    gc = n_groups * S5_GROUP
    rep = lambda t: jnp.repeat(t, S5_GROUP, axis=0)
    lam_p = rep(jnp.concatenate([lam_re, lam_im], axis=-1))
    logdt = rep(log_dt.reshape(n_groups, 1))
    bt_p = jnp.concatenate([b_re.transpose(0, 2, 1), b_im.transpose(0, 2, 1)], axis=-1).reshape(gc, 2 * S5_STATE)
    c_p = jnp.concatenate([c_re, c_im], axis=-1).reshape(gc, 2 * S5_STATE)
    t_shape = (n_groups, S5_SUB, S5_GROUP, S5_BLOCK)
    cpt_shape = (n_groups, S5_SUB, S5_GROUP, 2 * S5_STATE)
    bp_shape = (n_groups, S5_SUB, S5_GROUP, 4 * S5_STATE)
    lam_shape = (gc, 2 * S5_STATE)
    var = [(bp_shape, BF16), (lam_shape, F32)] * len(n_valids)
    t, cpt, *rest = pl.pallas_call(
        functools.partial(_s5_prep_kernel, n_valids=tuple(n_valids)),
        grid=(1,),
        in_specs=[_full(lam_p.shape), _full(logdt.shape), _full(bt_p.shape), _full(c_p.shape), _full((gc, 1))],
        out_specs=[_full(t_shape), _full(cpt_shape), *[_full(s) for s, _ in var]],
        out_shape=[jax.ShapeDtypeStruct(t_shape, BF16), jax.ShapeDtypeStruct(cpt_shape, BF16),
                   *[jax.ShapeDtypeStruct(s, dt) for s, dt in var]],
        compiler_params=_params("arbitrary"),
        name="s5_prep",
    )(lam_p, logdt, bt_p, c_p, d_skip.reshape(gc, 1))
    merge = lambda a: a.reshape(n_groups, S5_BLOCK, a.shape[-1])
    return [(merge(t), merge(bp), merge(cpt), lam_n[::S5_GROUP]) for bp, lam_n in zip(rest[0::2], rest[1::2])]


SCAN_GROUPS = 8
SUBLANES = 8
GLU_ROWS = 512


def _s5_core_kernel(v_ref, x0_ref, x0s_ref, t_ref, bp_ref, cpt_ref, lam_ref, wglu_ref, bglu_ref,
                    mix_ref, xfin_ref, s_scr, xprev_scr, m_scr, stage, *, n_valid, n_sub):
    rows = v_ref.shape[0]
    n_groups = t_ref.shape[0]
    lane = lax.broadcasted_iota(jnp.int32, (1, 2 * S5_STATE), 1)
    first = lane < S5_STATE
    blk = lambda g: slice(g * S5_BLOCK, (g + 1) * S5_BLOCK)
    tile = lambda i: slice(i * 2 * S5_STATE, (i + 1) * 2 * S5_STATE)
    for g in range(n_groups):
        s_scr[:, blk(g)] = _dot(v_ref[:, blk(g)], bp_ref[g])
        lam = lam_ref[g:g + 1, :]
        lam_sw = _swap_halves(lam)
        m_scr[0:1, tile(g)] = jnp.where(first, lam, lam_sw)
        m_scr[1:2, tile(g)] = jnp.where(first, -lam_sw, lam)
    if n_sub == 1:
        for g in range(n_groups):
            x, xs = x0_ref[:, tile(g)], x0s_ref[:, tile(g)]
            xprev_scr[:, tile(g)] = x
            xfin_ref[:, tile(g)] = (x * m_scr[0:1, tile(g)] + xs * m_scr[1:2, tile(g)]
                                    + s_scr[:, tile(2 * g)])
    else:
        assert rows == n_sub
        for g0 in range(0, n_groups, SCAN_GROUPS):
            gs = range(g0, min(n_groups, g0 + SCAN_GROUPS))

            def body(i, carry, gs=gs):
                rows8 = pl.ds(pl.multiple_of(i * SUBLANES, SUBLANES), SUBLANES)
                sub_i = lax.broadcasted_iota(jnp.int32, (SUBLANES, 2 * S5_STATE), 0)
                out = []
                for g, (x, xs) in zip(gs, carry):
                    s8, ssw8 = s_scr[rows8, tile(2 * g)], s_scr[rows8, tile(2 * g + 1)]
                    prev8 = jnp.zeros((SUBLANES, 2 * S5_STATE), F32)
                    for r in range(SUBLANES):
                        prev8 = jnp.where(sub_i == r, x, prev8)
                        m_re, m_sw = m_scr[0:1, tile(g)], m_scr[1:2, tile(g)]
                        x, xs = (x * m_re + xs * m_sw + s8[r:r + 1, :],
                                 xs * m_re - x * m_sw + ssw8[r:r + 1, :])
                    xprev_scr[rows8, tile(g)] = prev8
                    out.append((x, xs))
                return tuple(out)
            init = tuple((x0_ref[0:1, tile(g)], x0s_ref[0:1, tile(g)]) for g in gs)
            fin = lax.fori_loop(0, n_sub // SUBLANES, body, init)
            for g, (x, _) in zip(gs, fin):
                xfin_ref[:, tile(g)] = jnp.broadcast_to(x, (xfin_ref.shape[0], 2 * S5_STATE))
    for g in range(n_groups):
        s_scr[:, blk(g)] = (_dot(v_ref[:, blk(g)], t_ref[g])
                            + _dot_nt(xprev_scr[:, tile(g)].astype(BF16), cpt_ref[g]))
    for cb in range(ATT_WIDTH // HEAD_DIM):
        for half in range(-(-n_valid // SLOTS)):
            toks = _slot_transpose([s_scr[:, tile(2 * (cb * SLOTS + p) + half)] for p in range(SLOTS)])
            for s in range(min(SLOTS, n_valid - half * SLOTS)):
                stage[cb, pl.ds(half * SLOTS + s, rows, stride=n_valid), :] = toks[s]
    n_tok = rows * n_valid
    chunk = min(n_tok, GLU_ROWS)

    def glu(i, carry):
        r = pl.ds(pl.multiple_of(i * chunk, chunk), chunk)
        y = jax.nn.gelu(jnp.concatenate([stage[cb, r, :] for cb in range(ATT_WIDTH // HEAD_DIM)], axis=-1))
        z = _dot(y.astype(BF16), wglu_ref[...]) + bglu_ref[...]
        mix_ref[r, :] = (y * jax.nn.sigmoid(z)).astype(mix_ref.dtype)
        return carry
    lax.fori_loop(0, n_tok // chunk, glu, 0)


def _s5_mixer(v, rows_per_tile, n_valid, n_sub, x0, x0s, t, bp, cpt, lam_n, w_glu_bf, b_glu):
    m, width = v.shape
    n_tiles = m // rows_per_tile
    r0 = x0.shape[0] // n_tiles
    n_state = x0.shape[1]
    n_tok = rows_per_tile * n_valid
    return pl.pallas_call(
        functools.partial(_s5_core_kernel, n_valid=n_valid, n_sub=n_sub),
        grid=(n_tiles,),
        in_specs=[pl.BlockSpec((rows_per_tile, width), lambda i: (i, 0)),
                  pl.BlockSpec((r0, n_state), lambda i: (i, 0)),
                  pl.BlockSpec((r0, n_state), lambda i: (i, 0)),
                  _full(t.shape), _full(bp.shape), _full(cpt.shape), _full(lam_n.shape),
                  _full(w_glu_bf.shape), _full((1, ATT_WIDTH))],
        out_specs=[pl.BlockSpec((n_tok, ATT_WIDTH), lambda i: (i, 0)),
                   pl.BlockSpec((r0, n_state), lambda i: (i, 0))],
        out_shape=[jax.ShapeDtypeStruct((m * n_valid, ATT_WIDTH), BF16),
                   jax.ShapeDtypeStruct(x0.shape, F32)],
        scratch_shapes=[pltpu.VMEM((rows_per_tile, width), F32),
                        pltpu.VMEM((rows_per_tile, n_state), F32),
                        pltpu.VMEM((8, n_state), F32),
                        pltpu.VMEM((ATT_WIDTH // HEAD_DIM, n_tok, HEAD_DIM), F32)],
        compiler_params=_params("arbitrary"),
        name="s5_core",
    )(v, x0, x0s, t, bp, cpt, lam_n, w_glu_bf, b_glu.reshape(1, ATT_WIDTH))


def _pack_state(s):
    n = s.shape[0]
    packed = s.transpose(0, 2, 1, 3).reshape(n, -1)
    swapped = jnp.stack([s[:, 1], s[:, 0]], axis=1).transpose(0, 2, 1, 3).reshape(n, -1)
    return packed, swapped


def _unpack_state(x):
    n = x.shape[0]
    return x.reshape(n, -1, 2, S5_STATE).transpose(0, 2, 1, 3)


PROMPT_TILE = 512
Q_PAD = 16


def _pad_rows(t, n):
    return jnp.pad(t, ((0, 0), (0, n - t.shape[1]), (0, 0)))


def kernel(x_prompt, x_sample, cache_win0_kv, cache_win1_kv, cache_win2_kv, cache_mem_kv, state_s5,
           state_ffn_conv, mem_prompt, g_mix, g_ffn, w_in_a, g_q_dil, g_k_dil, w_in_b, s5_lam_re,
           s5_lam_im, s5_log_dt, s5_b_re, s5_b_im, s5_c_re, s5_c_im, s5_d, w_glu, b_glu, g_mem,
           w_mem_kv, g_q_cross, g_k_cross, w_out, w_up, conv_w, conv_b, w_down):
    nb, seq, d = x_prompt.shape
    db, ts, _ = x_sample.shape
    depth = g_mix.shape[0]
    n_mem = mem_prompt.shape[1]
    d_ff2 = w_up.shape[2]
    assert ts >= CONV_W - 1 and seq % PROMPT_TILE == 0 and ts <= Q_PAD
    caches = (cache_win0_kv, cache_win1_kv, cache_win2_kv)
    w_mem_bf, w_out_bf, w_up_bf, w_down_bf = (w.astype(BF16) for w in (w_mem_kv, w_out, w_up, w_down))
    mem_cache = cache_mem_kv.reshape(depth * db, n_mem * KV_ROWS, HEAD_DIM)

    tab_p = _rope_tables(jnp.arange(seq, dtype=jnp.int32))
    tab_s = tuple(jnp.tile(t, (db, 1)) for t in _rope_tables(PAST_LEN + jnp.arange(ts, dtype=jnp.int32)))
    win_keep = tuple(min(w, seq) for w, _ in DIL_GROUPS)
    rows_s = db * ts
    kv_tail = (2, N_HEADS, HEAD_DIM)

    mem_p = _mem_kv(mem_prompt.reshape(nb * n_mem, d), g_mem, w_mem_bf, g_k_cross)

    xp = x_prompt.reshape(nb * seq, d)
    xs = x_sample.reshape(rows_s, d)
    p_win, s_win = [[] for _ in DIL_GROUPS], [[] for _ in DIL_GROUPS]
    p_s5, s_s5, p_conv, s_conv = [], [], [], []
    tiles_per_seq = seq // PROMPT_TILE
    for i in range(depth):
        mem_i = mem_p[i].reshape(nb, n_mem * KV_ROWS, HEAD_DIM)
        if i % 2 == 0:
            ia = i // 2
            w_in_a_bf = w_in_a[ia].astype(BF16)
            *qkv_groups, w0, w1, w2, cross_p = _in_proj_a(
                xp, seq, PROMPT_TILE, g_mix[i], w_in_a_bf, tab_p, g_q_dil[ia], g_k_dil[ia],
                g_q_cross[i], mem_i, win_keep, dilated=True)
            for g, w in enumerate((w0, w1, w2)):
                p_win[g].append(w.reshape((nb, win_keep[g]) + kv_tail))
            mix_p = _attn_prompt(qkv_groups, nb, seq).reshape(nb * seq, ATT_WIDTH)
            qkv_s, *kv_new, qc_s = _in_proj_a(
                xs, rows_s, rows_s, g_mix[i], w_in_a_bf, tab_s, g_q_dil[ia], g_k_dil[ia],
                None, None, (rows_s,) * N_GROUPS, dilated=False)
            old_bufs = [c[ia].reshape(db, -1, HEAD_DIM) for c in caches]
            moves = (old_bufs, [k.reshape(db, ts * KV_ROWS, HEAD_DIM) for k in kv_new])
        else:
            moves = None
            ib = i // 2
            prm = (s5_lam_re[ib], s5_lam_im[ib], s5_log_dt[ib], s5_b_re[ib], s5_b_im[ib], s5_c_re[ib],
                   s5_c_im[ib], s5_d[ib])
            n_state = s5_lam_re.shape[1] * 2 * S5_STATE
            w_in_b_bf, w_glu_bf = w_in_b[ib].astype(BF16), w_glu[ib].astype(BF16)
            u_p, cross_p = _in_proj_b(xp, seq, PROMPT_TILE, S5_SUB, g_mix[i], w_in_b_bf, g_q_cross[i], mem_i)
            zero = jnp.zeros((nb * CARRY_ROWS, n_state), F32)
            prep_p, prep_s = _s5_prep(*prm, (S5_SUB, ts))
            mix_p, fin_p = _s5_mixer(u_p, seq // S5_SUB, S5_SUB, seq // S5_SUB, zero, zero,
                                     *prep_p, w_glu_bf, b_glu[ib])
            p_s5.append(_unpack_state(fin_p.reshape(nb, CARRY_ROWS, n_state)[:, 0]))
            u_s, qc_s = _in_proj_b(xs, rows_s, rows_s, ts, g_mix[i], w_in_b_bf, None, None)
            mix_s, fin_s = _s5_mixer(u_s, db, ts, 1, *_pack_state(state_s5[ib]), *prep_s, w_glu_bf, b_glu[ib])
            s_s5.append(_unpack_state(fin_s))
        cross_s = _cross_sample(_pad_rows(qc_s.reshape(db, ts, ATT_WIDTH), Q_PAD), g_q_cross[i], mem_cache,
                                first=i * db)
        cross_s = cross_s[:, :ts].reshape(rows_s, ATT_WIDTH)

        ffn_w = (i, w_out_bf, g_ffn, w_up_bf, conv_w, conv_b, w_down_bf)
        xp, tails, *new_bufs = _out_ffn(xp, mix_p, cross_p, *ffn_w, PROMPT_TILE, tiles_per_seq=tiles_per_seq,
                                        moves=moves)
        p_conv.append(tails.reshape(nb, tiles_per_seq, CARRY_ROWS, d_ff2)[:, -1, CARRY_ROWS - (CONV_W - 1):])
        if moves is not None:
            outs, lses = [], []
            for g, (_, r) in enumerate(DIL_GROUPS):
                q_g = qkv_s[g * N_HEADS:(g + 1) * N_HEADS]
                q_g = _pad_rows(q_g.transpose(1, 0, 2).reshape(db, ts, ATT_WIDTH), Q_PAD)
                o_g, lse_g = _attn_sample(q_g, new_bufs[g], moves[0][g], r, ts)
                outs.append(o_g.reshape(rows_s, ATT_WIDTH))
                lses.append(lse_g.reshape(rows_s, ATT_WIDTH))
                s_win[g].append(new_bufs[g].reshape(caches[g][i // 2].shape))
            mix_s = _combine_groups(outs, lses)
        buf = state_ffn_conv[i]
        zero = jnp.zeros((db, ts - 2, d_ff2), F32)
        e1 = jnp.concatenate([buf[:, 1:2], zero, zero[:, :1]], axis=1).reshape(rows_s, d_ff2)
        e2 = jnp.concatenate([buf, zero], axis=1).reshape(rows_s, d_ff2)
        xs, tails = _out_ffn(xs, mix_s, cross_s, *ffn_w, rows_s, seq_len=ts, e1=e1, e2=e2, tail_rows=rows_s)
        s_conv.append(tails.reshape(db, ts, d_ff2)[:, ts - (CONV_W - 1):])

    return (xp.reshape(nb, seq, d), xs.reshape(db, ts, d),
            jnp.stack(p_win[0]), jnp.stack(p_win[1]), jnp.stack(p_win[2]),
            mem_p.reshape((depth, nb, n_mem) + kv_tail),
            jnp.stack(p_s5), jnp.stack(p_conv),
            jnp.stack(s_win[0]), jnp.stack(s_win[1]), jnp.stack(s_win[2]),
            jnp.stack(s_s5), jnp.stack(s_conv))
```

```python
import functools
import math

import jax
import jax.numpy as jnp
from jax import lax
from jax.experimental import pallas as pl
from jax.experimental.pallas import tpu as pltpu

HEAD_DIM = 128
N_HEADS = 4
DIL_GROUPS = ((128, 1), (512, 4), (2048, 16))
N_GROUPS = len(DIL_GROUPS)
DIL_SPAN = 128
BLOCK = 128
ATT_WIDTH = N_HEADS * HEAD_DIM
KV_ROWS = 2 * N_HEADS
ROT_DIM = HEAD_DIM // 4
ROT_HALF = ROT_DIM // 2
ROPE_THETA = 500000.0
S5_GROUP = 16
S5_STATE = 64
S5_SUB = 16
S5_BLOCK = S5_SUB * S5_GROUP
CONV_W = 3
EPS = 1e-6
NEG = -1e30
SCALE = HEAD_DIM ** -0.5
PAST_LEN = 16384

VMEM_LIMIT_V7X = 56 * 1024 * 1024
BF16 = jnp.bfloat16
F32 = jnp.float32
PREP_PRECISION = lax.Precision.HIGHEST


def _params(*sem):
    return pltpu.CompilerParams(dimension_semantics=sem, vmem_limit_bytes=VMEM_LIMIT_V7X)


def _rms(x, g):
    return x * lax.rsqrt(jnp.mean(x * x, axis=-1, keepdims=True) + EPS) * g


def _rms_head(x, g):
    ones = jnp.ones((HEAD_DIM, HEAD_DIM), BF16)
    ssq = _dot((x * x).astype(BF16), ones)
    return x * lax.rsqrt(ssq * (1.0 / HEAD_DIM) + EPS) * g


def _dot(a, b):
    return jnp.dot(a, b, preferred_element_type=F32)


def _dot_nt(a, b, precision=None):
    return lax.dot_general(a, b, (((1,), (1,)), ((), ())), precision=precision,
                           preferred_element_type=F32)


def _full(shape):
    nd = len(shape)
    return pl.BlockSpec(shape, lambda *_: (0,) * nd)


def _kv_rows(kv, head, n):
    return pl.ds(kv * N_HEADS + head, n, stride=KV_ROWS)


def _mem_kv_kernel(mem_ref, g_ref, w_ref, gk_ref, out_ref):
    tm = mem_ref.shape[0]
    h = _rms(mem_ref[...], g_ref[0]).astype(BF16)
    kv = _dot(h, w_ref[0])
    for hd in range(N_HEADS):
        sl = slice(hd * HEAD_DIM, (hd + 1) * HEAD_DIM)
        out_ref[0, _kv_rows(0, hd, tm), :] = _rms(kv[:, sl], gk_ref[0])
        out_ref[0, _kv_rows(1, hd, tm), :] = kv[:, ATT_WIDTH + hd * HEAD_DIM:ATT_WIDTH + (hd + 1) * HEAD_DIM]


def _mem_kv(mem2d, g_mem, w_kv_bf, g_k):
    depth, d, _ = w_kv_bf.shape
    rows = mem2d.shape[0]
    tm = min(rows, 512)
    return pl.pallas_call(
        _mem_kv_kernel,
        grid=(depth, rows // tm),
        in_specs=[
            pl.BlockSpec((tm, d), lambda l, i: (i, 0)),
            pl.BlockSpec((1, 1, d), lambda l, i: (l, 0, 0)),
            pl.BlockSpec((1, d, 2 * ATT_WIDTH), lambda l, i: (l, 0, 0)),
            pl.BlockSpec((1, 1, HEAD_DIM), lambda l, i: (l, 0, 0)),
        ],
        out_specs=pl.BlockSpec((1, tm * KV_ROWS, HEAD_DIM), lambda l, i: (l, i, 0)),
        out_shape=jax.ShapeDtypeStruct((depth, rows * KV_ROWS, HEAD_DIM), F32),
        compiler_params=_params("arbitrary", "arbitrary"),
        name="mem_kv",
    )(mem2d, g_mem.reshape(depth, 1, d), w_kv_bf, g_k.reshape(depth, 1, HEAD_DIM))


def _cross_heads(qc, mem_ref, gq):
    n_mem = mem_ref.shape[1] // KV_ROWS
    outs = []
    for hd in range(N_HEADS):
        q = _rms(qc[:, hd * HEAD_DIM:(hd + 1) * HEAD_DIM], gq).astype(BF16)
        k = mem_ref[0, _kv_rows(0, hd, n_mem), :].astype(BF16)
        v = mem_ref[0, _kv_rows(1, hd, n_mem), :].astype(BF16)
        s = _dot_nt(q, k) * SCALE
        m = jnp.max(s, axis=-1, keepdims=True)
        p = jnp.exp(s - m)
        l = jnp.sum(p, axis=-1, keepdims=True)
        outs.append(_dot(p.astype(BF16), v) / l)
    return outs


def _rope(x, cos_t, sin_lo, sin_hi):
    return (x * cos_t + pltpu.roll(x, HEAD_DIM - ROT_HALF, 1) * sin_lo
            + pltpu.roll(x, ROT_HALF, 1) * sin_hi)


def _in_proj_a_kernel(*refs, fuse_cross, dilated, win_rows, win_first, tiles_per_seq):
    n_in = 10 if fuse_cross else 8
    x_ref, g_ref, w_ref, cos_ref, slo_ref, shi_ref, gq_ref, gk_ref = refs[:8]
    n_qkv = N_GROUPS if dilated else 1
    qkv_refs = refs[n_in:n_in + n_qkv]
    win_refs = refs[n_in + n_qkv:n_in + n_qkv + N_GROUPS]
    cr_ref = refs[n_in + n_qkv + N_GROUPS]
    kv_stash = refs[n_in + n_qkv + N_GROUPS + 1]
    stage = refs[n_in + n_qkv + N_GROUPS + 2] if dilated else None
    tm = x_ref.shape[0]
    h = _rms(x_ref[...], g_ref[...]).astype(BF16)
    cos_t, sin_lo, sin_hi = cos_ref[...], slo_ref[...], shi_ref[...]
    def proj(c):
        return _dot(h, w_ref[:, c * ATT_WIDTH:(c + 1) * ATT_WIDTH])

    tile = pl.program_id(0) % tiles_per_seq
    in_window = [tile >= first for first in win_first]
    for g in range(N_GROUPS):
        @pl.when(jnp.logical_not(in_window[g]))
        def _(g=g):
            win_refs[g][...] = jnp.zeros_like(win_refs[g])

    y = proj(0)
    for c in range(3 * N_GROUPS):
        y_next = proj(c + 1)
        role, g = divmod(c, N_GROUPS)
        r = DIL_GROUPS[g][1]
        wr = win_rows[g]
        for hd in range(N_HEADS):
            yh = y[:, hd * HEAD_DIM:(hd + 1) * HEAD_DIM]
            if role == 0:
                yh = _rope(_rms_head(yh, gq_ref[g]), cos_t, sin_lo, sin_hi)
            elif role == 1:
                yh = _rope(_rms_head(yh, gk_ref[g]), cos_t, sin_lo, sin_hi)
            if role > 0:
                kv_stash[((role - 1) * N_GROUPS + g) * N_HEADS + hd] = yh
            if not dilated:
                qkv_refs[0][role * N_GROUPS * N_HEADS + g * N_HEADS + hd] = yh.astype(BF16)
            elif r == 1:
                qkv_refs[g][role * N_HEADS + hd, 0] = yh.astype(BF16)
            else:
                stage[hd] = yh
                for rho in range(r):
                    qkv_refs[g][role * N_HEADS + hd, 0, :, rho * HEAD_DIM:(rho + 1) * HEAD_DIM] = (
                        stage[hd, pl.ds(rho, tm // r, stride=r), :].astype(BF16))
        y = y_next
    qc = y
    if fuse_cross:
        outs = _cross_heads(qc, refs[9], refs[8][...])
        for hd in range(N_HEADS):
            cr_ref[:, hd * HEAD_DIM:(hd + 1) * HEAD_DIM] = outs[hd].astype(cr_ref.dtype)
    else:
        cr_ref[...] = qc
    for g in range(N_GROUPS):
        @pl.when(in_window[g])
        def _(g=g):
            wr = win_rows[g]
            for kv in range(2):
                for hd in range(N_HEADS):
                    win_refs[g][_kv_rows(kv, hd, wr), :] = kv_stash[(kv * N_GROUPS + g) * N_HEADS + hd,
                                                                    tm - wr:, :]


SLOTS = HEAD_DIM // S5_GROUP


def _slot_transpose(vs):
    slot = lax.broadcasted_iota(jnp.int32, vs[0].shape, 1) // S5_GROUP
    d = SLOTS // 2
    while d:
        low = (slot & d) == 0
        new = list(vs)
        for i in range(SLOTS):
            if not i & d:
                new[i] = jnp.where(low, vs[i], pltpu.roll(vs[i | d], d * S5_GROUP, 1))
                new[i | d] = jnp.where(low, pltpu.roll(vs[i], HEAD_DIM - d * S5_GROUP, 1), vs[i | d])
        vs = new
        d //= 2
    return vs


def _in_proj_b_kernel(*refs, fuse_cross, n_valid):
    if fuse_cross:
        x_ref, g_ref, w_ref, gqc_ref, mem_ref, u_ref, cr_ref, stage = refs
    else:
        x_ref, g_ref, w_ref, u_ref, cr_ref, stage = refs
    tm = x_ref.shape[0]
    n_rows = tm // n_valid
    h = _rms(x_ref[...], g_ref[...]).astype(BF16)
    u = _dot(h, w_ref[:, :ATT_WIDTH])
    zero = jnp.zeros((n_rows, HEAD_DIM), F32)
    for cb in range(ATT_WIDTH // HEAD_DIM):
        stage[cb] = u[:, cb * HEAD_DIM:(cb + 1) * HEAD_DIM]
        for half in range(S5_SUB // SLOTS):
            taus = range(half * SLOTS, (half + 1) * SLOTS)
            groups = _slot_transpose([stage[cb, pl.ds(tau, n_rows, stride=n_valid), :] if tau < n_valid else zero
                                      for tau in taus])
            for p in range(SLOTS):
                lanes = (2 * (cb * SLOTS + p) + half) * HEAD_DIM
                u_ref[:, lanes:lanes + HEAD_DIM] = groups[p].astype(BF16)
    qc = _dot(h, w_ref[:, ATT_WIDTH:])
    if fuse_cross:
        outs = _cross_heads(qc, mem_ref, gqc_ref[...])
        for hd in range(N_HEADS):
            cr_ref[:, hd * HEAD_DIM:(hd + 1) * HEAD_DIM] = outs[hd].astype(cr_ref.dtype)
    else:
        cr_ref[...] = qc


def _rope_tables(pos):
    inv = jnp.exp(-math.log(ROPE_THETA) * jnp.arange(ROT_HALF, dtype=F32) / ROT_HALF)
    ang = pos.astype(F32)[:, None] * inv[None, :]
    cos, sin = jnp.cos(ang), jnp.sin(ang)
    rows = pos.shape[0]
    ones = jnp.ones((rows, HEAD_DIM - ROT_DIM), F32)
    zeros = jnp.zeros((rows, HEAD_DIM - ROT_DIM), F32)
    z16 = jnp.zeros((rows, ROT_HALF), F32)
    cos_t = jnp.concatenate([cos, cos, ones], axis=1)
    sin_lo = jnp.concatenate([-sin, z16, zeros], axis=1)
    sin_hi = jnp.concatenate([z16, sin, zeros], axis=1)
    return cos_t, sin_lo, sin_hi


def _in_proj_a(x2d, seq, tm, g_mix, w_bf, tables, g_q, g_k, g_qc, mem_kv, win_keep, dilated):
    rows, d = x2d.shape
    n_in = w_bf.shape[1]
    tiles_per_seq = seq // tm
    n_seq = rows // seq
    fuse_cross = mem_kv is not None
    tab_tiles = tables[0].shape[0] // tm
    win_rows = tuple(min(k, tm) for k in win_keep)
    win_first = tuple(tiles_per_seq - k // wr for k, wr in zip(win_keep, win_rows))

    def win_spec(keep):
        wr = min(keep, tm)
        nblk = keep // wr
        first = tiles_per_seq - nblk

        def imap(i):
            b = i // tiles_per_seq
            t = i % tiles_per_seq
            return (b * nblk + jnp.maximum(t - first, 0), 0)
        return pl.BlockSpec((wr * KV_ROWS, HEAD_DIM), imap)

    in_specs = [
        pl.BlockSpec((tm, d), lambda i: (i, 0)),
        _full((1, d)),
        _full((d, n_in)),
        pl.BlockSpec((tm, HEAD_DIM), lambda i: (i % tab_tiles, 0)),
        pl.BlockSpec((tm, HEAD_DIM), lambda i: (i % tab_tiles, 0)),
        pl.BlockSpec((tm, HEAD_DIM), lambda i: (i % tab_tiles, 0)),
        _full((N_GROUPS, HEAD_DIM)),
        _full((N_GROUPS, HEAD_DIM)),
    ]
    args = [x2d, g_mix.reshape(1, d), w_bf, *tables, g_q, g_k]
    if fuse_cross:
        in_specs += [_full((1, HEAD_DIM)),
                     pl.BlockSpec((1,) + mem_kv.shape[1:], lambda i: (i // tiles_per_seq, 0, 0))]
        args += [g_qc.reshape(1, HEAD_DIM), mem_kv]
    if dilated:
        qkv_specs = [pl.BlockSpec((3 * N_HEADS, 1, tm // r, r * HEAD_DIM),
                                  lambda i: (0, i // tiles_per_seq, i % tiles_per_seq, 0))
                     for _, r in DIL_GROUPS]
        qkv_shapes = [jax.ShapeDtypeStruct((3 * N_HEADS, n_seq, seq // r, r * HEAD_DIM), BF16)
                      for _, r in DIL_GROUPS]
        scratch = [pltpu.VMEM((N_HEADS, tm, HEAD_DIM), F32)]
    else:
        qkv_specs = [pl.BlockSpec((3 * N_GROUPS * N_HEADS, tm, HEAD_DIM), lambda i: (0, i, 0))]
        qkv_shapes = [jax.ShapeDtypeStruct((3 * N_GROUPS * N_HEADS, rows, HEAD_DIM), BF16)]
        scratch = []
    scratch = [pltpu.VMEM((2 * N_GROUPS * N_HEADS, tm, HEAD_DIM), F32)] + scratch
    out_specs = [*qkv_specs, *[win_spec(k) for k in win_keep],
                 pl.BlockSpec((tm, ATT_WIDTH), lambda i: (i, 0))]
    out_shape = [*qkv_shapes,
                 *[jax.ShapeDtypeStruct((n_seq * k * KV_ROWS, HEAD_DIM), F32) for k in win_keep],
                 jax.ShapeDtypeStruct((rows, ATT_WIDTH), BF16 if fuse_cross else F32)]
    return pl.pallas_call(
        functools.partial(_in_proj_a_kernel, fuse_cross=fuse_cross, dilated=dilated, win_rows=win_rows,
                          win_first=win_first, tiles_per_seq=tiles_per_seq),
        grid=(rows // tm,),
        in_specs=in_specs, out_specs=out_specs, out_shape=out_shape, scratch_shapes=scratch,
        compiler_params=_params("arbitrary"),
        name="in_proj_a",
    )(*args)


def _in_proj_b(x2d, seq, tm, n_valid, g_mix, w_bf, g_qc, mem_kv):
    rows, d = x2d.shape
    width = (ATT_WIDTH // S5_GROUP) * S5_BLOCK
    n_in = w_bf.shape[1]
    tiles_per_seq = seq // tm
    fuse_cross = mem_kv is not None
    in_specs = [pl.BlockSpec((tm, d), lambda i: (i, 0)), _full((1, d)), _full((d, n_in))]
    args = [x2d, g_mix.reshape(1, d), w_bf]
    if fuse_cross:
        in_specs += [_full((1, HEAD_DIM)),
                     pl.BlockSpec((1,) + mem_kv.shape[1:], lambda i: (i // tiles_per_seq, 0, 0))]
        args += [g_qc.reshape(1, HEAD_DIM), mem_kv]
    return pl.pallas_call(
        functools.partial(_in_proj_b_kernel, fuse_cross=fuse_cross, n_valid=n_valid),
        grid=(rows // tm,),
        in_specs=in_specs,
        out_specs=[pl.BlockSpec((tm // n_valid, width), lambda i: (i, 0)),
                   pl.BlockSpec((tm, ATT_WIDTH), lambda i: (i, 0))],
        out_shape=[jax.ShapeDtypeStruct((rows // n_valid, width), BF16),
                   jax.ShapeDtypeStruct((rows, ATT_WIDTH), BF16 if fuse_cross else F32)],
        scratch_shapes=[pltpu.VMEM((ATT_WIDTH // HEAD_DIM, tm, HEAD_DIM), F32)],
        compiler_params=_params("arbitrary"),
        name="in_proj_b",
    )(*args)


CROSS_BATCH = 8


def _cross_sample_kernel(qc_ref, gq_ref, mem_ref, out_ref):
    for b in range(qc_ref.shape[0]):
        outs = _cross_heads(qc_ref[b], mem_ref.at[pl.ds(b, 1)], gq_ref[...])
        for hd in range(N_HEADS):
            out_ref[b, :, hd * HEAD_DIM:(hd + 1) * HEAD_DIM] = outs[hd].astype(out_ref.dtype)


def _cross_sample(qc, g_qc, mem_kv, first):
    nb, tq, _ = qc.shape
    cb = math.gcd(nb, CROSS_BATCH)
    assert first % cb == 0
    return pl.pallas_call(
        _cross_sample_kernel,
        grid=(nb // cb,),
        in_specs=[pl.BlockSpec((cb, tq, ATT_WIDTH), lambda b: (b, 0, 0)),
                  _full((1, HEAD_DIM)),
                  pl.BlockSpec((cb,) + mem_kv.shape[1:], lambda b: (first // cb + b, 0, 0))],
        out_specs=pl.BlockSpec((cb, tq, ATT_WIDTH), lambda b: (b, 0, 0)),
        out_shape=jax.ShapeDtypeStruct((nb, tq, ATT_WIDTH), BF16),
        compiler_params=_params("arbitrary"),
        name="cross_sample",
    )(qc, g_qc.reshape(1, HEAD_DIM), mem_kv)


INFLIGHT = 8


def _band_block(q, k, v):
    n = k.shape[0]
    dist = (n - BLOCK + lax.broadcasted_iota(jnp.int32, (BLOCK, n), 0)
            - lax.broadcasted_iota(jnp.int32, (BLOCK, n), 1))
    s = jnp.where((dist >= 0) & (dist <= DIL_SPAN), _dot_nt(q, k) * SCALE, NEG)
    m = jnp.max(s, axis=-1, keepdims=True)
    p = jnp.exp(s - m)
    l = jnp.sum(p, axis=-1, keepdims=True)
    return _dot(p.astype(BF16), v) / l, m + jnp.log(l)


def _attn_prompt_kernel(q0, q1, q2, k0, k1, k2, v0, v1, v2, out_ref, o_scr, l_scr, *, seq):
    qkv_refs = ((q0, k0, v0), (q1, k1, v1), (q2, k2, v2))
    for g, (_, r) in enumerate(DIL_GROUPS):
        q_ref, k_ref, v_ref = qkv_refs[g]
        nblk = seq // r // BLOCK

        def store(rho, blk, o, lse, g=g, r=r):
            start = blk * (BLOCK * r) + rho
            if r == 1:
                idx = pl.ds(pl.multiple_of(start, BLOCK), BLOCK)
            else:
                idx = pl.ds(start, BLOCK, stride=r)
            o_scr[g, idx, :] = o
            l_scr[g, idx, :] = jnp.broadcast_to(lse, (BLOCK, HEAD_DIM))

        def first(rho, q_ref=q_ref, k_ref=k_ref, v_ref=v_ref, store=store):
            lanes = slice(rho * HEAD_DIM, (rho + 1) * HEAD_DIM)
            o, lse = _band_block(q_ref[0, 0, :BLOCK, lanes], k_ref[0, 0, :BLOCK, lanes],
                                 v_ref[0, 0, :BLOCK, lanes])
            store(rho, 0, o, lse)

        def later(rho, blk, q_ref=q_ref, k_ref=k_ref, v_ref=v_ref, store=store):
            lanes = slice(rho * HEAD_DIM, (rho + 1) * HEAD_DIM)
            cur = pl.ds(pl.multiple_of(blk * BLOCK, BLOCK), BLOCK)
            both = pl.ds(pl.multiple_of((blk - 1) * BLOCK, BLOCK), 2 * BLOCK)
            o, lse = _band_block(q_ref[0, 0, cur, lanes], k_ref[0, 0, both, lanes], v_ref[0, 0, both, lanes])
            store(rho, blk, o, lse)

        for rho0 in range(0, r, INFLIGHT):
            rhos = range(rho0, min(r, rho0 + INFLIGHT))
            for rho in rhos:
                first(rho)

            def body(blk, carry, rhos=rhos, later=later):
                for rho in rhos:
                    later(rho, blk)
                return carry
            if nblk > 1:
                lax.fori_loop(1, nblk, body, 0, unroll=max(1, INFLIGHT // len(rhos)))

    def combine(c, carry):
        rows = pl.ds(pl.multiple_of(c * BLOCK, BLOCK), BLOCK)
        ls = [l_scr[g, rows, :] for g in range(N_GROUPS)]
        m = jnp.maximum(jnp.maximum(ls[0], ls[1]), ls[2])
        es = [jnp.exp(l - m) for l in ls]
        num = es[0] * o_scr[0, rows, :] + es[1] * o_scr[1, rows, :] + es[2] * o_scr[2, rows, :]
        out_ref[0, rows, :] = (num / (es[0] + es[1] + es[2])).astype(out_ref.dtype)
        return carry
    lax.fori_loop(0, seq // BLOCK, combine, 0)


def _attn_prompt(qkv_groups, n_seq, seq):
    in_specs, args = [], []
    for role in range(3):
        for g, (_, r) in enumerate(DIL_GROUPS):
            in_specs.append(pl.BlockSpec((1, 1, seq // r, r * HEAD_DIM),
                                         lambda b, h, role=role: (role * N_HEADS + h, b, 0, 0)))
            args.append(qkv_groups[g])
    return pl.pallas_call(
        functools.partial(_attn_prompt_kernel, seq=seq),
        grid=(n_seq, N_HEADS),
        in_specs=in_specs,
        out_specs=pl.BlockSpec((1, seq, HEAD_DIM), lambda b, h: (b, 0, h)),
        out_shape=jax.ShapeDtypeStruct((n_seq, seq, ATT_WIDTH), BF16),
        scratch_shapes=[pltpu.VMEM((N_GROUPS, seq, HEAD_DIM), F32),
                        pltpu.VMEM((N_GROUPS, seq, HEAD_DIM), F32)],
        compiler_params=_params("arbitrary", "arbitrary"),
        name="attn_prompt",
    )(*args)


def _attn_sample_kernel(q_ref, new_ref, old_ref, o_ref, lse_ref, *scratch, r, t_new, lb, compact):
    qf = q_ref[0].astype(F32)
    if compact:
        (flat,) = scratch
        n_key = lb // r
        for u in range(t_new):
            flat[u] = new_ref[:, u * KV_ROWS:(u + 1) * KV_ROWS, :].reshape(n_key * KV_ROWS, HEAD_DIM)
    else:
        n_key = lb
    i_key = lax.broadcasted_iota(jnp.int32, (n_key, 1), 0)
    j_old = lax.broadcasted_iota(jnp.int32, (t_new, 1), 0)
    for t in range(t_new):
        if compact:
            idx = i_key * r + (r - t_new + t)
            rows = lambda kvh, t=t: flat[t, pl.ds(kvh, n_key, stride=KV_ROWS), :]
        else:
            idx = i_key
            rows = lambda kvh: new_ref[pl.ds(kvh, n_key, stride=KV_ROWS), :]
        dist = lb - t_new + t - idx
        ok = (dist >= 0) & (dist % r == 0) & (dist <= r * DIL_SPAN)
        dist_old = lb + t - j_old
        ok_old = (dist_old % r == 0) & (dist_old <= r * DIL_SPAN)
        for hd in range(N_HEADS):
            sl = slice(hd * HEAD_DIM, (hd + 1) * HEAD_DIM)
            q = qf[t:t + 1, sl]
            k, v = rows(hd), rows(N_HEADS + hd)
            k_old = old_ref[pl.ds(hd, t_new, stride=KV_ROWS), :]
            v_old = old_ref[pl.ds(N_HEADS + hd, t_new, stride=KV_ROWS), :]
            s = jnp.where(ok, jnp.sum(k * q, axis=-1, keepdims=True) * SCALE, NEG)
            s_old = jnp.where(ok_old, jnp.sum(k_old * q, axis=-1, keepdims=True) * SCALE, NEG)
            m = jnp.maximum(jnp.max(s, axis=0, keepdims=True), jnp.max(s_old, axis=0, keepdims=True))
            p, p_old = jnp.exp(s - m), jnp.exp(s_old - m)
            l = jnp.sum(p, axis=0, keepdims=True) + jnp.sum(p_old, axis=0, keepdims=True)
            o = jnp.sum(p * v, axis=0, keepdims=True) + jnp.sum(p_old * v_old, axis=0, keepdims=True)
            o_ref[0, t:t + 1, sl] = o / l
            lse_ref[0, t:t + 1, sl] = jnp.broadcast_to(m + jnp.log(l), (1, HEAD_DIM))


def _attn_sample(q, new_buf, old_buf, r, t_new):
    nb, tq, _ = q.shape
    lb = new_buf.shape[1] // KV_ROWS
    compact = r % t_new == 0
    if compact:
        new_view = new_buf.reshape(nb, lb // r, r // t_new, t_new * KV_ROWS, HEAD_DIM)
        new_spec = pl.BlockSpec((None, lb // r, None, t_new * KV_ROWS, HEAD_DIM),
                                lambda b: (b, 0, r // t_new - 1, 0, 0))
        scratch = [pltpu.VMEM((t_new, lb // r * KV_ROWS, HEAD_DIM), F32)]
    else:
        new_view = new_buf
        new_spec = pl.BlockSpec((None, lb * KV_ROWS, HEAD_DIM), lambda b: (b, 0, 0))
        scratch = []
    return pl.pallas_call(
        functools.partial(_attn_sample_kernel, r=r, t_new=t_new, lb=lb, compact=compact),
        grid=(nb,),
        in_specs=[pl.BlockSpec((1, tq, ATT_WIDTH), lambda b: (b, 0, 0)),
                  new_spec,
                  pl.BlockSpec((None, t_new * KV_ROWS, HEAD_DIM), lambda b: (b, 0, 0))],
        out_specs=[pl.BlockSpec((1, t_new, ATT_WIDTH), lambda b: (b, 0, 0)),
                   pl.BlockSpec((1, t_new, ATT_WIDTH), lambda b: (b, 0, 0))],
        out_shape=[jax.ShapeDtypeStruct((nb, t_new, ATT_WIDTH), F32),
                   jax.ShapeDtypeStruct((nb, t_new, ATT_WIDTH), F32)],
        scratch_shapes=scratch,
        compiler_params=_params("arbitrary"),
        name=f"attn_sample_r{r}",
    )(q, new_view, old_buf)


def _combine_kernel(o0, o1, o2, l0, l1, l2, out_ref):
    ls = [l0[...], l1[...], l2[...]]
    m = jnp.maximum(jnp.maximum(ls[0], ls[1]), ls[2])
    es = [jnp.exp(l - m) for l in ls]
    num = es[0] * o0[...] + es[1] * o1[...] + es[2] * o2[...]
    out_ref[...] = (num / (es[0] + es[1] + es[2])).astype(out_ref.dtype)


def _combine_groups(outs, lses):
    shape = outs[0].shape
    return pl.pallas_call(
        _combine_kernel,
        in_specs=[_full(shape)] * 6,
        out_specs=_full(shape),
        out_shape=jax.ShapeDtypeStruct(shape, BF16),
        grid=(1,),
        compiler_params=_params("arbitrary"),
        name="combine_groups",
    )(*outs, *lses)


FF_CHUNK = 256
CARRY_ROWS = 8
SHIFT_SLOTS = 4


MOVE_ROWS = 8192


def _move_regions(keeps):
    regions = [(g, start, min(MOVE_ROWS, keep - start)) for g, keep in enumerate(keeps)
               for start in range(0, keep, MOVE_ROWS)]
    halves = ([], [])
    for reg in sorted(regions, key=lambda reg: -reg[2]):
        min(halves, key=lambda h: sum(r[2] for r in h)).append(reg)
    return halves


def _buffer_move(old_refs, kvn_refs, new_refs, bufs, tails, sems, b, n_steps):
    n_new = [k.shape[1] for k in kvn_refs]
    halves = _move_regions([o.shape[1] - n for o, n in zip(old_refs, n_new)])

    def copies(batch):
        out, sem = [], 0
        for buf, half in zip(bufs, halves):
            row, cps = 0, []
            for g, start, size in half:
                stage = buf.at[pl.ds(row, size)]
                cps.append((pltpu.make_async_copy(old_refs[g].at[batch, pl.ds(n_new[g] + start, size)], stage,
                                                  sems.at[sem]),
                            pltpu.make_async_copy(stage, new_refs[g].at[batch, pl.ds(start, size)],
                                                  sems.at[sem + 1])))
                row += size
                sem += 2
            out.append(cps)
        tail = []
        for g in range(N_GROUPS):
            keep = old_refs[g].shape[1] - n_new[g]
            tail.append((pltpu.make_async_copy(kvn_refs[g].at[batch], tails.at[g], sems.at[sem]),
                         pltpu.make_async_copy(tails.at[g], new_refs[g].at[batch, pl.ds(keep, n_new[g])],
                                               sems.at[sem + 1])))
            sem += 2
        return out[0], out[1], tail
    first, second, tail = copies(b)
    _, prev_second, _ = copies(jnp.maximum(b - 1, 0))

    def top():
        for cp_in, _ in first + tail:
            cp_in.start()

    def mid():
        for cp_in, cp_out in first + tail:
            cp_in.wait()
            cp_out.start()

        @pl.when(b > 0)
        def _():
            for _, cp_out in prev_second:
                cp_out.wait()
        for cp_in, _ in second:
            cp_in.start()

    def end():
        for cp_in, cp_out in second:
            cp_in.wait()
            cp_out.start()
        for _, cp_out in first + tail:
            cp_out.wait()

        @pl.when(b == n_steps - 1)
        def _():
            for _, cp_out in second:
                cp_out.wait()
    return top, mid, end


def _move_scratch(old, new_rows):
    halves = _move_regions([o.shape[1] - n.shape[1] for o, n in zip(old, new_rows)])
    n_copies = sum(len(h) for h in halves) + N_GROUPS
    return ([pltpu.VMEM((sum(r[2] for r in h), HEAD_DIM), F32) for h in halves]
            + [pltpu.VMEM((N_GROUPS,) + new_rows[0].shape[1:], F32), pltpu.SemaphoreType.DMA((2 * n_copies,))])


def _out_ffn_kernel(*refs, seq_len, tiles_per_seq, tail_rows, move):
    move_top = move_mid = move_end = lambda: None
    if move:
        (x_ref, mix_ref, cr_ref, wo_ref, g_ref, wup_ref, cw_ref, cb_ref, wdn_ref, o0, o1, o2, k0, k1, k2,
         out_ref, tail_ref, n0, n1, n2, shift, act_scr, carry, buf_a, buf_b, tails, sems) = refs
        move_top, move_mid, move_end = _buffer_move(
            (o0, o1, o2), (k0, k1, k2), (n0, n1, n2), (buf_a, buf_b), tails, sems,
            pl.program_id(0), pl.num_programs(0))
    elif seq_len is None:
        (x_ref, mix_ref, cr_ref, wo_ref, g_ref, wup_ref, cw_ref, cb_ref, wdn_ref,
         out_ref, tail_ref, shift, act_scr, carry) = refs
    else:
        (x_ref, mix_ref, cr_ref, wo_ref, g_ref, wup_ref, cw_ref, cb_ref, wdn_ref, e1_ref, e2_ref,
         out_ref, tail_ref, shift, act_scr) = refs
    tm = x_ref.shape[0]
    d_ff = wdn_ref.shape[0]
    move_top()
    x1 = x_ref[...] + _dot(jnp.concatenate([mix_ref[...], cr_ref[...]], axis=-1), wo_ref[...])
    h = _rms(x1, g_ref[...]).astype(BF16)
    if seq_len is None:
        @pl.when(pl.program_id(0) % tiles_per_seq == 0)
        def _():
            carry[...] = jnp.zeros_like(carry)
    else:
        t = lax.broadcasted_iota(jnp.int32, (tm, 1), 0) % seq_len
        has1 = t >= 1
        has2 = t >= 2

    def chunk_cols(j):
        return (slice(j * FF_CHUNK, (j + 1) * FF_CHUNK),
                slice(d_ff + j * FF_CHUNK, d_ff + (j + 1) * FF_CHUNK))

    def up_proj(j):
        return tuple(_dot(h, wup_ref[:, cols]) for cols in chunk_cols(j))

    def conv(up, cols, slot):
        buf = shift.at[slot]
        if seq_len is None:
            buf[:CARRY_ROWS, :] = carry[:, cols]
            carry[:, cols] = up[tm - CARRY_ROWS:, :]
        else:
            buf[:CARRY_ROWS, :] = jnp.zeros((CARRY_ROWS, FF_CHUNK), F32)
        buf[CARRY_ROWS:, :] = up
        tail_ref[0, :, cols] = up[tm - tail_rows:, :]
        prev1 = buf[CARRY_ROWS - 1:CARRY_ROWS - 1 + tm, :]
        prev2 = buf[CARRY_ROWS - 2:CARRY_ROWS - 2 + tm, :]
        if seq_len is not None:
            prev1 = jnp.where(has1, prev1, e1_ref[:, cols])
            prev2 = jnp.where(has2, prev2, e2_ref[:, cols])
        return (cb_ref[:, cols] + cw_ref[0:1, cols] * prev2 + cw_ref[1:2, cols] * prev1
                + cw_ref[2:3, cols] * up)

    n_chunks = d_ff // FF_CHUNK
    ups = up_proj(0)
    for j in range(n_chunks):
        nxt = up_proj(j + 1) if j + 1 < n_chunks else None
        a, b = (conv(up, cols, 2 * (j % 2) + s) for s, (up, cols) in enumerate(zip(ups, chunk_cols(j))))
        act_scr[:, j * FF_CHUNK:(j + 1) * FF_CHUNK] = (a * jax.nn.sigmoid(a) * b).astype(BF16)
        ups = nxt
        if j == n_chunks // 2:
            move_mid()
    out_ref[...] = x1 + _dot(act_scr[...], wdn_ref[...])
    move_end()


def _layer_spec(stacked, layer):
    nd = stacked.ndim - 1
    return pl.BlockSpec((None,) + stacked.shape[1:], lambda *_: (layer,) + (0,) * nd,
                        pipeline_mode=pl.Buffered(1))


def _out_ffn(x2d, mix, cross, layer, w_out_bf, g_ffn, w_up_bf, conv_w, conv_b, w_down_bf, tm,
             tiles_per_seq=None, seq_len=None, e1=None, e2=None, tail_rows=CARRY_ROWS, moves=None):
    rows, d = x2d.shape
    depth, d_ff, _ = w_down_bf.shape
    n_tiles = rows // tm
    args = [x2d, mix, cross, w_out_bf, g_ffn.reshape(depth, 1, d), w_up_bf, conv_w,
            conv_b.reshape(depth, 1, 2 * d_ff), w_down_bf]
    in_specs = [
        pl.BlockSpec((tm, d), lambda i: (i, 0)),
        pl.BlockSpec((tm, ATT_WIDTH), lambda i: (i, 0)),
        pl.BlockSpec((tm, ATT_WIDTH), lambda i: (i, 0)),
        *[_layer_spec(a, layer) for a in args[3:]],
    ]
    scratch = [pltpu.VMEM((SHIFT_SLOTS, CARRY_ROWS + tm, FF_CHUNK), F32), pltpu.VMEM((tm, d_ff), BF16)]
    if seq_len is None:
        scratch += [pltpu.VMEM((CARRY_ROWS, 2 * d_ff), F32)]
    else:
        in_specs += [pl.BlockSpec((tm, 2 * d_ff), lambda i: (i, 0))] * 2
        args += [e1, e2]
    out_specs = [pl.BlockSpec((tm, d), lambda i: (i, 0)),
                 pl.BlockSpec((1, tail_rows, 2 * d_ff), lambda i: (i, 0, 0))]
    out_shape = [jax.ShapeDtypeStruct((rows, d), F32),
                 jax.ShapeDtypeStruct((n_tiles, tail_rows, 2 * d_ff), F32)]
    if moves is not None:
        old, new_rows = moves
        assert seq_len is None and all(o.shape[0] == n_tiles for o in old)
        any_spec = pl.BlockSpec(memory_space=pl.ANY)
        in_specs += [any_spec] * (2 * N_GROUPS)
        args += [*old, *new_rows]
        out_specs += [any_spec] * N_GROUPS
        out_shape += [jax.ShapeDtypeStruct(o.shape, o.dtype) for o in old]
        scratch += _move_scratch(old, new_rows)
    return pl.pallas_call(
        functools.partial(_out_ffn_kernel, seq_len=seq_len, tiles_per_seq=tiles_per_seq,
                          tail_rows=tail_rows, move=moves is not None),
        grid=(n_tiles,),
        in_specs=in_specs,
        out_specs=out_specs,
        out_shape=out_shape,
        scratch_shapes=scratch,
        compiler_params=_params("arbitrary"),
        name="out_ffn",
    )(*args)


def _swap_halves(x):
    return pltpu.roll(x, S5_STATE, 1)


def _s5_prep_kernel(lam_ref, logdt_ref, bt_ref, c_ref, d_ref, t_ref, cpt_ref, *var_refs, n_valids):
    gc = lam_ref.shape[0]
    n_groups = gc // S5_GROUP
    lane = lax.broadcasted_iota(jnp.int32, lam_ref.shape, 1)
    first = lane < S5_STATE
    sign = jnp.where(first, -1.0, 1.0)
    a = lam_ref[...]
    a_sw = _swap_halves(a)
    are = jnp.where(first, a, a_sw)
    aim = jnp.where(first, a_sw, a)
    dt = jnp.exp(logdt_ref[...])
    mag = jnp.exp(are * dt)
    lr = mag * jnp.cos(aim * dt)
    li = mag * jnp.sin(aim * dt)
    den = are * are + aim * aim
    xr = lr - 1.0
    f_re = (xr * are + li * aim) / den
    f_im = (li * are - xr * aim) / den
    lb = sign * li

    def cmul(x, m_re, m_sw):
        return x * m_re + _swap_halves(x) * m_sw

    c = c_ref[...]
    c_neg = c * -sign
    ri = lax.broadcasted_iota(jnp.int32, (gc, gc), 0)
    ci = lax.broadcasted_iota(jnp.int32, (gc, gc), 1)
    same_group = (ri // S5_GROUP) == (ci // S5_GROUP)
    e = cmul(bt_ref[...], f_re, sign * f_im)
    cl = c
    pw = jnp.where(first, 1.0, 0.0)
    zr = lax.broadcasted_iota(jnp.int32, (gc, S5_BLOCK), 0)
    zc = lax.broadcasted_iota(jnp.int32, (gc, S5_BLOCK), 1)
    same_out = (zr % S5_GROUP) == (zc % S5_GROUP)
    lags = jnp.zeros((gc, S5_BLOCK), F32)
    for k in range(S5_SUB):
        kmat = jnp.where(same_group, _dot_nt(e, c_neg, precision=PREP_PRECISION), 0.0)
        if k == 0:
            kmat = kmat + jnp.where(ri == ci, d_ref[...], 0.0)
        place = jnp.where(same_out & (zc // S5_GROUP == k), 1.0, 0.0)
        lags = lags + jnp.dot(kmat, place, precision=PREP_PRECISION, preferred_element_type=F32)
        both = jnp.concatenate([e, _swap_halves(e)], axis=-1).reshape(n_groups, S5_GROUP, 4 * S5_STATE)
        for n_valid, bp_ref in zip(n_valids, var_refs[0::2]):
            if n_valid - 1 - k >= 0:
                bp_ref[:, n_valid - 1 - k] = both.astype(bp_ref.dtype)
        cl = cmul(cl, lr, lb)
        cpt_ref[:, k] = (cl * -sign).reshape(n_groups, S5_GROUP, 2 * S5_STATE).astype(cpt_ref.dtype)
        e = cmul(e, lr, lb)
        pw = cmul(pw, lr, lb)
        for n_valid, lam_out_ref in zip(n_valids, var_refs[1::2]):
            if k + 1 == n_valid:
                lam_out_ref[...] = pw
    for n_valid, bp_ref in zip(n_valids, var_refs[0::2]):
        for tau in range(n_valid, S5_SUB):
            bp_ref[:, tau] = jnp.zeros((n_groups, S5_GROUP, 4 * S5_STATE), bp_ref.dtype)
    for taup in range(S5_SUB):
        moved = lags if taup == 0 else jnp.where(zc >= taup * S5_GROUP, pltpu.roll(lags, taup * S5_GROUP, 1), 0.0)
        t_ref[:, taup] = moved.reshape(n_groups, S5_GROUP, S5_BLOCK).astype(t_ref.dtype)


def _s5_prep(lam_re, lam_im, log_dt, b_re, b_im, c_re, c_im, d_skip, n_valids):
    n_groups = lam_re.shape[0]
    gc = n_groups * S5_GROUP
    rep = lambda t: jnp.repeat(t, S5_GROUP, axis=0)
    lam_p = rep(jnp.concatenate([lam_re, lam_im], axis=-1))
    logdt = rep(log_dt.reshape(n_groups, 1))
    bt_p = jnp.concatenate([b_re.transpose(0, 2, 1), b_im.transpose(0, 2, 1)], axis=-1).reshape(gc, 2 * S5_STATE)
    c_p = jnp.concatenate([c_re, c_im], axis=-1).reshape(gc, 2 * S5_STATE)
    t_shape = (n_groups, S5_SUB, S5_GROUP, S5_BLOCK)
    cpt_shape = (n_groups, S5_SUB, S5_GROUP, 2 * S5_STATE)
    bp_shape = (n_groups, S5_SUB, S5_GROUP, 4 * S5_STATE)
    lam_shape = (gc, 2 * S5_STATE)
    var = [(bp_shape, BF16), (lam_shape, F32)] * len(n_valids)
    t, cpt, *rest = pl.pallas_call(
        functools.partial(_s5_prep_kernel, n_valids=tuple(n_valids)),
        grid=(1,),
        in_specs=[_full(lam_p.shape), _full(logdt.shape), _full(bt_p.shape), _full(c_p.shape), _full((gc, 1))],
        out_specs=[_full(t_shape), _full(cpt_shape), *[_full(s) for s, _ in var]],
        out_shape=[jax.ShapeDtypeStruct(t_shape, BF16), jax.ShapeDtypeStruct(cpt_shape, BF16),
                   *[jax.ShapeDtypeStruct(s, dt) for s, dt in var]],
        compiler_params=_params("arbitrary"),
        name="s5_prep",
    )(lam_p, logdt, bt_p, c_p, d_skip.reshape(gc, 1))
    merge = lambda a: a.reshape(n_groups, S5_BLOCK, a.shape[-1])
    return [(merge(t), merge(bp), merge(cpt), lam_n[::S5_GROUP]) for bp, lam_n in zip(rest[0::2], rest[1::2])]


SCAN_GROUPS = 16
SUBLANES = 8
GLU_ROWS = 512


def _s5_core_kernel(v_ref, x0_ref, x0s_ref, t_ref, bp_ref, cpt_ref, lam_ref, wglu_ref, bglu_ref,
                    mix_ref, xfin_ref, s_scr, xprev_scr, m_scr, stage, *, n_valid, n_sub):
    rows = v_ref.shape[0]
    n_groups = t_ref.shape[0]
    lane = lax.broadcasted_iota(jnp.int32, (1, 2 * S5_STATE), 1)
    first = lane < S5_STATE
    blk = lambda g: slice(g * S5_BLOCK, (g + 1) * S5_BLOCK)
    tile = lambda i: slice(i * 2 * S5_STATE, (i + 1) * 2 * S5_STATE)
    for g in range(n_groups):
        s_scr[:, blk(g)] = _dot(v_ref[:, blk(g)], bp_ref[g])
        lam = lam_ref[g:g + 1, :]
        lam_sw = _swap_halves(lam)
        m_scr[0:1, tile(g)] = jnp.where(first, lam, lam_sw)
        m_scr[1:2, tile(g)] = jnp.where(first, -lam_sw, lam)
    if n_sub == 1:
        for g in range(n_groups):
            x, xs = x0_ref[:, tile(g)], x0s_ref[:, tile(g)]
            xprev_scr[:, tile(g)] = x
            xfin_ref[:, tile(g)] = (x * m_scr[0:1, tile(g)] + xs * m_scr[1:2, tile(g)]
                                    + s_scr[:, tile(2 * g)])
    else:
        assert rows == n_sub
        for g0 in range(0, n_groups, SCAN_GROUPS):
            gs = range(g0, min(n_groups, g0 + SCAN_GROUPS))

            def body(i, carry, gs=gs):
                rows8 = pl.ds(pl.multiple_of(i * SUBLANES, SUBLANES), SUBLANES)
                sub_i = lax.broadcasted_iota(jnp.int32, (SUBLANES, 2 * S5_STATE), 0)
                out = []
                for g, (x, xs) in zip(gs, carry):
                    s8, ssw8 = s_scr[rows8, tile(2 * g)], s_scr[rows8, tile(2 * g + 1)]
                    prev8 = jnp.zeros((SUBLANES, 2 * S5_STATE), F32)
                    for r in range(SUBLANES):
                        prev8 = jnp.where(sub_i == r, x, prev8)
                        m_re, m_sw = m_scr[0:1, tile(g)], m_scr[1:2, tile(g)]
                        x, xs = (x * m_re + xs * m_sw + s8[r:r + 1, :],
                                 xs * m_re - x * m_sw + ssw8[r:r + 1, :])
                    xprev_scr[rows8, tile(g)] = prev8
                    out.append((x, xs))
                return tuple(out)
            init = tuple((x0_ref[0:1, tile(g)], x0s_ref[0:1, tile(g)]) for g in gs)
            fin = lax.fori_loop(0, n_sub // SUBLANES, body, init)
            for g, (x, _) in zip(gs, fin):
                xfin_ref[:, tile(g)] = jnp.broadcast_to(x, (xfin_ref.shape[0], 2 * S5_STATE))
    for g in range(n_groups):
        s_scr[:, blk(g)] = (_dot(v_ref[:, blk(g)], t_ref[g])
                            + _dot_nt(xprev_scr[:, tile(g)].astype(BF16), cpt_ref[g]))
    for cb in range(ATT_WIDTH // HEAD_DIM):
        for half in range(-(-n_valid // SLOTS)):
            toks = _slot_transpose([s_scr[:, tile(2 * (cb * SLOTS + p) + half)] for p in range(SLOTS)])
            for s in range(min(SLOTS, n_valid - half * SLOTS)):
                stage[cb, pl.ds(half * SLOTS + s, rows, stride=n_valid), :] = toks[s]
    n_tok = rows * n_valid
    chunk = min(n_tok, GLU_ROWS)

    def glu(i, carry):
        r = pl.ds(pl.multiple_of(i * chunk, chunk), chunk)
        y = jax.nn.gelu(jnp.concatenate([stage[cb, r, :] for cb in range(ATT_WIDTH // HEAD_DIM)], axis=-1))
        z = _dot(y.astype(BF16), wglu_ref[...]) + bglu_ref[...]
        mix_ref[r, :] = (y * jax.nn.sigmoid(z)).astype(mix_ref.dtype)
        return carry
    lax.fori_loop(0, n_tok // chunk, glu, 0)


def _s5_mixer(v, rows_per_tile, n_valid, n_sub, x0, x0s, t, bp, cpt, lam_n, w_glu_bf, b_glu):
    m, width = v.shape
    n_tiles = m // rows_per_tile
    r0 = x0.shape[0] // n_tiles
    n_state = x0.shape[1]
    n_tok = rows_per_tile * n_valid
    return pl.pallas_call(
        functools.partial(_s5_core_kernel, n_valid=n_valid, n_sub=n_sub),
        grid=(n_tiles,),
        in_specs=[pl.BlockSpec((rows_per_tile, width), lambda i: (i, 0)),
                  pl.BlockSpec((r0, n_state), lambda i: (i, 0)),
                  pl.BlockSpec((r0, n_state), lambda i: (i, 0)),
                  _full(t.shape), _full(bp.shape), _full(cpt.shape), _full(lam_n.shape),
                  _full(w_glu_bf.shape), _full((1, ATT_WIDTH))],
        out_specs=[pl.BlockSpec((n_tok, ATT_WIDTH), lambda i: (i, 0)),
                   pl.BlockSpec((r0, n_state), lambda i: (i, 0))],
        out_shape=[jax.ShapeDtypeStruct((m * n_valid, ATT_WIDTH), BF16),
                   jax.ShapeDtypeStruct(x0.shape, F32)],
        scratch_shapes=[pltpu.VMEM((rows_per_tile, width), F32),
                        pltpu.VMEM((rows_per_tile, n_state), F32),
                        pltpu.VMEM((8, n_state), F32),
                        pltpu.VMEM((ATT_WIDTH // HEAD_DIM, n_tok, HEAD_DIM), F32)],
        compiler_params=_params("arbitrary"),
        name="s5_core",
    )(v, x0, x0s, t, bp, cpt, lam_n, w_glu_bf, b_glu.reshape(1, ATT_WIDTH))


def _pack_state(s):
    n = s.shape[0]
    packed = s.transpose(0, 2, 1, 3).reshape(n, -1)
    swapped = jnp.stack([s[:, 1], s[:, 0]], axis=1).transpose(0, 2, 1, 3).reshape(n, -1)
    return packed, swapped


def _unpack_state(x):
    n = x.shape[0]
    return x.reshape(n, -1, 2, S5_STATE).transpose(0, 2, 1, 3)


PROMPT_TILE = 512
Q_PAD = 16


def _pad_rows(t, n):
    return jnp.pad(t, ((0, 0), (0, n - t.shape[1]), (0, 0)))


def kernel(x_prompt, x_sample, cache_win0_kv, cache_win1_kv, cache_win2_kv, cache_mem_kv, state_s5,
           state_ffn_conv, mem_prompt, g_mix, g_ffn, w_in_a, g_q_dil, g_k_dil, w_in_b, s5_lam_re,
           s5_lam_im, s5_log_dt, s5_b_re, s5_b_im, s5_c_re, s5_c_im, s5_d, w_glu, b_glu, g_mem,
           w_mem_kv, g_q_cross, g_k_cross, w_out, w_up, conv_w, conv_b, w_down):
    nb, seq, d = x_prompt.shape
    db, ts, _ = x_sample.shape
    depth = g_mix.shape[0]
    n_mem = mem_prompt.shape[1]
    d_ff2 = w_up.shape[2]
    assert ts >= CONV_W - 1 and seq % PROMPT_TILE == 0 and ts <= Q_PAD
    caches = (cache_win0_kv, cache_win1_kv, cache_win2_kv)
    w_mem_bf, w_out_bf, w_up_bf, w_down_bf = (w.astype(BF16) for w in (w_mem_kv, w_out, w_up, w_down))
    mem_cache = cache_mem_kv.reshape(depth * db, n_mem * KV_ROWS, HEAD_DIM)

    tab_p = _rope_tables(jnp.arange(seq, dtype=jnp.int32))
    tab_s = tuple(jnp.tile(t, (db, 1)) for t in _rope_tables(PAST_LEN + jnp.arange(ts, dtype=jnp.int32)))
    win_keep = tuple(min(w, seq) for w, _ in DIL_GROUPS)
    rows_s = db * ts
    kv_tail = (2, N_HEADS, HEAD_DIM)

    mem_p = _mem_kv(mem_prompt.reshape(nb * n_mem, d), g_mem, w_mem_bf, g_k_cross)

    xp = x_prompt.reshape(nb * seq, d)
    xs = x_sample.reshape(rows_s, d)
    p_win, s_win = [[] for _ in DIL_GROUPS], [[] for _ in DIL_GROUPS]
    p_s5, s_s5, p_conv, s_conv = [], [], [], []
    tiles_per_seq = seq // PROMPT_TILE
    for i in range(depth):
        mem_i = mem_p[i].reshape(nb, n_mem * KV_ROWS, HEAD_DIM)
        if i % 2 == 0:
            ia = i // 2
            w_in_a_bf = w_in_a[ia].astype(BF16)
            *qkv_groups, w0, w1, w2, cross_p = _in_proj_a(
                xp, seq, PROMPT_TILE, g_mix[i], w_in_a_bf, tab_p, g_q_dil[ia], g_k_dil[ia],
                g_q_cross[i], mem_i, win_keep, dilated=True)
            for g, w in enumerate((w0, w1, w2)):
                p_win[g].append(w.reshape((nb, win_keep[g]) + kv_tail))
            mix_p = _attn_prompt(qkv_groups, nb, seq).reshape(nb * seq, ATT_WIDTH)
            qkv_s, *kv_new, qc_s = _in_proj_a(
                xs, rows_s, rows_s, g_mix[i], w_in_a_bf, tab_s, g_q_dil[ia], g_k_dil[ia],
                None, None, (rows_s,) * N_GROUPS, dilated=False)
            old_bufs = [c[ia].reshape(db, -1, HEAD_DIM) for c in caches]
            moves = (old_bufs, [k.reshape(db, ts * KV_ROWS, HEAD_DIM) for k in kv_new])
        else:
            moves = None
            ib = i // 2
            prm = (s5_lam_re[ib], s5_lam_im[ib], s5_log_dt[ib], s5_b_re[ib], s5_b_im[ib], s5_c_re[ib],
                   s5_c_im[ib], s5_d[ib])
            n_state = s5_lam_re.shape[1] * 2 * S5_STATE
            w_in_b_bf, w_glu_bf = w_in_b[ib].astype(BF16), w_glu[ib].astype(BF16)
            u_p, cross_p = _in_proj_b(xp, seq, PROMPT_TILE, S5_SUB, g_mix[i], w_in_b_bf, g_q_cross[i], mem_i)
            zero = jnp.zeros((nb * CARRY_ROWS, n_state), F32)
            prep_p, prep_s = _s5_prep(*prm, (S5_SUB, ts))
            mix_p, fin_p = _s5_mixer(u_p, seq // S5_SUB, S5_SUB, seq // S5_SUB, zero, zero,
                                     *prep_p, w_glu_bf, b_glu[ib])
            p_s5.append(_unpack_state(fin_p.reshape(nb, CARRY_ROWS, n_state)[:, 0]))
            u_s, qc_s = _in_proj_b(xs, rows_s, rows_s, ts, g_mix[i], w_in_b_bf, None, None)
            mix_s, fin_s = _s5_mixer(u_s, db, ts, 1, *_pack_state(state_s5[ib]), *prep_s, w_glu_bf, b_glu[ib])
            s_s5.append(_unpack_state(fin_s))
        cross_s = _cross_sample(_pad_rows(qc_s.reshape(db, ts, ATT_WIDTH), Q_PAD), g_q_cross[i], mem_cache,
                                first=i * db)
        cross_s = cross_s[:, :ts].reshape(rows_s, ATT_WIDTH)

        ffn_w = (i, w_out_bf, g_ffn, w_up_bf, conv_w, conv_b, w_down_bf)
        xp, tails, *new_bufs = _out_ffn(xp, mix_p, cross_p, *ffn_w, PROMPT_TILE, tiles_per_seq=tiles_per_seq,
                                        moves=moves)
        p_conv.append(tails.reshape(nb, tiles_per_seq, CARRY_ROWS, d_ff2)[:, -1, CARRY_ROWS - (CONV_W - 1):])
        if moves is not None:
            outs, lses = [], []
            for g, (_, r) in enumerate(DIL_GROUPS):
                q_g = qkv_s[g * N_HEADS:(g + 1) * N_HEADS]
                q_g = _pad_rows(q_g.transpose(1, 0, 2).reshape(db, ts, ATT_WIDTH), Q_PAD)
                o_g, lse_g = _attn_sample(q_g, new_bufs[g], moves[0][g], r, ts)
                outs.append(o_g.reshape(rows_s, ATT_WIDTH))
                lses.append(lse_g.reshape(rows_s, ATT_WIDTH))
                s_win[g].append(new_bufs[g].reshape(caches[g][i // 2].shape))
            mix_s = _combine_groups(outs, lses)
        buf = state_ffn_conv[i]
        zero = jnp.zeros((db, ts - 2, d_ff2), F32)
        e1 = jnp.concatenate([buf[:, 1:2], zero, zero[:, :1]], axis=1).reshape(rows_s, d_ff2)
        e2 = jnp.concatenate([buf, zero], axis=1).reshape(rows_s, d_ff2)
        xs, tails = _out_ffn(xs, mix_s, cross_s, *ffn_w, rows_s, seq_len=ts, e1=e1, e2=e2, tail_rows=rows_s)
        s_conv.append(tails.reshape(db, ts, d_ff2)[:, ts - (CONV_W - 1):])

    return (xp.reshape(nb, seq, d), xs.reshape(db, ts, d),
            jnp.stack(p_win[0]), jnp.stack(p_win[1]), jnp.stack(p_win[2]),
            mem_p.reshape((depth, nb, n_mem) + kv_tail),
            jnp.stack(p_s5), jnp.stack(p_conv),
            jnp.stack(s_win[0]), jnp.stack(s_win[1]), jnp.stack(s_win[2]),
            jnp.stack(s_s5), jnp.stack(s_conv))
```

```python
import functools
import math

import jax
import jax.numpy as jnp
from jax import lax
from jax.experimental import pallas as pl
from jax.experimental.pallas import tpu as pltpu

HEAD_DIM = 128
N_HEADS = 4
DIL_GROUPS = ((128, 1), (512, 4), (2048, 16))
N_GROUPS = len(DIL_GROUPS)
DIL_SPAN = 128
BLOCK = 128
ATT_WIDTH = N_HEADS * HEAD_DIM
KV_ROWS = 2 * N_HEADS
ROT_DIM = HEAD_DIM // 4
ROT_HALF = ROT_DIM // 2
ROPE_THETA = 500000.0
S5_GROUP = 16
S5_STATE = 64
S5_SUB = 16
S5_BLOCK = S5_SUB * S5_GROUP
CONV_W = 3
EPS = 1e-6
NEG = -1e30
SCALE = HEAD_DIM ** -0.5
PAST_LEN = 16384

VMEM_LIMIT_V7X = 56 * 1024 * 1024
BF16 = jnp.bfloat16
F32 = jnp.float32
PREP_PRECISION = lax.Precision.HIGHEST


def _params(*sem):
    return pltpu.CompilerParams(dimension_semantics=sem, vmem_limit_bytes=VMEM_LIMIT_V7X)


def _rms(x, g):
    return x * lax.rsqrt(jnp.mean(x * x, axis=-1, keepdims=True) + EPS) * g


def _rms_head(x, g):
    ones = jnp.ones((HEAD_DIM, HEAD_DIM), BF16)
    ssq = _dot((x * x).astype(BF16), ones)
    return x * lax.rsqrt(ssq * (1.0 / HEAD_DIM) + EPS) * g


def _dot(a, b):
    return jnp.dot(a, b, preferred_element_type=F32)


def _dot_nt(a, b, precision=None):
    return lax.dot_general(a, b, (((1,), (1,)), ((), ())), precision=precision,
                           preferred_element_type=F32)


def _full(shape):
    nd = len(shape)
    return pl.BlockSpec(shape, lambda *_: (0,) * nd)


def _kv_rows(kv, head, n):
    return pl.ds(kv * N_HEADS + head, n, stride=KV_ROWS)


def _mem_kv_kernel(mem_ref, g_ref, w_ref, gk_ref, out_ref):
    tm = mem_ref.shape[0]
    h = _rms(mem_ref[...], g_ref[0]).astype(BF16)
    kv = _dot(h, w_ref[0])
    for hd in range(N_HEADS):
        sl = slice(hd * HEAD_DIM, (hd + 1) * HEAD_DIM)
        out_ref[0, _kv_rows(0, hd, tm), :] = _rms(kv[:, sl], gk_ref[0])
        out_ref[0, _kv_rows(1, hd, tm), :] = kv[:, ATT_WIDTH + hd * HEAD_DIM:ATT_WIDTH + (hd + 1) * HEAD_DIM]


def _mem_kv(mem2d, g_mem, w_kv_bf, g_k):
    depth, d, _ = w_kv_bf.shape
    rows = mem2d.shape[0]
    tm = min(rows, 512)
    return pl.pallas_call(
        _mem_kv_kernel,
        grid=(depth, rows // tm),
        in_specs=[
            pl.BlockSpec((tm, d), lambda l, i: (i, 0)),
            pl.BlockSpec((1, 1, d), lambda l, i: (l, 0, 0)),
            pl.BlockSpec((1, d, 2 * ATT_WIDTH), lambda l, i: (l, 0, 0)),
            pl.BlockSpec((1, 1, HEAD_DIM), lambda l, i: (l, 0, 0)),
        ],
        out_specs=pl.BlockSpec((1, tm * KV_ROWS, HEAD_DIM), lambda l, i: (l, i, 0)),
        out_shape=jax.ShapeDtypeStruct((depth, rows * KV_ROWS, HEAD_DIM), F32),
        compiler_params=_params("arbitrary", "arbitrary"),
        name="mem_kv",
    )(mem2d, g_mem.reshape(depth, 1, d), w_kv_bf, g_k.reshape(depth, 1, HEAD_DIM))


def _cross_heads(qc, mem_ref, gq):
    n_mem = mem_ref.shape[1] // KV_ROWS
    outs = []
    for hd in range(N_HEADS):
        q = _rms(qc[:, hd * HEAD_DIM:(hd + 1) * HEAD_DIM], gq).astype(BF16)
        k = mem_ref[0, _kv_rows(0, hd, n_mem), :].astype(BF16)
        v = mem_ref[0, _kv_rows(1, hd, n_mem), :].astype(BF16)
        s = _dot_nt(q, k) * SCALE
        m = jnp.max(s, axis=-1, keepdims=True)
        p = jnp.exp(s - m)
        l = jnp.sum(p, axis=-1, keepdims=True)
        outs.append(_dot(p.astype(BF16), v) / l)
    return outs


def _rope(x, cos_t, sin_lo, sin_hi):
    return (x * cos_t + pltpu.roll(x, HEAD_DIM - ROT_HALF, 1) * sin_lo
            + pltpu.roll(x, ROT_HALF, 1) * sin_hi)


def _in_proj_a_kernel(*refs, fuse_cross, dilated, win_rows, win_first, tiles_per_seq):
    n_in = 10 if fuse_cross else 8
    x_ref, g_ref, w_ref, cos_ref, slo_ref, shi_ref, gq_ref, gk_ref = refs[:8]
    n_qkv = N_GROUPS if dilated else 1
    qkv_refs = refs[n_in:n_in + n_qkv]
    win_refs = refs[n_in + n_qkv:n_in + n_qkv + N_GROUPS]
    cr_ref = refs[n_in + n_qkv + N_GROUPS]
    kv_stash = refs[n_in + n_qkv + N_GROUPS + 1]
    stage = refs[n_in + n_qkv + N_GROUPS + 2] if dilated else None
    tm = x_ref.shape[0]
    h = _rms(x_ref[...], g_ref[...]).astype(BF16)
    cos_t, sin_lo, sin_hi = cos_ref[...], slo_ref[...], shi_ref[...]
    def proj(c):
        return _dot(h, w_ref[:, c * ATT_WIDTH:(c + 1) * ATT_WIDTH])

    tile = pl.program_id(0) % tiles_per_seq
    in_window = [tile >= first for first in win_first]
    for g in range(N_GROUPS):
        @pl.when(jnp.logical_not(in_window[g]))
        def _(g=g):
            win_refs[g][...] = jnp.zeros_like(win_refs[g])

    y = proj(0)
    for c in range(3 * N_GROUPS):
        y_next = proj(c + 1)
        role, g = divmod(c, N_GROUPS)
        r = DIL_GROUPS[g][1]
        wr = win_rows[g]
        for hd in range(N_HEADS):
            yh = y[:, hd * HEAD_DIM:(hd + 1) * HEAD_DIM]
            if role == 0:
                yh = _rope(_rms_head(yh, gq_ref[g]), cos_t, sin_lo, sin_hi)
            elif role == 1:
                yh = _rope(_rms_head(yh, gk_ref[g]), cos_t, sin_lo, sin_hi)
            if role > 0:
                kv_stash[((role - 1) * N_GROUPS + g) * N_HEADS + hd] = yh
            if not dilated:
                qkv_refs[0][role * N_GROUPS * N_HEADS + g * N_HEADS + hd] = yh.astype(BF16)
            elif r == 1:
                qkv_refs[g][role * N_HEADS + hd, 0] = yh.astype(BF16)
            else:
                stage[hd] = yh
                for rho in range(r):
                    qkv_refs[g][role * N_HEADS + hd, 0, :, rho * HEAD_DIM:(rho + 1) * HEAD_DIM] = (
                        stage[hd, pl.ds(rho, tm // r, stride=r), :].astype(BF16))
        y = y_next
    qc = y
    if fuse_cross:
        outs = _cross_heads(qc, refs[9], refs[8][...])
        for hd in range(N_HEADS):
            cr_ref[:, hd * HEAD_DIM:(hd + 1) * HEAD_DIM] = outs[hd].astype(cr_ref.dtype)
    else:
        cr_ref[...] = qc
    for g in range(N_GROUPS):
        @pl.when(in_window[g])
        def _(g=g):
            wr = win_rows[g]
            for kv in range(2):
                for hd in range(N_HEADS):
                    win_refs[g][_kv_rows(kv, hd, wr), :] = kv_stash[(kv * N_GROUPS + g) * N_HEADS + hd,
                                                                    tm - wr:, :]


SLOTS = HEAD_DIM // S5_GROUP


def _slot_transpose(vs):
    slot = lax.broadcasted_iota(jnp.int32, vs[0].shape, 1) // S5_GROUP
    d = SLOTS // 2
    while d:
        low = (slot & d) == 0
        new = list(vs)
        for i in range(SLOTS):
            if not i & d:
                new[i] = jnp.where(low, vs[i], pltpu.roll(vs[i | d], d * S5_GROUP, 1))
                new[i | d] = jnp.where(low, pltpu.roll(vs[i], HEAD_DIM - d * S5_GROUP, 1), vs[i | d])
        vs = new
        d //= 2
    return vs


def _in_proj_b_kernel(*refs, fuse_cross, n_valid):
    if fuse_cross:
        x_ref, g_ref, w_ref, gqc_ref, mem_ref, u_ref, cr_ref, stage = refs
    else:
        x_ref, g_ref, w_ref, u_ref, cr_ref, stage = refs
    tm = x_ref.shape[0]
    n_rows = tm // n_valid
    h = _rms(x_ref[...], g_ref[...]).astype(BF16)
    u = _dot(h, w_ref[:, :ATT_WIDTH])
    zero = jnp.zeros((n_rows, HEAD_DIM), F32)
    for cb in range(ATT_WIDTH // HEAD_DIM):
        stage[cb] = u[:, cb * HEAD_DIM:(cb + 1) * HEAD_DIM]
        for half in range(S5_SUB // SLOTS):
            taus = range(half * SLOTS, (half + 1) * SLOTS)
            groups = _slot_transpose([stage[cb, pl.ds(tau, n_rows, stride=n_valid), :] if tau < n_valid else zero
                                      for tau in taus])
            for p in range(SLOTS):
                lanes = (2 * (cb * SLOTS + p) + half) * HEAD_DIM
                u_ref[:, lanes:lanes + HEAD_DIM] = groups[p].astype(BF16)
    qc = _dot(h, w_ref[:, ATT_WIDTH:])
    if fuse_cross:
        outs = _cross_heads(qc, mem_ref, gqc_ref[...])
        for hd in range(N_HEADS):
            cr_ref[:, hd * HEAD_DIM:(hd + 1) * HEAD_DIM] = outs[hd].astype(cr_ref.dtype)
    else:
        cr_ref[...] = qc


def _rope_tables(pos):
    inv = jnp.exp(-math.log(ROPE_THETA) * jnp.arange(ROT_HALF, dtype=F32) / ROT_HALF)
    ang = pos.astype(F32)[:, None] * inv[None, :]
    cos, sin = jnp.cos(ang), jnp.sin(ang)
    rows = pos.shape[0]
    ones = jnp.ones((rows, HEAD_DIM - ROT_DIM), F32)
    zeros = jnp.zeros((rows, HEAD_DIM - ROT_DIM), F32)
    z16 = jnp.zeros((rows, ROT_HALF), F32)
    cos_t = jnp.concatenate([cos, cos, ones], axis=1)
    sin_lo = jnp.concatenate([-sin, z16, zeros], axis=1)
    sin_hi = jnp.concatenate([z16, sin, zeros], axis=1)
    return cos_t, sin_lo, sin_hi


def _in_proj_a(x2d, seq, tm, g_mix, w_bf, tables, g_q, g_k, g_qc, mem_kv, win_keep, dilated):
    rows, d = x2d.shape
    n_in = w_bf.shape[1]
    tiles_per_seq = seq // tm
    n_seq = rows // seq
    fuse_cross = mem_kv is not None
    tab_tiles = tables[0].shape[0] // tm
    win_rows = tuple(min(k, tm) for k in win_keep)
    win_first = tuple(tiles_per_seq - k // wr for k, wr in zip(win_keep, win_rows))

    def win_spec(keep):
        wr = min(keep, tm)
        nblk = keep // wr
        first = tiles_per_seq - nblk

        def imap(i):
            b = i // tiles_per_seq
            t = i % tiles_per_seq
            return (b * nblk + jnp.maximum(t - first, 0), 0)
        return pl.BlockSpec((wr * KV_ROWS, HEAD_DIM), imap)

    in_specs = [
        pl.BlockSpec((tm, d), lambda i: (i, 0)),
        _full((1, d)),
        _full((d, n_in)),
        pl.BlockSpec((tm, HEAD_DIM), lambda i: (i % tab_tiles, 0)),
        pl.BlockSpec((tm, HEAD_DIM), lambda i: (i % tab_tiles, 0)),
        pl.BlockSpec((tm, HEAD_DIM), lambda i: (i % tab_tiles, 0)),
        _full((N_GROUPS, HEAD_DIM)),
        _full((N_GROUPS, HEAD_DIM)),
    ]
    args = [x2d, g_mix.reshape(1, d), w_bf, *tables, g_q, g_k]
    if fuse_cross:
        in_specs += [_full((1, HEAD_DIM)),
                     pl.BlockSpec((1,) + mem_kv.shape[1:], lambda i: (i // tiles_per_seq, 0, 0))]
        args += [g_qc.reshape(1, HEAD_DIM), mem_kv]
    if dilated:
        qkv_specs = [pl.BlockSpec((3 * N_HEADS, 1, tm // r, r * HEAD_DIM),
                                  lambda i: (0, i // tiles_per_seq, i % tiles_per_seq, 0))
                     for _, r in DIL_GROUPS]
        qkv_shapes = [jax.ShapeDtypeStruct((3 * N_HEADS, n_seq, seq // r, r * HEAD_DIM), BF16)
                      for _, r in DIL_GROUPS]
        scratch = [pltpu.VMEM((N_HEADS, tm, HEAD_DIM), F32)]
    else:
        qkv_specs = [pl.BlockSpec((3 * N_GROUPS * N_HEADS, tm, HEAD_DIM), lambda i: (0, i, 0))]
        qkv_shapes = [jax.ShapeDtypeStruct((3 * N_GROUPS * N_HEADS, rows, HEAD_DIM), BF16)]
        scratch = []
    scratch = [pltpu.VMEM((2 * N_GROUPS * N_HEADS, tm, HEAD_DIM), F32)] + scratch
    out_specs = [*qkv_specs, *[win_spec(k) for k in win_keep],
                 pl.BlockSpec((tm, ATT_WIDTH), lambda i: (i, 0))]
    out_shape = [*qkv_shapes,
                 *[jax.ShapeDtypeStruct((n_seq * k * KV_ROWS, HEAD_DIM), F32) for k in win_keep],
                 jax.ShapeDtypeStruct((rows, ATT_WIDTH), BF16 if fuse_cross else F32)]
    return pl.pallas_call(
        functools.partial(_in_proj_a_kernel, fuse_cross=fuse_cross, dilated=dilated, win_rows=win_rows,
                          win_first=win_first, tiles_per_seq=tiles_per_seq),
        grid=(rows // tm,),
        in_specs=in_specs, out_specs=out_specs, out_shape=out_shape, scratch_shapes=scratch,
        compiler_params=_params("arbitrary"),
        name="in_proj_a",
    )(*args)


def _in_proj_b(x2d, seq, tm, n_valid, g_mix, w_bf, g_qc, mem_kv):
    rows, d = x2d.shape
    width = (ATT_WIDTH // S5_GROUP) * S5_BLOCK
    n_in = w_bf.shape[1]
    tiles_per_seq = seq // tm
    fuse_cross = mem_kv is not None
    in_specs = [pl.BlockSpec((tm, d), lambda i: (i, 0)), _full((1, d)), _full((d, n_in))]
    args = [x2d, g_mix.reshape(1, d), w_bf]
    if fuse_cross:
        in_specs += [_full((1, HEAD_DIM)),
                     pl.BlockSpec((1,) + mem_kv.shape[1:], lambda i: (i // tiles_per_seq, 0, 0))]
        args += [g_qc.reshape(1, HEAD_DIM), mem_kv]
    return pl.pallas_call(
        functools.partial(_in_proj_b_kernel, fuse_cross=fuse_cross, n_valid=n_valid),
        grid=(rows // tm,),
        in_specs=in_specs,
        out_specs=[pl.BlockSpec((tm // n_valid, width), lambda i: (i, 0)),
                   pl.BlockSpec((tm, ATT_WIDTH), lambda i: (i, 0))],
        out_shape=[jax.ShapeDtypeStruct((rows // n_valid, width), BF16),
                   jax.ShapeDtypeStruct((rows, ATT_WIDTH), BF16 if fuse_cross else F32)],
        scratch_shapes=[pltpu.VMEM((ATT_WIDTH // HEAD_DIM, tm, HEAD_DIM), F32)],
        compiler_params=_params("arbitrary"),
        name="in_proj_b",
    )(*args)


CROSS_BATCH = 4


def _cross_sample_kernel(qc_ref, gq_ref, mem_ref, out_ref):
    for b in range(qc_ref.shape[0]):
        outs = _cross_heads(qc_ref[b], mem_ref.at[pl.ds(b, 1)], gq_ref[...])
        for hd in range(N_HEADS):
            out_ref[b, :, hd * HEAD_DIM:(hd + 1) * HEAD_DIM] = outs[hd].astype(out_ref.dtype)


def _cross_sample(qc, g_qc, mem_kv, first):
    nb, tq, _ = qc.shape
    cb = math.gcd(nb, CROSS_BATCH)
    assert first % cb == 0
    return pl.pallas_call(
        _cross_sample_kernel,
        grid=(nb // cb,),
        in_specs=[pl.BlockSpec((cb, tq, ATT_WIDTH), lambda b: (b, 0, 0)),
                  _full((1, HEAD_DIM)),
                  pl.BlockSpec((cb,) + mem_kv.shape[1:], lambda b: (first // cb + b, 0, 0))],
        out_specs=pl.BlockSpec((cb, tq, ATT_WIDTH), lambda b: (b, 0, 0)),
        out_shape=jax.ShapeDtypeStruct((nb, tq, ATT_WIDTH), BF16),
        compiler_params=_params("arbitrary"),
        name="cross_sample",
    )(qc, g_qc.reshape(1, HEAD_DIM), mem_kv)


INFLIGHT = 16


def _band_block(q, k, v):
    n = k.shape[0]
    dist = (n - BLOCK + lax.broadcasted_iota(jnp.int32, (BLOCK, n), 0)
            - lax.broadcasted_iota(jnp.int32, (BLOCK, n), 1))
    s = jnp.where((dist >= 0) & (dist <= DIL_SPAN), _dot_nt(q, k) * SCALE, NEG)
    m = jnp.max(s, axis=-1, keepdims=True)
    p = jnp.exp(s - m)
    l = jnp.sum(p, axis=-1, keepdims=True)
    return _dot(p.astype(BF16), v) / l, m + jnp.log(l)


def _attn_prompt_kernel(q0, q1, q2, k0, k1, k2, v0, v1, v2, out_ref, o_scr, l_scr, *, seq):
    qkv_refs = ((q0, k0, v0), (q1, k1, v1), (q2, k2, v2))
    for g, (_, r) in enumerate(DIL_GROUPS):
        q_ref, k_ref, v_ref = qkv_refs[g]
        nblk = seq // r // BLOCK

        def store(rho, blk, o, lse, g=g, r=r):
            start = blk * (BLOCK * r) + rho
            if r == 1:
                idx = pl.ds(pl.multiple_of(start, BLOCK), BLOCK)
            else:
                idx = pl.ds(start, BLOCK, stride=r)
            o_scr[g, idx, :] = o
            l_scr[g, idx, :] = jnp.broadcast_to(lse, (BLOCK, HEAD_DIM))

        def first(rho, q_ref=q_ref, k_ref=k_ref, v_ref=v_ref, store=store):
            lanes = slice(rho * HEAD_DIM, (rho + 1) * HEAD_DIM)
            o, lse = _band_block(q_ref[0, 0, :BLOCK, lanes], k_ref[0, 0, :BLOCK, lanes],
                                 v_ref[0, 0, :BLOCK, lanes])
            store(rho, 0, o, lse)

        def later(rho, blk, q_ref=q_ref, k_ref=k_ref, v_ref=v_ref, store=store):
            lanes = slice(rho * HEAD_DIM, (rho + 1) * HEAD_DIM)
            cur = pl.ds(pl.multiple_of(blk * BLOCK, BLOCK), BLOCK)
            both = pl.ds(pl.multiple_of((blk - 1) * BLOCK, BLOCK), 2 * BLOCK)
            o, lse = _band_block(q_ref[0, 0, cur, lanes], k_ref[0, 0, both, lanes], v_ref[0, 0, both, lanes])
            store(rho, blk, o, lse)

        for rho0 in range(0, r, INFLIGHT):
            rhos = range(rho0, min(r, rho0 + INFLIGHT))
            for rho in rhos:
                first(rho)

            def body(blk, carry, rhos=rhos, later=later):
                for rho in rhos:
                    later(rho, blk)
                return carry
            if nblk > 1:
                lax.fori_loop(1, nblk, body, 0, unroll=max(1, INFLIGHT // len(rhos)))

    def combine(c, carry):
        rows = pl.ds(pl.multiple_of(c * BLOCK, BLOCK), BLOCK)
        ls = [l_scr[g, rows, :] for g in range(N_GROUPS)]
        m = jnp.maximum(jnp.maximum(ls[0], ls[1]), ls[2])
        es = [jnp.exp(l - m) for l in ls]
        num = es[0] * o_scr[0, rows, :] + es[1] * o_scr[1, rows, :] + es[2] * o_scr[2, rows, :]
        out_ref[0, rows, :] = (num / (es[0] + es[1] + es[2])).astype(out_ref.dtype)
        return carry
    lax.fori_loop(0, seq // BLOCK, combine, 0)


def _attn_prompt(qkv_groups, n_seq, seq):
    in_specs, args = [], []
    for role in range(3):
        for g, (_, r) in enumerate(DIL_GROUPS):
            in_specs.append(pl.BlockSpec((1, 1, seq // r, r * HEAD_DIM),
                                         lambda b, h, role=role: (role * N_HEADS + h, b, 0, 0)))
            args.append(qkv_groups[g])
    return pl.pallas_call(
        functools.partial(_attn_prompt_kernel, seq=seq),
        grid=(n_seq, N_HEADS),
        in_specs=in_specs,
        out_specs=pl.BlockSpec((1, seq, HEAD_DIM), lambda b, h: (b, 0, h)),
        out_shape=jax.ShapeDtypeStruct((n_seq, seq, ATT_WIDTH), BF16),
        scratch_shapes=[pltpu.VMEM((N_GROUPS, seq, HEAD_DIM), F32),
                        pltpu.VMEM((N_GROUPS, seq, HEAD_DIM), F32)],
        compiler_params=_params("arbitrary", "arbitrary"),
        name="attn_prompt",
    )(*args)


def _attn_sample_kernel(q_ref, new_ref, old_ref, o_ref, lse_ref, *scratch, r, t_new, lb, compact):
    qf = q_ref[0].astype(F32)
    if compact:
        (flat,) = scratch
        n_key = lb // r
        for u in range(t_new):
            flat[u] = new_ref[:, u * KV_ROWS:(u + 1) * KV_ROWS, :].reshape(n_key * KV_ROWS, HEAD_DIM)
    else:
        n_key = lb
    i_key = lax.broadcasted_iota(jnp.int32, (n_key, 1), 0)
    j_old = lax.broadcasted_iota(jnp.int32, (t_new, 1), 0)
    for t in range(t_new):
        if compact:
            idx = i_key * r + (r - t_new + t)
            rows = lambda kvh, t=t: flat[t, pl.ds(kvh, n_key, stride=KV_ROWS), :]
        else:
            idx = i_key
            rows = lambda kvh: new_ref[pl.ds(kvh, n_key, stride=KV_ROWS), :]
        dist = lb - t_new + t - idx
        ok = (dist >= 0) & (dist % r == 0) & (dist <= r * DIL_SPAN)
        dist_old = lb + t - j_old
        ok_old = (dist_old % r == 0) & (dist_old <= r * DIL_SPAN)
        for hd in range(N_HEADS):
            sl = slice(hd * HEAD_DIM, (hd + 1) * HEAD_DIM)
            q = qf[t:t + 1, sl]
            k, v = rows(hd), rows(N_HEADS + hd)
            k_old = old_ref[pl.ds(hd, t_new, stride=KV_ROWS), :]
            v_old = old_ref[pl.ds(N_HEADS + hd, t_new, stride=KV_ROWS), :]
            s = jnp.where(ok, jnp.sum(k * q, axis=-1, keepdims=True) * SCALE, NEG)
            s_old = jnp.where(ok_old, jnp.sum(k_old * q, axis=-1, keepdims=True) * SCALE, NEG)
            m = jnp.maximum(jnp.max(s, axis=0, keepdims=True), jnp.max(s_old, axis=0, keepdims=True))
            p, p_old = jnp.exp(s - m), jnp.exp(s_old - m)
            l = jnp.sum(p, axis=0, keepdims=True) + jnp.sum(p_old, axis=0, keepdims=True)
            o = jnp.sum(p * v, axis=0, keepdims=True) + jnp.sum(p_old * v_old, axis=0, keepdims=True)
            o_ref[0, t:t + 1, sl] = o / l
            lse_ref[0, t:t + 1, sl] = jnp.broadcast_to(m + jnp.log(l), (1, HEAD_DIM))


def _attn_sample(q, new_buf, old_buf, r, t_new):
    nb, tq, _ = q.shape
    lb = new_buf.shape[1] // KV_ROWS
    compact = r % t_new == 0
    if compact:
        new_view = new_buf.reshape(nb, lb // r, r // t_new, t_new * KV_ROWS, HEAD_DIM)
        new_spec = pl.BlockSpec((None, lb // r, None, t_new * KV_ROWS, HEAD_DIM),
                                lambda b: (b, 0, r // t_new - 1, 0, 0))
        scratch = [pltpu.VMEM((t_new, lb // r * KV_ROWS, HEAD_DIM), F32)]
    else:
        new_view = new_buf
        new_spec = pl.BlockSpec((None, lb * KV_ROWS, HEAD_DIM), lambda b: (b, 0, 0))
        scratch = []
    return pl.pallas_call(
        functools.partial(_attn_sample_kernel, r=r, t_new=t_new, lb=lb, compact=compact),
        grid=(nb,),
        in_specs=[pl.BlockSpec((1, tq, ATT_WIDTH), lambda b: (b, 0, 0)),
                  new_spec,
                  pl.BlockSpec((None, t_new * KV_ROWS, HEAD_DIM), lambda b: (b, 0, 0))],
        out_specs=[pl.BlockSpec((1, t_new, ATT_WIDTH), lambda b: (b, 0, 0)),
                   pl.BlockSpec((1, t_new, ATT_WIDTH), lambda b: (b, 0, 0))],
        out_shape=[jax.ShapeDtypeStruct((nb, t_new, ATT_WIDTH), F32),
                   jax.ShapeDtypeStruct((nb, t_new, ATT_WIDTH), F32)],
        scratch_shapes=scratch,
        compiler_params=_params("arbitrary"),
        name=f"attn_sample_r{r}",
    )(q, new_view, old_buf)


def _combine_kernel(o0, o1, o2, l0, l1, l2, out_ref):
    ls = [l0[...], l1[...], l2[...]]
    m = jnp.maximum(jnp.maximum(ls[0], ls[1]), ls[2])
    es = [jnp.exp(l - m) for l in ls]
    num = es[0] * o0[...] + es[1] * o1[...] + es[2] * o2[...]
    out_ref[...] = (num / (es[0] + es[1] + es[2])).astype(out_ref.dtype)


def _combine_groups(outs, lses):
    shape = outs[0].shape
    return pl.pallas_call(
        _combine_kernel,
        in_specs=[_full(shape)] * 6,
        out_specs=_full(shape),
        out_shape=jax.ShapeDtypeStruct(shape, BF16),
        grid=(1,),
        compiler_params=_params("arbitrary"),
        name="combine_groups",
    )(*outs, *lses)


FF_CHUNK = 256
CARRY_ROWS = 8
SHIFT_SLOTS = 4


MOVE_ROWS = 8192


def _move_regions(keeps):
    regions = [(g, start, min(MOVE_ROWS, keep - start)) for g, keep in enumerate(keeps)
               for start in range(0, keep, MOVE_ROWS)]
    halves = ([], [])
    for reg in sorted(regions, key=lambda reg: -reg[2]):
        min(halves, key=lambda h: sum(r[2] for r in h)).append(reg)
    return halves


def _buffer_move(old_refs, kvn_refs, new_refs, bufs, tails, sems, b, n_steps):
    n_new = [k.shape[1] for k in kvn_refs]
    halves = _move_regions([o.shape[1] - n for o, n in zip(old_refs, n_new)])

    def copies(batch):
        out, sem = [], 0
        for buf, half in zip(bufs, halves):
            row, cps = 0, []
            for g, start, size in half:
                stage = buf.at[pl.ds(row, size)]
                cps.append((pltpu.make_async_copy(old_refs[g].at[batch, pl.ds(n_new[g] + start, size)], stage,
                                                  sems.at[sem]),
                            pltpu.make_async_copy(stage, new_refs[g].at[batch, pl.ds(start, size)],
                                                  sems.at[sem + 1])))
                row += size
                sem += 2
            out.append(cps)
        tail = []
        for g in range(N_GROUPS):
            keep = old_refs[g].shape[1] - n_new[g]
            tail.append((pltpu.make_async_copy(kvn_refs[g].at[batch], tails.at[g], sems.at[sem]),
                         pltpu.make_async_copy(tails.at[g], new_refs[g].at[batch, pl.ds(keep, n_new[g])],
                                               sems.at[sem + 1])))
            sem += 2
        return out[0], out[1], tail
    first, second, tail = copies(b)
    _, prev_second, _ = copies(jnp.maximum(b - 1, 0))

    def top():
        for cp_in, _ in first + tail:
            cp_in.start()

    def mid():
        for cp_in, cp_out in first + tail:
            cp_in.wait()
            cp_out.start()

        @pl.when(b > 0)
        def _():
            for _, cp_out in prev_second:
                cp_out.wait()
        for cp_in, _ in second:
            cp_in.start()

    def end():
        for cp_in, cp_out in second:
            cp_in.wait()
            cp_out.start()
        for _, cp_out in first + tail:
            cp_out.wait()

        @pl.when(b == n_steps - 1)
        def _():
            for _, cp_out in second:
                cp_out.wait()
    return top, mid, end


def _move_scratch(old, new_rows):
    halves = _move_regions([o.shape[1] - n.shape[1] for o, n in zip(old, new_rows)])
    n_copies = sum(len(h) for h in halves) + N_GROUPS
    return ([pltpu.VMEM((sum(r[2] for r in h), HEAD_DIM), F32) for h in halves]
            + [pltpu.VMEM((N_GROUPS,) + new_rows[0].shape[1:], F32), pltpu.SemaphoreType.DMA((2 * n_copies,))])


def _out_ffn_kernel(*refs, seq_len, tiles_per_seq, tail_rows, move):
    move_top = move_mid = move_end = lambda: None
    if move:
        (x_ref, mix_ref, cr_ref, wo_ref, g_ref, wup_ref, cw_ref, cb_ref, wdn_ref, o0, o1, o2, k0, k1, k2,
         out_ref, tail_ref, n0, n1, n2, shift, act_scr, carry, buf_a, buf_b, tails, sems) = refs
        move_top, move_mid, move_end = _buffer_move(
            (o0, o1, o2), (k0, k1, k2), (n0, n1, n2), (buf_a, buf_b), tails, sems,
            pl.program_id(0), pl.num_programs(0))
    elif seq_len is None:
        (x_ref, mix_ref, cr_ref, wo_ref, g_ref, wup_ref, cw_ref, cb_ref, wdn_ref,
         out_ref, tail_ref, shift, act_scr, carry) = refs
    else:
        (x_ref, mix_ref, cr_ref, wo_ref, g_ref, wup_ref, cw_ref, cb_ref, wdn_ref, e1_ref, e2_ref,
         out_ref, tail_ref, shift, act_scr) = refs
    tm = x_ref.shape[0]
    d_ff = wdn_ref.shape[0]
    move_top()
    x1 = x_ref[...] + _dot(jnp.concatenate([mix_ref[...], cr_ref[...]], axis=-1), wo_ref[...])
    h = _rms(x1, g_ref[...]).astype(BF16)
    if seq_len is None:
        @pl.when(pl.program_id(0) % tiles_per_seq == 0)
        def _():
            carry[...] = jnp.zeros_like(carry)
    else:
        t = lax.broadcasted_iota(jnp.int32, (tm, 1), 0) % seq_len
        has1 = t >= 1
        has2 = t >= 2

    def chunk_cols(j):
        return (slice(j * FF_CHUNK, (j + 1) * FF_CHUNK),
                slice(d_ff + j * FF_CHUNK, d_ff + (j + 1) * FF_CHUNK))

    def up_proj(j):
        return tuple(_dot(h, wup_ref[:, cols]) for cols in chunk_cols(j))

    def conv(up, cols, slot):
        buf = shift.at[slot]
        if seq_len is None:
            buf[:CARRY_ROWS, :] = carry[:, cols]
            carry[:, cols] = up[tm - CARRY_ROWS:, :]
        else:
            buf[:CARRY_ROWS, :] = jnp.zeros((CARRY_ROWS, FF_CHUNK), F32)
        buf[CARRY_ROWS:, :] = up
        tail_ref[0, :, cols] = up[tm - tail_rows:, :]
        prev1 = buf[CARRY_ROWS - 1:CARRY_ROWS - 1 + tm, :]
        prev2 = buf[CARRY_ROWS - 2:CARRY_ROWS - 2 + tm, :]
        if seq_len is not None:
            prev1 = jnp.where(has1, prev1, e1_ref[:, cols])
            prev2 = jnp.where(has2, prev2, e2_ref[:, cols])
        return (cb_ref[:, cols] + cw_ref[0:1, cols] * prev2 + cw_ref[1:2, cols] * prev1
                + cw_ref[2:3, cols] * up)

    n_chunks = d_ff // FF_CHUNK
    ups = up_proj(0)
    for j in range(n_chunks):
        nxt = up_proj(j + 1) if j + 1 < n_chunks else None
        a, b = (conv(up, cols, 2 * (j % 2) + s) for s, (up, cols) in enumerate(zip(ups, chunk_cols(j))))
        act_scr[:, j * FF_CHUNK:(j + 1) * FF_CHUNK] = (a * jax.nn.sigmoid(a) * b).astype(BF16)
        ups = nxt
        if j == n_chunks // 2:
            move_mid()
    out_ref[...] = x1 + _dot(act_scr[...], wdn_ref[...])
    move_end()


def _layer_spec(stacked, layer):
    nd = stacked.ndim - 1
    return pl.BlockSpec((None,) + stacked.shape[1:], lambda *_: (layer,) + (0,) * nd,
                        pipeline_mode=pl.Buffered(1))


def _out_ffn(x2d, mix, cross, layer, w_out_bf, g_ffn, w_up_bf, conv_w, conv_b, w_down_bf, tm,
             tiles_per_seq=None, seq_len=None, e1=None, e2=None, tail_rows=CARRY_ROWS, moves=None):
    rows, d = x2d.shape
    depth, d_ff, _ = w_down_bf.shape
    n_tiles = rows // tm
    args = [x2d, mix, cross, w_out_bf, g_ffn.reshape(depth, 1, d), w_up_bf, conv_w,
            conv_b.reshape(depth, 1, 2 * d_ff), w_down_bf]
    in_specs = [
        pl.BlockSpec((tm, d), lambda i: (i, 0)),
        pl.BlockSpec((tm, ATT_WIDTH), lambda i: (i, 0)),
        pl.BlockSpec((tm, ATT_WIDTH), lambda i: (i, 0)),
        *[_layer_spec(a, layer) for a in args[3:]],
    ]
    scratch = [pltpu.VMEM((SHIFT_SLOTS, CARRY_ROWS + tm, FF_CHUNK), F32), pltpu.VMEM((tm, d_ff), BF16)]
    if seq_len is None:
        scratch += [pltpu.VMEM((CARRY_ROWS, 2 * d_ff), F32)]
    else:
        in_specs += [pl.BlockSpec((tm, 2 * d_ff), lambda i: (i, 0))] * 2
        args += [e1, e2]
    out_specs = [pl.BlockSpec((tm, d), lambda i: (i, 0)),
                 pl.BlockSpec((1, tail_rows, 2 * d_ff), lambda i: (i, 0, 0))]
    out_shape = [jax.ShapeDtypeStruct((rows, d), F32),
                 jax.ShapeDtypeStruct((n_tiles, tail_rows, 2 * d_ff), F32)]
    if moves is not None:
        old, new_rows = moves
        assert seq_len is None and all(o.shape[0] == n_tiles for o in old)
        any_spec = pl.BlockSpec(memory_space=pl.ANY)
        in_specs += [any_spec] * (2 * N_GROUPS)
        args += [*old, *new_rows]
        out_specs += [any_spec] * N_GROUPS
        out_shape += [jax.ShapeDtypeStruct(o.shape, o.dtype) for o in old]
        scratch += _move_scratch(old, new_rows)
    return pl.pallas_call(
        functools.partial(_out_ffn_kernel, seq_len=seq_len, tiles_per_seq=tiles_per_seq,
                          tail_rows=tail_rows, move=moves is not None),
        grid=(n_tiles,),
        in_specs=in_specs,
        out_specs=out_specs,
        out_shape=out_shape,
        scratch_shapes=scratch,
        compiler_params=_params("arbitrary"),
        name="out_ffn",
    )(*args)


def _swap_halves(x):
    return pltpu.roll(x, S5_STATE, 1)


def _s5_prep_kernel(lam_ref, logdt_ref, bt_ref, c_ref, d_ref, t_ref, cpt_ref, *var_refs, n_valids):
    gc = lam_ref.shape[0]
    n_groups = gc // S5_GROUP
    lane = lax.broadcasted_iota(jnp.int32, lam_ref.shape, 1)
    first = lane < S5_STATE
    sign = jnp.where(first, -1.0, 1.0)
    a = lam_ref[...]
    a_sw = _swap_halves(a)
    are = jnp.where(first, a, a_sw)
    aim = jnp.where(first, a_sw, a)
    dt = jnp.exp(logdt_ref[...])
    mag = jnp.exp(are * dt)
    lr = mag * jnp.cos(aim * dt)
    li = mag * jnp.sin(aim * dt)
    den = are * are + aim * aim
    xr = lr - 1.0
    f_re = (xr * are + li * aim) / den
    f_im = (li * are - xr * aim) / den
    lb = sign * li

    def cmul(x, m_re, m_sw):
        return x * m_re + _swap_halves(x) * m_sw

    c = c_ref[...]
    c_neg = c * -sign
    ri = lax.broadcasted_iota(jnp.int32, (gc, gc), 0)
    ci = lax.broadcasted_iota(jnp.int32, (gc, gc), 1)
    same_group = (ri // S5_GROUP) == (ci // S5_GROUP)
    e = cmul(bt_ref[...], f_re, sign * f_im)
    cl = c
    pw = jnp.where(first, 1.0, 0.0)
    zr = lax.broadcasted_iota(jnp.int32, (gc, S5_BLOCK), 0)
    zc = lax.broadcasted_iota(jnp.int32, (gc, S5_BLOCK), 1)
    same_out = (zr % S5_GROUP) == (zc % S5_GROUP)
    lags = jnp.zeros((gc, S5_BLOCK), F32)
    for k in range(S5_SUB):
        kmat = jnp.where(same_group, _dot_nt(e, c_neg, precision=PREP_PRECISION), 0.0)
        if k == 0:
            kmat = kmat + jnp.where(ri == ci, d_ref[...], 0.0)
        place = jnp.where(same_out & (zc // S5_GROUP == k), 1.0, 0.0)
        lags = lags + jnp.dot(kmat, place, precision=PREP_PRECISION, preferred_element_type=F32)
        both = jnp.concatenate([e, _swap_halves(e)], axis=-1).reshape(n_groups, S5_GROUP, 4 * S5_STATE)
        for n_valid, bp_ref in zip(n_valids, var_refs[0::2]):
            if n_valid - 1 - k >= 0:
                bp_ref[:, n_valid - 1 - k] = both.astype(bp_ref.dtype)
        cl = cmul(cl, lr, lb)
        cpt_ref[:, k] = (cl * -sign).reshape(n_groups, S5_GROUP, 2 * S5_STATE).astype(cpt_ref.dtype)
        e = cmul(e, lr, lb)
        pw = cmul(pw, lr, lb)
        for n_valid, lam_out_ref in zip(n_valids, var_refs[1::2]):
            if k + 1 == n_valid:
                lam_out_ref[...] = pw
    for n_valid, bp_ref in zip(n_valids, var_refs[0::2]):
        for tau in range(n_valid, S5_SUB):
            bp_ref[:, tau] = jnp.zeros((n_groups, S5_GROUP, 4 * S5_STATE), bp_ref.dtype)
    for taup in range(S5_SUB):
        moved = lags if taup == 0 else jnp.where(zc >= taup * S5_GROUP, pltpu.roll(lags, taup * S5_GROUP, 1), 0.0)
        t_ref[:, taup] = moved.reshape(n_groups, S5_GROUP, S5_BLOCK).astype(t_ref.dtype)


def _s5_prep(lam_re, lam_im, log_dt, b_re, b_im, c_re, c_im, d_skip, n_valids):
    n_groups = lam_re.shape[0]
    gc = n_groups * S5_GROUP
    rep = lambda t: jnp.repeat(t, S5_GROUP, axis=0)
    lam_p = rep(jnp.concatenate([lam_re, lam_im], axis=-1))
    logdt = rep(log_dt.reshape(n_groups, 1))
    bt_p = jnp.concatenate([b_re.transpose(0, 2, 1), b_im.transpose(0, 2, 1)], axis=-1).reshape(gc, 2 * S5_STATE)
    c_p = jnp.concatenate([c_re, c_im], axis=-1).reshape(gc, 2 * S5_STATE)
    t_shape = (n_groups, S5_SUB, S5_GROUP, S5_BLOCK)
    cpt_shape = (n_groups, S5_SUB, S5_GROUP, 2 * S5_STATE)
    bp_shape = (n_groups, S5_SUB, S5_GROUP, 4 * S5_STATE)
    lam_shape = (gc, 2 * S5_STATE)
    var = [(bp_shape, BF16), (lam_shape, F32)] * len(n_valids)
    t, cpt, *rest = pl.pallas_call(
        functools.partial(_s5_prep_kernel, n_valids=tuple(n_valids)),
        grid=(1,),
        in_specs=[_full(lam_p.shape), _full(logdt.shape), _full(bt_p.shape), _full(c_p.shape), _full((gc, 1))],
        out_specs=[_full(t_shape), _full(cpt_shape), *[_full(s) for s, _ in var]],
        out_shape=[jax.ShapeDtypeStruct(t_shape, BF16), jax.ShapeDtypeStruct(cpt_shape, BF16),
                   *[jax.ShapeDtypeStruct(s, dt) for s, dt in var]],
        compiler_params=_params("arbitrary"),
        name="s5_prep",
    )(lam_p, logdt, bt_p, c_p, d_skip.reshape(gc, 1))
    merge = lambda a: a.reshape(n_groups, S5_BLOCK, a.shape[-1])
    return [(merge(t), merge(bp), merge(cpt), lam_n[::S5_GROUP]) for bp, lam_n in zip(rest[0::2], rest[1::2])]


SCAN_GROUPS = 16
SUBLANES = 8
GLU_ROWS = 512


def _s5_core_kernel(v_ref, x0_ref, x0s_ref, t_ref, bp_ref, cpt_ref, lam_ref, wglu_ref, bglu_ref,
                    mix_ref, xfin_ref, s_scr, xprev_scr, m_scr, stage, *, n_valid, n_sub):
    rows = v_ref.shape[0]
    n_groups = t_ref.shape[0]
    lane = lax.broadcasted_iota(jnp.int32, (1, 2 * S5_STATE), 1)
    first = lane < S5_STATE
    blk = lambda g: slice(g * S5_BLOCK, (g + 1) * S5_BLOCK)
    tile = lambda i: slice(i * 2 * S5_STATE, (i + 1) * 2 * S5_STATE)
    for g in range(n_groups):
        s_scr[:, blk(g)] = _dot(v_ref[:, blk(g)], bp_ref[g])
        lam = lam_ref[g:g + 1, :]
        lam_sw = _swap_halves(lam)
        m_scr[0:1, tile(g)] = jnp.where(first, lam, lam_sw)
        m_scr[1:2, tile(g)] = jnp.where(first, -lam_sw, lam)
    if n_sub == 1:
        for g in range(n_groups):
            x, xs = x0_ref[:, tile(g)], x0s_ref[:, tile(g)]
            xprev_scr[:, tile(g)] = x
            xfin_ref[:, tile(g)] = (x * m_scr[0:1, tile(g)] + xs * m_scr[1:2, tile(g)]
                                    + s_scr[:, tile(2 * g)])
    else:
        assert rows == n_sub
        for g0 in range(0, n_groups, SCAN_GROUPS):
            gs = range(g0, min(n_groups, g0 + SCAN_GROUPS))

            def body(i, carry, gs=gs):
                rows8 = pl.ds(pl.multiple_of(i * SUBLANES, SUBLANES), SUBLANES)
                sub_i = lax.broadcasted_iota(jnp.int32, (SUBLANES, 2 * S5_STATE), 0)
                out = []
                for g, (x, xs) in zip(gs, carry):
                    s8, ssw8 = s_scr[rows8, tile(2 * g)], s_scr[rows8, tile(2 * g + 1)]
                    prev8 = jnp.zeros((SUBLANES, 2 * S5_STATE), F32)
                    for r in range(SUBLANES):
                        prev8 = jnp.where(sub_i == r, x, prev8)
                        m_re, m_sw = m_scr[0:1, tile(g)], m_scr[1:2, tile(g)]
                        x, xs = (x * m_re + xs * m_sw + s8[r:r + 1, :],
                                 xs * m_re - x * m_sw + ssw8[r:r + 1, :])
                    xprev_scr[rows8, tile(g)] = prev8
                    out.append((x, xs))
                return tuple(out)
            init = tuple((x0_ref[0:1, tile(g)], x0s_ref[0:1, tile(g)]) for g in gs)
            fin = lax.fori_loop(0, n_sub // SUBLANES, body, init)
            for g, (x, _) in zip(gs, fin):
                xfin_ref[:, tile(g)] = jnp.broadcast_to(x, (xfin_ref.shape[0], 2 * S5_STATE))
    for g in range(n_groups):
        s_scr[:, blk(g)] = (_dot(v_ref[:, blk(g)], t_ref[g])
                            + _dot_nt(xprev_scr[:, tile(g)].astype(BF16), cpt_ref[g]))
    for cb in range(ATT_WIDTH // HEAD_DIM):
        for half in range(-(-n_valid // SLOTS)):
            toks = _slot_transpose([s_scr[:, tile(2 * (cb * SLOTS + p) + half)] for p in range(SLOTS)])
            for s in range(min(SLOTS, n_valid - half * SLOTS)):
                stage[cb, pl.ds(half * SLOTS + s, rows, stride=n_valid), :] = toks[s]
    n_tok = rows * n_valid
    chunk = min(n_tok, GLU_ROWS)

    def glu(i, carry):
        r = pl.ds(pl.multiple_of(i * chunk, chunk), chunk)
        y = jax.nn.gelu(jnp.concatenate([stage[cb, r, :] for cb in range(ATT_WIDTH // HEAD_DIM)], axis=-1))
        z = _dot(y.astype(BF16), wglu_ref[...]) + bglu_ref[...]
        mix_ref[r, :] = (y * jax.nn.sigmoid(z)).astype(mix_ref.dtype)
        return carry
    lax.fori_loop(0, n_tok // chunk, glu, 0)


def _s5_mixer(v, rows_per_tile, n_valid, n_sub, x0, x0s, t, bp, cpt, lam_n, w_glu_bf, b_glu):
    m, width = v.shape
    n_tiles = m // rows_per_tile
    r0 = x0.shape[0] // n_tiles
    n_state = x0.shape[1]
    n_tok = rows_per_tile * n_valid
    return pl.pallas_call(
        functools.partial(_s5_core_kernel, n_valid=n_valid, n_sub=n_sub),
        grid=(n_tiles,),
        in_specs=[pl.BlockSpec((rows_per_tile, width), lambda i: (i, 0)),
                  pl.BlockSpec((r0, n_state), lambda i: (i, 0)),
                  pl.BlockSpec((r0, n_state), lambda i: (i, 0)),
                  _full(t.shape), _full(bp.shape), _full(cpt.shape), _full(lam_n.shape),
                  _full(w_glu_bf.shape), _full((1, ATT_WIDTH))],
        out_specs=[pl.BlockSpec((n_tok, ATT_WIDTH), lambda i: (i, 0)),
                   pl.BlockSpec((r0, n_state), lambda i: (i, 0))],
        out_shape=[jax.ShapeDtypeStruct((m * n_valid, ATT_WIDTH), BF16),
                   jax.ShapeDtypeStruct(x0.shape, F32)],
        scratch_shapes=[pltpu.VMEM((rows_per_tile, width), F32),
                        pltpu.VMEM((rows_per_tile, n_state), F32),
                        pltpu.VMEM((8, n_state), F32),
                        pltpu.VMEM((ATT_WIDTH // HEAD_DIM, n_tok, HEAD_DIM), F32)],
        compiler_params=_params("arbitrary"),
        name="s5_core",
    )(v, x0, x0s, t, bp, cpt, lam_n, w_glu_bf, b_glu.reshape(1, ATT_WIDTH))


def _pack_state(s):
    n = s.shape[0]
    packed = s.transpose(0, 2, 1, 3).reshape(n, -1)
    swapped = jnp.stack([s[:, 1], s[:, 0]], axis=1).transpose(0, 2, 1, 3).reshape(n, -1)
    return packed, swapped


def _unpack_state(x):
    n = x.shape[0]
    return x.reshape(n, -1, 2, S5_STATE).transpose(0, 2, 1, 3)


PROMPT_TILE = 512
Q_PAD = 16


def _pad_rows(t, n):
    return jnp.pad(t, ((0, 0), (0, n - t.shape[1]), (0, 0)))


def kernel(x_prompt, x_sample, cache_win0_kv, cache_win1_kv, cache_win2_kv, cache_mem_kv, state_s5,
           state_ffn_conv, mem_prompt, g_mix, g_ffn, w_in_a, g_q_dil, g_k_dil, w_in_b, s5_lam_re,
           s5_lam_im, s5_log_dt, s5_b_re, s5_b_im, s5_c_re, s5_c_im, s5_d, w_glu, b_glu, g_mem,
           w_mem_kv, g_q_cross, g_k_cross, w_out, w_up, conv_w, conv_b, w_down):
    nb, seq, d = x_prompt.shape
    db, ts, _ = x_sample.shape
    depth = g_mix.shape[0]
    n_mem = mem_prompt.shape[1]
    d_ff2 = w_up.shape[2]
    assert ts >= CONV_W - 1 and seq % PROMPT_TILE == 0 and ts <= Q_PAD
    caches = (cache_win0_kv, cache_win1_kv, cache_win2_kv)
    w_mem_bf, w_out_bf, w_up_bf, w_down_bf = (w.astype(BF16) for w in (w_mem_kv, w_out, w_up, w_down))
    mem_cache = cache_mem_kv.reshape(depth * db, n_mem * KV_ROWS, HEAD_DIM)

    tab_p = _rope_tables(jnp.arange(seq, dtype=jnp.int32))
    tab_s = tuple(jnp.tile(t, (db, 1)) for t in _rope_tables(PAST_LEN + jnp.arange(ts, dtype=jnp.int32)))
    win_keep = tuple(min(w, seq) for w, _ in DIL_GROUPS)
    rows_s = db * ts
    kv_tail = (2, N_HEADS, HEAD_DIM)

    mem_p = _mem_kv(mem_prompt.reshape(nb * n_mem, d), g_mem, w_mem_bf, g_k_cross)

    xp = x_prompt.reshape(nb * seq, d)
    xs = x_sample.reshape(rows_s, d)
    p_win, s_win = [[] for _ in DIL_GROUPS], [[] for _ in DIL_GROUPS]
    p_s5, s_s5, p_conv, s_conv = [], [], [], []
    tiles_per_seq = seq // PROMPT_TILE
    for i in range(depth):
        mem_i = mem_p[i].reshape(nb, n_mem * KV_ROWS, HEAD_DIM)
        if i % 2 == 0:
            ia = i // 2
            w_in_a_bf = w_in_a[ia].astype(BF16)
            *qkv_groups, w0, w1, w2, cross_p = _in_proj_a(
                xp, seq, PROMPT_TILE, g_mix[i], w_in_a_bf, tab_p, g_q_dil[ia], g_k_dil[ia],
                g_q_cross[i], mem_i, win_keep, dilated=True)
            for g, w in enumerate((w0, w1, w2)):
                p_win[g].append(w.reshape((nb, win_keep[g]) + kv_tail))
            mix_p = _attn_prompt(qkv_groups, nb, seq).reshape(nb * seq, ATT_WIDTH)
            qkv_s, *kv_new, qc_s = _in_proj_a(
                xs, rows_s, rows_s, g_mix[i], w_in_a_bf, tab_s, g_q_dil[ia], g_k_dil[ia],
                None, None, (rows_s,) * N_GROUPS, dilated=False)
            old_bufs = [c[ia].reshape(db, -1, HEAD_DIM) for c in caches]
            moves = (old_bufs, [k.reshape(db, ts * KV_ROWS, HEAD_DIM) for k in kv_new])
        else:
            moves = None
            ib = i // 2
            prm = (s5_lam_re[ib], s5_lam_im[ib], s5_log_dt[ib], s5_b_re[ib], s5_b_im[ib], s5_c_re[ib],
                   s5_c_im[ib], s5_d[ib])
            n_state = s5_lam_re.shape[1] * 2 * S5_STATE
            w_in_b_bf, w_glu_bf = w_in_b[ib].astype(BF16), w_glu[ib].astype(BF16)
            u_p, cross_p = _in_proj_b(xp, seq, PROMPT_TILE, S5_SUB, g_mix[i], w_in_b_bf, g_q_cross[i], mem_i)
            zero = jnp.zeros((nb * CARRY_ROWS, n_state), F32)
            prep_p, prep_s = _s5_prep(*prm, (S5_SUB, ts))
            mix_p, fin_p = _s5_mixer(u_p, seq // S5_SUB, S5_SUB, seq // S5_SUB, zero, zero,
                                     *prep_p, w_glu_bf, b_glu[ib])
            p_s5.append(_unpack_state(fin_p.reshape(nb, CARRY_ROWS, n_state)[:, 0]))
            u_s, qc_s = _in_proj_b(xs, rows_s, rows_s, ts, g_mix[i], w_in_b_bf, None, None)
            mix_s, fin_s = _s5_mixer(u_s, db, ts, 1, *_pack_state(state_s5[ib]), *prep_s, w_glu_bf, b_glu[ib])
            s_s5.append(_unpack_state(fin_s))
        cross_s = _cross_sample(_pad_rows(qc_s.reshape(db, ts, ATT_WIDTH), Q_PAD), g_q_cross[i], mem_cache,
                                first=i * db)
        cross_s = cross_s[:, :ts].reshape(rows_s, ATT_WIDTH)

        ffn_w = (i, w_out_bf, g_ffn, w_up_bf, conv_w, conv_b, w_down_bf)
        xp, tails, *new_bufs = _out_ffn(xp, mix_p, cross_p, *ffn_w, PROMPT_TILE, tiles_per_seq=tiles_per_seq,
                                        moves=moves)
        p_conv.append(tails.reshape(nb, tiles_per_seq, CARRY_ROWS, d_ff2)[:, -1, CARRY_ROWS - (CONV_W - 1):])
        if moves is not None:
            outs, lses = [], []
            for g, (_, r) in enumerate(DIL_GROUPS):
                q_g = qkv_s[g * N_HEADS:(g + 1) * N_HEADS]
                q_g = _pad_rows(q_g.transpose(1, 0, 2).reshape(db, ts, ATT_WIDTH), Q_PAD)
                o_g, lse_g = _attn_sample(q_g, new_bufs[g], moves[0][g], r, ts)
                outs.append(o_g.reshape(rows_s, ATT_WIDTH))
                lses.append(lse_g.reshape(rows_s, ATT_WIDTH))
                s_win[g].append(new_bufs[g].reshape(caches[g][i // 2].shape))
            mix_s = _combine_groups(outs, lses)
        buf = state_ffn_conv[i]
        zero = jnp.zeros((db, ts - 2, d_ff2), F32)
        e1 = jnp.concatenate([buf[:, 1:2], zero, zero[:, :1]], axis=1).reshape(rows_s, d_ff2)
        e2 = jnp.concatenate([buf, zero], axis=1).reshape(rows_s, d_ff2)
        xs, tails = _out_ffn(xs, mix_s, cross_s, *ffn_w, rows_s, seq_len=ts, e1=e1, e2=e2, tail_rows=rows_s)
        s_conv.append(tails.reshape(db, ts, d_ff2)[:, ts - (CONV_W - 1):])

    return (xp.reshape(nb, seq, d), xs.reshape(db, ts, d),
            jnp.stack(p_win[0]), jnp.stack(p_win[1]), jnp.stack(p_win[2]),
            mem_p.reshape((depth, nb, n_mem) + kv_tail),
            jnp.stack(p_s5), jnp.stack(p_conv),
            jnp.stack(s_win[0]), jnp.stack(s_win[1]), jnp.stack(s_win[2]),
            jnp.stack(s_s5), jnp.stack(s_conv))
```

```python
import functools
import math

import jax
import jax.numpy as jnp
from jax import lax
from jax.experimental import pallas as pl
from jax.experimental.pallas import tpu as pltpu

HEAD_DIM = 128
N_HEADS = 4
DIL_GROUPS = ((128, 1), (512, 4), (2048, 16))
N_GROUPS = len(DIL_GROUPS)
DIL_SPAN = 128
BLOCK = 128
ATT_WIDTH = N_HEADS * HEAD_DIM
KV_ROWS = 2 * N_HEADS
ROT_DIM = HEAD_DIM // 4
ROT_HALF = ROT_DIM // 2
ROPE_THETA = 500000.0
S5_GROUP = 16
S5_STATE = 64
S5_SUB = 16
S5_BLOCK = S5_SUB * S5_GROUP
CONV_W = 3
EPS = 1e-6
NEG = -1e30
SCALE = HEAD_DIM ** -0.5
PAST_LEN = 16384

VMEM_LIMIT_V7X = 56 * 1024 * 1024
BF16 = jnp.bfloat16
F32 = jnp.float32
PREP_PRECISION = lax.Precision.HIGHEST


def _params(*sem):
    return pltpu.CompilerParams(dimension_semantics=sem, vmem_limit_bytes=VMEM_LIMIT_V7X)


def _rms(x, g):
    return x * lax.rsqrt(jnp.mean(x * x, axis=-1, keepdims=True) + EPS) * g


def _rms_head(x, g):
    ones = jnp.ones((HEAD_DIM, HEAD_DIM), BF16)
    ssq = _dot((x * x).astype(BF16), ones)
    return x * lax.rsqrt(ssq * (1.0 / HEAD_DIM) + EPS) * g


def _dot(a, b):
    return jnp.dot(a, b, preferred_element_type=F32)


def _dot_nt(a, b, precision=None):
    return lax.dot_general(a, b, (((1,), (1,)), ((), ())), precision=precision,
                           preferred_element_type=F32)


def _full(shape):
    nd = len(shape)
    return pl.BlockSpec(shape, lambda *_: (0,) * nd)


def _kv_rows(kv, head, n):
    return pl.ds(kv * N_HEADS + head, n, stride=KV_ROWS)


def _mem_kv_kernel(mem_ref, g_ref, w_ref, gk_ref, out_ref):
    tm = mem_ref.shape[0]
    h = _rms(mem_ref[...], g_ref[0]).astype(BF16)
    kv = _dot(h, w_ref[0])
    for hd in range(N_HEADS):
        sl = slice(hd * HEAD_DIM, (hd + 1) * HEAD_DIM)
        out_ref[0, _kv_rows(0, hd, tm), :] = _rms(kv[:, sl], gk_ref[0])
        out_ref[0, _kv_rows(1, hd, tm), :] = kv[:, ATT_WIDTH + hd * HEAD_DIM:ATT_WIDTH + (hd + 1) * HEAD_DIM]


def _mem_kv(mem2d, g_mem, w_kv_bf, g_k):
    depth, d, _ = w_kv_bf.shape
    rows = mem2d.shape[0]
    tm = min(rows, 512)
    return pl.pallas_call(
        _mem_kv_kernel,
        grid=(depth, rows // tm),
        in_specs=[
            pl.BlockSpec((tm, d), lambda l, i: (i, 0)),
            pl.BlockSpec((1, 1, d), lambda l, i: (l, 0, 0)),
            pl.BlockSpec((1, d, 2 * ATT_WIDTH), lambda l, i: (l, 0, 0)),
            pl.BlockSpec((1, 1, HEAD_DIM), lambda l, i: (l, 0, 0)),
        ],
        out_specs=pl.BlockSpec((1, tm * KV_ROWS, HEAD_DIM), lambda l, i: (l, i, 0)),
        out_shape=jax.ShapeDtypeStruct((depth, rows * KV_ROWS, HEAD_DIM), F32),
        compiler_params=_params("arbitrary", "arbitrary"),
        name="mem_kv",
    )(mem2d, g_mem.reshape(depth, 1, d), w_kv_bf, g_k.reshape(depth, 1, HEAD_DIM))


def _cross_heads(qc, mem_ref, gq):
    n_mem = mem_ref.shape[1] // KV_ROWS
    outs = []
    for hd in range(N_HEADS):
        q = _rms(qc[:, hd * HEAD_DIM:(hd + 1) * HEAD_DIM], gq).astype(BF16)
        k = mem_ref[0, _kv_rows(0, hd, n_mem), :].astype(BF16)
        v = mem_ref[0, _kv_rows(1, hd, n_mem), :].astype(BF16)
        s = _dot_nt(q, k) * SCALE
        m = jnp.max(s, axis=-1, keepdims=True)
        p = jnp.exp(s - m)
        l = jnp.sum(p, axis=-1, keepdims=True)
        outs.append(_dot(p.astype(BF16), v) / l)
    return outs


def _rope(x, cos_t, sin_lo, sin_hi):
    return (x * cos_t + pltpu.roll(x, HEAD_DIM - ROT_HALF, 1) * sin_lo
            + pltpu.roll(x, ROT_HALF, 1) * sin_hi)


def _in_proj_a_kernel(*refs, fuse_cross, dilated, win_rows, win_first, tiles_per_seq):
    n_in = 10 if fuse_cross else 8
    x_ref, g_ref, w_ref, cos_ref, slo_ref, shi_ref, gq_ref, gk_ref = refs[:8]
    n_qkv = N_GROUPS if dilated else 1
    qkv_refs = refs[n_in:n_in + n_qkv]
    win_refs = refs[n_in + n_qkv:n_in + n_qkv + N_GROUPS]
    cr_ref = refs[n_in + n_qkv + N_GROUPS]
    kv_stash = refs[n_in + n_qkv + N_GROUPS + 1]
    stage = refs[n_in + n_qkv + N_GROUPS + 2] if dilated else None
    tm = x_ref.shape[0]
    h = _rms(x_ref[...], g_ref[...]).astype(BF16)
    cos_t, sin_lo, sin_hi = cos_ref[...], slo_ref[...], shi_ref[...]
    def proj(c):
        return _dot(h, w_ref[:, c * ATT_WIDTH:(c + 1) * ATT_WIDTH])

    tile = pl.program_id(0) % tiles_per_seq
    in_window = [tile >= first for first in win_first]
    for g in range(N_GROUPS):
        @pl.when(jnp.logical_not(in_window[g]))
        def _(g=g):
            win_refs[g][...] = jnp.zeros_like(win_refs[g])

    y = proj(0)
    for c in range(3 * N_GROUPS):
        y_next = proj(c + 1)
        role, g = divmod(c, N_GROUPS)
        r = DIL_GROUPS[g][1]
        wr = win_rows[g]
        for hd in range(N_HEADS):
            yh = y[:, hd * HEAD_DIM:(hd + 1) * HEAD_DIM]
            if role == 0:
                yh = _rope(_rms_head(yh, gq_ref[g]), cos_t, sin_lo, sin_hi)
            elif role == 1:
                yh = _rope(_rms_head(yh, gk_ref[g]), cos_t, sin_lo, sin_hi)
            if role > 0:
                kv_stash[((role - 1) * N_GROUPS + g) * N_HEADS + hd] = yh
            if not dilated:
                qkv_refs[0][role * N_GROUPS * N_HEADS + g * N_HEADS + hd] = yh.astype(BF16)
            elif r == 1:
                qkv_refs[g][role * N_HEADS + hd, 0] = yh.astype(BF16)
            else:
                stage[hd] = yh
                for rho in range(r):
                    qkv_refs[g][role * N_HEADS + hd, 0, :, rho * HEAD_DIM:(rho + 1) * HEAD_DIM] = (
                        stage[hd, pl.ds(rho, tm // r, stride=r), :].astype(BF16))
        y = y_next
    qc = y
    if fuse_cross:
        outs = _cross_heads(qc, refs[9], refs[8][...])
        for hd in range(N_HEADS):
            cr_ref[:, hd * HEAD_DIM:(hd + 1) * HEAD_DIM] = outs[hd].astype(cr_ref.dtype)
    else:
        cr_ref[...] = qc
    for g in range(N_GROUPS):
        @pl.when(in_window[g])
        def _(g=g):
            wr = win_rows[g]
            for kv in range(2):
                for hd in range(N_HEADS):
                    win_refs[g][_kv_rows(kv, hd, wr), :] = kv_stash[(kv * N_GROUPS + g) * N_HEADS + hd,
                                                                    tm - wr:, :]


SLOTS = HEAD_DIM // S5_GROUP


def _slot_transpose(vs):
    slot = lax.broadcasted_iota(jnp.int32, vs[0].shape, 1) // S5_GROUP
    d = SLOTS // 2
    while d:
        low = (slot & d) == 0
        new = list(vs)
        for i in range(SLOTS):
            if not i & d:
                new[i] = jnp.where(low, vs[i], pltpu.roll(vs[i | d], d * S5_GROUP, 1))
                new[i | d] = jnp.where(low, pltpu.roll(vs[i], HEAD_DIM - d * S5_GROUP, 1), vs[i | d])
        vs = new
        d //= 2
    return vs


def _in_proj_b_kernel(*refs, fuse_cross, n_valid):
    if fuse_cross:
        x_ref, g_ref, w_ref, gqc_ref, mem_ref, u_ref, cr_ref, stage = refs
    else:
        x_ref, g_ref, w_ref, u_ref, cr_ref, stage = refs
    tm = x_ref.shape[0]
    n_rows = tm // n_valid
    h = _rms(x_ref[...], g_ref[...]).astype(BF16)
    u = _dot(h, w_ref[:, :ATT_WIDTH])
    zero = jnp.zeros((n_rows, HEAD_DIM), F32)
    for cb in range(ATT_WIDTH // HEAD_DIM):
        stage[cb] = u[:, cb * HEAD_DIM:(cb + 1) * HEAD_DIM]
        for half in range(S5_SUB // SLOTS):
            taus = range(half * SLOTS, (half + 1) * SLOTS)
            groups = _slot_transpose([stage[cb, pl.ds(tau, n_rows, stride=n_valid), :] if tau < n_valid else zero
                                      for tau in taus])
            for p in range(SLOTS):
                lanes = (2 * (cb * SLOTS + p) + half) * HEAD_DIM
                u_ref[:, lanes:lanes + HEAD_DIM] = groups[p].astype(BF16)
    qc = _dot(h, w_ref[:, ATT_WIDTH:])
    if fuse_cross:
        outs = _cross_heads(qc, mem_ref, gqc_ref[...])
        for hd in range(N_HEADS):
            cr_ref[:, hd * HEAD_DIM:(hd + 1) * HEAD_DIM] = outs[hd].astype(cr_ref.dtype)
    else:
        cr_ref[...] = qc


def _rope_tables(pos):
    inv = jnp.exp(-math.log(ROPE_THETA) * jnp.arange(ROT_HALF, dtype=F32) / ROT_HALF)
    ang = pos.astype(F32)[:, None] * inv[None, :]
    cos, sin = jnp.cos(ang), jnp.sin(ang)
    rows = pos.shape[0]
    ones = jnp.ones((rows, HEAD_DIM - ROT_DIM), F32)
    zeros = jnp.zeros((rows, HEAD_DIM - ROT_DIM), F32)
    z16 = jnp.zeros((rows, ROT_HALF), F32)
    cos_t = jnp.concatenate([cos, cos, ones], axis=1)
    sin_lo = jnp.concatenate([-sin, z16, zeros], axis=1)
    sin_hi = jnp.concatenate([z16, sin, zeros], axis=1)
    return cos_t, sin_lo, sin_hi


def _in_proj_a(x2d, seq, tm, g_mix, w_bf, tables, g_q, g_k, g_qc, mem_kv, win_keep, dilated):
    rows, d = x2d.shape
    n_in = w_bf.shape[1]
    tiles_per_seq = seq // tm
    n_seq = rows // seq
    fuse_cross = mem_kv is not None
    tab_tiles = tables[0].shape[0] // tm
    win_rows = tuple(min(k, tm) for k in win_keep)
    win_first = tuple(tiles_per_seq - k // wr for k, wr in zip(win_keep, win_rows))

    def win_spec(keep):
        wr = min(keep, tm)
        nblk = keep // wr
        first = tiles_per_seq - nblk

        def imap(i):
            b = i // tiles_per_seq
            t = i % tiles_per_seq
            return (b * nblk + jnp.maximum(t - first, 0), 0)
        return pl.BlockSpec((wr * KV_ROWS, HEAD_DIM), imap)

    in_specs = [
        pl.BlockSpec((tm, d), lambda i: (i, 0)),
        _full((1, d)),
        _full((d, n_in)),
        pl.BlockSpec((tm, HEAD_DIM), lambda i: (i % tab_tiles, 0)),
        pl.BlockSpec((tm, HEAD_DIM), lambda i: (i % tab_tiles, 0)),
        pl.BlockSpec((tm, HEAD_DIM), lambda i: (i % tab_tiles, 0)),
        _full((N_GROUPS, HEAD_DIM)),
        _full((N_GROUPS, HEAD_DIM)),
    ]
    args = [x2d, g_mix.reshape(1, d), w_bf, *tables, g_q, g_k]
    if fuse_cross:
        in_specs += [_full((1, HEAD_DIM)),
                     pl.BlockSpec((1,) + mem_kv.shape[1:], lambda i: (i // tiles_per_seq, 0, 0))]
        args += [g_qc.reshape(1, HEAD_DIM), mem_kv]
    if dilated:
        qkv_specs = [pl.BlockSpec((3 * N_HEADS, 1, tm // r, r * HEAD_DIM),
                                  lambda i: (0, i // tiles_per_seq, i % tiles_per_seq, 0))
                     for _, r in DIL_GROUPS]
        qkv_shapes = [jax.ShapeDtypeStruct((3 * N_HEADS, n_seq, seq // r, r * HEAD_DIM), BF16)
                      for _, r in DIL_GROUPS]
        scratch = [pltpu.VMEM((N_HEADS, tm, HEAD_DIM), F32)]
    else:
        qkv_specs = [pl.BlockSpec((3 * N_GROUPS * N_HEADS, tm, HEAD_DIM), lambda i: (0, i, 0))]
        qkv_shapes = [jax.ShapeDtypeStruct((3 * N_GROUPS * N_HEADS, rows, HEAD_DIM), BF16)]
        scratch = []
    scratch = [pltpu.VMEM((2 * N_GROUPS * N_HEADS, tm, HEAD_DIM), F32)] + scratch
    out_specs = [*qkv_specs, *[win_spec(k) for k in win_keep],
                 pl.BlockSpec((tm, ATT_WIDTH), lambda i: (i, 0))]
    out_shape = [*qkv_shapes,
                 *[jax.ShapeDtypeStruct((n_seq * k * KV_ROWS, HEAD_DIM), F32) for k in win_keep],
                 jax.ShapeDtypeStruct((rows, ATT_WIDTH), BF16 if fuse_cross else F32)]
    return pl.pallas_call(
        functools.partial(_in_proj_a_kernel, fuse_cross=fuse_cross, dilated=dilated, win_rows=win_rows,
                          win_first=win_first, tiles_per_seq=tiles_per_seq),
        grid=(rows // tm,),
        in_specs=in_specs, out_specs=out_specs, out_shape=out_shape, scratch_shapes=scratch,
        compiler_params=_params("arbitrary"),
        name="in_proj_a",
    )(*args)


def _in_proj_b(x2d, seq, tm, n_valid, g_mix, w_bf, g_qc, mem_kv):
    rows, d = x2d.shape
    width = (ATT_WIDTH // S5_GROUP) * S5_BLOCK
    n_in = w_bf.shape[1]
    tiles_per_seq = seq // tm
    fuse_cross = mem_kv is not None
    in_specs = [pl.BlockSpec((tm, d), lambda i: (i, 0)), _full((1, d)), _full((d, n_in))]
    args = [x2d, g_mix.reshape(1, d), w_bf]
    if fuse_cross:
        in_specs += [_full((1, HEAD_DIM)),
                     pl.BlockSpec((1,) + mem_kv.shape[1:], lambda i: (i // tiles_per_seq, 0, 0))]
        args += [g_qc.reshape(1, HEAD_DIM), mem_kv]
    return pl.pallas_call(
        functools.partial(_in_proj_b_kernel, fuse_cross=fuse_cross, n_valid=n_valid),
        grid=(rows // tm,),
        in_specs=in_specs,
        out_specs=[pl.BlockSpec((tm // n_valid, width), lambda i: (i, 0)),
                   pl.BlockSpec((tm, ATT_WIDTH), lambda i: (i, 0))],
        out_shape=[jax.ShapeDtypeStruct((rows // n_valid, width), BF16),
                   jax.ShapeDtypeStruct((rows, ATT_WIDTH), BF16 if fuse_cross else F32)],
        scratch_shapes=[pltpu.VMEM((ATT_WIDTH // HEAD_DIM, tm, HEAD_DIM), F32)],
        compiler_params=_params("arbitrary"),
        name="in_proj_b",
    )(*args)


CROSS_BATCH = 4


def _cross_sample_kernel(qc_ref, gq_ref, mem_ref, out_ref):
    for b in range(qc_ref.shape[0]):
        outs = _cross_heads(qc_ref[b], mem_ref.at[pl.ds(b, 1)], gq_ref[...])
        for hd in range(N_HEADS):
            out_ref[b, :, hd * HEAD_DIM:(hd + 1) * HEAD_DIM] = outs[hd].astype(out_ref.dtype)


def _cross_sample(qc, g_qc, mem_kv, first):
    nb, tq, _ = qc.shape
    cb = math.gcd(nb, CROSS_BATCH)
    assert first % cb == 0
    return pl.pallas_call(
        _cross_sample_kernel,
        grid=(nb // cb,),
        in_specs=[pl.BlockSpec((cb, tq, ATT_WIDTH), lambda b: (b, 0, 0)),
                  _full((1, HEAD_DIM)),
                  pl.BlockSpec((cb,) + mem_kv.shape[1:], lambda b: (first // cb + b, 0, 0))],
        out_specs=pl.BlockSpec((cb, tq, ATT_WIDTH), lambda b: (b, 0, 0)),
        out_shape=jax.ShapeDtypeStruct((nb, tq, ATT_WIDTH), BF16),
        compiler_params=_params("arbitrary"),
        name="cross_sample",
    )(qc, g_qc.reshape(1, HEAD_DIM), mem_kv)


INFLIGHT = 32


def _band_block(q, k, v):
    n = k.shape[0]
    dist = (n - BLOCK + lax.broadcasted_iota(jnp.int32, (BLOCK, n), 0)
            - lax.broadcasted_iota(jnp.int32, (BLOCK, n), 1))
    s = jnp.where((dist >= 0) & (dist <= DIL_SPAN), _dot_nt(q, k) * SCALE, NEG)
    m = jnp.max(s, axis=-1, keepdims=True)
    p = jnp.exp(s - m)
    l = jnp.sum(p, axis=-1, keepdims=True)
    return _dot(p.astype(BF16), v) / l, m + jnp.log(l)


def _attn_prompt_kernel(q0, q1, q2, k0, k1, k2, v0, v1, v2, out_ref, o_scr, l_scr, *, seq):
    qkv_refs = ((q0, k0, v0), (q1, k1, v1), (q2, k2, v2))
    for g, (_, r) in enumerate(DIL_GROUPS):
        q_ref, k_ref, v_ref = qkv_refs[g]
        nblk = seq // r // BLOCK

        def store(rho, blk, o, lse, g=g, r=r):
            start = blk * (BLOCK * r) + rho
            if r == 1:
                idx = pl.ds(pl.multiple_of(start, BLOCK), BLOCK)
            else:
                idx = pl.ds(start, BLOCK, stride=r)
            o_scr[g, idx, :] = o
            l_scr[g, idx, :] = jnp.broadcast_to(lse, (BLOCK, HEAD_DIM))

        def first(rho, q_ref=q_ref, k_ref=k_ref, v_ref=v_ref, store=store):
            lanes = slice(rho * HEAD_DIM, (rho + 1) * HEAD_DIM)
            o, lse = _band_block(q_ref[0, 0, :BLOCK, lanes], k_ref[0, 0, :BLOCK, lanes],
                                 v_ref[0, 0, :BLOCK, lanes])
            store(rho, 0, o, lse)

        def later(rho, blk, q_ref=q_ref, k_ref=k_ref, v_ref=v_ref, store=store):
            lanes = slice(rho * HEAD_DIM, (rho + 1) * HEAD_DIM)
            cur = pl.ds(pl.multiple_of(blk * BLOCK, BLOCK), BLOCK)
            both = pl.ds(pl.multiple_of((blk - 1) * BLOCK, BLOCK), 2 * BLOCK)
            o, lse = _band_block(q_ref[0, 0, cur, lanes], k_ref[0, 0, both, lanes], v_ref[0, 0, both, lanes])
            store(rho, blk, o, lse)

        for rho0 in range(0, r, INFLIGHT):
            rhos = range(rho0, min(r, rho0 + INFLIGHT))
            for rho in rhos:
                first(rho)

            def body(blk, carry, rhos=rhos, later=later):
                for rho in rhos:
                    later(rho, blk)
                return carry
            if nblk > 1:
                lax.fori_loop(1, nblk, body, 0, unroll=max(1, INFLIGHT // len(rhos)))

    def combine(c, carry):
        rows = pl.ds(pl.multiple_of(c * BLOCK, BLOCK), BLOCK)
        ls = [l_scr[g, rows, :] for g in range(N_GROUPS)]
        m = jnp.maximum(jnp.maximum(ls[0], ls[1]), ls[2])
        es = [jnp.exp(l - m) for l in ls]
        num = es[0] * o_scr[0, rows, :] + es[1] * o_scr[1, rows, :] + es[2] * o_scr[2, rows, :]
        out_ref[0, rows, :] = (num / (es[0] + es[1] + es[2])).astype(out_ref.dtype)
        return carry
    lax.fori_loop(0, seq // BLOCK, combine, 0)


def _attn_prompt(qkv_groups, n_seq, seq):
    in_specs, args = [], []
    for role in range(3):
        for g, (_, r) in enumerate(DIL_GROUPS):
            in_specs.append(pl.BlockSpec((1, 1, seq // r, r * HEAD_DIM),
                                         lambda b, h, role=role: (role * N_HEADS + h, b, 0, 0)))
            args.append(qkv_groups[g])
    return pl.pallas_call(
        functools.partial(_attn_prompt_kernel, seq=seq),
        grid=(n_seq, N_HEADS),
        in_specs=in_specs,
        out_specs=pl.BlockSpec((1, seq, HEAD_DIM), lambda b, h: (b, 0, h)),
        out_shape=jax.ShapeDtypeStruct((n_seq, seq, ATT_WIDTH), BF16),
        scratch_shapes=[pltpu.VMEM((N_GROUPS, seq, HEAD_DIM), F32),
                        pltpu.VMEM((N_GROUPS, seq, HEAD_DIM), F32)],
        compiler_params=_params("arbitrary", "arbitrary"),
        name="attn_prompt",
    )(*args)


def _attn_sample_kernel(q_ref, new_ref, old_ref, o_ref, lse_ref, *scratch, r, t_new, lb, compact):
    qf = q_ref[0].astype(F32)
    if compact:
        (flat,) = scratch
        n_key = lb // r
        for u in range(t_new):
            flat[u] = new_ref[:, u * KV_ROWS:(u + 1) * KV_ROWS, :].reshape(n_key * KV_ROWS, HEAD_DIM)
    else:
        n_key = lb
    i_key = lax.broadcasted_iota(jnp.int32, (n_key, 1), 0)
    j_old = lax.broadcasted_iota(jnp.int32, (t_new, 1), 0)
    for t in range(t_new):
        if compact:
            idx = i_key * r + (r - t_new + t)
            rows = lambda kvh, t=t: flat[t, pl.ds(kvh, n_key, stride=KV_ROWS), :]
        else:
            idx = i_key
            rows = lambda kvh: new_ref[pl.ds(kvh, n_key, stride=KV_ROWS), :]
        dist = lb - t_new + t - idx
        ok = (dist >= 0) & (dist % r == 0) & (dist <= r * DIL_SPAN)
        dist_old = lb + t - j_old
        ok_old = (dist_old % r == 0) & (dist_old <= r * DIL_SPAN)
        for hd in range(N_HEADS):
            sl = slice(hd * HEAD_DIM, (hd + 1) * HEAD_DIM)
            q = qf[t:t + 1, sl]
            k, v = rows(hd), rows(N_HEADS + hd)
            k_old = old_ref[pl.ds(hd, t_new, stride=KV_ROWS), :]
            v_old = old_ref[pl.ds(N_HEADS + hd, t_new, stride=KV_ROWS), :]
            s = jnp.where(ok, jnp.sum(k * q, axis=-1, keepdims=True) * SCALE, NEG)
            s_old = jnp.where(ok_old, jnp.sum(k_old * q, axis=-1, keepdims=True) * SCALE, NEG)
            m = jnp.maximum(jnp.max(s, axis=0, keepdims=True), jnp.max(s_old, axis=0, keepdims=True))
            p, p_old = jnp.exp(s - m), jnp.exp(s_old - m)
            l = jnp.sum(p, axis=0, keepdims=True) + jnp.sum(p_old, axis=0, keepdims=True)
            o = jnp.sum(p * v, axis=0, keepdims=True) + jnp.sum(p_old * v_old, axis=0, keepdims=True)
            o_ref[0, t:t + 1, sl] = o / l
            lse_ref[0, t:t + 1, sl] = jnp.broadcast_to(m + jnp.log(l), (1, HEAD_DIM))


def _attn_sample(q, new_buf, old_buf, r, t_new):
    nb, tq, _ = q.shape
    lb = new_buf.shape[1] // KV_ROWS
    compact = r % t_new == 0
    if compact:
        new_view = new_buf.reshape(nb, lb // r, r // t_new, t_new * KV_ROWS, HEAD_DIM)
        new_spec = pl.BlockSpec((None, lb // r, None, t_new * KV_ROWS, HEAD_DIM),
                                lambda b: (b, 0, r // t_new - 1, 0, 0))
        scratch = [pltpu.VMEM((t_new, lb // r * KV_ROWS, HEAD_DIM), F32)]
    else:
        new_view = new_buf
        new_spec = pl.BlockSpec((None, lb * KV_ROWS, HEAD_DIM), lambda b: (b, 0, 0))
        scratch = []
    return pl.pallas_call(
        functools.partial(_attn_sample_kernel, r=r, t_new=t_new, lb=lb, compact=compact),
        grid=(nb,),
        in_specs=[pl.BlockSpec((1, tq, ATT_WIDTH), lambda b: (b, 0, 0)),
                  new_spec,
                  pl.BlockSpec((None, t_new * KV_ROWS, HEAD_DIM), lambda b: (b, 0, 0))],
        out_specs=[pl.BlockSpec((1, t_new, ATT_WIDTH), lambda b: (b, 0, 0)),
                   pl.BlockSpec((1, t_new, ATT_WIDTH), lambda b: (b, 0, 0))],
        out_shape=[jax.ShapeDtypeStruct((nb, t_new, ATT_WIDTH), F32),
                   jax.ShapeDtypeStruct((nb, t_new, ATT_WIDTH), F32)],
        scratch_shapes=scratch,
        compiler_params=_params("arbitrary"),
        name=f"attn_sample_r{r}",
    )(q, new_view, old_buf)


def _combine_kernel(o0, o1, o2, l0, l1, l2, out_ref):
    ls = [l0[...], l1[...], l2[...]]
    m = jnp.maximum(jnp.maximum(ls[0], ls[1]), ls[2])
    es = [jnp.exp(l - m) for l in ls]
    num = es[0] * o0[...] + es[1] * o1[...] + es[2] * o2[...]
    out_ref[...] = (num / (es[0] + es[1] + es[2])).astype(out_ref.dtype)


def _combine_groups(outs, lses):
    shape = outs[0].shape
    return pl.pallas_call(
        _combine_kernel,
        in_specs=[_full(shape)] * 6,
        out_specs=_full(shape),
        out_shape=jax.ShapeDtypeStruct(shape, BF16),
        grid=(1,),
        compiler_params=_params("arbitrary"),
        name="combine_groups",
    )(*outs, *lses)


FF_CHUNK = 256
CARRY_ROWS = 8
SHIFT_SLOTS = 4


MOVE_ROWS = 8192


def _move_regions(keeps):
    regions = [(g, start, min(MOVE_ROWS, keep - start)) for g, keep in enumerate(keeps)
               for start in range(0, keep, MOVE_ROWS)]
    halves = ([], [])
    for reg in sorted(regions, key=lambda reg: -reg[2]):
        min(halves, key=lambda h: sum(r[2] for r in h)).append(reg)
    return halves


def _buffer_move(old_refs, kvn_refs, new_refs, bufs, tails, sems, b, n_steps):
    n_new = [k.shape[1] for k in kvn_refs]
    halves = _move_regions([o.shape[1] - n for o, n in zip(old_refs, n_new)])

    def copies(batch):
        out, sem = [], 0
        for buf, half in zip(bufs, halves):
            row, cps = 0, []
            for g, start, size in half:
                stage = buf.at[pl.ds(row, size)]
                cps.append((pltpu.make_async_copy(old_refs[g].at[batch, pl.ds(n_new[g] + start, size)], stage,
                                                  sems.at[sem]),
                            pltpu.make_async_copy(stage, new_refs[g].at[batch, pl.ds(start, size)],
                                                  sems.at[sem + 1])))
                row += size
                sem += 2
            out.append(cps)
        tail = []
        for g in range(N_GROUPS):
            keep = old_refs[g].shape[1] - n_new[g]
            tail.append((pltpu.make_async_copy(kvn_refs[g].at[batch], tails.at[g], sems.at[sem]),
                         pltpu.make_async_copy(tails.at[g], new_refs[g].at[batch, pl.ds(keep, n_new[g])],
                                               sems.at[sem + 1])))
            sem += 2
        return out[0], out[1], tail
    first, second, tail = copies(b)
    _, prev_second, _ = copies(jnp.maximum(b - 1, 0))

    def top():
        for cp_in, _ in first + tail:
            cp_in.start()

    def mid():
        for cp_in, cp_out in first + tail:
            cp_in.wait()
            cp_out.start()

        @pl.when(b > 0)
        def _():
            for _, cp_out in prev_second:
                cp_out.wait()
        for cp_in, _ in second:
            cp_in.start()

    def end():
        for cp_in, cp_out in second:
            cp_in.wait()
            cp_out.start()
        for _, cp_out in first + tail:
            cp_out.wait()

        @pl.when(b == n_steps - 1)
        def _():
            for _, cp_out in second:
                cp_out.wait()
    return top, mid, end


def _move_scratch(old, new_rows):
    halves = _move_regions([o.shape[1] - n.shape[1] for o, n in zip(old, new_rows)])
    n_copies = sum(len(h) for h in halves) + N_GROUPS
    return ([pltpu.VMEM((sum(r[2] for r in h), HEAD_DIM), F32) for h in halves]
            + [pltpu.VMEM((N_GROUPS,) + new_rows[0].shape[1:], F32), pltpu.SemaphoreType.DMA((2 * n_copies,))])


def _out_ffn_kernel(*refs, seq_len, tiles_per_seq, tail_rows, move):
    move_top = move_mid = move_end = lambda: None
    if move:
        (x_ref, mix_ref, cr_ref, wo_ref, g_ref, wup_ref, cw_ref, cb_ref, wdn_ref, o0, o1, o2, k0, k1, k2,
         out_ref, tail_ref, n0, n1, n2, shift, act_scr, carry, buf_a, buf_b, tails, sems) = refs
        move_top, move_mid, move_end = _buffer_move(
            (o0, o1, o2), (k0, k1, k2), (n0, n1, n2), (buf_a, buf_b), tails, sems,
            pl.program_id(0), pl.num_programs(0))
    elif seq_len is None:
        (x_ref, mix_ref, cr_ref, wo_ref, g_ref, wup_ref, cw_ref, cb_ref, wdn_ref,
         out_ref, tail_ref, shift, act_scr, carry) = refs
    else:
        (x_ref, mix_ref, cr_ref, wo_ref, g_ref, wup_ref, cw_ref, cb_ref, wdn_ref, e1_ref, e2_ref,
         out_ref, tail_ref, shift, act_scr) = refs
    tm = x_ref.shape[0]
    d_ff = wdn_ref.shape[0]
    move_top()
    x1 = x_ref[...] + _dot(jnp.concatenate([mix_ref[...], cr_ref[...]], axis=-1), wo_ref[...])
    h = _rms(x1, g_ref[...]).astype(BF16)
    if seq_len is None:
        @pl.when(pl.program_id(0) % tiles_per_seq == 0)
        def _():
            carry[...] = jnp.zeros_like(carry)
    else:
        t = lax.broadcasted_iota(jnp.int32, (tm, 1), 0) % seq_len
        has1 = t >= 1
        has2 = t >= 2

    def chunk_cols(j):
        return (slice(j * FF_CHUNK, (j + 1) * FF_CHUNK),
                slice(d_ff + j * FF_CHUNK, d_ff + (j + 1) * FF_CHUNK))

    def up_proj(j):
        return tuple(_dot(h, wup_ref[:, cols]) for cols in chunk_cols(j))

    def conv(up, cols, slot):
        buf = shift.at[slot]
        if seq_len is None:
            buf[:CARRY_ROWS, :] = carry[:, cols]
            carry[:, cols] = up[tm - CARRY_ROWS:, :]
        else:
            buf[:CARRY_ROWS, :] = jnp.zeros((CARRY_ROWS, FF_CHUNK), F32)
        buf[CARRY_ROWS:, :] = up
        tail_ref[0, :, cols] = up[tm - tail_rows:, :]
        prev1 = buf[CARRY_ROWS - 1:CARRY_ROWS - 1 + tm, :]
        prev2 = buf[CARRY_ROWS - 2:CARRY_ROWS - 2 + tm, :]
        if seq_len is not None:
            prev1 = jnp.where(has1, prev1, e1_ref[:, cols])
            prev2 = jnp.where(has2, prev2, e2_ref[:, cols])
        return (cb_ref[:, cols] + cw_ref[0:1, cols] * prev2 + cw_ref[1:2, cols] * prev1
                + cw_ref[2:3, cols] * up)

    n_chunks = d_ff // FF_CHUNK
    ups = up_proj(0)
    for j in range(n_chunks):
        nxt = up_proj(j + 1) if j + 1 < n_chunks else None
        a, b = (conv(up, cols, 2 * (j % 2) + s) for s, (up, cols) in enumerate(zip(ups, chunk_cols(j))))
        act_scr[:, j * FF_CHUNK:(j + 1) * FF_CHUNK] = (a * jax.nn.sigmoid(a) * b).astype(BF16)
        ups = nxt
        if j == n_chunks // 2:
            move_mid()
    out_ref[...] = x1 + _dot(act_scr[...], wdn_ref[...])
    move_end()


def _layer_spec(stacked, layer):
    nd = stacked.ndim - 1
    return pl.BlockSpec((None,) + stacked.shape[1:], lambda *_: (layer,) + (0,) * nd,
                        pipeline_mode=pl.Buffered(1))


def _out_ffn(x2d, mix, cross, layer, w_out_bf, g_ffn, w_up_bf, conv_w, conv_b, w_down_bf, tm,
             tiles_per_seq=None, seq_len=None, e1=None, e2=None, tail_rows=CARRY_ROWS, moves=None):
    rows, d = x2d.shape
    depth, d_ff, _ = w_down_bf.shape
    n_tiles = rows // tm
    args = [x2d, mix, cross, w_out_bf, g_ffn.reshape(depth, 1, d), w_up_bf, conv_w,
            conv_b.reshape(depth, 1, 2 * d_ff), w_down_bf]
    in_specs = [
        pl.BlockSpec((tm, d), lambda i: (i, 0)),
        pl.BlockSpec((tm, ATT_WIDTH), lambda i: (i, 0)),
        pl.BlockSpec((tm, ATT_WIDTH), lambda i: (i, 0)),
        *[_layer_spec(a, layer) for a in args[3:]],
    ]
    scratch = [pltpu.VMEM((SHIFT_SLOTS, CARRY_ROWS + tm, FF_CHUNK), F32), pltpu.VMEM((tm, d_ff), BF16)]
    if seq_len is None:
        scratch += [pltpu.VMEM((CARRY_ROWS, 2 * d_ff), F32)]
    else:
        in_specs += [pl.BlockSpec((tm, 2 * d_ff), lambda i: (i, 0))] * 2
        args += [e1, e2]
    out_specs = [pl.BlockSpec((tm, d), lambda i: (i, 0)),
                 pl.BlockSpec((1, tail_rows, 2 * d_ff), lambda i: (i, 0, 0))]
    out_shape = [jax.ShapeDtypeStruct((rows, d), F32),
                 jax.ShapeDtypeStruct((n_tiles, tail_rows, 2 * d_ff), F32)]
    if moves is not None:
        old, new_rows = moves
        assert seq_len is None and all(o.shape[0] == n_tiles for o in old)
        any_spec = pl.BlockSpec(memory_space=pl.ANY)
        in_specs += [any_spec] * (2 * N_GROUPS)
        args += [*old, *new_rows]
        out_specs += [any_spec] * N_GROUPS
        out_shape += [jax.ShapeDtypeStruct(o.shape, o.dtype) for o in old]
        scratch += _move_scratch(old, new_rows)
    return pl.pallas_call(
        functools.partial(_out_ffn_kernel, seq_len=seq_len, tiles_per_seq=tiles_per_seq,
                          tail_rows=tail_rows, move=moves is not None),
        grid=(n_tiles,),
        in_specs=in_specs,
        out_specs=out_specs,
        out_shape=out_shape,
        scratch_shapes=scratch,
        compiler_params=_params("arbitrary"),
        name="out_ffn",
    )(*args)


def _swap_halves(x):
    return pltpu.roll(x, S5_STATE, 1)


def _s5_prep_kernel(lam_ref, logdt_ref, bt_ref, c_ref, d_ref, t_ref, cpt_ref, *var_refs, n_valids):
    gc = lam_ref.shape[0]
    n_groups = gc // S5_GROUP
    lane = lax.broadcasted_iota(jnp.int32, lam_ref.shape, 1)
    first = lane < S5_STATE
    sign = jnp.where(first, -1.0, 1.0)
    a = lam_ref[...]
    a_sw = _swap_halves(a)
    are = jnp.where(first, a, a_sw)
    aim = jnp.where(first, a_sw, a)
    dt = jnp.exp(logdt_ref[...])
    mag = jnp.exp(are * dt)
    lr = mag * jnp.cos(aim * dt)
    li = mag * jnp.sin(aim * dt)
    den = are * are + aim * aim
    xr = lr - 1.0
    f_re = (xr * are + li * aim) / den
    f_im = (li * are - xr * aim) / den
    lb = sign * li

    def cmul(x, m_re, m_sw):
        return x * m_re + _swap_halves(x) * m_sw

    c = c_ref[...]
    c_neg = c * -sign
    ri = lax.broadcasted_iota(jnp.int32, (gc, gc), 0)
    ci = lax.broadcasted_iota(jnp.int32, (gc, gc), 1)
    same_group = (ri // S5_GROUP) == (ci // S5_GROUP)
    e = cmul(bt_ref[...], f_re, sign * f_im)
    cl = c
    pw = jnp.where(first, 1.0, 0.0)
    zr = lax.broadcasted_iota(jnp.int32, (gc, S5_BLOCK), 0)
    zc = lax.broadcasted_iota(jnp.int32, (gc, S5_BLOCK), 1)
    same_out = (zr % S5_GROUP) == (zc % S5_GROUP)
    lags = jnp.zeros((gc, S5_BLOCK), F32)
    for k in range(S5_SUB):
        kmat = jnp.where(same_group, _dot_nt(e, c_neg, precision=PREP_PRECISION), 0.0)
        if k == 0:
            kmat = kmat + jnp.where(ri == ci, d_ref[...], 0.0)
        place = jnp.where(same_out & (zc // S5_GROUP == k), 1.0, 0.0)
        lags = lags + jnp.dot(kmat, place, precision=PREP_PRECISION, preferred_element_type=F32)
        both = jnp.concatenate([e, _swap_halves(e)], axis=-1).reshape(n_groups, S5_GROUP, 4 * S5_STATE)
        for n_valid, bp_ref in zip(n_valids, var_refs[0::2]):
            if n_valid - 1 - k >= 0:
                bp_ref[:, n_valid - 1 - k] = both.astype(bp_ref.dtype)
        cl = cmul(cl, lr, lb)
        cpt_ref[:, k] = (cl * -sign).reshape(n_groups, S5_GROUP, 2 * S5_STATE).astype(cpt_ref.dtype)
        e = cmul(e, lr, lb)
        pw = cmul(pw, lr, lb)
        for n_valid, lam_out_ref in zip(n_valids, var_refs[1::2]):
            if k + 1 == n_valid:
                lam_out_ref[...] = pw
    for n_valid, bp_ref in zip(n_valids, var_refs[0::2]):
        for tau in range(n_valid, S5_SUB):
            bp_ref[:, tau] = jnp.zeros((n_groups, S5_GROUP, 4 * S5_STATE), bp_ref.dtype)
    for taup in range(S5_SUB):
        moved = lags if taup == 0 else jnp.where(zc >= taup * S5_GROUP, pltpu.roll(lags, taup * S5_GROUP, 1), 0.0)
        t_ref[:, taup] = moved.reshape(n_groups, S5_GROUP, S5_BLOCK).astype(t_ref.dtype)


def _s5_prep(lam_re, lam_im, log_dt, b_re, b_im, c_re, c_im, d_skip, n_valids):
    n_groups = lam_re.shape[0]
    gc = n_groups * S5_GROUP
    rep = lambda t: jnp.repeat(t, S5_GROUP, axis=0)
    lam_p = rep(jnp.concatenate([lam_re, lam_im], axis=-1))
    logdt = rep(log_dt.reshape(n_groups, 1))
    bt_p = jnp.concatenate([b_re.transpose(0, 2, 1), b_im.transpose(0, 2, 1)], axis=-1).reshape(gc, 2 * S5_STATE)
    c_p = jnp.concatenate([c_re, c_im], axis=-1).reshape(gc, 2 * S5_STATE)
    t_shape = (n_groups, S5_SUB, S5_GROUP, S5_BLOCK)
    cpt_shape = (n_groups, S5_SUB, S5_GROUP, 2 * S5_STATE)
    bp_shape = (n_groups, S5_SUB, S5_GROUP, 4 * S5_STATE)
    lam_shape = (gc, 2 * S5_STATE)
    var = [(bp_shape, BF16), (lam_shape, F32)] * len(n_valids)
    t, cpt, *rest = pl.pallas_call(
        functools.partial(_s5_prep_kernel, n_valids=tuple(n_valids)),
        grid=(1,),
        in_specs=[_full(lam_p.shape), _full(logdt.shape), _full(bt_p.shape), _full(c_p.shape), _full((gc, 1))],
        out_specs=[_full(t_shape), _full(cpt_shape), *[_full(s) for s, _ in var]],
        out_shape=[jax.ShapeDtypeStruct(t_shape, BF16), jax.ShapeDtypeStruct(cpt_shape, BF16),
                   *[jax.ShapeDtypeStruct(s, dt) for s, dt in var]],
        compiler_params=_params("arbitrary"),
        name="s5_prep",
    )(lam_p, logdt, bt_p, c_p, d_skip.reshape(gc, 1))
    merge = lambda a: a.reshape(n_groups, S5_BLOCK, a.shape[-1])
    return [(merge(t), merge(bp), merge(cpt), lam_n[::S5_GROUP]) for bp, lam_n in zip(rest[0::2], rest[1::2])]


SCAN_GROUPS = 16
SUBLANES = 8
GLU_ROWS = 512


def _s5_core_kernel(v_ref, x0_ref, x0s_ref, t_ref, bp_ref, cpt_ref, lam_ref, wglu_ref, bglu_ref,
                    mix_ref, xfin_ref, s_scr, xprev_scr, m_scr, stage, *, n_valid, n_sub):
    rows = v_ref.shape[0]
    n_groups = t_ref.shape[0]
    lane = lax.broadcasted_iota(jnp.int32, (1, 2 * S5_STATE), 1)
    first = lane < S5_STATE
    blk = lambda g: slice(g * S5_BLOCK, (g + 1) * S5_BLOCK)
    tile = lambda i: slice(i * 2 * S5_STATE, (i + 1) * 2 * S5_STATE)
    for g in range(n_groups):
        s_scr[:, blk(g)] = _dot(v_ref[:, blk(g)], bp_ref[g])
        lam = lam_ref[g:g + 1, :]
        lam_sw = _swap_halves(lam)
        m_scr[0:1, tile(g)] = jnp.where(first, lam, lam_sw)
        m_scr[1:2, tile(g)] = jnp.where(first, -lam_sw, lam)
    if n_sub == 1:
        for g in range(n_groups):
            x, xs = x0_ref[:, tile(g)], x0s_ref[:, tile(g)]
            xprev_scr[:, tile(g)] = x
            xfin_ref[:, tile(g)] = (x * m_scr[0:1, tile(g)] + xs * m_scr[1:2, tile(g)]
                                    + s_scr[:, tile(2 * g)])
    else:
        assert rows == n_sub
        for g0 in range(0, n_groups, SCAN_GROUPS):
            gs = range(g0, min(n_groups, g0 + SCAN_GROUPS))

            def body(i, carry, gs=gs):
                rows8 = pl.ds(pl.multiple_of(i * SUBLANES, SUBLANES), SUBLANES)
                sub_i = lax.broadcasted_iota(jnp.int32, (SUBLANES, 2 * S5_STATE), 0)
                out = []
                for g, (x, xs) in zip(gs, carry):
                    s8, ssw8 = s_scr[rows8, tile(2 * g)], s_scr[rows8, tile(2 * g + 1)]
                    prev8 = jnp.zeros((SUBLANES, 2 * S5_STATE), F32)
                    for r in range(SUBLANES):
                        prev8 = jnp.where(sub_i == r, x, prev8)
                        m_re, m_sw = m_scr[0:1, tile(g)], m_scr[1:2, tile(g)]
                        x, xs = (x * m_re + xs * m_sw + s8[r:r + 1, :],
                                 xs * m_re - x * m_sw + ssw8[r:r + 1, :])
                    xprev_scr[rows8, tile(g)] = prev8
                    out.append((x, xs))
                return tuple(out)
            init = tuple((x0_ref[0:1, tile(g)], x0s_ref[0:1, tile(g)]) for g in gs)
            fin = lax.fori_loop(0, n_sub // SUBLANES, body, init)
            for g, (x, _) in zip(gs, fin):
                xfin_ref[:, tile(g)] = jnp.broadcast_to(x, (xfin_ref.shape[0], 2 * S5_STATE))
    for g in range(n_groups):
        s_scr[:, blk(g)] = (_dot(v_ref[:, blk(g)], t_ref[g])
                            + _dot_nt(xprev_scr[:, tile(g)].astype(BF16), cpt_ref[g]))
    for cb in range(ATT_WIDTH // HEAD_DIM):
        for half in range(-(-n_valid // SLOTS)):
            toks = _slot_transpose([s_scr[:, tile(2 * (cb * SLOTS + p) + half)] for p in range(SLOTS)])
            for s in range(min(SLOTS, n_valid - half * SLOTS)):
                stage[cb, pl.ds(half * SLOTS + s, rows, stride=n_valid), :] = toks[s]
    n_tok = rows * n_valid
    chunk = min(n_tok, GLU_ROWS)

    def glu(i, carry):
        r = pl.ds(pl.multiple_of(i * chunk, chunk), chunk)
        y = jax.nn.gelu(jnp.concatenate([stage[cb, r, :] for cb in range(ATT_WIDTH // HEAD_DIM)], axis=-1))
        z = _dot(y.astype(BF16), wglu_ref[...]) + bglu_ref[...]
        mix_ref[r, :] = (y * jax.nn.sigmoid(z)).astype(mix_ref.dtype)
        return carry
    lax.fori_loop(0, n_tok // chunk, glu, 0)


def _s5_mixer(v, rows_per_tile, n_valid, n_sub, x0, x0s, t, bp, cpt, lam_n, w_glu_bf, b_glu):
    m, width = v.shape
    n_tiles = m // rows_per_tile
    r0 = x0.shape[0] // n_tiles
    n_state = x0.shape[1]
    n_tok = rows_per_tile * n_valid
    return pl.pallas_call(
        functools.partial(_s5_core_kernel, n_valid=n_valid, n_sub=n_sub),
        grid=(n_tiles,),
        in_specs=[pl.BlockSpec((rows_per_tile, width), lambda i: (i, 0)),
                  pl.BlockSpec((r0, n_state), lambda i: (i, 0)),
                  pl.BlockSpec((r0, n_state), lambda i: (i, 0)),
                  _full(t.shape), _full(bp.shape), _full(cpt.shape), _full(lam_n.shape),
                  _full(w_glu_bf.shape), _full((1, ATT_WIDTH))],
        out_specs=[pl.BlockSpec((n_tok, ATT_WIDTH), lambda i: (i, 0)),
                   pl.BlockSpec((r0, n_state), lambda i: (i, 0))],
        out_shape=[jax.ShapeDtypeStruct((m * n_valid, ATT_WIDTH), BF16),
                   jax.ShapeDtypeStruct(x0.shape, F32)],
        scratch_shapes=[pltpu.VMEM((rows_per_tile, width), F32),
                        pltpu.VMEM((rows_per_tile, n_state), F32),
                        pltpu.VMEM((8, n_state), F32),
                        pltpu.VMEM((ATT_WIDTH // HEAD_DIM, n_tok, HEAD_DIM), F32)],
        compiler_params=_params("arbitrary"),
        name="s5_core",
    )(v, x0, x0s, t, bp, cpt, lam_n, w_glu_bf, b_glu.reshape(1, ATT_WIDTH))


def _pack_state(s):
    n = s.shape[0]
    packed = s.transpose(0, 2, 1, 3).reshape(n, -1)
    swapped = jnp.stack([s[:, 1], s[:, 0]], axis=1).transpose(0, 2, 1, 3).reshape(n, -1)
    return packed, swapped


def _unpack_state(x):
    n = x.shape[0]
    return x.reshape(n, -1, 2, S5_STATE).transpose(0, 2, 1, 3)


PROMPT_TILE = 512
Q_PAD = 16


def _pad_rows(t, n):
    return jnp.pad(t, ((0, 0), (0, n - t.shape[1]), (0, 0)))


def kernel(x_prompt, x_sample, cache_win0_kv, cache_win1_kv, cache_win2_kv, cache_mem_kv, state_s5,
           state_ffn_conv, mem_prompt, g_mix, g_ffn, w_in_a, g_q_dil, g_k_dil, w_in_b, s5_lam_re,
           s5_lam_im, s5_log_dt, s5_b_re, s5_b_im, s5_c_re, s5_c_im, s5_d, w_glu, b_glu, g_mem,
           w_mem_kv, g_q_cross, g_k_cross, w_out, w_up, conv_w, conv_b, w_down):
    nb, seq, d = x_prompt.shape
    db, ts, _ = x_sample.shape
    depth = g_mix.shape[0]
    n_mem = mem_prompt.shape[1]
    d_ff2 = w_up.shape[2]
    assert ts >= CONV_W - 1 and seq % PROMPT_TILE == 0 and ts <= Q_PAD
    caches = (cache_win0_kv, cache_win1_kv, cache_win2_kv)
    w_mem_bf, w_out_bf, w_up_bf, w_down_bf = (w.astype(BF16) for w in (w_mem_kv, w_out, w_up, w_down))
    mem_cache = cache_mem_kv.reshape(depth * db, n_mem * KV_ROWS, HEAD_DIM)

    tab_p = _rope_tables(jnp.arange(seq, dtype=jnp.int32))
    tab_s = tuple(jnp.tile(t, (db, 1)) for t in _rope_tables(PAST_LEN + jnp.arange(ts, dtype=jnp.int32)))
    win_keep = tuple(min(w, seq) for w, _ in DIL_GROUPS)
    rows_s = db * ts
    kv_tail = (2, N_HEADS, HEAD_DIM)

    mem_p = _mem_kv(mem_prompt.reshape(nb * n_mem, d), g_mem, w_mem_bf, g_k_cross)

    xp = x_prompt.reshape(nb * seq, d)
    xs = x_sample.reshape(rows_s, d)
    p_win, s_win = [[] for _ in DIL_GROUPS], [[] for _ in DIL_GROUPS]
    p_s5, s_s5, p_conv, s_conv = [], [], [], []
    tiles_per_seq = seq // PROMPT_TILE
    for i in range(depth):
        mem_i = mem_p[i].reshape(nb, n_mem * KV_ROWS, HEAD_DIM)
        if i % 2 == 0:
            ia = i // 2
            w_in_a_bf = w_in_a[ia].astype(BF16)
            *qkv_groups, w0, w1, w2, cross_p = _in_proj_a(
                xp, seq, PROMPT_TILE, g_mix[i], w_in_a_bf, tab_p, g_q_dil[ia], g_k_dil[ia],
                g_q_cross[i], mem_i, win_keep, dilated=True)
            for g, w in enumerate((w0, w1, w2)):
                p_win[g].append(w.reshape((nb, win_keep[g]) + kv_tail))
            mix_p = _attn_prompt(qkv_groups, nb, seq).reshape(nb * seq, ATT_WIDTH)
            qkv_s, *kv_new, qc_s = _in_proj_a(
                xs, rows_s, rows_s, g_mix[i], w_in_a_bf, tab_s, g_q_dil[ia], g_k_dil[ia],
                None, None, (rows_s,) * N_GROUPS, dilated=False)
            old_bufs = [c[ia].reshape(db, -1, HEAD_DIM) for c in caches]
            moves = (old_bufs, [k.reshape(db, ts * KV_ROWS, HEAD_DIM) for k in kv_new])
        else:
            moves = None
            ib = i // 2
            prm = (s5_lam_re[ib], s5_lam_im[ib], s5_log_dt[ib], s5_b_re[ib], s5_b_im[ib], s5_c_re[ib],
                   s5_c_im[ib], s5_d[ib])
            n_state = s5_lam_re.shape[1] * 2 * S5_STATE
            w_in_b_bf, w_glu_bf = w_in_b[ib].astype(BF16), w_glu[ib].astype(BF16)
            u_p, cross_p = _in_proj_b(xp, seq, PROMPT_TILE, S5_SUB, g_mix[i], w_in_b_bf, g_q_cross[i], mem_i)
            zero = jnp.zeros((nb * CARRY_ROWS, n_state), F32)
            prep_p, prep_s = _s5_prep(*prm, (S5_SUB, ts))
            mix_p, fin_p = _s5_mixer(u_p, seq // S5_SUB, S5_SUB, seq // S5_SUB, zero, zero,
                                     *prep_p, w_glu_bf, b_glu[ib])
            p_s5.append(_unpack_state(fin_p.reshape(nb, CARRY_ROWS, n_state)[:, 0]))
            u_s, qc_s = _in_proj_b(xs, rows_s, rows_s, ts, g_mix[i], w_in_b_bf, None, None)
            mix_s, fin_s = _s5_mixer(u_s, db, ts, 1, *_pack_state(state_s5[ib]), *prep_s, w_glu_bf, b_glu[ib])
            s_s5.append(_unpack_state(fin_s))
        cross_s = _cross_sample(_pad_rows(qc_s.reshape(db, ts, ATT_WIDTH), Q_PAD), g_q_cross[i], mem_cache,
                                first=i * db)
        cross_s = cross_s[:, :ts].reshape(rows_s, ATT_WIDTH)

        ffn_w = (i, w_out_bf, g_ffn, w_up_bf, conv_w, conv_b, w_down_bf)
        xp, tails, *new_bufs = _out_ffn(xp, mix_p, cross_p, *ffn_w, PROMPT_TILE, tiles_per_seq=tiles_per_seq,
                                        moves=moves)
        p_conv.append(tails.reshape(nb, tiles_per_seq, CARRY_ROWS, d_ff2)[:, -1, CARRY_ROWS - (CONV_W - 1):])
        if moves is not None:
            outs, lses = [], []
            for g, (_, r) in enumerate(DIL_GROUPS):
                q_g = qkv_s[g * N_HEADS:(g + 1) * N_HEADS]
                q_g = _pad_rows(q_g.transpose(1, 0, 2).reshape(db, ts, ATT_WIDTH), Q_PAD)
                o_g, lse_g = _attn_sample(q_g, new_bufs[g], moves[0][g], r, ts)
                outs.append(o_g.reshape(rows_s, ATT_WIDTH))
                lses.append(lse_g.reshape(rows_s, ATT_WIDTH))
                s_win[g].append(new_bufs[g].reshape(caches[g][i // 2].shape))
            mix_s = _combine_groups(outs, lses)
        buf = state_ffn_conv[i]
        zero = jnp.zeros((db, ts - 2, d_ff2), F32)
        e1 = jnp.concatenate([buf[:, 1:2], zero, zero[:, :1]], axis=1).reshape(rows_s, d_ff2)
        e2 = jnp.concatenate([buf, zero], axis=1).reshape(rows_s, d_ff2)
        xs, tails = _out_ffn(xs, mix_s, cross_s, *ffn_w, rows_s, seq_len=ts, e1=e1, e2=e2, tail_rows=rows_s)
        s_conv.append(tails.reshape(db, ts, d_ff2)[:, ts - (CONV_W - 1):])

    return (xp.reshape(nb, seq, d), xs.reshape(db, ts, d),
            jnp.stack(p_win[0]), jnp.stack(p_win[1]), jnp.stack(p_win[2]),
            mem_p.reshape((depth, nb, n_mem) + kv_tail),
            jnp.stack(p_s5), jnp.stack(p_conv),
            jnp.stack(s_win[0]), jnp.stack(s_win[1]), jnp.stack(s_win[2]),
            jnp.stack(s_s5), jnp.stack(s_conv))
```
